```python
import math
import jax, jax.numpy as jnp
from jax import lax
import numpy as np

D_MODEL = 1024
BATCH = 4
SEQ = 4096
DEPTH = 1

CHUNK = 64
Q_BLOCK = 128
N_MEM = 256
EPS = 1e-6
NEG_INF = -1e30

D_MIX = D_MODEL
D_SSM = D_MIX // 2
SSM_GROUP = 16
SSM_GROUPS = D_SSM // SSM_GROUP
SSM_STATE = 64
DT_MIN = 1e-3
DT_MAX = 1e-1
D_MLA = D_MIX - D_SSM
MLA_HEADS = 8
MLA_NOPE = 64
MLA_ROPE = 32
MLA_QK = MLA_NOPE + MLA_ROPE
MLA_V = D_MLA // MLA_HEADS
MLA_Q_RANK = 768
MLA_KV_RANK = 256
ROPE_THETA = 10000.0
D_IN = D_SSM + MLA_Q_RANK + MLA_KV_RANK + MLA_ROPE
XATTN_HEADS = 4
XATTN_HEAD_DIM = D_MODEL // XATTN_HEADS
MOE_GROUPS = 4
MOE_PER_GROUP = 8
MOE_EXPERTS = MOE_GROUPS * MOE_PER_GROUP
MOE_TOPK = 2
MOE_FF = D_MODEL // 4

kernel_name = 'hybrid_s5_mla_hmoe_encoder'


def _rms(x, g):
    xf = x.astype(jnp.float32)
    y = xf * lax.rsqrt(jnp.mean(xf * xf, axis=-1, keepdims=True) + EPS)
    return (y * g.astype(jnp.float32)).astype(x.dtype)


def _rope(x, pos):
    half = MLA_ROPE // 2
    inv_freq = ROPE_THETA ** (-jnp.arange(half, dtype=jnp.float32) / half)
    ang = pos.astype(jnp.float32)[..., None] * inv_freq
    if x.ndim == 4:
        ang = ang[:, :, None, :]
    cos, sin = jnp.cos(ang), jnp.sin(ang)
    xf = x.astype(jnp.float32)
    x1, x2 = xf[..., :half], xf[..., half:]
    return jnp.concatenate([x1 * cos - x2 * sin, x1 * sin + x2 * cos], axis=-1).astype(x.dtype)


def _s5_mixer(u, lam_re, lam_im, log_dt, b_re, b_im, c_re, c_im, d_skip, w_glu, b_glu):
    bsz, seq, _ = u.shape
    uf = u.astype(jnp.float32).reshape(bsz, seq, SSM_GROUPS, SSM_GROUP)
    lam = lax.complex(jnp.minimum(lam_re.astype(jnp.float32), -1e-4), lam_im.astype(jnp.float32))
    dt = jnp.exp(log_dt.astype(jnp.float32))[:, None]
    a_bar = jnp.exp(lam * dt)
    b = lax.complex(b_re.astype(jnp.float32), b_im.astype(jnp.float32))
    b_bar = ((a_bar - 1.0) / lam)[..., None] * b
    c = lax.complex(c_re.astype(jnp.float32), c_im.astype(jnp.float32))
    bu = jnp.einsum('gph,bsgh->bsgp', b_bar, uf.astype(jnp.complex64))
    a_full = jnp.broadcast_to(a_bar, bu.shape)

    def combine(earlier, later):
        a_e, s_e = earlier
        a_l, s_l = later
        return a_l * a_e, a_l * s_e + s_l

    _, states = lax.associative_scan(combine, (a_full, bu), axis=1)
    y = jnp.einsum('ghp,bsgp->bsgh', c, states).real
    y = y + d_skip.astype(jnp.float32).reshape(SSM_GROUPS, SSM_GROUP) * uf
    y = jax.nn.gelu(y.reshape(bsz, seq, D_SSM))
    y = y * jax.nn.sigmoid(y @ w_glu.astype(jnp.float32) + b_glu.astype(jnp.float32))
    return y.astype(u.dtype)


def _mla_mixer(c_q, c_kv, k_rope, pos, q_norm_g, w_q_up, kv_norm_g, w_kv_up):
    bsz, seq, _ = c_q.shape
    q = (_rms(c_q, q_norm_g) @ w_q_up).reshape(bsz, seq, MLA_HEADS, MLA_QK)
    q = jnp.concatenate([q[..., :MLA_NOPE], _rope(q[..., MLA_NOPE:], pos)], axis=-1)
    kv = (_rms(c_kv, kv_norm_g) @ w_kv_up).reshape(bsz, seq, MLA_HEADS, MLA_NOPE + MLA_V)
    k_pe = _rope(k_rope, pos)
    k = jnp.concatenate(
        [kv[..., :MLA_NOPE], jnp.broadcast_to(k_pe[:, :, None, :], (bsz, seq, MLA_HEADS, MLA_ROPE))],
        axis=-1)
    v = kv[..., MLA_NOPE:]
    scale = MLA_QK ** -0.5
    n_blk = seq // Q_BLOCK
    q_blocks = q.reshape(bsz, n_blk, Q_BLOCK, MLA_HEADS, MLA_QK).transpose(1, 0, 2, 3, 4)
    key_chunk = jnp.arange(seq) // CHUNK

    def attend(args):
        blk, qb = args
        s = jnp.einsum('bqhd,bkhd->bhqk', qb, k).astype(jnp.float32) * scale
        q_chunk = (blk * Q_BLOCK + jnp.arange(Q_BLOCK)) // CHUNK
        mask = key_chunk[None, :] <= q_chunk[:, None]
        s = jnp.where(mask[None, None], s, NEG_INF)
        p = jax.nn.softmax(s, axis=-1).astype(v.dtype)
        return jnp.einsum('bhqk,bkhd->bqhd', p, v)

    o = lax.map(attend, (jnp.arange(n_blk), q_blocks))
    return o.transpose(1, 0, 2, 3, 4).reshape(bsz, seq, D_MLA)


def _memory_xattn(h, mem_n, w_q, w_k, w_v, w_o):
    bsz, seq, _ = h.shape
    n_mem = mem_n.shape[1]
    q = (h @ w_q).reshape(bsz, seq, XATTN_HEADS, XATTN_HEAD_DIM)
    k = (mem_n @ w_k).reshape(bsz, n_mem, XATTN_HEADS, XATTN_HEAD_DIM)
    v = (mem_n @ w_v).reshape(bsz, n_mem, XATTN_HEADS, XATTN_HEAD_DIM)
    s = jnp.einsum('bshd,bmhd->bhsm', q, k).astype(jnp.float32) * (XATTN_HEAD_DIM ** -0.5)
    p = jax.nn.softmax(s, axis=-1).astype(v.dtype)
    o = jnp.einsum('bhsm,bmhd->bshd', p, v).reshape(bsz, seq, D_MODEL)
    return o @ w_o


def _hier_moe(h, w_group, b_group, w_expert, b_expert, w_gate, w_up, w_down):
    bsz, seq, d = h.shape
    t = h.reshape(bsz * seq, d)
    g_prob = jax.nn.softmax((t @ w_group).astype(jnp.float32) + b_group.astype(jnp.float32), axis=-1)
    g_idx = jnp.argmax(g_prob, axis=-1)
    g_w = jnp.max(g_prob, axis=-1)
    g_onehot = jax.nn.one_hot(g_idx, MOE_GROUPS, dtype=jnp.float32)
    e_logits = jnp.einsum('td,dge->tge', t, w_expert).astype(jnp.float32) + b_expert.astype(jnp.float32)
    e_logits = jnp.sum(e_logits * g_onehot[:, :, None], axis=1)
    top_v, top_i = lax.top_k(e_logits, MOE_TOPK)
    top_w = jax.nn.softmax(top_v, axis=-1) * g_w[:, None]
    eid = g_idx[:, None] * MOE_PER_GROUP + top_i
    gates = jnp.sum(jax.nn.one_hot(eid, MOE_EXPERTS, dtype=jnp.float32) * top_w[..., None], axis=1)
    gates = gates.astype(t.dtype)
    out = jnp.zeros_like(t)
    for g in range(MOE_GROUPS):
        sl = slice(g * MOE_PER_GROUP, (g + 1) * MOE_PER_GROUP)
        a = jax.nn.silu(jnp.einsum('td,edf->tef', t, w_gate[sl])) * jnp.einsum('td,edf->tef', t, w_up[sl])
        a = a * gates[:, sl, None]
        out = out + jnp.einsum('tef,efd->td', a, w_down[sl])
    return out.reshape(bsz, seq, d)


def setup_inputs(seed: int = 0) -> dict:
    key = jax.random.key(seed)
    split = jax.random.split(key, 48)
    ks = iter([split[i] for i in range(48)])
    f32 = jnp.float32
    L = DEPTH

    def nrm(shape, scale):
        return jax.random.normal(next(ks), shape, f32) * scale

    def gain(n):
        return 1.0 + nrm((L, n), 0.02)

    x = nrm((BATCH, SEQ, D_MODEL), 1.0)
    mem = nrm((BATCH, N_MEM, D_MODEL), 1.0)
    offset = jax.random.randint(next(ks), (BATCH, 1), 0, 4096, dtype=jnp.int32)
    positions = offset + jnp.arange(SEQ, dtype=jnp.int32)[None, :]
    n_idx = jnp.arange(SSM_STATE, dtype=f32)
    return {
        'x': x,
        'mem': mem,
        'positions': positions,
        'norm_mix_g': gain(D_MODEL),
        'w_in': nrm((L, D_MODEL, D_IN), D_MODEL ** -0.5),
        'ssm_lam_re': -0.5 + nrm((L, SSM_GROUPS, SSM_STATE), 0.01),
        'ssm_lam_im': math.pi * n_idx + nrm((L, SSM_GROUPS, SSM_STATE), 0.01),
        'ssm_log_dt': jax.random.uniform(next(ks), (L, SSM_GROUPS), f32, math.log(DT_MIN), math.log(DT_MAX)),
        'ssm_b_re': nrm((L, SSM_GROUPS, SSM_STATE, SSM_GROUP), (2 * SSM_GROUP) ** -0.5),
        'ssm_b_im': nrm((L, SSM_GROUPS, SSM_STATE, SSM_GROUP), (2 * SSM_GROUP) ** -0.5),
        'ssm_c_re': nrm((L, SSM_GROUPS, SSM_GROUP, SSM_STATE), (2 * SSM_STATE) ** -0.5),
        'ssm_c_im': nrm((L, SSM_GROUPS, SSM_GROUP, SSM_STATE), (2 * SSM_STATE) ** -0.5),
        'ssm_d': nrm((L, D_SSM), 0.5),
        'ssm_w_glu': nrm((L, D_SSM, D_SSM), D_SSM ** -0.5),
        'ssm_b_glu': nrm((L, D_SSM), 0.01),
        'mla_q_norm_g': gain(MLA_Q_RANK),
        'mla_w_q_up': nrm((L, MLA_Q_RANK, MLA_HEADS * MLA_QK), MLA_Q_RANK ** -0.5),
        'mla_kv_norm_g': gain(MLA_KV_RANK),
        'mla_w_kv_up': nrm((L, MLA_KV_RANK, MLA_HEADS * (MLA_NOPE + MLA_V)), MLA_KV_RANK ** -0.5),
        'out_norm_ssm_g': gain(D_SSM),
        'out_norm_mla_g': gain(D_MLA),
        'w_out': nrm((L, D_MIX, D_MODEL), D_MIX ** -0.5),
        'norm_xattn_g': gain(D_MODEL),
        'norm_mem_g': gain(D_MODEL),
        'xattn_w_q': nrm((L, D_MODEL, D_MODEL), D_MODEL ** -0.5),
        'xattn_w_k': nrm((L, D_MODEL, D_MODEL), D_MODEL ** -0.5),
        'xattn_w_v': nrm((L, D_MODEL, D_MODEL), D_MODEL ** -0.5),
        'xattn_w_o': nrm((L, D_MODEL, D_MODEL), D_MODEL ** -0.5),
        'norm_moe_g': gain(D_MODEL),
        'moe_w_group': nrm((L, D_MODEL, MOE_GROUPS), D_MODEL ** -0.5),
        'moe_b_group': nrm((L, MOE_GROUPS), 0.01),
        'moe_w_expert': nrm((L, D_MODEL, MOE_GROUPS, MOE_PER_GROUP), D_MODEL ** -0.5),
        'moe_b_expert': nrm((L, MOE_GROUPS, MOE_PER_GROUP), 0.01),
        'moe_w_gate': nrm((L, MOE_EXPERTS, D_MODEL, MOE_FF), D_MODEL ** -0.5),
        'moe_w_up': nrm((L, MOE_EXPERTS, D_MODEL, MOE_FF), D_MODEL ** -0.5),
        'moe_w_down': nrm((L, MOE_EXPERTS, MOE_FF, D_MODEL), MOE_FF ** -0.5),
        'norm_final_g': 1.0 + nrm((D_MODEL,), 0.02),
    }


def reference(x, mem, positions, norm_mix_g, w_in, ssm_lam_re, ssm_lam_im, ssm_log_dt,
              ssm_b_re, ssm_b_im, ssm_c_re, ssm_c_im, ssm_d, ssm_w_glu, ssm_b_glu,
              mla_q_norm_g, mla_w_q_up, mla_kv_norm_g, mla_w_kv_up,
              out_norm_ssm_g, out_norm_mla_g, w_out,
              norm_xattn_g, norm_mem_g, xattn_w_q, xattn_w_k, xattn_w_v, xattn_w_o,
              norm_moe_g, moe_w_group, moe_b_group, moe_w_expert, moe_b_expert,
              moe_w_gate, moe_w_up, moe_w_down, norm_final_g):
    s1 = D_SSM
    s2 = s1 + MLA_Q_RANK
    s3 = s2 + MLA_KV_RANK
    for l in range(DEPTH):
        h = _rms(x, norm_mix_g[l])
        z = h @ w_in[l]
        u_ssm, c_q, c_kv, k_rope = z[..., :s1], z[..., s1:s2], z[..., s2:s3], z[..., s3:]
        y_ssm = _s5_mixer(u_ssm, ssm_lam_re[l], ssm_lam_im[l], ssm_log_dt[l], ssm_b_re[l], ssm_b_im[l],
                          ssm_c_re[l], ssm_c_im[l], ssm_d[l], ssm_w_glu[l], ssm_b_glu[l])
        y_mla = _mla_mixer(c_q, c_kv, k_rope, positions, mla_q_norm_g[l], mla_w_q_up[l],
                           mla_kv_norm_g[l], mla_w_kv_up[l])
        y = jnp.concatenate([_rms(y_ssm, out_norm_ssm_g[l]), _rms(y_mla, out_norm_mla_g[l])], axis=-1)
        x = x + y @ w_out[l]
        h = _rms(x, norm_xattn_g[l])
        x = x + _memory_xattn(h, _rms(mem, norm_mem_g[l]), xattn_w_q[l], xattn_w_k[l],
                              xattn_w_v[l], xattn_w_o[l])
        h = _rms(x, norm_moe_g[l])
        x = x + _hier_moe(h, moe_w_group[l], moe_b_group[l], moe_w_expert[l], moe_b_expert[l],
                          moe_w_gate[l], moe_w_up[l], moe_w_down[l])
    return _rms(x, norm_final_g)
```

```python
import functools
import math

import jax
import jax.numpy as jnp
from jax import lax
from jax.experimental import pallas as pl
from jax.experimental.pallas import tpu as pltpu

F32 = jnp.float32
BF16 = jnp.bfloat16

EPS = 1e-6
NEG_INF = -1e30
CHUNK = 64

LANES = 128
SSM_GROUP = 16
SSM_STATE = 64
MLA_HEADS = 8
MLA_NOPE = 64
MLA_ROPE = 32
MLA_QK = MLA_NOPE + MLA_ROPE
MLA_V = 64
ROPE_THETA = 10000.0
XATTN_HEADS = 4
MOE_GROUPS = 4
MOE_PER_GROUP = 8
MOE_EXPERTS = MOE_GROUPS * MOE_PER_GROUP
VMEM_LIMIT = 48 * 1024 * 1024


def _dot(a, b):
    return jnp.dot(a, b, preferred_element_type=F32)


def _dot_nt(a, b):
    return lax.dot_general(a, b, (((1,), (1,)), ((), ())), preferred_element_type=F32)


def _rms(x, g, n=None):
    n = x.shape[-1] if n is None else n
    ms = jnp.sum(x * x, axis=-1, keepdims=True) * (1.0 / n)
    return x * lax.rsqrt(ms + EPS) * g


def _params(*sem):
    return pltpu.CompilerParams(dimension_semantics=sem, vmem_limit_bytes=VMEM_LIMIT)


def _full(shape):
    zeros = (0,) * len(shape)
    return pl.BlockSpec(shape, lambda *_: zeros)


def _ssm_prep_kernel(lre_ref, lim_ref, ldt_ref, bre_ref, bim_ref, are_ref, aim_ref, bbre_ref, bbim_ref):
    lre = jnp.minimum(lre_ref[...], -1e-4)
    lim = lim_ref[...]
    dt = jnp.exp(ldt_ref[...])
    mag = jnp.exp(lre * dt)
    are = mag * jnp.cos(lim * dt)
    aim = mag * jnp.sin(lim * dt)
    are_ref[...] = are
    aim_ref[...] = aim
    nre = are - 1.0
    den = lre * lre + lim * lim
    fre = (nre * lre + aim * lim) / den
    fim = (aim * lre - nre * lim) / den
    bre = bre_ref[...]
    bim = bim_ref[...]
    bbre_ref[...] = fre * bre - fim * bim
    bbim_ref[...] = fre * bim + fim * bre


def _ssm_prep(lam_re, lam_im, log_dt, b_re, b_im):
    g, p = lam_re.shape
    hh = b_re.shape[-1]
    bt_re = jnp.transpose(b_re, (0, 2, 1))
    bt_im = jnp.transpose(b_im, (0, 2, 1))
    outs = pl.pallas_call(
        _ssm_prep_kernel,
        out_shape=[jax.ShapeDtypeStruct((g, 1, p), F32)] * 2 + [jax.ShapeDtypeStruct((g, hh, p), F32)] * 2,
        name="ssm_prep",
    )(lam_re.reshape(g, 1, p), lam_im.reshape(g, 1, p), log_dt.reshape(g, 1, 1), bt_re, bt_im)
    a_re, a_im, bb_re, bb_im = outs
    return a_re.reshape(g, p), a_im.reshape(g, p), bb_re, bb_im


def _proj_kernel(x_ref, pos_ref, gmix_ref, wu_ref, wcq_ref, wckv_ref, wkr_ref, qg_ref, wq_ref, kvg_ref,
                 wk_ref, wv_ref, invf_ref, u_ref, q_ref, k_ref, v_ref):
    x = x_ref[0]
    h = _rms(x, gmix_ref[...]).astype(BF16)
    u_ref[0] = _dot(h, wu_ref[...]).astype(BF16)
    cq = _dot(h, wcq_ref[...])
    q = _dot(_rms(cq, qg_ref[...]).astype(BF16), wq_ref[...]) * (MLA_QK ** -0.5)
    ckv = _dot(h, wckv_ref[...])
    ckvn = _rms(ckv, kvg_ref[...]).astype(BF16)
    kk = _dot(ckvn, wk_ref[...])
    vv = _dot(ckvn, wv_ref[...])
    kr = _dot(h, wkr_ref[...])

    ang = pos_ref[0].astype(F32) * invf_ref[...]
    cosv = jnp.cos(ang)
    sinv = jnp.sin(ang)
    lane = lax.broadcasted_iota(jnp.int32, (1, LANES), 1)
    half = MLA_ROPE // 2
    c_tab = jnp.where(lane < MLA_NOPE, 1.0, cosv)
    s_lo = jnp.where((lane >= MLA_NOPE) & (lane < MLA_NOPE + half), -sinv, 0.0)
    s_hi = jnp.where((lane >= MLA_NOPE + half) & (lane < MLA_QK), sinv, 0.0)
    ones_col = jnp.where(lane == MLA_V, 1.0, 0.0)

    def rot(val):
        return (val * c_tab + pltpu.roll(val, LANES - half, 1) * s_lo + pltpu.roll(val, half, 1) * s_hi)

    kpe = rot(kr)
    for hh in range(MLA_HEADS):
        sl = slice(hh * LANES, (hh + 1) * LANES)
        q_ref[0, hh] = rot(q[:, sl]).astype(BF16)
        k_ref[0, hh] = (kk[:, sl] + kpe).astype(BF16)
        v_ref[0, hh] = (vv[:, sl] + ones_col).astype(BF16)


def _proj(x, pos, gmix, wu, wcq, wckv, wkr, qg, wq, kvg, wk, wv, invf, tm):
    b, s, d = x.shape
    grid = (b, s // tm)
    hd = MLA_HEADS * LANES
    tok = lambda n: pl.BlockSpec((1, tm, n), lambda i, j: (i, j, 0))
    head = pl.BlockSpec((1, MLA_HEADS, tm, LANES), lambda i, j: (i, 0, j, 0))
    return pl.pallas_call(
        _proj_kernel,
        grid=grid,
        in_specs=[tok(d), tok(1), _full(gmix.shape), _full(wu.shape), _full(wcq.shape), _full(wckv.shape),
                  _full(wkr.shape), _full(qg.shape), _full(wq.shape), _full(kvg.shape), _full(wk.shape),
                  _full(wv.shape), _full(invf.shape)],
        out_specs=[tok(wu.shape[1]), head, head, head],
        out_shape=[jax.ShapeDtypeStruct((b, s, wu.shape[1]), BF16)]
        + [jax.ShapeDtypeStruct((b, MLA_HEADS, s, LANES), BF16)] * 3,
        compiler_params=_params("parallel", "parallel"),
        name="proj",
    )(x, pos, gmix, wu, wcq, wckv, wkr, qg, wq, kvg, wk, wv, invf)


def _ssm_kernel(u_ref, bbd_ref, atab_ref, cmat_ref, dskip_ref, wglu_ref, bglu_ref, gout_ref, y_ref,
                xs_ref, st_ref, *, tt, slab, nb, lane_tiles):
    j = pl.program_id(0)
    half_w = u_ref.shape[-1] // 2
    n_state = lane_tiles * LANES

    @pl.when(j == 0)
    def _():
        st_ref[...] = jnp.zeros_like(st_ref)

    for hf in range(2):
        ub = u_ref[:, :, hf * half_w:(hf + 1) * half_w].reshape(nb * tt, half_w)
        bu = _dot(ub, bbd_ref[hf])
        for b in range(nb):
            r0 = (hf * nb + b) * slab
            for c in range(2 * lane_tiles):
                xs_ref[c, r0:r0 + tt, :] = bu[b * tt:(b + 1) * tt, c * LANES:(c + 1) * LANES]

    rows = 2 * nb
    group = 4
    for c0 in range(0, lane_tiles, group):
        cs = list(range(c0, c0 + group))
        a_re = [atab_ref[0, :, c * LANES:(c + 1) * LANES] for c in cs]
        a_im = [atab_ref[1, :, c * LANES:(c + 1) * LANES] for c in cs]
        init = tuple(st_ref[0, :, c * LANES:(c + 1) * LANES] for c in cs) + \
            tuple(st_ref[1, :, c * LANES:(c + 1) * LANES] for c in cs)

        def step(t, carry, cs=cs, a_re=a_re, a_im=a_im):
            new_re, new_im = [], []
            for i, c in enumerate(cs):
                x_re, x_im = carry[i], carry[group + i]
                idx = pl.ds(t, rows, stride=slab)
                n_re = a_re[i] * x_re - a_im[i] * x_im + xs_ref[c, idx, :]
                n_im = a_re[i] * x_im + a_im[i] * x_re + xs_ref[lane_tiles + c, idx, :]
                xs_ref[c, idx, :] = n_re
                xs_ref[lane_tiles + c, idx, :] = n_im
                new_re.append(n_re)
                new_im.append(n_im)
            return tuple(new_re) + tuple(new_im)

        fin = lax.fori_loop(0, tt, step, init, unroll=4)
        for i, c in enumerate(cs):
            st_ref[0, :, c * LANES:(c + 1) * LANES] = fin[i]
            st_ref[1, :, c * LANES:(c + 1) * LANES] = fin[group + i]

    ys = []
    for hf in range(2):
        xb = []
        for b in range(nb):
            r0 = (hf * nb + b) * slab
            xb.append(jnp.concatenate([xs_ref[c, r0:r0 + tt, :] for c in range(2 * lane_tiles)], axis=1))
        xh = jnp.concatenate(xb, axis=0).astype(BF16)
        ys.append(_dot(xh, cmat_ref[hf]))
    y = jnp.concatenate(ys, axis=1)
    u = u_ref[...].astype(F32).reshape(nb * tt, 2 * half_w)
    y = jax.nn.gelu(y + dskip_ref[...] * u)
    y = y * jax.nn.sigmoid(_dot(y.astype(BF16), wglu_ref[...]) + bglu_ref[...])
    y = _rms(y, gout_ref[...])
    y_ref[...] = y.reshape(nb, tt, 2 * half_w).astype(BF16)


def _ssm(u, bbd, atab, cmat, dskip, wglu, bglu, gout, tt):
    nb, s, dssm = u.shape
    lane_tiles = bbd.shape[-1] // (2 * LANES)
    slab = tt + 8
    kern = functools.partial(_ssm_kernel, tt=tt, slab=slab, nb=nb, lane_tiles=lane_tiles)
    return pl.pallas_call(
        kern,
        grid=(s // tt,),
        in_specs=[pl.BlockSpec((nb, tt, dssm), lambda j: (0, j, 0)), _full(bbd.shape), _full(atab.shape),
                  _full(cmat.shape), _full(dskip.shape), _full(wglu.shape), _full(bglu.shape), _full(gout.shape)],
        out_specs=pl.BlockSpec((nb, tt, dssm), lambda j: (0, j, 0)),
        out_shape=jax.ShapeDtypeStruct((nb, s, dssm), BF16),
        scratch_shapes=[pltpu.VMEM((2 * lane_tiles, 2 * nb * slab, LANES), F32),
                        pltpu.VMEM((2, 2 * nb, lane_tiles * LANES), F32)],
        compiler_params=_params("arbitrary"),
        name="ssm",
    )(u, bbd, atab, cmat, dskip, wglu, bglu, gout)


def _attn_kernel(q_ref, k_ref, v_ref, o_ref, *, tq):
    qi = pl.program_id(2)
    q = q_ref[0, 0]

    def tile(j, m, acc, masked):
        k = k_ref[0, 0, pl.ds(j * tq, tq), :]
        v = v_ref[0, 0, pl.ds(j * tq, tq), :]
        s = _dot_nt(q, k)
        if masked:
            row = lax.broadcasted_iota(jnp.int32, (tq, tq), 0) // CHUNK
            col = lax.broadcasted_iota(jnp.int32, (tq, tq), 1) // CHUNK
            s = jnp.where(col <= row, s, NEG_INF)
        m_new = jnp.maximum(m, jnp.max(s, axis=-1, keepdims=True))
        p = jnp.exp(s - m_new)
        acc = acc * jnp.exp(m - m_new) + _dot(p.astype(BF16), v)
        return m_new, acc

    m0 = jnp.full((tq, 1), NEG_INF, F32)
    acc0 = jnp.zeros((tq, LANES), F32)
    m, acc = lax.fori_loop(0, qi, lambda j, c: tile(j, c[0], c[1], False), (m0, acc0))
    m, acc = tile(qi, m, acc, True)
    lane = lax.broadcasted_iota(jnp.int32, (1, LANES), 1)
    out = acc * (1.0 / acc[:, MLA_V:MLA_V + 1])
    o_ref[0, 0] = jnp.where(lane < MLA_V, out, 0.0).astype(BF16)


def _attn(q, k, v, tq):
    b, h, s, _ = q.shape
    qspec = pl.BlockSpec((1, 1, tq, LANES), lambda i, j, t: (i, j, t, 0))
    kvspec = pl.BlockSpec((1, 1, s, LANES), lambda i, j, t: (i, j, 0, 0))
    return pl.pallas_call(
        functools.partial(_attn_kernel, tq=tq),
        grid=(b, h, s // tq),
        in_specs=[qspec, kvspec, kvspec],
        out_specs=qspec,
        out_shape=jax.ShapeDtypeStruct((b, h, s, LANES), BF16),
        compiler_params=_params("parallel", "parallel", "arbitrary"),
        name="attn",
    )(q, k, v)


def _memkv_kernel(mem_ref, g_ref, wk_ref, wv_ref, k_ref, v_ref, *, hd):
    mn = _rms(mem_ref[0], g_ref[...]).astype(BF16)
    kk = _dot(mn, wk_ref[...])
    vv = _dot(mn, wv_ref[...])
    for hh in range(XATTN_HEADS):
        k_ref[0, hh] = kk[:, hh * hd:(hh + 1) * hd].astype(BF16)
        v_ref[0, hh] = vv[:, hh * hd:(hh + 1) * hd].astype(BF16)


def _memkv(mem, g, wk, wv):
    b, nm, d = mem.shape
    hd = d // XATTN_HEADS
    ospec = pl.BlockSpec((1, XATTN_HEADS, nm, hd), lambda i: (i, 0, 0, 0))
    return pl.pallas_call(
        functools.partial(_memkv_kernel, hd=hd),
        grid=(b,),
        in_specs=[pl.BlockSpec((1, nm, d), lambda i: (i, 0, 0)), _full(g.shape), _full(wk.shape), _full(wv.shape)],
        out_specs=[ospec, ospec],
        out_shape=[jax.ShapeDtypeStruct((b, XATTN_HEADS, nm, hd), BF16)] * 2,
        compiler_params=_params("parallel"),
        name="mem_kv",
    )(mem, g, wk, wv)


def _post_kernel(x_ref, ys_ref, ym_ref, gm_ref, wos_ref, wom_ref, gx_ref, wq_ref, km_ref, vm_ref, wo_ref,
                 gmoe_ref, wr_ref, br_ref, x2_ref, h3_ref, gate_ref, *, hd):
    x = x_ref[0]
    ym = jnp.concatenate([ym_ref[0, hh] for hh in range(MLA_HEADS)], axis=1).astype(F32)
    ymn = _rms(ym, gm_ref[...], n=MLA_HEADS * MLA_V).astype(BF16)
    x1 = x + _dot(ys_ref[0], wos_ref[...]) + _dot(ymn, wom_ref[...])

    h2 = _rms(x1, gx_ref[...]).astype(BF16)
    qx = (_dot(h2, wq_ref[...]) * (hd ** -0.5)).astype(BF16)
    outs = []
    for hh in range(XATTN_HEADS):
        s = _dot_nt(qx[:, hh * hd:(hh + 1) * hd], km_ref[0, hh])
        p = jnp.exp(s - jnp.max(s, axis=-1, keepdims=True))
        p = p * (1.0 / jnp.sum(p, axis=-1, keepdims=True))
        outs.append(_dot(p.astype(BF16), vm_ref[0, hh]))
    o = jnp.concatenate(outs, axis=1).astype(BF16)
    x2 = x1 + _dot(o, wo_ref[...])
    x2_ref[0] = x2

    h3 = _rms(x2, gmoe_ref[...])
    h3_ref[0] = h3.astype(BF16)

    logits = jnp.dot(h3, wr_ref[...], preferred_element_type=F32, precision=lax.Precision.HIGHEST) + br_ref[...]
    lane = lax.broadcasted_iota(jnp.int32, logits.shape, 1)
    is_g = (lane >= MOE_EXPERTS) & (lane < MOE_EXPERTS + MOE_GROUPS)
    gl = jnp.where(is_g, logits, NEG_INF)
    gmax = jnp.max(gl, axis=-1, keepdims=True)
    g_w = 1.0 / jnp.sum(jnp.exp(gl - gmax), axis=-1, keepdims=True)
    g_idx = jnp.min(jnp.where(gl == gmax, lane, 4 * LANES), axis=-1, keepdims=True) - MOE_EXPERTS
    in_grp = (lane >= g_idx * MOE_PER_GROUP) & (lane < (g_idx + 1) * MOE_PER_GROUP)
    el = jnp.where(in_grp, logits, NEG_INF)
    v1 = jnp.max(el, axis=-1, keepdims=True)
    i1 = jnp.min(jnp.where(el == v1, lane, 4 * LANES), axis=-1, keepdims=True)
    el2 = jnp.where(lane == i1, NEG_INF, el)
    v2 = jnp.max(el2, axis=-1, keepdims=True)
    i2 = jnp.min(jnp.where(el2 == v2, lane, 4 * LANES), axis=-1, keepdims=True)
    e2 = jnp.exp(v2 - v1)
    w1 = g_w / (1.0 + e2)
    w2 = g_w * e2 / (1.0 + e2)
    gate_ref[0] = jnp.where(lane == i1, w1, 0.0) + jnp.where(lane == i2, w2, 0.0)


def _post(x, ys, ym, gm, wos, wom, gx, wq, km, vm, wo, gmoe, wr, br, tm):
    b, s, d = x.shape
    hd = d // XATTN_HEADS
    tok = lambda n: pl.BlockSpec((1, tm, n), lambda i, j: (i, j, 0))
    mem = pl.BlockSpec((1,) + km.shape[1:], lambda i, j: (i, 0, 0, 0))
    return pl.pallas_call(
        functools.partial(_post_kernel, hd=hd),
        grid=(b, s // tm),
        in_specs=[tok(d), tok(ys.shape[-1]), pl.BlockSpec((1, MLA_HEADS, tm, LANES), lambda i, j: (i, 0, j, 0)),
                  _full(gm.shape), _full(wos.shape), _full(wom.shape), _full(gx.shape), _full(wq.shape), mem, mem,
                  _full(wo.shape), _full(gmoe.shape), _full(wr.shape), _full(br.shape)],
        out_specs=[tok(d), tok(d), tok(LANES)],
        out_shape=[jax.ShapeDtypeStruct((b, s, d), F32), jax.ShapeDtypeStruct((b, s, d), BF16),
                   jax.ShapeDtypeStruct((b, s, LANES), F32)],
        compiler_params=_params("parallel", "parallel"),
        name="post",
    )(x, ys, ym, gm, wos, wom, gx, wq, km, vm, wo, gmoe, wr, br)


def _moe_kernel(h_ref, gate_ref, x2_ref, wg_ref, wu_ref, wd_ref, gf_ref, o_ref, acc_ref):
    e = pl.program_id(1)

    @pl.when(e == 0)
    def _():
        acc_ref[...] = jnp.zeros_like(acc_ref)

    h = h_ref[...]
    a = jax.nn.silu(_dot(h, wg_ref[0])) * _dot(h, wu_ref[0])
    lane = lax.broadcasted_iota(jnp.int32, gate_ref.shape, 1)
    gcol = jnp.sum(jnp.where(lane == e, gate_ref[...], 0.0), axis=-1, keepdims=True)
    acc_ref[...] += _dot((a * gcol).astype(BF16), wd_ref[0])

    @pl.when(e == pl.num_programs(1) - 1)
    def _():
        o_ref[...] = _rms(x2_ref[...] + acc_ref[...], gf_ref[...])


def _moe(h3, gates, x2, wg, wu, wd, gf, tm):
    t, d = h3.shape
    ne, _, ff = wg.shape
    tok = lambda n: pl.BlockSpec((tm, n), lambda i, e: (i, 0))
    return pl.pallas_call(
        _moe_kernel,
        grid=(t // tm, ne),
        in_specs=[tok(d), tok(LANES), tok(d),
                  pl.BlockSpec((1, d, ff), lambda i, e: (e, 0, 0)), pl.BlockSpec((1, d, ff), lambda i, e: (e, 0, 0)),
                  pl.BlockSpec((1, ff, d), lambda i, e: (e, 0, 0)), _full(gf.shape)],
        out_specs=tok(d),
        out_shape=jax.ShapeDtypeStruct((t, d), F32),
        scratch_shapes=[pltpu.VMEM((tm, d), F32)],
        compiler_params=_params("parallel", "arbitrary"),
        name="moe",
    )(h3, gates, x2, wg, wu, wd, gf)


def _pad_heads(w, per_head, offset=0):
    k = w.shape[0]
    w = w.reshape(k, MLA_HEADS, per_head)
    w = jnp.pad(w, ((0, 0), (0, 0), (offset, LANES - per_head - offset)))
    return w.reshape(k, MLA_HEADS * LANES)


def _block_diag(blocks):
    n, r, c = blocks.shape
    eye = jnp.eye(n, dtype=blocks.dtype)
    return (eye[:, None, :, None] * blocks[:, :, None, :]).reshape(n * r, n * c)


def kernel(x, mem, positions, norm_mix_g, w_in, ssm_lam_re, ssm_lam_im, ssm_log_dt, ssm_b_re, ssm_b_im, ssm_c_re, ssm_c_im, ssm_d, ssm_w_glu, ssm_b_glu, mla_q_norm_g, mla_w_q_up, mla_kv_norm_g, mla_w_kv_up, out_norm_ssm_g, out_norm_mla_g, w_out, norm_xattn_g, norm_mem_g, xattn_w_q, xattn_w_k, xattn_w_v, xattn_w_o, norm_moe_g, moe_w_group, moe_b_group, moe_w_expert, moe_b_expert, moe_w_gate, moe_w_up, moe_w_down, norm_final_g):
    bsz, seq, d = x.shape
    depth = w_in.shape[0]
    d_ssm = ssm_d.shape[-1]
    q_rank = mla_q_norm_g.shape[-1]
    kv_rank = mla_kv_norm_g.shape[-1]
    n_grp = d_ssm // SSM_GROUP
    s1, s2, s3 = d_ssm, d_ssm + q_rank, d_ssm + q_rank + kv_rank
    row = lambda v: v.reshape(1, -1).astype(F32)

    half = MLA_ROPE // 2
    inv_freq = ROPE_THETA ** (-jnp.arange(half, dtype=F32) / half)
    invf = jnp.zeros((LANES,), F32).at[MLA_NOPE:MLA_NOPE + half].set(inv_freq)
    invf = invf.at[MLA_NOPE + half:MLA_QK].set(inv_freq).reshape(1, LANES)
    pos = positions.reshape(bsz, seq, 1)

    for l in range(depth):
        a_re, a_im, bb_re, bb_im = _ssm_prep(ssm_lam_re[l], ssm_lam_im[l], ssm_log_dt[l], ssm_b_re[l], ssm_b_im[l])
        gh = n_grp // 2
        bbd = jnp.stack([
            jnp.concatenate([_block_diag(bb_re[hf * gh:(hf + 1) * gh]), _block_diag(bb_im[hf * gh:(hf + 1) * gh])],
                            axis=1) for hf in range(2)]).astype(BF16)
        c_re_t = jnp.transpose(ssm_c_re[l], (0, 2, 1))
        c_im_t = jnp.transpose(ssm_c_im[l], (0, 2, 1))
        cmat = jnp.stack([
            jnp.concatenate([_block_diag(c_re_t[hf * gh:(hf + 1) * gh]), -_block_diag(c_im_t[hf * gh:(hf + 1) * gh])],
                            axis=0) for hf in range(2)]).astype(BF16)
        atab = jnp.stack([
            jnp.repeat(arr.reshape(2, 1, gh * SSM_STATE), bsz, axis=1).reshape(2 * bsz, gh * SSM_STATE)
            for arr in (a_re, a_im)])

        wi = w_in[l]
        wu = wi[:, :s1].astype(BF16)
        wcq = wi[:, s1:s2].astype(BF16)
        wckv = wi[:, s2:s3].astype(BF16)
        wkr = jnp.pad(wi[:, s3:], ((0, 0), (MLA_NOPE, LANES - MLA_QK))).astype(BF16)
        wq = _pad_heads(mla_w_q_up[l], MLA_QK).astype(BF16)
        wkv = mla_w_kv_up[l].reshape(kv_rank, MLA_HEADS, MLA_NOPE + MLA_V)
        wk = _pad_heads(wkv[:, :, :MLA_NOPE].reshape(kv_rank, -1), MLA_NOPE).astype(BF16)
        wv = _pad_heads(wkv[:, :, MLA_NOPE:].reshape(kv_rank, -1), MLA_V).astype(BF16)
        u, q, k, v = _proj(x, pos, row(norm_mix_g[l]), wu, wcq, wckv, wkr, row(mla_q_norm_g[l]), wq,
                           row(mla_kv_norm_g[l]), wk, wv, invf, tm=min(512, seq))

        y_ssm = _ssm(u, bbd, atab, cmat, row(ssm_d[l]), ssm_w_glu[l].astype(BF16), row(ssm_b_glu[l]),
                     row(out_norm_ssm_g[l]), tt=min(128, seq))
        y_mla = _attn(q, k, v, tq=min(256, seq))

        km, vm = _memkv(mem, row(norm_mem_g[l]), xattn_w_k[l].astype(BF16), xattn_w_v[l].astype(BF16))
        gm = jnp.pad(out_norm_mla_g[l].reshape(MLA_HEADS, MLA_V), ((0, 0), (0, LANES - MLA_V))).reshape(1, -1)
        wos = w_out[l][:d_ssm].astype(BF16)
        wom = jnp.pad(w_out[l][d_ssm:].reshape(MLA_HEADS, MLA_V, d), ((0, 0), (0, LANES - MLA_V), (0, 0)))
        wom = wom.reshape(MLA_HEADS * LANES, d).astype(BF16)
        wr = jnp.concatenate([moe_w_expert[l].reshape(d, MOE_EXPERTS), moe_w_group[l]], axis=1)
        wr = jnp.pad(wr, ((0, 0), (0, LANES - MOE_EXPERTS - MOE_GROUPS))).astype(F32)
        br = jnp.concatenate([moe_b_expert[l].reshape(-1), moe_b_group[l]])
        br = jnp.pad(br, (0, LANES - MOE_EXPERTS - MOE_GROUPS)).reshape(1, LANES).astype(F32)
        x2, h3, gates = _post(x, y_ssm, y_mla, gm.astype(F32), wos, wom, row(norm_xattn_g[l]),
                              xattn_w_q[l].astype(BF16), km, vm, xattn_w_o[l].astype(BF16), row(norm_moe_g[l]),
                              wr, br, tm=min(512, seq))

        gf = row(norm_final_g) if l == depth - 1 else None
        out = _moe(h3.reshape(bsz * seq, d), gates.reshape(bsz * seq, LANES), x2.reshape(bsz * seq, d),
                   moe_w_gate[l].astype(BF16), moe_w_up[l].astype(BF16), moe_w_down[l].astype(BF16), gf,
                   tm=min(1024, bsz * seq))
        x = out.reshape(bsz, seq, d)
    return x
```

```python
import functools
import math

import jax
import jax.numpy as jnp
from jax import lax
from jax.experimental import pallas as pl
from jax.experimental.pallas import tpu as pltpu

F32 = jnp.float32
BF16 = jnp.bfloat16

EPS = 1e-6
NEG_INF = -1e30
CHUNK = 64

LANES = 128
SSM_GROUP = 16
SSM_STATE = 64
MLA_HEADS = 8
MLA_NOPE = 64
MLA_ROPE = 32
MLA_QK = MLA_NOPE + MLA_ROPE
MLA_V = 64
ROPE_THETA = 10000.0
XATTN_HEADS = 4
MOE_GROUPS = 4
MOE_PER_GROUP = 8
MOE_EXPERTS = MOE_GROUPS * MOE_PER_GROUP
VMEM_LIMIT = 48 * 1024 * 1024
LOG2E = math.log2(math.e)


def _dot(a, b):
    return jnp.dot(a, b, preferred_element_type=F32)


def _dot_nt(a, b):
    return lax.dot_general(a, b, (((1,), (1,)), ((), ())), preferred_element_type=F32)


def _rms(x, g, n=None):
    n = x.shape[-1] if n is None else n
    ms = jnp.sum(x * x, axis=-1, keepdims=True) * (1.0 / n)
    return x * lax.rsqrt(ms + EPS) * g


def _params(*sem):
    return pltpu.CompilerParams(dimension_semantics=sem, vmem_limit_bytes=VMEM_LIMIT)


def _full(shape):
    zeros = (0,) * len(shape)
    return pl.BlockSpec(shape, lambda *_: zeros)


def _ssm_prep_kernel(lre_ref, lim_ref, ldt_ref, bre_ref, bim_ref, are_ref, aim_ref, bbre_ref, bbim_ref):
    lre = jnp.minimum(lre_ref[...], -1e-4)
    lim = lim_ref[...]
    dt = jnp.exp(ldt_ref[...])
    mag = jnp.exp(lre * dt)
    are = mag * jnp.cos(lim * dt)
    aim = mag * jnp.sin(lim * dt)
    are_ref[...] = are
    aim_ref[...] = aim
    nre = are - 1.0
    den = lre * lre + lim * lim
    fre = (nre * lre + aim * lim) / den
    fim = (aim * lre - nre * lim) / den
    bre = bre_ref[...]
    bim = bim_ref[...]
    bbre_ref[...] = fre * bre - fim * bim
    bbim_ref[...] = fre * bim + fim * bre


def _ssm_prep(lam_re, lam_im, log_dt, b_re, b_im):
    g, p = lam_re.shape
    hh = b_re.shape[-1]
    bt_re = jnp.transpose(b_re, (0, 2, 1))
    bt_im = jnp.transpose(b_im, (0, 2, 1))
    outs = pl.pallas_call(
        _ssm_prep_kernel,
        out_shape=[jax.ShapeDtypeStruct((g, 1, p), F32)] * 2 + [jax.ShapeDtypeStruct((g, hh, p), F32)] * 2,
        name="ssm_prep",
    )(lam_re.reshape(g, 1, p), lam_im.reshape(g, 1, p), log_dt.reshape(g, 1, 1), bt_re, bt_im)
    a_re, a_im, bb_re, bb_im = outs
    return a_re.reshape(g, p), a_im.reshape(g, p), bb_re, bb_im


def _proj_kernel(x_ref, pos_ref, gmix_ref, wu_ref, wcq_ref, wckv_ref, wkr_ref, qg_ref, wq_ref, kvg_ref,
                 wk_ref, wv_ref, invf_ref, u_ref, q_ref, k_ref, v_ref):
    x = x_ref[0]
    h = _rms(x, gmix_ref[...]).astype(BF16)
    u_ref[0] = _dot(h, wu_ref[...]).astype(BF16)
    cq = _dot(h, wcq_ref[...])
    q = _dot(_rms(cq, qg_ref[...]).astype(BF16), wq_ref[...]) * (MLA_QK ** -0.5 * LOG2E)
    ckv = _dot(h, wckv_ref[...])
    ckvn = _rms(ckv, kvg_ref[...]).astype(BF16)
    kk = _dot(ckvn, wk_ref[...])
    vv = _dot(ckvn, wv_ref[...])
    kr = _dot(h, wkr_ref[...])

    ang = pos_ref[0].astype(F32) * invf_ref[...]
    cosv = jnp.cos(ang)
    sinv = jnp.sin(ang)
    lane = lax.broadcasted_iota(jnp.int32, (1, LANES), 1)
    half = MLA_ROPE // 2
    c_tab = jnp.where(lane < MLA_NOPE, 1.0, cosv)
    s_lo = jnp.where((lane >= MLA_NOPE) & (lane < MLA_NOPE + half), -sinv, 0.0)
    s_hi = jnp.where((lane >= MLA_NOPE + half) & (lane < MLA_QK), sinv, 0.0)
    ones_col = jnp.where(lane == MLA_V, 1.0, 0.0)

    def rot(val):
        return (val * c_tab + pltpu.roll(val, LANES - half, 1) * s_lo + pltpu.roll(val, half, 1) * s_hi)

    kpe = rot(kr)
    for hh in range(MLA_HEADS):
        sl = slice(hh * LANES, (hh + 1) * LANES)
        q_ref[0, hh] = rot(q[:, sl]).T.astype(BF16)
        k_ref[0, hh] = (kk[:, sl] + kpe).astype(BF16)
        v_ref[0, hh] = (vv[:, sl] + ones_col).T.astype(BF16)


def _proj(x, pos, gmix, wu, wcq, wckv, wkr, qg, wq, kvg, wk, wv, invf, tm):
    b, s, d = x.shape
    grid = (b, s // tm)
    tok = lambda n: pl.BlockSpec((1, tm, n), lambda i, j: (i, j, 0))
    head = pl.BlockSpec((1, MLA_HEADS, tm, LANES), lambda i, j: (i, 0, j, 0))
    head_t = pl.BlockSpec((1, MLA_HEADS, LANES, tm), lambda i, j: (i, 0, 0, j))
    tshape = jax.ShapeDtypeStruct((b, MLA_HEADS, LANES, s), BF16)
    return pl.pallas_call(
        _proj_kernel,
        grid=grid,
        in_specs=[tok(d), tok(1), _full(gmix.shape), _full(wu.shape), _full(wcq.shape), _full(wckv.shape),
                  _full(wkr.shape), _full(qg.shape), _full(wq.shape), _full(kvg.shape), _full(wk.shape),
                  _full(wv.shape), _full(invf.shape)],
        out_specs=[tok(wu.shape[1]), head_t, head, head_t],
        out_shape=[jax.ShapeDtypeStruct((b, s, wu.shape[1]), BF16), tshape,
                   jax.ShapeDtypeStruct((b, MLA_HEADS, s, LANES), BF16), tshape],
        compiler_params=_params("parallel", "parallel"),
        name="proj",
    )(x, pos, gmix, wu, wcq, wckv, wkr, qg, wq, kvg, wk, wv, invf)


def _ssm_kernel(u_ref, bbd_ref, atab_ref, cmat_ref, dskip_ref, wglu_ref, bglu_ref, gout_ref, y_ref,
                xs_ref, st_ref, *, tt, slab, nb, lane_tiles):
    j = pl.program_id(0)
    half_w = u_ref.shape[-1] // 2
    n_state = lane_tiles * LANES

    @pl.when(j == 0)
    def _():
        st_ref[...] = jnp.zeros_like(st_ref)

    for hf in range(2):
        ub = u_ref[:, :, hf * half_w:(hf + 1) * half_w].reshape(nb * tt, half_w)
        bu = _dot(ub, bbd_ref[hf])
        for b in range(nb):
            r0 = (hf * nb + b) * slab
            for c in range(2 * lane_tiles):
                xs_ref[c, r0:r0 + tt, :] = bu[b * tt:(b + 1) * tt, c * LANES:(c + 1) * LANES]

    rows = 2 * nb
    group = 4
    for c0 in range(0, lane_tiles, group):
        cs = list(range(c0, c0 + group))
        a_re = [atab_ref[0, :, c * LANES:(c + 1) * LANES] for c in cs]
        a_im = [atab_ref[1, :, c * LANES:(c + 1) * LANES] for c in cs]
        init = tuple(st_ref[0, :, c * LANES:(c + 1) * LANES] for c in cs) + \
            tuple(st_ref[1, :, c * LANES:(c + 1) * LANES] for c in cs)

        def step(t, carry, cs=cs, a_re=a_re, a_im=a_im):
            new_re, new_im = [], []
            for i, c in enumerate(cs):
                x_re, x_im = carry[i], carry[group + i]
                idx = pl.ds(t, rows, stride=slab)
                n_re = a_re[i] * x_re - a_im[i] * x_im + xs_ref[c, idx, :]
                n_im = a_re[i] * x_im + a_im[i] * x_re + xs_ref[lane_tiles + c, idx, :]
                xs_ref[c, idx, :] = n_re
                xs_ref[lane_tiles + c, idx, :] = n_im
                new_re.append(n_re)
                new_im.append(n_im)
            return tuple(new_re) + tuple(new_im)

        fin = lax.fori_loop(0, tt, step, init, unroll=4)
        for i, c in enumerate(cs):
            st_ref[0, :, c * LANES:(c + 1) * LANES] = fin[i]
            st_ref[1, :, c * LANES:(c + 1) * LANES] = fin[group + i]

    ys = []
    for hf in range(2):
        xb = []
        for b in range(nb):
            r0 = (hf * nb + b) * slab
            xb.append(jnp.concatenate([xs_ref[c, r0:r0 + tt, :] for c in range(2 * lane_tiles)], axis=1))
        xh = jnp.concatenate(xb, axis=0).astype(BF16)
        ys.append(_dot(xh, cmat_ref[hf]))
    y = jnp.concatenate(ys, axis=1)
    u = u_ref[...].astype(F32).reshape(nb * tt, 2 * half_w)
    y = jax.nn.gelu(y + dskip_ref[...] * u)
    y = y * jax.nn.sigmoid(_dot(y.astype(BF16), wglu_ref[...]) + bglu_ref[...])
    y = _rms(y, gout_ref[...])
    y_ref[...] = y.reshape(nb, tt, 2 * half_w).astype(BF16)


def _ssm(u, bbd, atab, cmat, dskip, wglu, bglu, gout, tt):
    nb, s, dssm = u.shape
    lane_tiles = bbd.shape[-1] // (2 * LANES)
    slab = tt + 8
    kern = functools.partial(_ssm_kernel, tt=tt, slab=slab, nb=nb, lane_tiles=lane_tiles)
    return pl.pallas_call(
        kern,
        grid=(s // tt,),
        in_specs=[pl.BlockSpec((nb, tt, dssm), lambda j: (0, j, 0)), _full(bbd.shape), _full(atab.shape),
                  _full(cmat.shape), _full(dskip.shape), _full(wglu.shape), _full(bglu.shape), _full(gout.shape)],
        out_specs=pl.BlockSpec((nb, tt, dssm), lambda j: (0, j, 0)),
        out_shape=jax.ShapeDtypeStruct((nb, s, dssm), BF16),
        scratch_shapes=[pltpu.VMEM((2 * lane_tiles, 2 * nb * slab, LANES), F32),
                        pltpu.VMEM((2, 2 * nb, lane_tiles * LANES), F32)],
        compiler_params=_params("arbitrary"),
        name="ssm",
    )(u, bbd, atab, cmat, dskip, wglu, bglu, gout)


def _attn_kernel(qt_ref, k_ref, vt_ref, o_ref, s_ref, p_ref, m_ref, a_ref, acc_ref, *, tq, hp, strip):
    qi = pl.program_id(2)
    m_ref[...] = jnp.full(m_ref.shape, NEG_INF, F32)
    acc_ref[...] = jnp.zeros(acc_ref.shape, F32)
    q_chunk = lax.broadcasted_iota(jnp.int32, (1, tq), 1) // CHUNK

    def tile(j, masked):
        for hh in range(hp):
            s_ref[hh] = _dot(k_ref[0, hh, pl.ds(j * tq, tq), :], qt_ref[0, hh])

        def strip_of(hh, r):
            s = s_ref[hh, r:r + strip, :]
            return jnp.where(q_chunk >= r // CHUNK, s, NEG_INF) if masked else s

        for hh in range(hp):
            mt = strip_of(hh, 0)
            for r in range(strip, tq, strip):
                mt = jnp.maximum(mt, strip_of(hh, r))
            m_old = m_ref[hh]
            m_new = jnp.maximum(m_old, jnp.max(mt, axis=0, keepdims=True))
            a_ref[hh] = jnp.exp2(m_old - m_new)
            m_ref[hh] = m_new
            for r in range(0, tq, strip):
                p_ref[hh, r:r + strip, :] = jnp.exp2(strip_of(hh, r) - m_new).astype(BF16)
        for hh in range(hp):
            pv = _dot(vt_ref[0, hh, :, pl.ds(j * tq, tq)], p_ref[hh])
            acc_ref[hh] = acc_ref[hh] * a_ref[hh] + pv

    def body(j, c):
        tile(j, False)
        return c

    lax.fori_loop(0, qi, body, 0)
    tile(qi, True)
    feat = lax.broadcasted_iota(jnp.int32, (LANES, 1), 0)
    for hh in range(hp):
        acc = acc_ref[hh]
        out = acc * (1.0 / acc[MLA_V:MLA_V + 1, :])
        o_ref[0, hh] = jnp.where(feat < MLA_V, out, 0.0).T.astype(BF16)


def _attn(qt, k, vt, tq, hp, strip):
    b, h, s, _ = k.shape
    assert strip <= CHUNK and CHUNK % strip == 0 and tq % CHUNK == 0
    return pl.pallas_call(
        functools.partial(_attn_kernel, tq=tq, hp=hp, strip=strip),
        grid=(b, h // hp, s // tq),
        in_specs=[pl.BlockSpec((1, hp, LANES, tq), lambda i, j, t: (i, j, 0, t)),
                  pl.BlockSpec((1, hp, s, LANES), lambda i, j, t: (i, j, 0, 0)),
                  pl.BlockSpec((1, hp, LANES, s), lambda i, j, t: (i, j, 0, 0))],
        out_specs=pl.BlockSpec((1, hp, tq, LANES), lambda i, j, t: (i, j, t, 0)),
        out_shape=jax.ShapeDtypeStruct((b, h, s, LANES), BF16),
        scratch_shapes=[pltpu.VMEM((hp, tq, tq), F32), pltpu.VMEM((hp, tq, tq), BF16),
                        pltpu.VMEM((hp, 1, tq), F32), pltpu.VMEM((hp, 1, tq), F32),
                        pltpu.VMEM((hp, LANES, tq), F32)],
        compiler_params=_params("parallel", "parallel", "arbitrary"),
        name="attn",
    )(qt, k, vt)


def _memkv_kernel(mem_ref, g_ref, wk_ref, wv_ref, k_ref, v_ref, *, hd):
    mn = _rms(mem_ref[0], g_ref[...]).astype(BF16)
    kk = _dot(mn, wk_ref[...])
    vv = _dot(mn, wv_ref[...])
    for hh in range(XATTN_HEADS):
        k_ref[0, hh] = kk[:, hh * hd:(hh + 1) * hd].astype(BF16)
        v_ref[0, hh] = vv[:, hh * hd:(hh + 1) * hd].astype(BF16)


def _memkv(mem, g, wk, wv):
    b, nm, d = mem.shape
    hd = d // XATTN_HEADS
    ospec = pl.BlockSpec((1, XATTN_HEADS, nm, hd), lambda i: (i, 0, 0, 0))
    return pl.pallas_call(
        functools.partial(_memkv_kernel, hd=hd),
        grid=(b,),
        in_specs=[pl.BlockSpec((1, nm, d), lambda i: (i, 0, 0)), _full(g.shape), _full(wk.shape), _full(wv.shape)],
        out_specs=[ospec, ospec],
        out_shape=[jax.ShapeDtypeStruct((b, XATTN_HEADS, nm, hd), BF16)] * 2,
        compiler_params=_params("parallel"),
        name="mem_kv",
    )(mem, g, wk, wv)


def _post_kernel(x_ref, ys_ref, ym_ref, gm_ref, wos_ref, wom_ref, gx_ref, wq_ref, km_ref, vm_ref, wo_ref,
                 gmoe_ref, wr_ref, br_ref, x2_ref, h3_ref, gate_ref, *, hd):
    x = x_ref[0]
    ym = jnp.concatenate([ym_ref[0, hh] for hh in range(MLA_HEADS)], axis=1).astype(F32)
    ymn = _rms(ym, gm_ref[...], n=MLA_HEADS * MLA_V).astype(BF16)
    x1 = x + _dot(ys_ref[0], wos_ref[...]) + _dot(ymn, wom_ref[...])

    h2 = _rms(x1, gx_ref[...]).astype(BF16)
    qx = (_dot(h2, wq_ref[...]) * (hd ** -0.5)).astype(BF16)
    outs = []
    for hh in range(XATTN_HEADS):
        s = _dot_nt(qx[:, hh * hd:(hh + 1) * hd], km_ref[0, hh])
        p = jnp.exp(s - jnp.max(s, axis=-1, keepdims=True))
        p = p * (1.0 / jnp.sum(p, axis=-1, keepdims=True))
        outs.append(_dot(p.astype(BF16), vm_ref[0, hh]))
    o = jnp.concatenate(outs, axis=1).astype(BF16)
    x2 = x1 + _dot(o, wo_ref[...])
    x2_ref[0] = x2

    h3 = _rms(x2, gmoe_ref[...])
    h3_ref[0] = h3.astype(BF16)

    logits = jnp.dot(h3, wr_ref[...], preferred_element_type=F32, precision=lax.Precision.HIGHEST) + br_ref[...]
    lane = lax.broadcasted_iota(jnp.int32, logits.shape, 1)
    is_g = (lane >= MOE_EXPERTS) & (lane < MOE_EXPERTS + MOE_GROUPS)
    gl = jnp.where(is_g, logits, NEG_INF)
    gmax = jnp.max(gl, axis=-1, keepdims=True)
    g_w = 1.0 / jnp.sum(jnp.exp(gl - gmax), axis=-1, keepdims=True)
    g_idx = jnp.min(jnp.where(gl == gmax, lane, 4 * LANES), axis=-1, keepdims=True) - MOE_EXPERTS
    in_grp = (lane >= g_idx * MOE_PER_GROUP) & (lane < (g_idx + 1) * MOE_PER_GROUP)
    el = jnp.where(in_grp, logits, NEG_INF)
    v1 = jnp.max(el, axis=-1, keepdims=True)
    i1 = jnp.min(jnp.where(el == v1, lane, 4 * LANES), axis=-1, keepdims=True)
    el2 = jnp.where(lane == i1, NEG_INF, el)
    v2 = jnp.max(el2, axis=-1, keepdims=True)
    i2 = jnp.min(jnp.where(el2 == v2, lane, 4 * LANES), axis=-1, keepdims=True)
    e2 = jnp.exp(v2 - v1)
    w1 = g_w / (1.0 + e2)
    w2 = g_w * e2 / (1.0 + e2)
    gate_ref[0] = jnp.where(lane == i1, w1, 0.0) + jnp.where(lane == i2, w2, 0.0)


def _post(x, ys, ym, gm, wos, wom, gx, wq, km, vm, wo, gmoe, wr, br, tm):
    b, s, d = x.shape
    hd = d // XATTN_HEADS
    tok = lambda n: pl.BlockSpec((1, tm, n), lambda i, j: (i, j, 0))
    mem = pl.BlockSpec((1,) + km.shape[1:], lambda i, j: (i, 0, 0, 0))
    return pl.pallas_call(
        functools.partial(_post_kernel, hd=hd),
        grid=(b, s // tm),
        in_specs=[tok(d), tok(ys.shape[-1]), pl.BlockSpec((1, MLA_HEADS, tm, LANES), lambda i, j: (i, 0, j, 0)),
                  _full(gm.shape), _full(wos.shape), _full(wom.shape), _full(gx.shape), _full(wq.shape), mem, mem,
                  _full(wo.shape), _full(gmoe.shape), _full(wr.shape), _full(br.shape)],
        out_specs=[tok(d), tok(d), tok(LANES)],
        out_shape=[jax.ShapeDtypeStruct((b, s, d), F32), jax.ShapeDtypeStruct((b, s, d), BF16),
                   jax.ShapeDtypeStruct((b, s, LANES), F32)],
        compiler_params=_params("parallel", "parallel"),
        name="post",
    )(x, ys, ym, gm, wos, wom, gx, wq, km, vm, wo, gmoe, wr, br)


def _moe_kernel(h_ref, gate_ref, x2_ref, wg_ref, wu_ref, wd_ref, gf_ref, o_ref, acc_ref):
    e = pl.program_id(1)

    @pl.when(e == 0)
    def _():
        acc_ref[...] = jnp.zeros_like(acc_ref)

    h = h_ref[...]
    a = jax.nn.silu(_dot(h, wg_ref[0])) * _dot(h, wu_ref[0])
    lane = lax.broadcasted_iota(jnp.int32, gate_ref.shape, 1)
    gcol = jnp.sum(jnp.where(lane == e, gate_ref[...], 0.0), axis=-1, keepdims=True)
    acc_ref[...] += _dot((a * gcol).astype(BF16), wd_ref[0])

    @pl.when(e == pl.num_programs(1) - 1)
    def _():
        o_ref[...] = _rms(x2_ref[...] + acc_ref[...], gf_ref[...])


def _moe(h3, gates, x2, wg, wu, wd, gf, tm):
    t, d = h3.shape
    ne, _, ff = wg.shape
    tok = lambda n: pl.BlockSpec((tm, n), lambda i, e: (i, 0))
    return pl.pallas_call(
        _moe_kernel,
        grid=(t // tm, ne),
        in_specs=[tok(d), tok(LANES), tok(d),
                  pl.BlockSpec((1, d, ff), lambda i, e: (e, 0, 0)), pl.BlockSpec((1, d, ff), lambda i, e: (e, 0, 0)),
                  pl.BlockSpec((1, ff, d), lambda i, e: (e, 0, 0)), _full(gf.shape)],
        out_specs=tok(d),
        out_shape=jax.ShapeDtypeStruct((t, d), F32),
        scratch_shapes=[pltpu.VMEM((tm, d), F32)],
        compiler_params=_params("parallel", "arbitrary"),
        name="moe",
    )(h3, gates, x2, wg, wu, wd, gf)


def _pad_heads(w, per_head, offset=0):
    k = w.shape[0]
    w = w.reshape(k, MLA_HEADS, per_head)
    w = jnp.pad(w, ((0, 0), (0, 0), (offset, LANES - per_head - offset)))
    return w.reshape(k, MLA_HEADS * LANES)


def _block_diag(blocks):
    n, r, c = blocks.shape
    eye = jnp.eye(n, dtype=blocks.dtype)
    return (eye[:, None, :, None] * blocks[:, :, None, :]).reshape(n * r, n * c)


def kernel(x, mem, positions, norm_mix_g, w_in, ssm_lam_re, ssm_lam_im, ssm_log_dt, ssm_b_re, ssm_b_im, ssm_c_re, ssm_c_im, ssm_d, ssm_w_glu, ssm_b_glu, mla_q_norm_g, mla_w_q_up, mla_kv_norm_g, mla_w_kv_up, out_norm_ssm_g, out_norm_mla_g, w_out, norm_xattn_g, norm_mem_g, xattn_w_q, xattn_w_k, xattn_w_v, xattn_w_o, norm_moe_g, moe_w_group, moe_b_group, moe_w_expert, moe_b_expert, moe_w_gate, moe_w_up, moe_w_down, norm_final_g):
    bsz, seq, d = x.shape
    depth = w_in.shape[0]
    d_ssm = ssm_d.shape[-1]
    q_rank = mla_q_norm_g.shape[-1]
    kv_rank = mla_kv_norm_g.shape[-1]
    n_grp = d_ssm // SSM_GROUP
    s1, s2, s3 = d_ssm, d_ssm + q_rank, d_ssm + q_rank + kv_rank
    row = lambda v: v.reshape(1, -1).astype(F32)

    half = MLA_ROPE // 2
    inv_freq = ROPE_THETA ** (-jnp.arange(half, dtype=F32) / half)
    invf = jnp.zeros((LANES,), F32).at[MLA_NOPE:MLA_NOPE + half].set(inv_freq)
    invf = invf.at[MLA_NOPE + half:MLA_QK].set(inv_freq).reshape(1, LANES)
    pos = positions.reshape(bsz, seq, 1)

    for l in range(depth):
        a_re, a_im, bb_re, bb_im = _ssm_prep(ssm_lam_re[l], ssm_lam_im[l], ssm_log_dt[l], ssm_b_re[l], ssm_b_im[l])
        gh = n_grp // 2
        bbd = jnp.stack([
            jnp.concatenate([_block_diag(bb_re[hf * gh:(hf + 1) * gh]), _block_diag(bb_im[hf * gh:(hf + 1) * gh])],
                            axis=1) for hf in range(2)]).astype(BF16)
        c_re_t = jnp.transpose(ssm_c_re[l], (0, 2, 1))
        c_im_t = jnp.transpose(ssm_c_im[l], (0, 2, 1))
        cmat = jnp.stack([
            jnp.concatenate([_block_diag(c_re_t[hf * gh:(hf + 1) * gh]), -_block_diag(c_im_t[hf * gh:(hf + 1) * gh])],
                            axis=0) for hf in range(2)]).astype(BF16)
        atab = jnp.stack([
            jnp.repeat(arr.reshape(2, 1, gh * SSM_STATE), bsz, axis=1).reshape(2 * bsz, gh * SSM_STATE)
            for arr in (a_re, a_im)])

        wi = w_in[l]
        wu = wi[:, :s1].astype(BF16)
        wcq = wi[:, s1:s2].astype(BF16)
        wckv = wi[:, s2:s3].astype(BF16)
        wkr = jnp.pad(wi[:, s3:], ((0, 0), (MLA_NOPE, LANES - MLA_QK))).astype(BF16)
        wq = _pad_heads(mla_w_q_up[l], MLA_QK).astype(BF16)
        wkv = mla_w_kv_up[l].reshape(kv_rank, MLA_HEADS, MLA_NOPE + MLA_V)
        wk = _pad_heads(wkv[:, :, :MLA_NOPE].reshape(kv_rank, -1), MLA_NOPE).astype(BF16)
        wv = _pad_heads(wkv[:, :, MLA_NOPE:].reshape(kv_rank, -1), MLA_V).astype(BF16)
        u, q, k, v = _proj(x, pos, row(norm_mix_g[l]), wu, wcq, wckv, wkr, row(mla_q_norm_g[l]), wq,
                           row(mla_kv_norm_g[l]), wk, wv, invf, tm=min(512, seq))

        y_ssm = _ssm(u, bbd, atab, cmat, row(ssm_d[l]), ssm_w_glu[l].astype(BF16), row(ssm_b_glu[l]),
                     row(out_norm_ssm_g[l]), tt=min(128, seq))
        y_mla = _attn(q, k, v, tq=min(256, seq), hp=8, strip=32)

        km, vm = _memkv(mem, row(norm_mem_g[l]), xattn_w_k[l].astype(BF16), xattn_w_v[l].astype(BF16))
        gm = jnp.pad(out_norm_mla_g[l].reshape(MLA_HEADS, MLA_V), ((0, 0), (0, LANES - MLA_V))).reshape(1, -1)
        wos = w_out[l][:d_ssm].astype(BF16)
        wom = jnp.pad(w_out[l][d_ssm:].reshape(MLA_HEADS, MLA_V, d), ((0, 0), (0, LANES - MLA_V), (0, 0)))
        wom = wom.reshape(MLA_HEADS * LANES, d).astype(BF16)
        wr = jnp.concatenate([moe_w_expert[l].reshape(d, MOE_EXPERTS), moe_w_group[l]], axis=1)
        wr = jnp.pad(wr, ((0, 0), (0, LANES - MOE_EXPERTS - MOE_GROUPS))).astype(F32)
        br = jnp.concatenate([moe_b_expert[l].reshape(-1), moe_b_group[l]])
        br = jnp.pad(br, (0, LANES - MOE_EXPERTS - MOE_GROUPS)).reshape(1, LANES).astype(F32)
        x2, h3, gates = _post(x, y_ssm, y_mla, gm.astype(F32), wos, wom, row(norm_xattn_g[l]),
                              xattn_w_q[l].astype(BF16), km, vm, xattn_w_o[l].astype(BF16), row(norm_moe_g[l]),
                              wr, br, tm=min(512, seq))

        gf = row(norm_final_g) if l == depth - 1 else None
        out = _moe(h3.reshape(bsz * seq, d), gates.reshape(bsz * seq, LANES), x2.reshape(bsz * seq, d),
                   moe_w_gate[l].astype(BF16), moe_w_up[l].astype(BF16), moe_w_down[l].astype(BF16), gf,
                   tm=min(1024, bsz * seq))
        x = out.reshape(bsz, seq, d)
    return x
```

```python
import functools
import math

import jax
import jax.numpy as jnp
from jax import lax
from jax.experimental import pallas as pl
from jax.experimental.pallas import tpu as pltpu

F32 = jnp.float32
BF16 = jnp.bfloat16

EPS = 1e-6
NEG_INF = -1e30
CHUNK = 64

LANES = 128
SSM_GROUP = 16
SSM_STATE = 64
MLA_HEADS = 8
MLA_NOPE = 64
MLA_ROPE = 32
MLA_QK = MLA_NOPE + MLA_ROPE
MLA_V = 64
ROPE_THETA = 10000.0
XATTN_HEADS = 4
MOE_GROUPS = 4
MOE_PER_GROUP = 8
MOE_EXPERTS = MOE_GROUPS * MOE_PER_GROUP
VMEM_LIMIT = 48 * 1024 * 1024
LOG2E = math.log2(math.e)
MOE_TILE = 256
GROUP_LANE = 64


def _dot(a, b):
    return jnp.dot(a, b, preferred_element_type=F32)


def _dot_nt(a, b):
    return lax.dot_general(a, b, (((1,), (1,)), ((), ())), preferred_element_type=F32)


def _rms(x, g, n=None):
    n = x.shape[-1] if n is None else n
    ms = jnp.sum(x * x, axis=-1, keepdims=True) * (1.0 / n)
    return x * lax.rsqrt(ms + EPS) * g


def _params(*sem):
    return pltpu.CompilerParams(dimension_semantics=sem, vmem_limit_bytes=VMEM_LIMIT)


def _full(shape):
    zeros = (0,) * len(shape)
    return pl.BlockSpec(shape, lambda *_: zeros)


def _ssm_prep_kernel(lre_ref, lim_ref, ldt_ref, bre_ref, bim_ref, are_ref, aim_ref, bbre_ref, bbim_ref):
    lre = jnp.minimum(lre_ref[...], -1e-4)
    lim = lim_ref[...]
    dt = jnp.exp(ldt_ref[...])
    mag = jnp.exp(lre * dt)
    are = mag * jnp.cos(lim * dt)
    aim = mag * jnp.sin(lim * dt)
    are_ref[...] = are
    aim_ref[...] = aim
    nre = are - 1.0
    den = lre * lre + lim * lim
    fre = (nre * lre + aim * lim) / den
    fim = (aim * lre - nre * lim) / den
    bre = bre_ref[...]
    bim = bim_ref[...]
    bbre_ref[...] = fre * bre - fim * bim
    bbim_ref[...] = fre * bim + fim * bre


def _ssm_prep(lam_re, lam_im, log_dt, b_re, b_im):
    g, p = lam_re.shape
    hh = b_re.shape[-1]
    bt_re = jnp.transpose(b_re, (0, 2, 1))
    bt_im = jnp.transpose(b_im, (0, 2, 1))
    outs = pl.pallas_call(
        _ssm_prep_kernel,
        out_shape=[jax.ShapeDtypeStruct((g, 1, p), F32)] * 2 + [jax.ShapeDtypeStruct((g, hh, p), F32)] * 2,
        name="ssm_prep",
    )(lam_re.reshape(g, 1, p), lam_im.reshape(g, 1, p), log_dt.reshape(g, 1, 1), bt_re, bt_im)
    a_re, a_im, bb_re, bb_im = outs
    return a_re.reshape(g, p), a_im.reshape(g, p), bb_re, bb_im


def _proj_kernel(x_ref, pos_ref, gmix_ref, wu_ref, wcq_ref, wckv_ref, wkr_ref, qg_ref, wq_ref, kvg_ref,
                 wk_ref, wv_ref, invf_ref, u_ref, q_ref, k_ref, v_ref):
    x = x_ref[0]
    h = _rms(x, gmix_ref[...]).astype(BF16)
    u_ref[0] = _dot(h, wu_ref[...]).astype(BF16)
    cq = _dot(h, wcq_ref[...])
    q = _dot(_rms(cq, qg_ref[...]).astype(BF16), wq_ref[...]) * (MLA_QK ** -0.5 * LOG2E)
    ckv = _dot(h, wckv_ref[...])
    ckvn = _rms(ckv, kvg_ref[...]).astype(BF16)
    kk = _dot(ckvn, wk_ref[...])
    vv = _dot(ckvn, wv_ref[...])
    kr = _dot(h, wkr_ref[...])

    ang = pos_ref[0].astype(F32) * invf_ref[...]
    cosv = jnp.cos(ang)
    sinv = jnp.sin(ang)
    lane = lax.broadcasted_iota(jnp.int32, (1, LANES), 1)
    half = MLA_ROPE // 2
    c_tab = jnp.where(lane < MLA_NOPE, 1.0, cosv)
    s_lo = jnp.where((lane >= MLA_NOPE) & (lane < MLA_NOPE + half), -sinv, 0.0)
    s_hi = jnp.where((lane >= MLA_NOPE + half) & (lane < MLA_QK), sinv, 0.0)
    ones_col = jnp.where(lane == MLA_V, 1.0, 0.0)

    def rot(val):
        return (val * c_tab + pltpu.roll(val, LANES - half, 1) * s_lo + pltpu.roll(val, half, 1) * s_hi)

    kpe = rot(kr)
    for hh in range(MLA_HEADS):
        sl = slice(hh * LANES, (hh + 1) * LANES)
        q_ref[0, hh] = rot(q[:, sl]).T.astype(BF16)
        k_ref[0, hh] = (kk[:, sl] + kpe).astype(BF16)
        v_ref[0, hh] = (vv[:, sl] + ones_col).T.astype(BF16)


def _proj(x, pos, gmix, wu, wcq, wckv, wkr, qg, wq, kvg, wk, wv, invf, tm):
    b, s, d = x.shape
    grid = (b, s // tm)
    tok = lambda n: pl.BlockSpec((1, tm, n), lambda i, j: (i, j, 0))
    head = pl.BlockSpec((1, MLA_HEADS, tm, LANES), lambda i, j: (i, 0, j, 0))
    head_t = pl.BlockSpec((1, MLA_HEADS, LANES, tm), lambda i, j: (i, 0, 0, j))
    tshape = jax.ShapeDtypeStruct((b, MLA_HEADS, LANES, s), BF16)
    return pl.pallas_call(
        _proj_kernel,
        grid=grid,
        in_specs=[tok(d), tok(1), _full(gmix.shape), _full(wu.shape), _full(wcq.shape), _full(wckv.shape),
                  _full(wkr.shape), _full(qg.shape), _full(wq.shape), _full(kvg.shape), _full(wk.shape),
                  _full(wv.shape), _full(invf.shape)],
        out_specs=[tok(wu.shape[1]), head_t, head, head_t],
        out_shape=[jax.ShapeDtypeStruct((b, s, wu.shape[1]), BF16), tshape,
                   jax.ShapeDtypeStruct((b, MLA_HEADS, s, LANES), BF16), tshape],
        compiler_params=_params("parallel", "parallel"),
        name="proj",
    )(x, pos, gmix, wu, wcq, wckv, wkr, qg, wq, kvg, wk, wv, invf)


def _ssm_kernel(u_ref, bbd_ref, atab_ref, cmat_ref, dskip_ref, wglu_ref, bglu_ref, gout_ref, y_ref,
                xs_ref, st_ref, *, tt, slab, nb, lane_tiles):
    j = pl.program_id(0)
    half_w = u_ref.shape[-1] // 2
    n_state = lane_tiles * LANES

    @pl.when(j == 0)
    def _():
        st_ref[...] = jnp.zeros_like(st_ref)

    for hf in range(2):
        ub = u_ref[:, :, hf * half_w:(hf + 1) * half_w].reshape(nb * tt, half_w)
        bu = _dot(ub, bbd_ref[hf])
        for b in range(nb):
            r0 = (hf * nb + b) * slab
            for c in range(2 * lane_tiles):
                xs_ref[c, r0:r0 + tt, :] = bu[b * tt:(b + 1) * tt, c * LANES:(c + 1) * LANES]

    rows = 2 * nb
    group = 4
    for c0 in range(0, lane_tiles, group):
        cs = list(range(c0, c0 + group))
        a_re = [atab_ref[0, :, c * LANES:(c + 1) * LANES] for c in cs]
        a_im = [atab_ref[1, :, c * LANES:(c + 1) * LANES] for c in cs]
        init = tuple(st_ref[0, :, c * LANES:(c + 1) * LANES] for c in cs) + \
            tuple(st_ref[1, :, c * LANES:(c + 1) * LANES] for c in cs)

        def step(t, carry, cs=cs, a_re=a_re, a_im=a_im):
            new_re, new_im = [], []
            for i, c in enumerate(cs):
                x_re, x_im = carry[i], carry[group + i]
                idx = pl.ds(t, rows, stride=slab)
                n_re = a_re[i] * x_re - a_im[i] * x_im + xs_ref[c, idx, :]
                n_im = a_re[i] * x_im + a_im[i] * x_re + xs_ref[lane_tiles + c, idx, :]
                xs_ref[c, idx, :] = n_re
                xs_ref[lane_tiles + c, idx, :] = n_im
                new_re.append(n_re)
                new_im.append(n_im)
            return tuple(new_re) + tuple(new_im)

        fin = lax.fori_loop(0, tt, step, init, unroll=4)
        for i, c in enumerate(cs):
            st_ref[0, :, c * LANES:(c + 1) * LANES] = fin[i]
            st_ref[1, :, c * LANES:(c + 1) * LANES] = fin[group + i]

    ys = []
    for hf in range(2):
        xb = []
        for b in range(nb):
            r0 = (hf * nb + b) * slab
            xb.append(jnp.concatenate([xs_ref[c, r0:r0 + tt, :] for c in range(2 * lane_tiles)], axis=1))
        xh = jnp.concatenate(xb, axis=0).astype(BF16)
        ys.append(_dot(xh, cmat_ref[hf]))
    y = jnp.concatenate(ys, axis=1)
    u = u_ref[...].astype(F32).reshape(nb * tt, 2 * half_w)
    y = jax.nn.gelu(y + dskip_ref[...] * u)
    y = y * jax.nn.sigmoid(_dot(y.astype(BF16), wglu_ref[...]) + bglu_ref[...])
    y = _rms(y, gout_ref[...])
    y_ref[...] = y.reshape(nb, tt, 2 * half_w).astype(BF16)


def _ssm(u, bbd, atab, cmat, dskip, wglu, bglu, gout, tt):
    nb, s, dssm = u.shape
    lane_tiles = bbd.shape[-1] // (2 * LANES)
    slab = tt + 8
    kern = functools.partial(_ssm_kernel, tt=tt, slab=slab, nb=nb, lane_tiles=lane_tiles)
    return pl.pallas_call(
        kern,
        grid=(s // tt,),
        in_specs=[pl.BlockSpec((nb, tt, dssm), lambda j: (0, j, 0)), _full(bbd.shape), _full(atab.shape),
                  _full(cmat.shape), _full(dskip.shape), _full(wglu.shape), _full(bglu.shape), _full(gout.shape)],
        out_specs=pl.BlockSpec((nb, tt, dssm), lambda j: (0, j, 0)),
        out_shape=jax.ShapeDtypeStruct((nb, s, dssm), BF16),
        scratch_shapes=[pltpu.VMEM((2 * lane_tiles, 2 * nb * slab, LANES), F32),
                        pltpu.VMEM((2, 2 * nb, lane_tiles * LANES), F32)],
        compiler_params=_params("arbitrary"),
        name="ssm",
    )(u, bbd, atab, cmat, dskip, wglu, bglu, gout)


def _attn_kernel(qt_ref, k_ref, vt_ref, o_ref, s_ref, p_ref, m_ref, a_ref, acc_ref, *, tq, hp, strip):
    qi = pl.program_id(2)
    m_ref[...] = jnp.full(m_ref.shape, NEG_INF, F32)
    acc_ref[...] = jnp.zeros(acc_ref.shape, F32)
    q_chunk = lax.broadcasted_iota(jnp.int32, (1, tq), 1) // CHUNK

    def tile(j, masked):
        for hh in range(hp):
            s_ref[hh] = _dot(k_ref[0, hh, pl.ds(j * tq, tq), :], qt_ref[0, hh])

        def strip_of(hh, r):
            s = s_ref[hh, r:r + strip, :]
            return jnp.where(q_chunk >= r // CHUNK, s, NEG_INF) if masked else s

        for hh in range(hp):
            mt = strip_of(hh, 0)
            for r in range(strip, tq, strip):
                mt = jnp.maximum(mt, strip_of(hh, r))
            m_old = m_ref[hh]
            m_new = jnp.maximum(m_old, jnp.max(mt, axis=0, keepdims=True))
            a_ref[hh] = jnp.exp2(m_old - m_new)
            m_ref[hh] = m_new
            for r in range(0, tq, strip):
                p_ref[hh, r:r + strip, :] = jnp.exp2(strip_of(hh, r) - m_new).astype(BF16)
        for hh in range(hp):
            pv = _dot(vt_ref[0, hh, :, pl.ds(j * tq, tq)], p_ref[hh])
            acc_ref[hh] = acc_ref[hh] * a_ref[hh] + pv

    def body(j, c):
        tile(j, False)
        return c

    lax.fori_loop(0, qi, body, 0)
    tile(qi, True)
    feat = lax.broadcasted_iota(jnp.int32, (LANES, 1), 0)
    for hh in range(hp):
        acc = acc_ref[hh]
        out = acc * (1.0 / acc[MLA_V:MLA_V + 1, :])
        o_ref[0, hh] = jnp.where(feat < MLA_V, out, 0.0).T.astype(BF16)


def _attn(qt, k, vt, tq, hp, strip):
    b, h, s, _ = k.shape
    assert strip <= CHUNK and CHUNK % strip == 0 and tq % CHUNK == 0
    return pl.pallas_call(
        functools.partial(_attn_kernel, tq=tq, hp=hp, strip=strip),
        grid=(b, h // hp, s // tq),
        in_specs=[pl.BlockSpec((1, hp, LANES, tq), lambda i, j, t: (i, j, 0, t)),
                  pl.BlockSpec((1, hp, s, LANES), lambda i, j, t: (i, j, 0, 0)),
                  pl.BlockSpec((1, hp, LANES, s), lambda i, j, t: (i, j, 0, 0))],
        out_specs=pl.BlockSpec((1, hp, tq, LANES), lambda i, j, t: (i, j, t, 0)),
        out_shape=jax.ShapeDtypeStruct((b, h, s, LANES), BF16),
        scratch_shapes=[pltpu.VMEM((hp, tq, tq), F32), pltpu.VMEM((hp, tq, tq), BF16),
                        pltpu.VMEM((hp, 1, tq), F32), pltpu.VMEM((hp, 1, tq), F32),
                        pltpu.VMEM((hp, LANES, tq), F32)],
        compiler_params=_params("parallel", "parallel", "arbitrary"),
        name="attn",
    )(qt, k, vt)


def _memkv_kernel(mem_ref, g_ref, wk_ref, wv_ref, k_ref, v_ref, *, hd):
    mn = _rms(mem_ref[0], g_ref[...]).astype(BF16)
    kk = _dot(mn, wk_ref[...])
    vv = _dot(mn, wv_ref[...])
    for hh in range(XATTN_HEADS):
        k_ref[0, hh] = kk[:, hh * hd:(hh + 1) * hd].astype(BF16)
        v_ref[0, hh] = vv[:, hh * hd:(hh + 1) * hd].astype(BF16)


def _memkv(mem, g, wk, wv):
    b, nm, d = mem.shape
    hd = d // XATTN_HEADS
    ospec = pl.BlockSpec((1, XATTN_HEADS, nm, hd), lambda i: (i, 0, 0, 0))
    return pl.pallas_call(
        functools.partial(_memkv_kernel, hd=hd),
        grid=(b,),
        in_specs=[pl.BlockSpec((1, nm, d), lambda i: (i, 0, 0)), _full(g.shape), _full(wk.shape), _full(wv.shape)],
        out_specs=[ospec, ospec],
        out_shape=[jax.ShapeDtypeStruct((b, XATTN_HEADS, nm, hd), BF16)] * 2,
        compiler_params=_params("parallel"),
        name="mem_kv",
    )(mem, g, wk, wv)


def _post_kernel(x_ref, ys_ref, ym_ref, gm_ref, wos_ref, wom_ref, gx_ref, wq_ref, km_ref, vm_ref, wo_ref,
                 gmoe_ref, wr_ref, br_ref, x2_ref, h3_ref, route_ref, *, hd):
    x = x_ref[0]
    ym = jnp.concatenate([ym_ref[0, hh] for hh in range(MLA_HEADS)], axis=1).astype(F32)
    ymn = _rms(ym, gm_ref[...], n=MLA_HEADS * MLA_V).astype(BF16)
    x1 = x + _dot(ys_ref[0], wos_ref[...]) + _dot(ymn, wom_ref[...])

    h2 = _rms(x1, gx_ref[...]).astype(BF16)
    qx = (_dot(h2, wq_ref[...]) * (hd ** -0.5)).astype(BF16)
    outs = []
    for hh in range(XATTN_HEADS):
        s = _dot_nt(qx[:, hh * hd:(hh + 1) * hd], km_ref[0, hh])
        p = jnp.exp(s - jnp.max(s, axis=-1, keepdims=True))
        p = p * (1.0 / jnp.sum(p, axis=-1, keepdims=True))
        outs.append(_dot(p.astype(BF16), vm_ref[0, hh]))
    o = jnp.concatenate(outs, axis=1).astype(BF16)
    x2 = x1 + _dot(o, wo_ref[...])
    x2_ref[0] = x2

    h3 = _rms(x2, gmoe_ref[...])
    h3_ref[0] = h3

    logits = jnp.dot(h3, wr_ref[...], preferred_element_type=F32, precision=lax.Precision.HIGHEST) + br_ref[...]
    lane = lax.broadcasted_iota(jnp.int32, logits.shape, 1)
    is_g = (lane >= MOE_EXPERTS) & (lane < MOE_EXPERTS + MOE_GROUPS)
    gl = jnp.where(is_g, logits, NEG_INF)
    gmax = jnp.max(gl, axis=-1, keepdims=True)
    g_w = 1.0 / jnp.sum(jnp.exp(gl - gmax), axis=-1, keepdims=True)
    g_idx = jnp.min(jnp.where(gl == gmax, lane, 4 * LANES), axis=-1, keepdims=True) - MOE_EXPERTS
    in_grp = (lane >= g_idx * MOE_PER_GROUP) & (lane < (g_idx + 1) * MOE_PER_GROUP)
    el = jnp.where(in_grp, logits, NEG_INF)
    v1 = jnp.max(el, axis=-1, keepdims=True)
    i1 = jnp.min(jnp.where(el == v1, lane, 4 * LANES), axis=-1, keepdims=True)
    el2 = jnp.where(lane == i1, NEG_INF, el)
    v2 = jnp.max(el2, axis=-1, keepdims=True)
    i2 = jnp.min(jnp.where(el2 == v2, lane, 4 * LANES), axis=-1, keepdims=True)
    e2 = jnp.exp(v2 - v1)
    w1 = g_w / (1.0 + e2)
    w2 = g_w * e2 / (1.0 + e2)
    route_ref[0] = (jnp.where(lane == i1, w1, 0.0) + jnp.where(lane == i2, w2, 0.0)
                    + jnp.where(lane == GROUP_LANE, g_idx.astype(F32), 0.0))


def _post(x, ys, ym, gm, wos, wom, gx, wq, km, vm, wo, gmoe, wr, br, tm):
    b, s, d = x.shape
    hd = d // XATTN_HEADS
    tok = lambda n: pl.BlockSpec((1, tm, n), lambda i, j: (i, j, 0))
    mem = pl.BlockSpec((1,) + km.shape[1:], lambda i, j: (i, 0, 0, 0))
    return pl.pallas_call(
        functools.partial(_post_kernel, hd=hd),
        grid=(b, s // tm),
        in_specs=[tok(d), tok(ys.shape[-1]), pl.BlockSpec((1, MLA_HEADS, tm, LANES), lambda i, j: (i, 0, j, 0)),
                  _full(gm.shape), _full(wos.shape), _full(wom.shape), _full(gx.shape), _full(wq.shape), mem, mem,
                  _full(wo.shape), _full(gmoe.shape), _full(wr.shape), _full(br.shape)],
        out_specs=[tok(d), tok(d), tok(LANES)],
        out_shape=[jax.ShapeDtypeStruct((b, s, d), F32), jax.ShapeDtypeStruct((b, s, d), F32),
                   jax.ShapeDtypeStruct((b, s, LANES), F32)],
        compiler_params=_params("parallel", "parallel"),
        name="post",
    )(x, ys, ym, gm, wos, wom, gx, wq, km, vm, wo, gmoe, wr, br)


def _route_metadata(route, tm):
    b, s, _ = route.shape
    n_pad = s + MOE_GROUPS * tm
    n_tiles = n_pad // tm
    gid = route[..., GROUP_LANE].astype(jnp.int32)
    onehot = (gid[..., None] == jnp.arange(MOE_GROUPS, dtype=jnp.int32)).astype(jnp.int32)
    cnt = jnp.sum(onehot, axis=1)
    rank = jnp.sum((jnp.cumsum(onehot, axis=1) - onehot) * onehot, axis=-1)
    padded = (cnt + tm - 1) // tm * tm
    seg_end = jnp.cumsum(padded, axis=-1)
    dest = jnp.take_along_axis(seg_end - padded, gid, axis=1) + rank
    rows = jnp.arange(b, dtype=jnp.int32)[:, None]
    src = jnp.zeros((b, n_pad), jnp.int32).at[rows, dest].set(jnp.arange(s, dtype=jnp.int32)[None])
    tile_start = jnp.arange(n_tiles, dtype=jnp.int32) * tm
    tile_g = jnp.sum((seg_end[:, None, :] <= tile_start[None, :, None]).astype(jnp.int32), axis=-1)
    tile_g = jnp.minimum(tile_g, MOE_GROUPS - 1)
    n_valid = (seg_end[:, -1] // tm).astype(jnp.int32)
    return dict(src=src, pos=dest, tile_g=tile_g.reshape(b * n_tiles), n_valid=n_valid)


def _dispatch_kernel(h_ref, g_ref, src_ref, xs_ref, gs_ref, hbuf_ref, *, tm):
    def gather(r, c):
        t = src_ref[0, 0, r]
        hbuf_ref[pl.ds(r, 1), :] = h_ref[0, pl.ds(t, 1), :]
        gs_ref[0, pl.ds(r, 1), :] = g_ref[0, pl.ds(t, 1), :]
        return c

    lax.fori_loop(0, tm, gather, 0, unroll=8)
    xs_ref[0] = hbuf_ref[...].astype(BF16)


def _dispatch(h3, route, src, tm):
    b, s, d = h3.shape
    n_pad = src.shape[1]
    nt = n_pad // tm
    return pl.pallas_call(
        functools.partial(_dispatch_kernel, tm=tm),
        grid=(b, nt),
        in_specs=[pl.BlockSpec((1, s, d), lambda i, j: (i, 0, 0)),
                  pl.BlockSpec((1, s, LANES), lambda i, j: (i, 0, 0)),
                  pl.BlockSpec((1, 1, tm), lambda i, j: (i * nt + j, 0, 0), memory_space=pltpu.SMEM)],
        out_specs=[pl.BlockSpec((1, tm, d), lambda i, j: (i, j, 0)),
                   pl.BlockSpec((1, tm, LANES), lambda i, j: (i, j, 0))],
        out_shape=[jax.ShapeDtypeStruct((b, n_pad, d), BF16), jax.ShapeDtypeStruct((b, n_pad, LANES), F32)],
        scratch_shapes=[pltpu.VMEM((tm, d), F32)],
        compiler_params=_params("parallel", "arbitrary"),
        name="dispatch",
    )(h3, route, src.reshape(b * nt, 1, tm))


def _moe_kernel(tg_ref, nv_ref, h_ref, gate_ref, wg_ref, wu_ref, wd_ref, o_ref, *, nt):
    bi = pl.program_id(0)
    ti = pl.program_id(1)
    grp = tg_ref[bi * nt + ti]

    @pl.when(ti < nv_ref[bi])
    def _():
        h = h_ref[0]
        gate = gate_ref[0]
        lane = lax.broadcasted_iota(jnp.int32, gate.shape, 1)
        acc = None
        for e in range(MOE_PER_GROUP):
            a = jax.nn.silu(_dot(h, wg_ref[e])) * _dot(h, wu_ref[e])
            gcol = jnp.sum(jnp.where(lane == grp * MOE_PER_GROUP + e, gate, 0.0), axis=-1, keepdims=True)
            y = _dot((a * gcol).astype(BF16), wd_ref[e])
            acc = y if acc is None else acc + y
        o_ref[0] = acc

    @pl.when(ti >= nv_ref[bi])
    def _():
        o_ref[0] = jnp.zeros(o_ref.shape[1:], o_ref.dtype)


def _moe(xs, gs, tile_g, n_valid, wg, wu, wd, tm):
    b, n_pad, d = xs.shape
    nt = n_pad // tm
    ff = wg.shape[-1]
    group = lambda i, j, tg, nv: (tg[i * nt + j], 0, 0)
    tok = lambda n: pl.BlockSpec((1, tm, n), lambda i, j, tg, nv: (i, j, 0))
    grid_spec = pltpu.PrefetchScalarGridSpec(
        num_scalar_prefetch=2,
        grid=(b, nt),
        in_specs=[tok(d), tok(LANES), pl.BlockSpec((MOE_PER_GROUP, d, ff), group),
                  pl.BlockSpec((MOE_PER_GROUP, d, ff), group), pl.BlockSpec((MOE_PER_GROUP, ff, d), group)],
        out_specs=tok(d),
    )
    return pl.pallas_call(
        functools.partial(_moe_kernel, nt=nt),
        grid_spec=grid_spec,
        out_shape=jax.ShapeDtypeStruct((b, n_pad, d), F32),
        compiler_params=_params("parallel", "arbitrary"),
        name="moe",
    )(tile_g, n_valid, xs, gs, wg, wu, wd)


def _combine_kernel(ys_ref, pos_ref, x2_ref, gf_ref, o_ref, buf_ref, *, tt):
    def gather(t, c):
        buf_ref[pl.ds(t, 1), :] = ys_ref[0, pl.ds(pos_ref[0, 0, t], 1), :]
        return c

    lax.fori_loop(0, tt, gather, 0, unroll=8)
    o_ref[0] = _rms(x2_ref[0] + buf_ref[...], gf_ref[...])


def _combine(ys, pos, x2, gf, tt):
    b, s, d = x2.shape
    n_pad = ys.shape[1]
    nj = s // tt
    return pl.pallas_call(
        functools.partial(_combine_kernel, tt=tt),
        grid=(b, nj),
        in_specs=[pl.BlockSpec((1, n_pad, d), lambda i, j: (i, 0, 0), pipeline_mode=pl.Buffered(1)),
                  pl.BlockSpec((1, 1, tt), lambda i, j: (i * nj + j, 0, 0), memory_space=pltpu.SMEM),
                  pl.BlockSpec((1, tt, d), lambda i, j: (i, j, 0)), _full(gf.shape)],
        out_specs=pl.BlockSpec((1, tt, d), lambda i, j: (i, j, 0)),
        out_shape=jax.ShapeDtypeStruct((b, s, d), F32),
        scratch_shapes=[pltpu.VMEM((tt, d), F32)],
        compiler_params=_params("parallel", "arbitrary"),
        name="combine",
    )(ys, pos.reshape(b * nj, 1, tt), x2, gf)


def _pad_heads(w, per_head, offset=0):
    k = w.shape[0]
    w = w.reshape(k, MLA_HEADS, per_head)
    w = jnp.pad(w, ((0, 0), (0, 0), (offset, LANES - per_head - offset)))
    return w.reshape(k, MLA_HEADS * LANES)


def _block_diag(blocks):
    n, r, c = blocks.shape
    eye = jnp.eye(n, dtype=blocks.dtype)
    return (eye[:, None, :, None] * blocks[:, :, None, :]).reshape(n * r, n * c)


def kernel(x, mem, positions, norm_mix_g, w_in, ssm_lam_re, ssm_lam_im, ssm_log_dt, ssm_b_re, ssm_b_im, ssm_c_re, ssm_c_im, ssm_d, ssm_w_glu, ssm_b_glu, mla_q_norm_g, mla_w_q_up, mla_kv_norm_g, mla_w_kv_up, out_norm_ssm_g, out_norm_mla_g, w_out, norm_xattn_g, norm_mem_g, xattn_w_q, xattn_w_k, xattn_w_v, xattn_w_o, norm_moe_g, moe_w_group, moe_b_group, moe_w_expert, moe_b_expert, moe_w_gate, moe_w_up, moe_w_down, norm_final_g):
    bsz, seq, d = x.shape
    depth = w_in.shape[0]
    assert depth == 1, "the final RMSNorm is fused into the last stage of a single layer"
    d_ssm = ssm_d.shape[-1]
    q_rank = mla_q_norm_g.shape[-1]
    kv_rank = mla_kv_norm_g.shape[-1]
    n_grp = d_ssm // SSM_GROUP
    s1, s2, s3 = d_ssm, d_ssm + q_rank, d_ssm + q_rank + kv_rank
    row = lambda v: v.reshape(1, -1).astype(F32)

    half = MLA_ROPE // 2
    inv_freq = ROPE_THETA ** (-jnp.arange(half, dtype=F32) / half)
    invf = jnp.zeros((LANES,), F32).at[MLA_NOPE:MLA_NOPE + half].set(inv_freq)
    invf = invf.at[MLA_NOPE + half:MLA_QK].set(inv_freq).reshape(1, LANES)
    pos = positions.reshape(bsz, seq, 1)

    for l in range(depth):
        a_re, a_im, bb_re, bb_im = _ssm_prep(ssm_lam_re[l], ssm_lam_im[l], ssm_log_dt[l], ssm_b_re[l], ssm_b_im[l])
        gh = n_grp // 2
        bbd = jnp.stack([
            jnp.concatenate([_block_diag(bb_re[hf * gh:(hf + 1) * gh]), _block_diag(bb_im[hf * gh:(hf + 1) * gh])],
                            axis=1) for hf in range(2)]).astype(BF16)
        c_re_t = jnp.transpose(ssm_c_re[l], (0, 2, 1))
        c_im_t = jnp.transpose(ssm_c_im[l], (0, 2, 1))
        cmat = jnp.stack([
            jnp.concatenate([_block_diag(c_re_t[hf * gh:(hf + 1) * gh]), -_block_diag(c_im_t[hf * gh:(hf + 1) * gh])],
                            axis=0) for hf in range(2)]).astype(BF16)
        atab = jnp.stack([
            jnp.repeat(arr.reshape(2, 1, gh * SSM_STATE), bsz, axis=1).reshape(2 * bsz, gh * SSM_STATE)
            for arr in (a_re, a_im)])

        wi = w_in[l]
        wu = wi[:, :s1].astype(BF16)
        wcq = wi[:, s1:s2].astype(BF16)
        wckv = wi[:, s2:s3].astype(BF16)
        wkr = jnp.pad(wi[:, s3:], ((0, 0), (MLA_NOPE, LANES - MLA_QK))).astype(BF16)
        wq = _pad_heads(mla_w_q_up[l], MLA_QK).astype(BF16)
        wkv = mla_w_kv_up[l].reshape(kv_rank, MLA_HEADS, MLA_NOPE + MLA_V)
        wk = _pad_heads(wkv[:, :, :MLA_NOPE].reshape(kv_rank, -1), MLA_NOPE).astype(BF16)
        wv = _pad_heads(wkv[:, :, MLA_NOPE:].reshape(kv_rank, -1), MLA_V).astype(BF16)
        u, q, k, v = _proj(x, pos, row(norm_mix_g[l]), wu, wcq, wckv, wkr, row(mla_q_norm_g[l]), wq,
                           row(mla_kv_norm_g[l]), wk, wv, invf, tm=min(512, seq))

        y_ssm = _ssm(u, bbd, atab, cmat, row(ssm_d[l]), ssm_w_glu[l].astype(BF16), row(ssm_b_glu[l]),
                     row(out_norm_ssm_g[l]), tt=min(128, seq))
        y_mla = _attn(q, k, v, tq=min(256, seq), hp=8, strip=32)

        km, vm = _memkv(mem, row(norm_mem_g[l]), xattn_w_k[l].astype(BF16), xattn_w_v[l].astype(BF16))
        gm = jnp.pad(out_norm_mla_g[l].reshape(MLA_HEADS, MLA_V), ((0, 0), (0, LANES - MLA_V))).reshape(1, -1)
        wos = w_out[l][:d_ssm].astype(BF16)
        wom = jnp.pad(w_out[l][d_ssm:].reshape(MLA_HEADS, MLA_V, d), ((0, 0), (0, LANES - MLA_V), (0, 0)))
        wom = wom.reshape(MLA_HEADS * LANES, d).astype(BF16)
        wr = jnp.concatenate([moe_w_expert[l].reshape(d, MOE_EXPERTS), moe_w_group[l]], axis=1)
        wr = jnp.pad(wr, ((0, 0), (0, LANES - MOE_EXPERTS - MOE_GROUPS))).astype(F32)
        br = jnp.concatenate([moe_b_expert[l].reshape(-1), moe_b_group[l]])
        br = jnp.pad(br, (0, LANES - MOE_EXPERTS - MOE_GROUPS)).reshape(1, LANES).astype(F32)
        x2, h3, route = _post(x, y_ssm, y_mla, gm.astype(F32), wos, wom, row(norm_xattn_g[l]),
                              xattn_w_q[l].astype(BF16), km, vm, xattn_w_o[l].astype(BF16), row(norm_moe_g[l]),
                              wr, br, tm=min(512, seq))

        meta = _route_metadata(route, MOE_TILE)
        xs, gs = _dispatch(h3, route, meta["src"], tm=MOE_TILE)
        ys = _moe(xs, gs, meta["tile_g"], meta["n_valid"], moe_w_gate[l].astype(BF16), moe_w_up[l].astype(BF16),
                  moe_w_down[l].astype(BF16), tm=MOE_TILE)
        x = _combine(ys, meta["pos"], x2, row(norm_final_g), tt=min(256, seq))
    return x
```

```python
import functools
import math

import jax
import jax.numpy as jnp
from jax import lax
from jax.experimental import pallas as pl
from jax.experimental.pallas import tpu as pltpu

F32 = jnp.float32
BF16 = jnp.bfloat16

EPS = 1e-6
NEG_INF = -1e30
CHUNK = 64

LANES = 128
SSM_GROUP = 16
SSM_STATE = 64
MLA_HEADS = 8
MLA_NOPE = 64
MLA_ROPE = 32
MLA_QK = MLA_NOPE + MLA_ROPE
MLA_V = 64
ROPE_THETA = 10000.0
XATTN_HEADS = 4
MOE_GROUPS = 4
MOE_PER_GROUP = 8
MOE_EXPERTS = MOE_GROUPS * MOE_PER_GROUP
VMEM_LIMIT = 48 * 1024 * 1024
VMEM_LIMIT_LARGE = 56 * 1024 * 1024
LOG2E = math.log2(math.e)
MOE_TILE = 256
GROUP_LANE = 64


def _dot(a, b):
    return jnp.dot(a, b, preferred_element_type=F32)


def _dot_nt(a, b):
    return lax.dot_general(a, b, (((1,), (1,)), ((), ())), preferred_element_type=F32)


def _rms(x, g, n=None):
    n = x.shape[-1] if n is None else n
    ms = jnp.sum(x * x, axis=-1, keepdims=True) * (1.0 / n)
    return x * lax.rsqrt(ms + EPS) * g


def _params(*sem, vmem=VMEM_LIMIT):
    return pltpu.CompilerParams(dimension_semantics=sem, vmem_limit_bytes=vmem)


def _full(shape):
    zeros = (0,) * len(shape)
    return pl.BlockSpec(shape, lambda *_: zeros)


def _ssm_prep_kernel(lre_ref, lim_ref, ldt_ref, bre_ref, bim_ref, are_ref, aim_ref, bbre_ref, bbim_ref):
    lre = jnp.minimum(lre_ref[...], -1e-4)
    lim = lim_ref[...]
    dt = jnp.exp(ldt_ref[...])
    mag = jnp.exp(lre * dt)
    are = mag * jnp.cos(lim * dt)
    aim = mag * jnp.sin(lim * dt)
    are_ref[...] = are
    aim_ref[...] = aim
    nre = are - 1.0
    den = lre * lre + lim * lim
    fre = (nre * lre + aim * lim) / den
    fim = (aim * lre - nre * lim) / den
    bre = bre_ref[...]
    bim = bim_ref[...]
    bbre_ref[...] = fre * bre - fim * bim
    bbim_ref[...] = fre * bim + fim * bre


def _ssm_prep(lam_re, lam_im, log_dt, b_re, b_im):
    g, p = lam_re.shape
    hh = b_re.shape[-1]
    bt_re = jnp.transpose(b_re, (0, 2, 1))
    bt_im = jnp.transpose(b_im, (0, 2, 1))
    outs = pl.pallas_call(
        _ssm_prep_kernel,
        out_shape=[jax.ShapeDtypeStruct((g, 1, p), F32)] * 2 + [jax.ShapeDtypeStruct((g, hh, p), F32)] * 2,
        name="ssm_prep",
    )(lam_re.reshape(g, 1, p), lam_im.reshape(g, 1, p), log_dt.reshape(g, 1, 1), bt_re, bt_im)
    a_re, a_im, bb_re, bb_im = outs
    return a_re.reshape(g, p), a_im.reshape(g, p), bb_re, bb_im


def _proj_kernel(x_ref, pos_ref, gmix_ref, wu_ref, wcq_ref, wckv_ref, wkr_ref, qg_ref, wq_ref, kvg_ref,
                 wk_ref, wv_ref, invf_ref, u_ref, q_ref, k_ref, v_ref, *, sub):
    lane = lax.broadcasted_iota(jnp.int32, (1, LANES), 1)
    half = MLA_ROPE // 2
    ones_col = jnp.where(lane == MLA_V, 1.0, 0.0)
    for r0 in range(0, x_ref.shape[1], sub):
        rows = slice(r0, r0 + sub)
        h = _rms(x_ref[0, rows, :], gmix_ref[...]).astype(BF16)
        u_ref[0, rows, :] = _dot(h, wu_ref[...]).astype(BF16)
        cq = _dot(h, wcq_ref[...])
        q = _dot(_rms(cq, qg_ref[...]).astype(BF16), wq_ref[...]) * (MLA_QK ** -0.5 * LOG2E)
        ckv = _dot(h, wckv_ref[...])
        ckvn = _rms(ckv, kvg_ref[...]).astype(BF16)
        kk = _dot(ckvn, wk_ref[...])
        vv = _dot(ckvn, wv_ref[...])
        kr = _dot(h, wkr_ref[...])

        ang = pos_ref[0, rows, :].astype(F32) * invf_ref[...]
        cosv = jnp.cos(ang)
        sinv = jnp.sin(ang)
        c_tab = jnp.where(lane < MLA_NOPE, 1.0, cosv)
        s_lo = jnp.where((lane >= MLA_NOPE) & (lane < MLA_NOPE + half), -sinv, 0.0)
        s_hi = jnp.where((lane >= MLA_NOPE + half) & (lane < MLA_QK), sinv, 0.0)

        def rot(val, c_tab=c_tab, s_lo=s_lo, s_hi=s_hi):
            return (val * c_tab + pltpu.roll(val, LANES - half, 1) * s_lo + pltpu.roll(val, half, 1) * s_hi)

        kpe = rot(kr)
        for hh in range(MLA_HEADS):
            sl = slice(hh * LANES, (hh + 1) * LANES)
            q_ref[0, hh, :, rows] = rot(q[:, sl]).T.astype(BF16)
            k_ref[0, hh, rows, :] = (kk[:, sl] + kpe).astype(BF16)
            v_ref[0, hh, :, rows] = (vv[:, sl] + ones_col).T.astype(BF16)


def _proj(x, pos, gmix, wu, wcq, wckv, wkr, qg, wq, kvg, wk, wv, invf, tm, sub):
    b, s, d = x.shape
    grid = (b, s // tm)
    tok = lambda n: pl.BlockSpec((1, tm, n), lambda i, j: (i, j, 0))
    head = pl.BlockSpec((1, MLA_HEADS, tm, LANES), lambda i, j: (i, 0, j, 0))
    head_t = pl.BlockSpec((1, MLA_HEADS, LANES, tm), lambda i, j: (i, 0, 0, j))
    tshape = jax.ShapeDtypeStruct((b, MLA_HEADS, LANES, s), BF16)
    return pl.pallas_call(
        functools.partial(_proj_kernel, sub=sub),
        grid=grid,
        in_specs=[tok(d), tok(1), _full(gmix.shape), _full(wu.shape), _full(wcq.shape), _full(wckv.shape),
                  _full(wkr.shape), _full(qg.shape), _full(wq.shape), _full(kvg.shape), _full(wk.shape),
                  _full(wv.shape), _full(invf.shape)],
        out_specs=[tok(wu.shape[1]), head_t, head, head_t],
        out_shape=[jax.ShapeDtypeStruct((b, s, wu.shape[1]), BF16), tshape,
                   jax.ShapeDtypeStruct((b, MLA_HEADS, s, LANES), BF16), tshape],
        compiler_params=_params("parallel", "parallel"),
        name="proj",
    )(x, pos, gmix, wu, wcq, wckv, wkr, qg, wq, kvg, wk, wv, invf)


def _ssm_kernel(u_ref, bbd_ref, atab_ref, cmat_ref, dskip_ref, wglu_ref, bglu_ref, gout_ref, y_ref,
                xs_ref, st_ref, *, tt, slab, nb, lane_tiles):
    j = pl.program_id(0)
    half_w = u_ref.shape[-1] // 2
    n_state = lane_tiles * LANES

    @pl.when(j == 0)
    def _():
        st_ref[...] = jnp.zeros_like(st_ref)

    for hf in range(2):
        ub = u_ref[:, :, hf * half_w:(hf + 1) * half_w].reshape(nb * tt, half_w)
        bu = _dot(ub, bbd_ref[hf])
        for b in range(nb):
            r0 = (hf * nb + b) * slab
            for c in range(2 * lane_tiles):
                xs_ref[c, r0:r0 + tt, :] = bu[b * tt:(b + 1) * tt, c * LANES:(c + 1) * LANES]

    rows = 2 * nb
    group = 4
    for c0 in range(0, lane_tiles, group):
        cs = list(range(c0, c0 + group))
        a_re = [atab_ref[0, :, c * LANES:(c + 1) * LANES] for c in cs]
        a_im = [atab_ref[1, :, c * LANES:(c + 1) * LANES] for c in cs]
        init = tuple(st_ref[0, :, c * LANES:(c + 1) * LANES] for c in cs) + \
            tuple(st_ref[1, :, c * LANES:(c + 1) * LANES] for c in cs)

        def step(t, carry, cs=cs, a_re=a_re, a_im=a_im):
            new_re, new_im = [], []
            for i, c in enumerate(cs):
                x_re, x_im = carry[i], carry[group + i]
                idx = pl.ds(t, rows, stride=slab)
                n_re = a_re[i] * x_re - a_im[i] * x_im + xs_ref[c, idx, :]
                n_im = a_re[i] * x_im + a_im[i] * x_re + xs_ref[lane_tiles + c, idx, :]
                xs_ref[c, idx, :] = n_re
                xs_ref[lane_tiles + c, idx, :] = n_im
                new_re.append(n_re)
                new_im.append(n_im)
            return tuple(new_re) + tuple(new_im)

        fin = lax.fori_loop(0, tt, step, init, unroll=4)
        for i, c in enumerate(cs):
            st_ref[0, :, c * LANES:(c + 1) * LANES] = fin[i]
            st_ref[1, :, c * LANES:(c + 1) * LANES] = fin[group + i]

    ys = []
    for hf in range(2):
        xb = []
        for b in range(nb):
            r0 = (hf * nb + b) * slab
            xb.append(jnp.concatenate([xs_ref[c, r0:r0 + tt, :] for c in range(2 * lane_tiles)], axis=1))
        xh = jnp.concatenate(xb, axis=0).astype(BF16)
        ys.append(_dot(xh, cmat_ref[hf]))
    y = jnp.concatenate(ys, axis=1)
    u = u_ref[...].astype(F32).reshape(nb * tt, 2 * half_w)
    y = jax.nn.gelu(y + dskip_ref[...] * u)
    y = y * jax.nn.sigmoid(_dot(y.astype(BF16), wglu_ref[...]) + bglu_ref[...])
    y = _rms(y, gout_ref[...])
    y_ref[...] = y.reshape(nb, tt, 2 * half_w).astype(BF16)


def _ssm(u, bbd, atab, cmat, dskip, wglu, bglu, gout, tt):
    nb, s, dssm = u.shape
    lane_tiles = bbd.shape[-1] // (2 * LANES)
    slab = tt + 8
    kern = functools.partial(_ssm_kernel, tt=tt, slab=slab, nb=nb, lane_tiles=lane_tiles)
    return pl.pallas_call(
        kern,
        grid=(s // tt,),
        in_specs=[pl.BlockSpec((nb, tt, dssm), lambda j: (0, j, 0)), _full(bbd.shape), _full(atab.shape),
                  _full(cmat.shape), _full(dskip.shape), _full(wglu.shape), _full(bglu.shape), _full(gout.shape)],
        out_specs=pl.BlockSpec((nb, tt, dssm), lambda j: (0, j, 0)),
        out_shape=jax.ShapeDtypeStruct((nb, s, dssm), BF16),
        scratch_shapes=[pltpu.VMEM((2 * lane_tiles, 2 * nb * slab, LANES), F32),
                        pltpu.VMEM((2, 2 * nb, lane_tiles * LANES), F32)],
        compiler_params=_params("arbitrary"),
        name="ssm",
    )(u, bbd, atab, cmat, dskip, wglu, bglu, gout)


def _attn_kernel(qt_ref, k_ref, vt_ref, o_ref, s_ref, p_ref, m_ref, a_ref, acc_ref, *, tq, tk, hp, strip):
    qi = pl.program_id(2)
    m_ref[...] = jnp.full(m_ref.shape, NEG_INF, F32)
    acc_ref[...] = jnp.zeros(acc_ref.shape, F32)
    q_chunk = lax.broadcasted_iota(jnp.int32, (1, tq), 1) // CHUNK

    def tile(k0, diag_off):
        for hh in range(hp):
            s_ref[hh] = _dot(k_ref[0, hh, pl.ds(k0, tk), :], qt_ref[0, hh])

        def strip_of(hh, r):
            s = s_ref[hh, r:r + strip, :]
            if diag_off is None:
                return s
            return jnp.where(q_chunk >= (diag_off + r) // CHUNK, s, NEG_INF)

        for hh in range(hp):
            mt = strip_of(hh, 0)
            for r in range(strip, tk, strip):
                mt = jnp.maximum(mt, strip_of(hh, r))
            m_old = m_ref[hh]
            m_new = jnp.maximum(m_old, jnp.max(mt, axis=0, keepdims=True))
            a_ref[hh] = jnp.exp2(m_old - m_new)
            m_ref[hh] = m_new
            for r in range(0, tk, strip):
                p_ref[hh, r:r + strip, :] = jnp.exp2(strip_of(hh, r) - m_new).astype(BF16)
        for hh in range(hp):
            pv = _dot(vt_ref[0, hh, :, pl.ds(k0, tk)], p_ref[hh])
            acc_ref[hh] = acc_ref[hh] * a_ref[hh] + pv

    def body(j, c):
        tile(pl.multiple_of(j * tk, tk), None)
        return c

    lax.fori_loop(0, qi * (tq // tk), body, 0)
    for off in range(0, tq, tk):
        tile(pl.multiple_of(qi * tq + off, tk), off)
    feat = lax.broadcasted_iota(jnp.int32, (LANES, 1), 0)
    for hh in range(hp):
        acc = acc_ref[hh]
        out = acc * (1.0 / acc[MLA_V:MLA_V + 1, :])
        o_ref[0, hh] = jnp.where(feat < MLA_V, out, 0.0).T.astype(BF16)


def _attn(qt, k, vt, tq, tk, hp, strip):
    b, h, s, _ = k.shape
    assert strip <= CHUNK and CHUNK % strip == 0 and tk % CHUNK == 0 and tq % tk == 0
    return pl.pallas_call(
        functools.partial(_attn_kernel, tq=tq, tk=tk, hp=hp, strip=strip),
        grid=(b, h // hp, s // tq),
        in_specs=[pl.BlockSpec((1, hp, LANES, tq), lambda i, j, t: (i, j, 0, t)),
                  pl.BlockSpec((1, hp, s, LANES), lambda i, j, t: (i, j, 0, 0)),
                  pl.BlockSpec((1, hp, LANES, s), lambda i, j, t: (i, j, 0, 0))],
        out_specs=pl.BlockSpec((1, hp, tq, LANES), lambda i, j, t: (i, j, t, 0)),
        out_shape=jax.ShapeDtypeStruct((b, h, s, LANES), BF16),
        scratch_shapes=[pltpu.VMEM((hp, tk, tq), F32), pltpu.VMEM((hp, tk, tq), BF16),
                        pltpu.VMEM((hp, 1, tq), F32), pltpu.VMEM((hp, 1, tq), F32),
                        pltpu.VMEM((hp, LANES, tq), F32)],
        compiler_params=_params("parallel", "parallel", "arbitrary", vmem=VMEM_LIMIT_LARGE),
        name="attn",
    )(qt, k, vt)


def _memkv_kernel(mem_ref, g_ref, wk_ref, wv_ref, k_ref, v_ref, *, hd):
    mn = _rms(mem_ref[0], g_ref[...]).astype(BF16)
    kk = _dot(mn, wk_ref[...])
    vv = _dot(mn, wv_ref[...])
    for hh in range(XATTN_HEADS):
        k_ref[0, hh] = kk[:, hh * hd:(hh + 1) * hd].astype(BF16)
        v_ref[0, hh] = vv[:, hh * hd:(hh + 1) * hd].astype(BF16)


def _memkv(mem, g, wk, wv):
    b, nm, d = mem.shape
    hd = d // XATTN_HEADS
    ospec = pl.BlockSpec((1, XATTN_HEADS, nm, hd), lambda i: (i, 0, 0, 0))
    return pl.pallas_call(
        functools.partial(_memkv_kernel, hd=hd),
        grid=(b,),
        in_specs=[pl.BlockSpec((1, nm, d), lambda i: (i, 0, 0)), _full(g.shape), _full(wk.shape), _full(wv.shape)],
        out_specs=[ospec, ospec],
        out_shape=[jax.ShapeDtypeStruct((b, XATTN_HEADS, nm, hd), BF16)] * 2,
        compiler_params=_params("parallel"),
        name="mem_kv",
    )(mem, g, wk, wv)


def _post_kernel(x_ref, ys_ref, ym_ref, gm_ref, wos_ref, wom_ref, gx_ref, wq_ref, km_ref, vm_ref, wo_ref,
                 gmoe_ref, wrh_ref, wrl_ref, br_ref, x2_ref, h3_ref, route_ref, *, hd, sub):
    for r0 in range(0, x_ref.shape[1], sub):
        rows = slice(r0, r0 + sub)
        x = x_ref[0, rows, :]
        heads = [ym_ref[0, hh, rows, :].astype(F32) for hh in range(MLA_HEADS)]
        ym = jnp.concatenate([heads[k] + pltpu.roll(heads[k + 1], MLA_V, 1) for k in range(0, MLA_HEADS, 2)],
                             axis=1)
        ymn = _rms(ym, gm_ref[...]).astype(BF16)
        x1 = x + _dot(ys_ref[0, rows, :], wos_ref[...]) + _dot(ymn, wom_ref[...])

        h2 = _rms(x1, gx_ref[...]).astype(BF16)
        qx = (_dot(h2, wq_ref[...]) * (hd ** -0.5)).astype(BF16)
        outs = []
        for hh in range(XATTN_HEADS):
            s = _dot_nt(qx[:, hh * hd:(hh + 1) * hd], km_ref[0, hh])
            p = jnp.exp(s - jnp.max(s, axis=-1, keepdims=True))
            p = p * (1.0 / jnp.sum(p, axis=-1, keepdims=True))
            outs.append(_dot(p.astype(BF16), vm_ref[0, hh]))
        o = jnp.concatenate(outs, axis=1).astype(BF16)
        x2 = x1 + _dot(o, wo_ref[...])
        x2_ref[0, rows, :] = x2

        h3 = _rms(x2, gmoe_ref[...])
        h3_ref[0, rows, :] = h3

        h_hi = h3.astype(BF16)
        h_lo = (h3 - h_hi.astype(F32)).astype(BF16)
        logits = (_dot(h_hi, wrh_ref[...]) + _dot(h_hi, wrl_ref[...]) + _dot(h_lo, wrh_ref[...])) + br_ref[...]
        lane = lax.broadcasted_iota(jnp.int32, logits.shape, 1)
        is_g = (lane >= MOE_EXPERTS) & (lane < MOE_EXPERTS + MOE_GROUPS)
        gl = jnp.where(is_g, logits, NEG_INF)
        gmax = jnp.max(gl, axis=-1, keepdims=True)
        g_w = 1.0 / jnp.sum(jnp.exp(gl - gmax), axis=-1, keepdims=True)
        g_idx = jnp.min(jnp.where(gl == gmax, lane, 4 * LANES), axis=-1, keepdims=True) - MOE_EXPERTS
        in_grp = (lane >= g_idx * MOE_PER_GROUP) & (lane < (g_idx + 1) * MOE_PER_GROUP)
        el = jnp.where(in_grp, logits, NEG_INF)
        v1 = jnp.max(el, axis=-1, keepdims=True)
        i1 = jnp.min(jnp.where(el == v1, lane, 4 * LANES), axis=-1, keepdims=True)
        el2 = jnp.where(lane == i1, NEG_INF, el)
        v2 = jnp.max(el2, axis=-1, keepdims=True)
        i2 = jnp.min(jnp.where(el2 == v2, lane, 4 * LANES), axis=-1, keepdims=True)
        e2 = jnp.exp(v2 - v1)
        w1 = g_w / (1.0 + e2)
        w2 = g_w * e2 / (1.0 + e2)
        route_ref[0, rows, :] = (jnp.where(lane == i1, w1, 0.0) + jnp.where(lane == i2, w2, 0.0)
                                 + jnp.where(lane == GROUP_LANE, g_idx.astype(F32), 0.0))


def _post(x, ys, ym, gm, wos, wom, gx, wq, km, vm, wo, gmoe, wrh, wrl, br, tm, sub):
    b, s, d = x.shape
    hd = d // XATTN_HEADS
    tok = lambda n: pl.BlockSpec((1, tm, n), lambda i, j: (i, j, 0))
    mem = pl.BlockSpec((1,) + km.shape[1:], lambda i, j: (i, 0, 0, 0))
    return pl.pallas_call(
        functools.partial(_post_kernel, hd=hd, sub=sub),
        grid=(b, s // tm),
        in_specs=[tok(d), tok(ys.shape[-1]), pl.BlockSpec((1, MLA_HEADS, tm, LANES), lambda i, j: (i, 0, j, 0)),
                  _full(gm.shape), _full(wos.shape), _full(wom.shape), _full(gx.shape), _full(wq.shape), mem, mem,
                  _full(wo.shape), _full(gmoe.shape), _full(wrh.shape), _full(wrl.shape), _full(br.shape)],
        out_specs=[tok(d), tok(d), tok(LANES)],
        out_shape=[jax.ShapeDtypeStruct((b, s, d), F32), jax.ShapeDtypeStruct((b, s, d), F32),
                   jax.ShapeDtypeStruct((b, s, LANES), F32)],
        compiler_params=_params("parallel", "parallel"),
        name="post",
    )(x, ys, ym, gm, wos, wom, gx, wq, km, vm, wo, gmoe, wrh, wrl, br)


def _route_metadata(route, tm):
    b, s, _ = route.shape
    n_tiles = (s + MOE_GROUPS * tm) // tm
    gid = route[..., GROUP_LANE].astype(jnp.int32)
    onehot = (gid[..., None] == jnp.arange(MOE_GROUPS, dtype=jnp.int32)).astype(jnp.int32)
    csum = jnp.cumsum(onehot, axis=1)
    cnt = csum[:, -1, :]
    padded = (cnt + tm - 1) // tm * tm
    seg_end = jnp.cumsum(padded, axis=-1)
    dest = jnp.sum(onehot * ((seg_end - padded)[:, None, :] + csum - 1), axis=-1)
    tile_start = jnp.arange(n_tiles, dtype=jnp.int32) * tm
    tile_g = jnp.sum((seg_end[:, None, :] <= tile_start[None, :, None]).astype(jnp.int32), axis=-1)
    tile_g = jnp.minimum(tile_g, MOE_GROUPS - 1)
    n_valid = (seg_end[:, -1] // tm).astype(jnp.int32)
    return dict(pos=dest, tile_g=tile_g.reshape(b * n_tiles), n_valid=n_valid, n_pad=n_tiles * tm)


def _dispatch_kernel(h_ref, g_ref, dst_ref, xs_ref, gs_ref, *, tt):
    @pl.when(pl.program_id(1) == 0)
    def _():
        xs_ref[...] = jnp.zeros(xs_ref.shape, xs_ref.dtype)
        gs_ref[...] = jnp.zeros(gs_ref.shape, gs_ref.dtype)

    def scatter(t, c):
        r = dst_ref[0, 0, t]
        xs_ref[0, pl.ds(r, 1), :] = h_ref[0, pl.ds(t, 1), :]
        gs_ref[0, pl.ds(r, 1), :] = g_ref[0, pl.ds(t, 1), :]
        return c

    lax.fori_loop(0, tt, scatter, 0, unroll=8)


def _dispatch(h3, route, pos, n_pad, tt):
    b, s, d = h3.shape
    nj = s // tt
    row_block = lambda n: pl.BlockSpec((1, n_pad, n), lambda i, j: (i, 0, 0), pipeline_mode=pl.Buffered(1))
    return pl.pallas_call(
        functools.partial(_dispatch_kernel, tt=tt),
        grid=(b, nj),
        in_specs=[pl.BlockSpec((1, tt, d), lambda i, j: (i, j, 0)),
                  pl.BlockSpec((1, tt, LANES), lambda i, j: (i, j, 0)),
                  pl.BlockSpec((1, 1, tt), lambda i, j: (i * nj + j, 0, 0), memory_space=pltpu.SMEM)],
        out_specs=[row_block(d), row_block(LANES)],
        out_shape=[jax.ShapeDtypeStruct((b, n_pad, d), F32), jax.ShapeDtypeStruct((b, n_pad, LANES), F32)],
        compiler_params=_params("parallel", "arbitrary"),
        name="dispatch",
    )(h3, route, pos.reshape(b * nj, 1, tt))


def _moe_kernel(tg_ref, nv_ref, h_ref, gate_ref, wg_ref, wu_ref, wd_ref, o_ref, *, nt):
    bi = pl.program_id(0)
    ti = pl.program_id(1)
    grp = tg_ref[bi * nt + ti]

    @pl.when(ti < nv_ref[bi])
    def _():
        h = h_ref[0].astype(BF16)
        gate = gate_ref[0]
        lane = lax.broadcasted_iota(jnp.int32, gate.shape, 1)
        acc = None
        for e in range(MOE_PER_GROUP):
            a = jax.nn.silu(_dot(h, wg_ref[e])) * _dot(h, wu_ref[e])
            gcol = jnp.sum(jnp.where(lane == grp * MOE_PER_GROUP + e, gate, 0.0), axis=-1, keepdims=True)
            y = _dot((a * gcol).astype(BF16), wd_ref[e])
            acc = y if acc is None else acc + y
        o_ref[0] = acc

    @pl.when(ti >= nv_ref[bi])
    def _():
        o_ref[0] = jnp.zeros(o_ref.shape[1:], o_ref.dtype)


def _moe(xs, gs, tile_g, n_valid, wg, wu, wd, tm):
    b, n_pad, d = xs.shape
    nt = n_pad // tm
    ff = wg.shape[-1]
    group = lambda i, j, tg, nv: (tg[i * nt + j], 0, 0)
    tok = lambda n: pl.BlockSpec((1, tm, n), lambda i, j, tg, nv: (i, j, 0))
    grid_spec = pltpu.PrefetchScalarGridSpec(
        num_scalar_prefetch=2,
        grid=(b, nt),
        in_specs=[tok(d), tok(LANES), pl.BlockSpec((MOE_PER_GROUP, d, ff), group),
                  pl.BlockSpec((MOE_PER_GROUP, d, ff), group), pl.BlockSpec((MOE_PER_GROUP, ff, d), group)],
        out_specs=tok(d),
    )
    return pl.pallas_call(
        functools.partial(_moe_kernel, nt=nt),
        grid_spec=grid_spec,
        out_shape=jax.ShapeDtypeStruct((b, n_pad, d), F32),
        compiler_params=_params("parallel", "arbitrary"),
        name="moe",
    )(tile_g, n_valid, xs, gs, wg, wu, wd)


def _combine_kernel(ys_ref, pos_ref, x2_ref, gf_ref, o_ref, buf_ref, *, tt):
    def gather(t, c):
        buf_ref[pl.ds(t, 1), :] = ys_ref[0, pl.ds(pos_ref[0, 0, t], 1), :]
        return c

    lax.fori_loop(0, tt, gather, 0, unroll=8)
    o_ref[0] = _rms(x2_ref[0] + buf_ref[...], gf_ref[...])


def _combine(ys, pos, x2, gf, tt):
    b, s, d = x2.shape
    n_pad = ys.shape[1]
    nj = s // tt
    return pl.pallas_call(
        functools.partial(_combine_kernel, tt=tt),
        grid=(b, nj),
        in_specs=[pl.BlockSpec((1, n_pad, d), lambda i, j: (i, 0, 0), pipeline_mode=pl.Buffered(1)),
                  pl.BlockSpec((1, 1, tt), lambda i, j: (i * nj + j, 0, 0), memory_space=pltpu.SMEM),
                  pl.BlockSpec((1, tt, d), lambda i, j: (i, j, 0)), _full(gf.shape)],
        out_specs=pl.BlockSpec((1, tt, d), lambda i, j: (i, j, 0)),
        out_shape=jax.ShapeDtypeStruct((b, s, d), F32),
        scratch_shapes=[pltpu.VMEM((tt, d), F32)],
        compiler_params=_params("parallel", "arbitrary"),
        name="combine",
    )(ys, pos.reshape(b * nj, 1, tt), x2, gf)


def _pad_heads(w, per_head, offset=0):
    k = w.shape[0]
    w = w.reshape(k, MLA_HEADS, per_head)
    w = jnp.pad(w, ((0, 0), (0, 0), (offset, LANES - per_head - offset)))
    return w.reshape(k, MLA_HEADS * LANES)


def _block_diag(blocks):
    n, r, c = blocks.shape
    eye = jnp.eye(n, dtype=blocks.dtype)
    return (eye[:, None, :, None] * blocks[:, :, None, :]).reshape(n * r, n * c)


def kernel(x, mem, positions, norm_mix_g, w_in, ssm_lam_re, ssm_lam_im, ssm_log_dt, ssm_b_re, ssm_b_im, ssm_c_re, ssm_c_im, ssm_d, ssm_w_glu, ssm_b_glu, mla_q_norm_g, mla_w_q_up, mla_kv_norm_g, mla_w_kv_up, out_norm_ssm_g, out_norm_mla_g, w_out, norm_xattn_g, norm_mem_g, xattn_w_q, xattn_w_k, xattn_w_v, xattn_w_o, norm_moe_g, moe_w_group, moe_b_group, moe_w_expert, moe_b_expert, moe_w_gate, moe_w_up, moe_w_down, norm_final_g):
    bsz, seq, d = x.shape
    depth = w_in.shape[0]
    assert depth == 1, "the final RMSNorm is fused into the last stage of a single layer"
    d_ssm = ssm_d.shape[-1]
    q_rank = mla_q_norm_g.shape[-1]
    kv_rank = mla_kv_norm_g.shape[-1]
    n_grp = d_ssm // SSM_GROUP
    s1, s2, s3 = d_ssm, d_ssm + q_rank, d_ssm + q_rank + kv_rank
    row = lambda v: v.reshape(1, -1).astype(F32)

    half = MLA_ROPE // 2
    inv_freq = ROPE_THETA ** (-jnp.arange(half, dtype=F32) / half)
    invf = jnp.zeros((LANES,), F32).at[MLA_NOPE:MLA_NOPE + half].set(inv_freq)
    invf = invf.at[MLA_NOPE + half:MLA_QK].set(inv_freq).reshape(1, LANES)
    pos = positions.reshape(bsz, seq, 1)

    for l in range(depth):
        a_re, a_im, bb_re, bb_im = _ssm_prep(ssm_lam_re[l], ssm_lam_im[l], ssm_log_dt[l], ssm_b_re[l], ssm_b_im[l])
        gh = n_grp // 2
        bbd = jnp.stack([
            jnp.concatenate([_block_diag(bb_re[hf * gh:(hf + 1) * gh]), _block_diag(bb_im[hf * gh:(hf + 1) * gh])],
                            axis=1) for hf in range(2)]).astype(BF16)
        c_re_t = jnp.transpose(ssm_c_re[l], (0, 2, 1))
        c_im_t = jnp.transpose(ssm_c_im[l], (0, 2, 1))
        cmat = jnp.stack([
            jnp.concatenate([_block_diag(c_re_t[hf * gh:(hf + 1) * gh]), -_block_diag(c_im_t[hf * gh:(hf + 1) * gh])],
                            axis=0) for hf in range(2)]).astype(BF16)
        atab = jnp.stack([
            jnp.repeat(arr.reshape(2, 1, gh * SSM_STATE), bsz, axis=1).reshape(2 * bsz, gh * SSM_STATE)
            for arr in (a_re, a_im)])

        wi = w_in[l]
        wu = wi[:, :s1].astype(BF16)
        wcq = wi[:, s1:s2].astype(BF16)
        wckv = wi[:, s2:s3].astype(BF16)
        wkr = jnp.pad(wi[:, s3:], ((0, 0), (MLA_NOPE, LANES - MLA_QK))).astype(BF16)
        wq = _pad_heads(mla_w_q_up[l], MLA_QK).astype(BF16)
        wkv = mla_w_kv_up[l].reshape(kv_rank, MLA_HEADS, MLA_NOPE + MLA_V)
        wk = _pad_heads(wkv[:, :, :MLA_NOPE].reshape(kv_rank, -1), MLA_NOPE).astype(BF16)
        wv = _pad_heads(wkv[:, :, MLA_NOPE:].reshape(kv_rank, -1), MLA_V).astype(BF16)
        u, q, k, v = _proj(x, pos, row(norm_mix_g[l]), wu, wcq, wckv, wkr, row(mla_q_norm_g[l]), wq,
                           row(mla_kv_norm_g[l]), wk, wv, invf, tm=min(512, seq), sub=min(512, seq))

        y_ssm = _ssm(u, bbd, atab, cmat, row(ssm_d[l]), ssm_w_glu[l].astype(BF16), row(ssm_b_glu[l]),
                     row(out_norm_ssm_g[l]), tt=min(128, seq))
        y_mla = _attn(q, k, v, tq=min(512, seq), tk=min(256, seq), hp=8, strip=32)

        km, vm = _memkv(mem, row(norm_mem_g[l]), xattn_w_k[l].astype(BF16), xattn_w_v[l].astype(BF16))
        wos = w_out[l][:d_ssm].astype(BF16)
        wom = w_out[l][d_ssm:].astype(BF16)
        wr = jnp.concatenate([moe_w_expert[l].reshape(d, MOE_EXPERTS), moe_w_group[l]], axis=1)
        wr = jnp.pad(wr, ((0, 0), (0, LANES - MOE_EXPERTS - MOE_GROUPS))).astype(F32)
        wr_hi = wr.astype(BF16)
        wr_lo = (wr - wr_hi.astype(F32)).astype(BF16)
        br = jnp.concatenate([moe_b_expert[l].reshape(-1), moe_b_group[l]])
        br = jnp.pad(br, (0, LANES - MOE_EXPERTS - MOE_GROUPS)).reshape(1, LANES).astype(F32)
        x2, h3, route = _post(x, y_ssm, y_mla, row(out_norm_mla_g[l]), wos, wom, row(norm_xattn_g[l]),
                              xattn_w_q[l].astype(BF16), km, vm, xattn_w_o[l].astype(BF16), row(norm_moe_g[l]),
                              wr_hi, wr_lo, br, tm=min(512, seq), sub=min(256, seq))

        meta = _route_metadata(route, MOE_TILE)
        xs, gs = _dispatch(h3, route, meta["pos"], meta["n_pad"], tt=min(256, seq))
        ys = _moe(xs, gs, meta["tile_g"], meta["n_valid"], moe_w_gate[l].astype(BF16), moe_w_up[l].astype(BF16),
                  moe_w_down[l].astype(BF16), tm=MOE_TILE)
        x = _combine(ys, meta["pos"], x2, row(norm_final_g), tt=min(256, seq))
    return x
```

```python
import functools
import math

import jax
import jax.numpy as jnp
from jax import lax
from jax.experimental import pallas as pl
from jax.experimental.pallas import tpu as pltpu

F32 = jnp.float32
BF16 = jnp.bfloat16

EPS = 1e-6
NEG_INF = -1e30
CHUNK = 64

LANES = 128
SSM_GROUP = 16
SSM_STATE = 64
MLA_HEADS = 8
MLA_NOPE = 64
MLA_ROPE = 32
MLA_QK = MLA_NOPE + MLA_ROPE
MLA_V = 64
ROPE_THETA = 10000.0
XATTN_HEADS = 4
MOE_GROUPS = 4
MOE_PER_GROUP = 8
MOE_EXPERTS = MOE_GROUPS * MOE_PER_GROUP
VMEM_LIMIT = 48 * 1024 * 1024
VMEM_LIMIT_LARGE = 56 * 1024 * 1024
LOG2E = math.log2(math.e)
MOE_TILE = 256
GROUP_LANE = 64


def _dot(a, b):
    return jnp.dot(a, b, preferred_element_type=F32)


def _dot_nt(a, b):
    return lax.dot_general(a, b, (((1,), (1,)), ((), ())), preferred_element_type=F32)


def _rms(x, g, n=None):
    n = x.shape[-1] if n is None else n
    ms = jnp.sum(x * x, axis=-1, keepdims=True) * (1.0 / n)
    return x * lax.rsqrt(ms + EPS) * g


def _params(*sem, vmem=VMEM_LIMIT, flags=None):
    return pltpu.CompilerParams(dimension_semantics=sem, vmem_limit_bytes=vmem, flags=flags)


def _full(shape):
    zeros = (0,) * len(shape)
    return pl.BlockSpec(shape, lambda *_: zeros)


def _ssm_prep_kernel(lre_ref, lim_ref, ldt_ref, bre_ref, bim_ref, are_ref, aim_ref, bbre_ref, bbim_ref):
    lre = jnp.minimum(lre_ref[...], -1e-4)
    lim = lim_ref[...]
    dt = jnp.exp(ldt_ref[...])
    mag = jnp.exp(lre * dt)
    are = mag * jnp.cos(lim * dt)
    aim = mag * jnp.sin(lim * dt)
    are_ref[...] = are
    aim_ref[...] = aim
    nre = are - 1.0
    den = lre * lre + lim * lim
    fre = (nre * lre + aim * lim) / den
    fim = (aim * lre - nre * lim) / den
    bre = bre_ref[...]
    bim = bim_ref[...]
    bbre_ref[...] = fre * bre - fim * bim
    bbim_ref[...] = fre * bim + fim * bre


def _ssm_prep(lam_re, lam_im, log_dt, b_re, b_im):
    g, p = lam_re.shape
    hh = b_re.shape[-1]
    bt_re = jnp.transpose(b_re, (0, 2, 1))
    bt_im = jnp.transpose(b_im, (0, 2, 1))
    outs = pl.pallas_call(
        _ssm_prep_kernel,
        out_shape=[jax.ShapeDtypeStruct((g, 1, p), F32)] * 2 + [jax.ShapeDtypeStruct((g, hh, p), F32)] * 2,
        name="ssm_prep",
    )(lam_re.reshape(g, 1, p), lam_im.reshape(g, 1, p), log_dt.reshape(g, 1, 1), bt_re, bt_im)
    a_re, a_im, bb_re, bb_im = outs
    return a_re.reshape(g, p), a_im.reshape(g, p), bb_re, bb_im


def _proj_kernel(x_ref, pos_ref, gmix_ref, wu_ref, wcq_ref, wckv_ref, wkr_ref, qg_ref, wq_ref, kvg_ref,
                 wk_ref, wv_ref, invf_ref, u_ref, q_ref, k_ref, v_ref, *, sub):
    lane = lax.broadcasted_iota(jnp.int32, (1, LANES), 1)
    half = MLA_ROPE // 2
    ones_col = jnp.where(lane == MLA_V, 1.0, 0.0)
    for r0 in range(0, x_ref.shape[1], sub):
        rows = slice(r0, r0 + sub)
        h = _rms(x_ref[0, rows, :], gmix_ref[...]).astype(BF16)
        u_ref[0, rows, :] = _dot(h, wu_ref[...]).astype(BF16)
        cq = _dot(h, wcq_ref[...])
        q = _dot(_rms(cq, qg_ref[...]).astype(BF16), wq_ref[...]) * (MLA_QK ** -0.5 * LOG2E)
        ckv = _dot(h, wckv_ref[...])
        ckvn = _rms(ckv, kvg_ref[...]).astype(BF16)
        kk = _dot(ckvn, wk_ref[...])
        vv = _dot(ckvn, wv_ref[...])
        kr = _dot(h, wkr_ref[...])

        ang = pos_ref[0, rows, :].astype(F32) * invf_ref[...]
        cosv = jnp.cos(ang)
        sinv = jnp.sin(ang)
        c_tab = jnp.where(lane < MLA_NOPE, 1.0, cosv)
        s_lo = jnp.where((lane >= MLA_NOPE) & (lane < MLA_NOPE + half), -sinv, 0.0)
        s_hi = jnp.where((lane >= MLA_NOPE + half) & (lane < MLA_QK), sinv, 0.0)

        def rot(val, c_tab=c_tab, s_lo=s_lo, s_hi=s_hi):
            return (val * c_tab + pltpu.roll(val, LANES - half, 1) * s_lo + pltpu.roll(val, half, 1) * s_hi)

        kpe = rot(kr)
        for hh in range(MLA_HEADS):
            sl = slice(hh * LANES, (hh + 1) * LANES)
            q_ref[0, hh, :, rows] = rot(q[:, sl]).T.astype(BF16)
            k_ref[0, hh, rows, :] = (kk[:, sl] + kpe).astype(BF16)
            v_ref[0, hh, :, rows] = (vv[:, sl] + ones_col).T.astype(BF16)


def _proj(x, pos, gmix, wu, wcq, wckv, wkr, qg, wq, kvg, wk, wv, invf, tm, sub):
    b, s, d = x.shape
    grid = (b, s // tm)
    tok = lambda n: pl.BlockSpec((1, tm, n), lambda i, j: (i, j, 0))
    head = pl.BlockSpec((1, MLA_HEADS, tm, LANES), lambda i, j: (i, 0, j, 0))
    head_t = pl.BlockSpec((1, MLA_HEADS, LANES, tm), lambda i, j: (i, 0, 0, j))
    tshape = jax.ShapeDtypeStruct((b, MLA_HEADS, LANES, s), BF16)
    return pl.pallas_call(
        functools.partial(_proj_kernel, sub=sub),
        grid=grid,
        in_specs=[tok(d), tok(1), _full(gmix.shape), _full(wu.shape), _full(wcq.shape), _full(wckv.shape),
                  _full(wkr.shape), _full(qg.shape), _full(wq.shape), _full(kvg.shape), _full(wk.shape),
                  _full(wv.shape), _full(invf.shape)],
        out_specs=[tok(wu.shape[1]), head_t, head, head_t],
        out_shape=[jax.ShapeDtypeStruct((b, s, wu.shape[1]), BF16), tshape,
                   jax.ShapeDtypeStruct((b, MLA_HEADS, s, LANES), BF16), tshape],
        compiler_params=_params("parallel", "parallel"),
        name="proj",
    )(x, pos, gmix, wu, wcq, wckv, wkr, qg, wq, kvg, wk, wv, invf)


def _ssm_kernel(u_ref, bbd_ref, atab_ref, cmat_ref, dskip_ref, wglu_ref, bglu_ref, gout_ref, y_ref,
                xs_ref, st_ref, *, tt, slab, nb, lane_tiles):
    j = pl.program_id(0)
    half_w = u_ref.shape[-1] // 2
    n_state = lane_tiles * LANES

    @pl.when(j == 0)
    def _():
        st_ref[...] = jnp.zeros_like(st_ref)

    for hf in range(2):
        ub = u_ref[:, :, hf * half_w:(hf + 1) * half_w].reshape(nb * tt, half_w)
        bu = _dot(ub, bbd_ref[hf])
        for b in range(nb):
            r0 = (hf * nb + b) * slab
            for c in range(2 * lane_tiles):
                xs_ref[c, r0:r0 + tt, :] = bu[b * tt:(b + 1) * tt, c * LANES:(c + 1) * LANES]

    rows = 2 * nb
    group = 4
    for c0 in range(0, lane_tiles, group):
        cs = list(range(c0, c0 + group))
        a_re = [atab_ref[0, :, c * LANES:(c + 1) * LANES] for c in cs]
        a_im = [atab_ref[1, :, c * LANES:(c + 1) * LANES] for c in cs]
        init = tuple(st_ref[0, :, c * LANES:(c + 1) * LANES] for c in cs) + \
            tuple(st_ref[1, :, c * LANES:(c + 1) * LANES] for c in cs)

        def step(t, carry, cs=cs, a_re=a_re, a_im=a_im):
            new_re, new_im = [], []
            for i, c in enumerate(cs):
                x_re, x_im = carry[i], carry[group + i]
                idx = pl.ds(t, rows, stride=slab)
                n_re = a_re[i] * x_re - a_im[i] * x_im + xs_ref[c, idx, :]
                n_im = a_re[i] * x_im + a_im[i] * x_re + xs_ref[lane_tiles + c, idx, :]
                xs_ref[c, idx, :] = n_re
                xs_ref[lane_tiles + c, idx, :] = n_im
                new_re.append(n_re)
                new_im.append(n_im)
            return tuple(new_re) + tuple(new_im)

        fin = lax.fori_loop(0, tt, step, init, unroll=4)
        for i, c in enumerate(cs):
            st_ref[0, :, c * LANES:(c + 1) * LANES] = fin[i]
            st_ref[1, :, c * LANES:(c + 1) * LANES] = fin[group + i]

    ys = []
    for hf in range(2):
        xb = []
        for b in range(nb):
            r0 = (hf * nb + b) * slab
            xb.append(jnp.concatenate([xs_ref[c, r0:r0 + tt, :] for c in range(2 * lane_tiles)], axis=1))
        xh = jnp.concatenate(xb, axis=0).astype(BF16)
        ys.append(_dot(xh, cmat_ref[hf]))
    y = jnp.concatenate(ys, axis=1)
    u = u_ref[...].astype(F32).reshape(nb * tt, 2 * half_w)
    y = jax.nn.gelu(y + dskip_ref[...] * u)
    y = y * jax.nn.sigmoid(_dot(y.astype(BF16), wglu_ref[...]) + bglu_ref[...])
    y = _rms(y, gout_ref[...])
    y_ref[...] = y.reshape(nb, tt, 2 * half_w).astype(BF16)


def _ssm(u, bbd, atab, cmat, dskip, wglu, bglu, gout, tt):
    nb, s, dssm = u.shape
    lane_tiles = bbd.shape[-1] // (2 * LANES)
    slab = tt + 8
    kern = functools.partial(_ssm_kernel, tt=tt, slab=slab, nb=nb, lane_tiles=lane_tiles)
    return pl.pallas_call(
        kern,
        grid=(s // tt,),
        in_specs=[pl.BlockSpec((nb, tt, dssm), lambda j: (0, j, 0)), _full(bbd.shape), _full(atab.shape),
                  _full(cmat.shape), _full(dskip.shape), _full(wglu.shape), _full(bglu.shape), _full(gout.shape)],
        out_specs=pl.BlockSpec((nb, tt, dssm), lambda j: (0, j, 0)),
        out_shape=jax.ShapeDtypeStruct((nb, s, dssm), BF16),
        scratch_shapes=[pltpu.VMEM((2 * lane_tiles, 2 * nb * slab, LANES), F32),
                        pltpu.VMEM((2, 2 * nb, lane_tiles * LANES), F32)],
        compiler_params=_params("arbitrary"),
        name="ssm",
    )(u, bbd, atab, cmat, dskip, wglu, bglu, gout)


def _attn_kernel(qt_ref, k_ref, vt_ref, o_ref, s_ref, p_ref, m_ref, a_ref, acc_ref, *, tq, tk, hp, strip):
    qi = pl.program_id(2)
    m_ref[...] = jnp.full(m_ref.shape, NEG_INF, F32)
    acc_ref[...] = jnp.zeros(acc_ref.shape, F32)
    q_chunk = lax.broadcasted_iota(jnp.int32, (1, tq), 1) // CHUNK

    def tile(k0, diag_off):
        for hh in range(hp):
            s_ref[hh] = _dot(k_ref[0, hh, pl.ds(k0, tk), :], qt_ref[0, hh])

        def strip_of(hh, r):
            s = s_ref[hh, r:r + strip, :]
            if diag_off is None:
                return s
            return jnp.where(q_chunk >= (diag_off + r) // CHUNK, s, NEG_INF)

        for hh in range(hp):
            mt = strip_of(hh, 0)
            for r in range(strip, tk, strip):
                mt = jnp.maximum(mt, strip_of(hh, r))
            m_old = m_ref[hh]
            m_new = jnp.maximum(m_old, jnp.max(mt, axis=0, keepdims=True))
            a_ref[hh] = jnp.exp2(m_old - m_new)
            m_ref[hh] = m_new
            for r in range(0, tk, strip):
                p_ref[hh, r:r + strip, :] = jnp.exp2(strip_of(hh, r) - m_new).astype(BF16)
        for hh in range(hp):
            pv = _dot(vt_ref[0, hh, :, pl.ds(k0, tk)], p_ref[hh])
            acc_ref[hh] = acc_ref[hh] * a_ref[hh] + pv

    def body(j, c):
        tile(pl.multiple_of(j * tk, tk), None)
        return c

    lax.fori_loop(0, qi * (tq // tk), body, 0)
    for off in range(0, tq, tk):
        tile(pl.multiple_of(qi * tq + off, tk), off)
    feat = lax.broadcasted_iota(jnp.int32, (LANES, 1), 0)
    for hh in range(hp):
        acc = acc_ref[hh]
        out = acc * (1.0 / acc[MLA_V:MLA_V + 1, :])
        o_ref[0, hh] = jnp.where(feat < MLA_V, out, 0.0).T.astype(BF16)


def _attn(qt, k, vt, tq, tk, hp, strip):
    b, h, s, _ = k.shape
    assert strip <= CHUNK and CHUNK % strip == 0 and tk % CHUNK == 0 and tq % tk == 0
    return pl.pallas_call(
        functools.partial(_attn_kernel, tq=tq, tk=tk, hp=hp, strip=strip),
        grid=(b, h // hp, s // tq),
        in_specs=[pl.BlockSpec((1, hp, LANES, tq), lambda i, j, t: (i, j, 0, t)),
                  pl.BlockSpec((1, hp, s, LANES), lambda i, j, t: (i, j, 0, 0)),
                  pl.BlockSpec((1, hp, LANES, s), lambda i, j, t: (i, j, 0, 0))],
        out_specs=pl.BlockSpec((1, hp, tq, LANES), lambda i, j, t: (i, j, t, 0)),
        out_shape=jax.ShapeDtypeStruct((b, h, s, LANES), BF16),
        scratch_shapes=[pltpu.VMEM((hp, tk, tq), F32), pltpu.VMEM((hp, tk, tq), BF16),
                        pltpu.VMEM((hp, 1, tq), F32), pltpu.VMEM((hp, 1, tq), F32),
                        pltpu.VMEM((hp, LANES, tq), F32)],
        compiler_params=_params("parallel", "parallel", "arbitrary", vmem=VMEM_LIMIT_LARGE),
        name="attn",
    )(qt, k, vt)


def _memkv_kernel(mem_ref, g_ref, wk_ref, wv_ref, k_ref, v_ref, *, hd):
    mn = _rms(mem_ref[0], g_ref[...]).astype(BF16)
    kk = _dot(mn, wk_ref[...])
    vv = _dot(mn, wv_ref[...])
    for hh in range(XATTN_HEADS):
        k_ref[0, hh] = kk[:, hh * hd:(hh + 1) * hd].astype(BF16)
        v_ref[0, hh] = vv[:, hh * hd:(hh + 1) * hd].astype(BF16)


def _memkv(mem, g, wk, wv):
    b, nm, d = mem.shape
    hd = d // XATTN_HEADS
    ospec = pl.BlockSpec((1, XATTN_HEADS, nm, hd), lambda i: (i, 0, 0, 0))
    return pl.pallas_call(
        functools.partial(_memkv_kernel, hd=hd),
        grid=(b,),
        in_specs=[pl.BlockSpec((1, nm, d), lambda i: (i, 0, 0)), _full(g.shape), _full(wk.shape), _full(wv.shape)],
        out_specs=[ospec, ospec],
        out_shape=[jax.ShapeDtypeStruct((b, XATTN_HEADS, nm, hd), BF16)] * 2,
        compiler_params=_params("parallel"),
        name="mem_kv",
    )(mem, g, wk, wv)


def _post_kernel(x_ref, ys_ref, ym_ref, gm_ref, wos_ref, wom_ref, gx_ref, wq_ref, km_ref, vm_ref, wo_ref,
                 gmoe_ref, wrh_ref, wrl_ref, br_ref, x2_ref, h3_ref, route_ref, *, hd, sub):
    for r0 in range(0, x_ref.shape[1], sub):
        rows = slice(r0, r0 + sub)
        x = x_ref[0, rows, :]
        heads = [ym_ref[0, hh, rows, :].astype(F32) for hh in range(MLA_HEADS)]
        ym = jnp.concatenate([heads[k] + pltpu.roll(heads[k + 1], MLA_V, 1) for k in range(0, MLA_HEADS, 2)],
                             axis=1)
        ymn = _rms(ym, gm_ref[...]).astype(BF16)
        x1 = x + _dot(ys_ref[0, rows, :], wos_ref[...]) + _dot(ymn, wom_ref[...])

        h2 = _rms(x1, gx_ref[...]).astype(BF16)
        qx = (_dot(h2, wq_ref[...]) * (hd ** -0.5)).astype(BF16)
        outs = []
        for hh in range(XATTN_HEADS):
            s = _dot_nt(qx[:, hh * hd:(hh + 1) * hd], km_ref[0, hh])
            p = jnp.exp(s - jnp.max(s, axis=-1, keepdims=True))
            p = p * (1.0 / jnp.sum(p, axis=-1, keepdims=True))
            outs.append(_dot(p.astype(BF16), vm_ref[0, hh]))
        o = jnp.concatenate(outs, axis=1).astype(BF16)
        x2 = x1 + _dot(o, wo_ref[...])
        x2_ref[0, rows, :] = x2

        h3 = _rms(x2, gmoe_ref[...])
        h3_ref[0, rows, :] = h3

        h_hi = h3.astype(BF16)
        h_lo = (h3 - h_hi.astype(F32)).astype(BF16)
        logits = (_dot(h_hi, wrh_ref[...]) + _dot(h_hi, wrl_ref[...]) + _dot(h_lo, wrh_ref[...])) + br_ref[...]
        lane = lax.broadcasted_iota(jnp.int32, logits.shape, 1)
        is_g = (lane >= MOE_EXPERTS) & (lane < MOE_EXPERTS + MOE_GROUPS)
        gl = jnp.where(is_g, logits, NEG_INF)
        gmax = jnp.max(gl, axis=-1, keepdims=True)
        g_w = 1.0 / jnp.sum(jnp.exp(gl - gmax), axis=-1, keepdims=True)
        g_idx = jnp.min(jnp.where(gl == gmax, lane, 4 * LANES), axis=-1, keepdims=True) - MOE_EXPERTS
        in_grp = (lane >= g_idx * MOE_PER_GROUP) & (lane < (g_idx + 1) * MOE_PER_GROUP)
        el = jnp.where(in_grp, logits, NEG_INF)
        v1 = jnp.max(el, axis=-1, keepdims=True)
        i1 = jnp.min(jnp.where(el == v1, lane, 4 * LANES), axis=-1, keepdims=True)
        el2 = jnp.where(lane == i1, NEG_INF, el)
        v2 = jnp.max(el2, axis=-1, keepdims=True)
        i2 = jnp.min(jnp.where(el2 == v2, lane, 4 * LANES), axis=-1, keepdims=True)
        e2 = jnp.exp(v2 - v1)
        w1 = g_w / (1.0 + e2)
        w2 = g_w * e2 / (1.0 + e2)
        route_ref[0, rows, :] = (jnp.where(lane == i1, w1, 0.0) + jnp.where(lane == i2, w2, 0.0)
                                 + jnp.where(lane == GROUP_LANE, g_idx.astype(F32), 0.0))


def _post(x, ys, ym, gm, wos, wom, gx, wq, km, vm, wo, gmoe, wrh, wrl, br, tm, sub):
    b, s, d = x.shape
    hd = d // XATTN_HEADS
    tok = lambda n: pl.BlockSpec((1, tm, n), lambda i, j: (i, j, 0))
    mem = pl.BlockSpec((1,) + km.shape[1:], lambda i, j: (i, 0, 0, 0))
    return pl.pallas_call(
        functools.partial(_post_kernel, hd=hd, sub=sub),
        grid=(b, s // tm),
        in_specs=[tok(d), tok(ys.shape[-1]), pl.BlockSpec((1, MLA_HEADS, tm, LANES), lambda i, j: (i, 0, j, 0)),
                  _full(gm.shape), _full(wos.shape), _full(wom.shape), _full(gx.shape), _full(wq.shape), mem, mem,
                  _full(wo.shape), _full(gmoe.shape), _full(wrh.shape), _full(wrl.shape), _full(br.shape)],
        out_specs=[tok(d), tok(d), tok(LANES)],
        out_shape=[jax.ShapeDtypeStruct((b, s, d), F32), jax.ShapeDtypeStruct((b, s, d), F32),
                   jax.ShapeDtypeStruct((b, s, LANES), F32)],
        compiler_params=_params("parallel", "parallel"),
        name="post",
    )(x, ys, ym, gm, wos, wom, gx, wq, km, vm, wo, gmoe, wrh, wrl, br)


def _route_metadata(route, tm):
    b, s, _ = route.shape
    n_tiles = (s + MOE_GROUPS * tm) // tm
    gid = route[..., GROUP_LANE].astype(jnp.int32)
    groups = jnp.arange(MOE_GROUPS, dtype=jnp.int32)[None, :, None]
    onehot = (gid[:, None, :] == groups).astype(jnp.int32)
    csum = jnp.cumsum(onehot, axis=2)
    cnt = csum[:, :, -1]
    padded = (cnt + tm - 1) // tm * tm
    seg_end = jnp.cumsum(padded, axis=-1)
    dest = jnp.sum(onehot * ((seg_end - padded)[:, :, None] + csum - 1), axis=1)
    tile_start = jnp.arange(n_tiles, dtype=jnp.int32) * tm
    tile_g = jnp.sum((seg_end[:, None, :] <= tile_start[None, :, None]).astype(jnp.int32), axis=-1)
    tile_g = jnp.minimum(tile_g, MOE_GROUPS - 1)
    n_valid = (seg_end[:, -1] // tm).astype(jnp.int32)
    return dict(pos=dest, tile_g=tile_g.reshape(b * n_tiles), n_valid=n_valid, n_pad=n_tiles * tm)


def _dispatch_kernel(h_ref, g_ref, dst_ref, xs_ref, gs_ref, *, tt):
    @pl.when(pl.program_id(1) == 0)
    def _():
        xs_ref[...] = jnp.zeros(xs_ref.shape, xs_ref.dtype)
        gs_ref[...] = jnp.zeros(gs_ref.shape, gs_ref.dtype)

    for t in range(tt):
        r = dst_ref[0, 0, t]
        xs_ref[0, pl.ds(r, 1), :] = h_ref[0, t:t + 1, :]
        gs_ref[0, pl.ds(r, 1), :] = g_ref[0, t:t + 1, :]


def _dispatch(h3, route, pos, n_pad, tt):
    b, s, d = h3.shape
    nj = s // tt
    row_block = lambda n: pl.BlockSpec((1, n_pad, n), lambda i, j: (i, 0, 0), pipeline_mode=pl.Buffered(1))
    return pl.pallas_call(
        functools.partial(_dispatch_kernel, tt=tt),
        grid=(b, nj),
        in_specs=[pl.BlockSpec((1, tt, d), lambda i, j: (i, j, 0)),
                  pl.BlockSpec((1, tt, LANES), lambda i, j: (i, j, 0)),
                  pl.BlockSpec((1, 1, tt), lambda i, j: (i * nj + j, 0, 0), memory_space=pltpu.SMEM)],
        out_specs=[row_block(d), row_block(LANES)],
        out_shape=[jax.ShapeDtypeStruct((b, n_pad, d), F32), jax.ShapeDtypeStruct((b, n_pad, LANES), F32)],
        compiler_params=_params("parallel", "arbitrary"),
        name="dispatch",
    )(h3, route, pos.reshape(b * nj, 1, tt))


def _moe_kernel(tg_ref, nv_ref, h_ref, gate_ref, wg_ref, wu_ref, wd_ref, o_ref, *, nt):
    bi = pl.program_id(0)
    ti = pl.program_id(1)
    grp = tg_ref[bi * nt + ti]

    @pl.when(ti < nv_ref[bi])
    def _():
        h = h_ref[0].astype(BF16)
        gate = gate_ref[0]
        lane = lax.broadcasted_iota(jnp.int32, gate.shape, 1)
        acc = None
        for e in range(MOE_PER_GROUP):
            a = jax.nn.silu(_dot(h, wg_ref[e])) * _dot(h, wu_ref[e])
            gcol = jnp.sum(jnp.where(lane == grp * MOE_PER_GROUP + e, gate, 0.0), axis=-1, keepdims=True)
            y = _dot((a * gcol).astype(BF16), wd_ref[e])
            acc = y if acc is None else acc + y
        o_ref[0] = acc

    @pl.when(ti >= nv_ref[bi])
    def _():
        o_ref[0] = jnp.zeros(o_ref.shape[1:], o_ref.dtype)


def _moe(xs, gs, tile_g, n_valid, wg, wu, wd, tm):
    b, n_pad, d = xs.shape
    nt = n_pad // tm
    ff = wg.shape[-1]
    group = lambda i, j, tg, nv: (tg[i * nt + j], 0, 0)
    tok = lambda n: pl.BlockSpec((1, tm, n), lambda i, j, tg, nv: (i, j, 0))
    grid_spec = pltpu.PrefetchScalarGridSpec(
        num_scalar_prefetch=2,
        grid=(b, nt),
        in_specs=[tok(d), tok(LANES), pl.BlockSpec((MOE_PER_GROUP, d, ff), group),
                  pl.BlockSpec((MOE_PER_GROUP, d, ff), group), pl.BlockSpec((MOE_PER_GROUP, ff, d), group)],
        out_specs=tok(d),
    )
    return pl.pallas_call(
        functools.partial(_moe_kernel, nt=nt),
        grid_spec=grid_spec,
        out_shape=jax.ShapeDtypeStruct((b, n_pad, d), F32),
        compiler_params=_params("parallel", "arbitrary"),
        name="moe",
    )(tile_g, n_valid, xs, gs, wg, wu, wd)


def _combine_kernel(ys_ref, pos_ref, x2_ref, gf_ref, o_ref, buf_ref, *, tt):
    for t in range(tt):
        buf_ref[t:t + 1, :] = ys_ref[0, pl.ds(pos_ref[0, 0, t], 1), :]
    o_ref[0] = _rms(x2_ref[0] + buf_ref[...], gf_ref[...])


def _combine(ys, pos, x2, gf, tt):
    b, s, d = x2.shape
    n_pad = ys.shape[1]
    nj = s // tt
    return pl.pallas_call(
        functools.partial(_combine_kernel, tt=tt),
        grid=(b, nj),
        in_specs=[pl.BlockSpec((1, n_pad, d), lambda i, j: (i, 0, 0), pipeline_mode=pl.Buffered(1)),
                  pl.BlockSpec((1, 1, tt), lambda i, j: (i * nj + j, 0, 0), memory_space=pltpu.SMEM),
                  pl.BlockSpec((1, tt, d), lambda i, j: (i, j, 0)), _full(gf.shape)],
        out_specs=pl.BlockSpec((1, tt, d), lambda i, j: (i, j, 0)),
        out_shape=jax.ShapeDtypeStruct((b, s, d), F32),
        scratch_shapes=[pltpu.VMEM((tt, d), F32)],
        compiler_params=_params("parallel", "arbitrary"),
        name="combine",
    )(ys, pos.reshape(b * nj, 1, tt), x2, gf)


def _pad_heads(w, per_head, offset=0):
    k = w.shape[0]
    w = w.reshape(k, MLA_HEADS, per_head)
    w = jnp.pad(w, ((0, 0), (0, 0), (offset, LANES - per_head - offset)))
    return w.reshape(k, MLA_HEADS * LANES)


def _block_diag(blocks):
    n, r, c = blocks.shape
    eye = jnp.eye(n, dtype=blocks.dtype)
    return (eye[:, None, :, None] * blocks[:, :, None, :]).reshape(n * r, n * c)


def kernel(x, mem, positions, norm_mix_g, w_in, ssm_lam_re, ssm_lam_im, ssm_log_dt, ssm_b_re, ssm_b_im, ssm_c_re, ssm_c_im, ssm_d, ssm_w_glu, ssm_b_glu, mla_q_norm_g, mla_w_q_up, mla_kv_norm_g, mla_w_kv_up, out_norm_ssm_g, out_norm_mla_g, w_out, norm_xattn_g, norm_mem_g, xattn_w_q, xattn_w_k, xattn_w_v, xattn_w_o, norm_moe_g, moe_w_group, moe_b_group, moe_w_expert, moe_b_expert, moe_w_gate, moe_w_up, moe_w_down, norm_final_g):
    bsz, seq, d = x.shape
    depth = w_in.shape[0]
    assert depth == 1, "the final RMSNorm is fused into the last stage of a single layer"
    d_ssm = ssm_d.shape[-1]
    q_rank = mla_q_norm_g.shape[-1]
    kv_rank = mla_kv_norm_g.shape[-1]
    n_grp = d_ssm // SSM_GROUP
    s1, s2, s3 = d_ssm, d_ssm + q_rank, d_ssm + q_rank + kv_rank
    row = lambda v: v.reshape(1, -1).astype(F32)

    half = MLA_ROPE // 2
    inv_freq = ROPE_THETA ** (-jnp.arange(half, dtype=F32) / half)
    invf = jnp.zeros((LANES,), F32).at[MLA_NOPE:MLA_NOPE + half].set(inv_freq)
    invf = invf.at[MLA_NOPE + half:MLA_QK].set(inv_freq).reshape(1, LANES)
    pos = positions.reshape(bsz, seq, 1)

    for l in range(depth):
        a_re, a_im, bb_re, bb_im = _ssm_prep(ssm_lam_re[l], ssm_lam_im[l], ssm_log_dt[l], ssm_b_re[l], ssm_b_im[l])
        gh = n_grp // 2
        bbd = jnp.stack([
            jnp.concatenate([_block_diag(bb_re[hf * gh:(hf + 1) * gh]), _block_diag(bb_im[hf * gh:(hf + 1) * gh])],
                            axis=1) for hf in range(2)]).astype(BF16)
        c_re_t = jnp.transpose(ssm_c_re[l], (0, 2, 1))
        c_im_t = jnp.transpose(ssm_c_im[l], (0, 2, 1))
        cmat = jnp.stack([
            jnp.concatenate([_block_diag(c_re_t[hf * gh:(hf + 1) * gh]), -_block_diag(c_im_t[hf * gh:(hf + 1) * gh])],
                            axis=0) for hf in range(2)]).astype(BF16)
        atab = jnp.stack([
            jnp.repeat(arr.reshape(2, 1, gh * SSM_STATE), bsz, axis=1).reshape(2 * bsz, gh * SSM_STATE)
            for arr in (a_re, a_im)])

        wi = w_in[l]
        wu = wi[:, :s1].astype(BF16)
        wcq = wi[:, s1:s2].astype(BF16)
        wckv = wi[:, s2:s3].astype(BF16)
        wkr = jnp.pad(wi[:, s3:], ((0, 0), (MLA_NOPE, LANES - MLA_QK))).astype(BF16)
        wq = _pad_heads(mla_w_q_up[l], MLA_QK).astype(BF16)
        wkv = mla_w_kv_up[l].reshape(kv_rank, MLA_HEADS, MLA_NOPE + MLA_V)
        wk = _pad_heads(wkv[:, :, :MLA_NOPE].reshape(kv_rank, -1), MLA_NOPE).astype(BF16)
        wv = _pad_heads(wkv[:, :, MLA_NOPE:].reshape(kv_rank, -1), MLA_V).astype(BF16)
        u, q, k, v = _proj(x, pos, row(norm_mix_g[l]), wu, wcq, wckv, wkr, row(mla_q_norm_g[l]), wq,
                           row(mla_kv_norm_g[l]), wk, wv, invf, tm=min(512, seq), sub=min(512, seq))

        y_ssm = _ssm(u, bbd, atab, cmat, row(ssm_d[l]), ssm_w_glu[l].astype(BF16), row(ssm_b_glu[l]),
                     row(out_norm_ssm_g[l]), tt=min(128, seq))
        y_mla = _attn(q, k, v, tq=min(512, seq), tk=min(256, seq), hp=8, strip=32)

        km, vm = _memkv(mem, row(norm_mem_g[l]), xattn_w_k[l].astype(BF16), xattn_w_v[l].astype(BF16))
        wos = w_out[l][:d_ssm].astype(BF16)
        wom = w_out[l][d_ssm:].astype(BF16)
        wr = jnp.concatenate([moe_w_expert[l].reshape(d, MOE_EXPERTS), moe_w_group[l]], axis=1)
        wr = jnp.pad(wr, ((0, 0), (0, LANES - MOE_EXPERTS - MOE_GROUPS))).astype(F32)
        wr_hi = wr.astype(BF16)
        wr_lo = (wr - wr_hi.astype(F32)).astype(BF16)
        br = jnp.concatenate([moe_b_expert[l].reshape(-1), moe_b_group[l]])
        br = jnp.pad(br, (0, LANES - MOE_EXPERTS - MOE_GROUPS)).reshape(1, LANES).astype(F32)
        x2, h3, route = _post(x, y_ssm, y_mla, row(out_norm_mla_g[l]), wos, wom, row(norm_xattn_g[l]),
                              xattn_w_q[l].astype(BF16), km, vm, xattn_w_o[l].astype(BF16), row(norm_moe_g[l]),
                              wr_hi, wr_lo, br, tm=min(512, seq), sub=min(256, seq))

        meta = _route_metadata(route, MOE_TILE)
        xs, gs = _dispatch(h3, route, meta["pos"], meta["n_pad"], tt=min(256, seq))
        ys = _moe(xs, gs, meta["tile_g"], meta["n_valid"], moe_w_gate[l].astype(BF16), moe_w_up[l].astype(BF16),
                  moe_w_down[l].astype(BF16), tm=MOE_TILE)
        x = _combine(ys, meta["pos"], x2, row(norm_final_g), tt=min(256, seq))
    return x
```

```python
import functools
import math

import jax
import jax.numpy as jnp
from jax import lax
from jax.experimental import pallas as pl
from jax.experimental.pallas import tpu as pltpu

F32 = jnp.float32
BF16 = jnp.bfloat16

EPS = 1e-6
NEG_INF = -1e30
CHUNK = 64

LANES = 128
SSM_GROUP = 16
SSM_STATE = 64
MLA_HEADS = 8
MLA_NOPE = 64
MLA_ROPE = 32
MLA_QK = MLA_NOPE + MLA_ROPE
MLA_V = 64
ROPE_THETA = 10000.0
XATTN_HEADS = 4
MOE_GROUPS = 4
MOE_PER_GROUP = 8
MOE_EXPERTS = MOE_GROUPS * MOE_PER_GROUP
VMEM_LIMIT = 48 * 1024 * 1024
VMEM_LIMIT_LARGE = 56 * 1024 * 1024
LOG2E = math.log2(math.e)
VT_ROWS = 80
MOE_TILE = 256
GROUP_LANE = 64


def _dot(a, b):
    return jnp.dot(a, b, preferred_element_type=F32)


def _dot_nt(a, b):
    return lax.dot_general(a, b, (((1,), (1,)), ((), ())), preferred_element_type=F32)


def _rms(x, g, n=None):
    n = x.shape[-1] if n is None else n
    ms = jnp.sum(x * x, axis=-1, keepdims=True) * (1.0 / n)
    return x * lax.rsqrt(ms + EPS) * g


def _params(*sem, vmem=VMEM_LIMIT, flags=None):
    return pltpu.CompilerParams(dimension_semantics=sem, vmem_limit_bytes=vmem, flags=flags)


def _full(shape):
    zeros = (0,) * len(shape)
    return pl.BlockSpec(shape, lambda *_: zeros)


def _ssm_prep_kernel(lre_ref, lim_ref, ldt_ref, bre_ref, bim_ref, are_ref, aim_ref, bbre_ref, bbim_ref):
    lre = jnp.minimum(lre_ref[...], -1e-4)
    lim = lim_ref[...]
    dt = jnp.exp(ldt_ref[...])
    mag = jnp.exp(lre * dt)
    are = mag * jnp.cos(lim * dt)
    aim = mag * jnp.sin(lim * dt)
    are_ref[...] = are
    aim_ref[...] = aim
    nre = are - 1.0
    den = lre * lre + lim * lim
    fre = (nre * lre + aim * lim) / den
    fim = (aim * lre - nre * lim) / den
    bre = bre_ref[...]
    bim = bim_ref[...]
    bbre_ref[...] = fre * bre - fim * bim
    bbim_ref[...] = fre * bim + fim * bre


def _ssm_prep(lam_re, lam_im, log_dt, b_re, b_im):
    g, p = lam_re.shape
    hh = b_re.shape[-1]
    bt_re = jnp.transpose(b_re, (0, 2, 1))
    bt_im = jnp.transpose(b_im, (0, 2, 1))
    outs = pl.pallas_call(
        _ssm_prep_kernel,
        out_shape=[jax.ShapeDtypeStruct((g, 1, p), F32)] * 2 + [jax.ShapeDtypeStruct((g, hh, p), F32)] * 2,
        name="ssm_prep",
    )(lam_re.reshape(g, 1, p), lam_im.reshape(g, 1, p), log_dt.reshape(g, 1, 1), bt_re, bt_im)
    a_re, a_im, bb_re, bb_im = outs
    return a_re.reshape(g, p), a_im.reshape(g, p), bb_re, bb_im


def _proj_kernel(x_ref, pos_ref, gmix_ref, wu_ref, wcq_ref, wckv_ref, wkr_ref, qg_ref, wq_ref, kvg_ref,
                 wk_ref, wv_ref, invf_ref, u_ref, q_ref, k_ref, v_ref, *, sub):
    lane = lax.broadcasted_iota(jnp.int32, (1, LANES), 1)
    half = MLA_ROPE // 2
    ones_col = jnp.where(lane == MLA_V, 1.0, 0.0)
    for r0 in range(0, x_ref.shape[1], sub):
        rows = slice(r0, r0 + sub)
        h = _rms(x_ref[0, rows, :], gmix_ref[...]).astype(BF16)
        u_ref[0, rows, :] = _dot(h, wu_ref[...]).astype(BF16)
        cq = _dot(h, wcq_ref[...])
        q = _dot(_rms(cq, qg_ref[...]).astype(BF16), wq_ref[...]) * (MLA_QK ** -0.5 * LOG2E)
        ckv = _dot(h, wckv_ref[...])
        ckvn = _rms(ckv, kvg_ref[...]).astype(BF16)
        kk = _dot(ckvn, wk_ref[...])
        vv = _dot(ckvn, wv_ref[...])
        kr = _dot(h, wkr_ref[...])

        ang = pos_ref[0, rows, :].astype(F32) * invf_ref[...]
        cosv = jnp.cos(ang)
        sinv = jnp.sin(ang)
        c_tab = jnp.where(lane < MLA_NOPE, 1.0, cosv)
        s_lo = jnp.where((lane >= MLA_NOPE) & (lane < MLA_NOPE + half), -sinv, 0.0)
        s_hi = jnp.where((lane >= MLA_NOPE + half) & (lane < MLA_QK), sinv, 0.0)

        def rot(val, c_tab=c_tab, s_lo=s_lo, s_hi=s_hi):
            return (val * c_tab + pltpu.roll(val, LANES - half, 1) * s_lo + pltpu.roll(val, half, 1) * s_hi)

        kpe = rot(kr)
        for hh in range(MLA_HEADS):
            sl = slice(hh * LANES, (hh + 1) * LANES)
            q_ref[0, hh, :, rows] = rot(q[:, sl]).T.astype(BF16)
            k_ref[0, hh, rows, :] = (kk[:, sl] + kpe).astype(BF16)
            v_ref[0, hh, :, rows] = (vv[:, sl] + ones_col).T[:VT_ROWS].astype(BF16)


def _proj(x, pos, gmix, wu, wcq, wckv, wkr, qg, wq, kvg, wk, wv, invf, tm, sub):
    b, s, d = x.shape
    grid = (b, s // tm)
    tok = lambda n: pl.BlockSpec((1, tm, n), lambda i, j: (i, j, 0))
    head = pl.BlockSpec((1, MLA_HEADS, tm, LANES), lambda i, j: (i, 0, j, 0))
    head_t = pl.BlockSpec((1, MLA_HEADS, LANES, tm), lambda i, j: (i, 0, 0, j))
    head_vt = pl.BlockSpec((1, MLA_HEADS, VT_ROWS, tm), lambda i, j: (i, 0, 0, j))
    tshape = lambda n: jax.ShapeDtypeStruct((b, MLA_HEADS, n, s), BF16)
    return pl.pallas_call(
        functools.partial(_proj_kernel, sub=sub),
        grid=grid,
        in_specs=[tok(d), tok(1), _full(gmix.shape), _full(wu.shape), _full(wcq.shape), _full(wckv.shape),
                  _full(wkr.shape), _full(qg.shape), _full(wq.shape), _full(kvg.shape), _full(wk.shape),
                  _full(wv.shape), _full(invf.shape)],
        out_specs=[tok(wu.shape[1]), head_t, head, head_vt],
        out_shape=[jax.ShapeDtypeStruct((b, s, wu.shape[1]), BF16), tshape(LANES),
                   jax.ShapeDtypeStruct((b, MLA_HEADS, s, LANES), BF16), tshape(VT_ROWS)],
        compiler_params=_params("parallel", "parallel"),
        name="proj",
    )(x, pos, gmix, wu, wcq, wckv, wkr, qg, wq, kvg, wk, wv, invf)


def _ssm_kernel(u_ref, bbd_ref, atab_ref, cmat_ref, dskip_ref, wglu_ref, bglu_ref, gout_ref, y_ref,
                xs_ref, st_ref, *, tt, slab, nb, lane_tiles):
    j = pl.program_id(0)
    half_w = u_ref.shape[-1] // 2
    n_state = lane_tiles * LANES

    @pl.when(j == 0)
    def _():
        st_ref[...] = jnp.zeros_like(st_ref)

    for hf in range(2):
        ub = u_ref[:, :, hf * half_w:(hf + 1) * half_w].reshape(nb * tt, half_w)
        bu = _dot(ub, bbd_ref[hf])
        for b in range(nb):
            r0 = (hf * nb + b) * slab
            for c in range(2 * lane_tiles):
                xs_ref[c, r0:r0 + tt, :] = bu[b * tt:(b + 1) * tt, c * LANES:(c + 1) * LANES]

    rows = 2 * nb
    group = 4
    for c0 in range(0, lane_tiles, group):
        cs = list(range(c0, c0 + group))
        a_re = [atab_ref[0, :, c * LANES:(c + 1) * LANES] for c in cs]
        a_im = [atab_ref[1, :, c * LANES:(c + 1) * LANES] for c in cs]
        init = tuple(st_ref[0, :, c * LANES:(c + 1) * LANES] for c in cs) + \
            tuple(st_ref[1, :, c * LANES:(c + 1) * LANES] for c in cs)

        def step(t, carry, cs=cs, a_re=a_re, a_im=a_im):
            new_re, new_im = [], []
            for i, c in enumerate(cs):
                x_re, x_im = carry[i], carry[group + i]
                idx = pl.ds(t, rows, stride=slab)
                n_re = a_re[i] * x_re - a_im[i] * x_im + xs_ref[c, idx, :]
                n_im = a_re[i] * x_im + a_im[i] * x_re + xs_ref[lane_tiles + c, idx, :]
                xs_ref[c, idx, :] = n_re
                xs_ref[lane_tiles + c, idx, :] = n_im
                new_re.append(n_re)
                new_im.append(n_im)
            return tuple(new_re) + tuple(new_im)

        fin = lax.fori_loop(0, tt, step, init, unroll=4)
        for i, c in enumerate(cs):
            st_ref[0, :, c * LANES:(c + 1) * LANES] = fin[i]
            st_ref[1, :, c * LANES:(c + 1) * LANES] = fin[group + i]

    ys = []
    for hf in range(2):
        xb = []
        for b in range(nb):
            r0 = (hf * nb + b) * slab
            xb.append(jnp.concatenate([xs_ref[c, r0:r0 + tt, :] for c in range(2 * lane_tiles)], axis=1))
        xh = jnp.concatenate(xb, axis=0).astype(BF16)
        ys.append(_dot(xh, cmat_ref[hf]))
    y = jnp.concatenate(ys, axis=1)
    u = u_ref[...].astype(F32).reshape(nb * tt, 2 * half_w)
    y = jax.nn.gelu(y + dskip_ref[...] * u)
    y = y * jax.nn.sigmoid(_dot(y.astype(BF16), wglu_ref[...]) + bglu_ref[...])
    y = _rms(y, gout_ref[...])
    y_ref[...] = y.reshape(nb, tt, 2 * half_w).astype(BF16)


def _ssm(u, bbd, atab, cmat, dskip, wglu, bglu, gout, tt):
    nb, s, dssm = u.shape
    lane_tiles = bbd.shape[-1] // (2 * LANES)
    slab = tt + 4
    kern = functools.partial(_ssm_kernel, tt=tt, slab=slab, nb=nb, lane_tiles=lane_tiles)
    return pl.pallas_call(
        kern,
        grid=(s // tt,),
        in_specs=[pl.BlockSpec((nb, tt, dssm), lambda j: (0, j, 0)), _full(bbd.shape), _full(atab.shape),
                  _full(cmat.shape), _full(dskip.shape), _full(wglu.shape), _full(bglu.shape), _full(gout.shape)],
        out_specs=pl.BlockSpec((nb, tt, dssm), lambda j: (0, j, 0)),
        out_shape=jax.ShapeDtypeStruct((nb, s, dssm), BF16),
        scratch_shapes=[pltpu.VMEM((2 * lane_tiles, 2 * nb * slab, LANES), F32),
                        pltpu.VMEM((2, 2 * nb, lane_tiles * LANES), F32)],
        compiler_params=_params("arbitrary"),
        name="ssm",
    )(u, bbd, atab, cmat, dskip, wglu, bglu, gout)


def _attn_kernel(qt_ref, k_ref, vt_ref, o_ref, s_ref, p_ref, m_ref, a_ref, acc_ref, *, tq, tk, hp, strip):
    qi = pl.program_id(2)
    m_ref[...] = jnp.full(m_ref.shape, NEG_INF, F32)
    acc_ref[...] = jnp.zeros(acc_ref.shape, F32)
    q_chunk = lax.broadcasted_iota(jnp.int32, (1, tq), 1) // CHUNK

    def tile(k0, diag_off):
        for hh in range(hp):
            s_ref[hh] = _dot(k_ref[0, hh, pl.ds(k0, tk), :], qt_ref[0, hh])

        def strip_of(hh, r):
            s = s_ref[hh, r:r + strip, :]
            if diag_off is None:
                return s
            return jnp.where(q_chunk >= (diag_off + r) // CHUNK, s, NEG_INF)

        for hh in range(hp):
            mt = strip_of(hh, 0)
            for r in range(strip, tk, strip):
                mt = jnp.maximum(mt, strip_of(hh, r))
            m_old = m_ref[hh]
            m_new = jnp.maximum(m_old, jnp.max(mt, axis=0, keepdims=True))
            a_ref[hh] = jnp.exp2(m_old - m_new)
            m_ref[hh] = m_new
            for r in range(0, tk, strip):
                p_ref[hh, r:r + strip, :] = jnp.exp2(strip_of(hh, r) - m_new).astype(BF16)
        for hh in range(hp):
            pv = _dot(vt_ref[0, hh, :, pl.ds(k0, tk)], p_ref[hh])
            acc_ref[hh] = acc_ref[hh] * a_ref[hh] + pv

    def body(j, c):
        tile(pl.multiple_of(j * tk, tk), None)
        return c

    lax.fori_loop(0, qi * (tq // tk), body, 0)
    for off in range(0, tq, tk):
        tile(pl.multiple_of(qi * tq + off, tk), off)
    feat = lax.broadcasted_iota(jnp.int32, (VT_ROWS, 1), 0)
    pad = jnp.zeros((LANES - VT_ROWS, tq), F32)
    for hh in range(hp):
        acc = acc_ref[hh]
        out = acc * (1.0 / acc[MLA_V:MLA_V + 1, :])
        out = jnp.where(feat < MLA_V, out, 0.0)
        o_ref[0, hh] = jnp.concatenate([out, pad], axis=0).T.astype(BF16)


def _attn(qt, k, vt, tq, tk, hp, strip):
    b, h, s, _ = k.shape
    assert strip <= CHUNK and CHUNK % strip == 0 and tk % CHUNK == 0 and tq % tk == 0
    return pl.pallas_call(
        functools.partial(_attn_kernel, tq=tq, tk=tk, hp=hp, strip=strip),
        grid=(b, h // hp, s // tq),
        in_specs=[pl.BlockSpec((1, hp, LANES, tq), lambda i, j, t: (i, j, 0, t)),
                  pl.BlockSpec((1, hp, s, LANES), lambda i, j, t: (i, j, 0, 0)),
                  pl.BlockSpec((1, hp, VT_ROWS, s), lambda i, j, t: (i, j, 0, 0))],
        out_specs=pl.BlockSpec((1, hp, tq, LANES), lambda i, j, t: (i, j, t, 0)),
        out_shape=jax.ShapeDtypeStruct((b, h, s, LANES), BF16),
        scratch_shapes=[pltpu.VMEM((hp, tk, tq), F32), pltpu.VMEM((hp, tk, tq), BF16),
                        pltpu.VMEM((hp, 1, tq), F32), pltpu.VMEM((hp, 1, tq), F32),
                        pltpu.VMEM((hp, VT_ROWS, tq), F32)],
        compiler_params=_params("parallel", "parallel", "arbitrary", vmem=VMEM_LIMIT_LARGE),
        name="attn",
    )(qt, k, vt)


def _memkv_kernel(mem_ref, g_ref, wk_ref, wv_ref, k_ref, v_ref, *, hd):
    mn = _rms(mem_ref[0], g_ref[...]).astype(BF16)
    kk = _dot(mn, wk_ref[...])
    vv = _dot(mn, wv_ref[...])
    for hh in range(XATTN_HEADS):
        k_ref[0, hh] = kk[:, hh * hd:(hh + 1) * hd].astype(BF16)
        v_ref[0, hh] = vv[:, hh * hd:(hh + 1) * hd].astype(BF16)


def _memkv(mem, g, wk, wv):
    b, nm, d = mem.shape
    hd = d // XATTN_HEADS
    ospec = pl.BlockSpec((1, XATTN_HEADS, nm, hd), lambda i: (i, 0, 0, 0))
    return pl.pallas_call(
        functools.partial(_memkv_kernel, hd=hd),
        grid=(b,),
        in_specs=[pl.BlockSpec((1, nm, d), lambda i: (i, 0, 0)), _full(g.shape), _full(wk.shape), _full(wv.shape)],
        out_specs=[ospec, ospec],
        out_shape=[jax.ShapeDtypeStruct((b, XATTN_HEADS, nm, hd), BF16)] * 2,
        compiler_params=_params("parallel"),
        name="mem_kv",
    )(mem, g, wk, wv)


def _post_kernel(x_ref, ys_ref, ym_ref, gm_ref, wos_ref, wom_ref, gx_ref, wq_ref, km_ref, vm_ref, wo_ref,
                 gmoe_ref, wrh_ref, wrl_ref, br_ref, x2_ref, h3_ref, route_ref, *, hd, sub):
    for r0 in range(0, x_ref.shape[1], sub):
        rows = slice(r0, r0 + sub)
        x = x_ref[0, rows, :]
        heads = [ym_ref[0, hh, rows, :].astype(F32) for hh in range(MLA_HEADS)]
        ym = jnp.concatenate([heads[k] + pltpu.roll(heads[k + 1], MLA_V, 1) for k in range(0, MLA_HEADS, 2)],
                             axis=1)
        ymn = _rms(ym, gm_ref[...]).astype(BF16)
        x1 = x + _dot(ys_ref[0, rows, :], wos_ref[...]) + _dot(ymn, wom_ref[...])

        h2 = _rms(x1, gx_ref[...]).astype(BF16)
        qx = (_dot(h2, wq_ref[...]) * (hd ** -0.5)).astype(BF16)
        outs = []
        for hh in range(XATTN_HEADS):
            s = _dot_nt(qx[:, hh * hd:(hh + 1) * hd], km_ref[0, hh])
            p = jnp.exp(s - jnp.max(s, axis=-1, keepdims=True))
            p = p * (1.0 / jnp.sum(p, axis=-1, keepdims=True))
            outs.append(_dot(p.astype(BF16), vm_ref[0, hh]))
        o = jnp.concatenate(outs, axis=1).astype(BF16)
        x2 = x1 + _dot(o, wo_ref[...])
        x2_ref[0, rows, :] = x2

        h3 = _rms(x2, gmoe_ref[...])
        h3_ref[0, rows, :] = h3

        h_hi = h3.astype(BF16)
        h_lo = (h3 - h_hi.astype(F32)).astype(BF16)
        logits = (_dot(h_hi, wrh_ref[...]) + _dot(h_hi, wrl_ref[...]) + _dot(h_lo, wrh_ref[...])) + br_ref[...]
        lane = lax.broadcasted_iota(jnp.int32, logits.shape, 1)
        is_g = (lane >= MOE_EXPERTS) & (lane < MOE_EXPERTS + MOE_GROUPS)
        gl = jnp.where(is_g, logits, NEG_INF)
        gmax = jnp.max(gl, axis=-1, keepdims=True)
        g_w = 1.0 / jnp.sum(jnp.exp(gl - gmax), axis=-1, keepdims=True)
        g_idx = jnp.min(jnp.where(gl == gmax, lane, 4 * LANES), axis=-1, keepdims=True) - MOE_EXPERTS
        in_grp = (lane >= g_idx * MOE_PER_GROUP) & (lane < (g_idx + 1) * MOE_PER_GROUP)
        el = jnp.where(in_grp, logits, NEG_INF)
        v1 = jnp.max(el, axis=-1, keepdims=True)
        i1 = jnp.min(jnp.where(el == v1, lane, 4 * LANES), axis=-1, keepdims=True)
        el2 = jnp.where(lane == i1, NEG_INF, el)
        v2 = jnp.max(el2, axis=-1, keepdims=True)
        i2 = jnp.min(jnp.where(el2 == v2, lane, 4 * LANES), axis=-1, keepdims=True)
        e2 = jnp.exp(v2 - v1)
        w1 = g_w / (1.0 + e2)
        w2 = g_w * e2 / (1.0 + e2)
        route_ref[0, rows, :] = (jnp.where(lane == i1, w1, 0.0) + jnp.where(lane == i2, w2, 0.0)
                                 + jnp.where(lane == GROUP_LANE, g_idx.astype(F32), 0.0))


def _post(x, ys, ym, gm, wos, wom, gx, wq, km, vm, wo, gmoe, wrh, wrl, br, tm, sub):
    b, s, d = x.shape
    hd = d // XATTN_HEADS
    tok = lambda n: pl.BlockSpec((1, tm, n), lambda i, j: (i, j, 0))
    mem = pl.BlockSpec((1,) + km.shape[1:], lambda i, j: (i, 0, 0, 0))
    return pl.pallas_call(
        functools.partial(_post_kernel, hd=hd, sub=sub),
        grid=(b, s // tm),
        in_specs=[tok(d), tok(ys.shape[-1]), pl.BlockSpec((1, MLA_HEADS, tm, LANES), lambda i, j: (i, 0, j, 0)),
                  _full(gm.shape), _full(wos.shape), _full(wom.shape), _full(gx.shape), _full(wq.shape), mem, mem,
                  _full(wo.shape), _full(gmoe.shape), _full(wrh.shape), _full(wrl.shape), _full(br.shape)],
        out_specs=[tok(d), tok(d), tok(LANES)],
        out_shape=[jax.ShapeDtypeStruct((b, s, d), F32), jax.ShapeDtypeStruct((b, s, d), F32),
                   jax.ShapeDtypeStruct((b, s, LANES), F32)],
        compiler_params=_params("parallel", "parallel"),
        name="post",
    )(x, ys, ym, gm, wos, wom, gx, wq, km, vm, wo, gmoe, wrh, wrl, br)


def _route_metadata(route, tm):
    b, s, _ = route.shape
    n_tiles = (s + MOE_GROUPS * tm) // tm
    gid = route[..., GROUP_LANE].astype(jnp.int32)
    groups = jnp.arange(MOE_GROUPS, dtype=jnp.int32)[None, :, None]
    onehot = (gid[:, None, :] == groups).astype(jnp.int32)
    csum = jnp.cumsum(onehot, axis=2)
    cnt = csum[:, :, -1]
    padded = (cnt + tm - 1) // tm * tm
    seg_end = jnp.cumsum(padded, axis=-1)
    dest = jnp.sum(onehot * ((seg_end - padded)[:, :, None] + csum - 1), axis=1)
    tile_start = jnp.arange(n_tiles, dtype=jnp.int32) * tm
    tile_g = jnp.sum((seg_end[:, None, :] <= tile_start[None, :, None]).astype(jnp.int32), axis=-1)
    tile_g = jnp.minimum(tile_g, MOE_GROUPS - 1)
    n_valid = (seg_end[:, -1] // tm).astype(jnp.int32)
    return dict(pos=dest, tile_g=tile_g.reshape(b * n_tiles), n_valid=n_valid, n_pad=n_tiles * tm)


def _dispatch_kernel(h_ref, g_ref, dst_ref, xs_ref, gs_ref, *, tt):
    @pl.when(pl.program_id(1) == 0)
    def _():
        xs_ref[...] = jnp.zeros(xs_ref.shape, xs_ref.dtype)
        gs_ref[...] = jnp.zeros(gs_ref.shape, gs_ref.dtype)

    for t in range(tt):
        r = dst_ref[0, 0, t]
        xs_ref[0, pl.ds(r, 1), :] = h_ref[0, t:t + 1, :]
        gs_ref[0, pl.ds(r, 1), :] = g_ref[0, t:t + 1, :]


def _dispatch(h3, route, pos, n_pad, tt):
    b, s, d = h3.shape
    nj = s // tt
    row_block = lambda n: pl.BlockSpec((1, n_pad, n), lambda i, j: (i, 0, 0), pipeline_mode=pl.Buffered(1))
    return pl.pallas_call(
        functools.partial(_dispatch_kernel, tt=tt),
        grid=(b, nj),
        in_specs=[pl.BlockSpec((1, tt, d), lambda i, j: (i, j, 0)),
                  pl.BlockSpec((1, tt, LANES), lambda i, j: (i, j, 0)),
                  pl.BlockSpec((1, 1, tt), lambda i, j: (i * nj + j, 0, 0), memory_space=pltpu.SMEM)],
        out_specs=[row_block(d), row_block(LANES)],
        out_shape=[jax.ShapeDtypeStruct((b, n_pad, d), F32), jax.ShapeDtypeStruct((b, n_pad, LANES), F32)],
        compiler_params=_params("parallel", "arbitrary"),
        name="dispatch",
    )(h3, route, pos.reshape(b * nj, 1, tt))


def _moe_kernel(tg_ref, nv_ref, h_ref, gate_ref, wg_ref, wu_ref, wd_ref, o_ref, *, nt):
    bi = pl.program_id(0)
    ti = pl.program_id(1)
    grp = tg_ref[bi * nt + ti]

    @pl.when(ti < nv_ref[bi])
    def _():
        h = h_ref[0].astype(BF16)
        gate = gate_ref[0]
        lane = lax.broadcasted_iota(jnp.int32, gate.shape, 1)
        acc = None
        for e in range(MOE_PER_GROUP):
            a = jax.nn.silu(_dot(h, wg_ref[e])) * _dot(h, wu_ref[e])
            gcol = jnp.sum(jnp.where(lane == grp * MOE_PER_GROUP + e, gate, 0.0), axis=-1, keepdims=True)
            y = _dot((a * gcol).astype(BF16), wd_ref[e])
            acc = y if acc is None else acc + y
        o_ref[0] = acc

    @pl.when(ti >= nv_ref[bi])
    def _():
        o_ref[0] = jnp.zeros(o_ref.shape[1:], o_ref.dtype)


def _moe(xs, gs, tile_g, n_valid, wg, wu, wd, tm):
    b, n_pad, d = xs.shape
    nt = n_pad // tm
    ff = wg.shape[-1]
    group = lambda i, j, tg, nv: (tg[i * nt + j], 0, 0)
    tok = lambda n: pl.BlockSpec((1, tm, n), lambda i, j, tg, nv: (i, j, 0))
    grid_spec = pltpu.PrefetchScalarGridSpec(
        num_scalar_prefetch=2,
        grid=(b, nt),
        in_specs=[tok(d), tok(LANES), pl.BlockSpec((MOE_PER_GROUP, d, ff), group),
                  pl.BlockSpec((MOE_PER_GROUP, d, ff), group), pl.BlockSpec((MOE_PER_GROUP, ff, d), group)],
        out_specs=tok(d),
    )
    return pl.pallas_call(
        functools.partial(_moe_kernel, nt=nt),
        grid_spec=grid_spec,
        out_shape=jax.ShapeDtypeStruct((b, n_pad, d), F32),
        compiler_params=_params("parallel", "arbitrary"),
        name="moe",
    )(tile_g, n_valid, xs, gs, wg, wu, wd)


def _combine_kernel(ys_ref, pos_ref, x2_ref, gf_ref, o_ref, buf_ref, *, tt):
    for t in range(tt):
        buf_ref[t:t + 1, :] = ys_ref[0, pl.ds(pos_ref[0, 0, t], 1), :]
    o_ref[0] = _rms(x2_ref[0] + buf_ref[...], gf_ref[...])


def _combine(ys, pos, x2, gf, tt):
    b, s, d = x2.shape
    n_pad = ys.shape[1]
    nj = s // tt
    return pl.pallas_call(
        functools.partial(_combine_kernel, tt=tt),
        grid=(b, nj),
        in_specs=[pl.BlockSpec((1, n_pad, d), lambda i, j: (i, 0, 0), pipeline_mode=pl.Buffered(1)),
                  pl.BlockSpec((1, 1, tt), lambda i, j: (i * nj + j, 0, 0), memory_space=pltpu.SMEM),
                  pl.BlockSpec((1, tt, d), lambda i, j: (i, j, 0)), _full(gf.shape)],
        out_specs=pl.BlockSpec((1, tt, d), lambda i, j: (i, j, 0)),
        out_shape=jax.ShapeDtypeStruct((b, s, d), F32),
        scratch_shapes=[pltpu.VMEM((tt, d), F32)],
        compiler_params=_params("parallel", "arbitrary"),
        name="combine",
    )(ys, pos.reshape(b * nj, 1, tt), x2, gf)


def _pad_heads(w, per_head, offset=0):
    k = w.shape[0]
    w = w.reshape(k, MLA_HEADS, per_head)
    w = jnp.pad(w, ((0, 0), (0, 0), (offset, LANES - per_head - offset)))
    return w.reshape(k, MLA_HEADS * LANES)


def _block_diag(blocks):
    n, r, c = blocks.shape
    eye = jnp.eye(n, dtype=blocks.dtype)
    return (eye[:, None, :, None] * blocks[:, :, None, :]).reshape(n * r, n * c)


def kernel(x, mem, positions, norm_mix_g, w_in, ssm_lam_re, ssm_lam_im, ssm_log_dt, ssm_b_re, ssm_b_im, ssm_c_re, ssm_c_im, ssm_d, ssm_w_glu, ssm_b_glu, mla_q_norm_g, mla_w_q_up, mla_kv_norm_g, mla_w_kv_up, out_norm_ssm_g, out_norm_mla_g, w_out, norm_xattn_g, norm_mem_g, xattn_w_q, xattn_w_k, xattn_w_v, xattn_w_o, norm_moe_g, moe_w_group, moe_b_group, moe_w_expert, moe_b_expert, moe_w_gate, moe_w_up, moe_w_down, norm_final_g):
    bsz, seq, d = x.shape
    depth = w_in.shape[0]
    assert depth == 1, "the final RMSNorm is fused into the last stage of a single layer"
    d_ssm = ssm_d.shape[-1]
    q_rank = mla_q_norm_g.shape[-1]
    kv_rank = mla_kv_norm_g.shape[-1]
    n_grp = d_ssm // SSM_GROUP
    s1, s2, s3 = d_ssm, d_ssm + q_rank, d_ssm + q_rank + kv_rank
    row = lambda v: v.reshape(1, -1).astype(F32)

    half = MLA_ROPE // 2
    inv_freq = ROPE_THETA ** (-jnp.arange(half, dtype=F32) / half)
    invf = jnp.zeros((LANES,), F32).at[MLA_NOPE:MLA_NOPE + half].set(inv_freq)
    invf = invf.at[MLA_NOPE + half:MLA_QK].set(inv_freq).reshape(1, LANES)
    pos = positions.reshape(bsz, seq, 1)

    for l in range(depth):
        a_re, a_im, bb_re, bb_im = _ssm_prep(ssm_lam_re[l], ssm_lam_im[l], ssm_log_dt[l], ssm_b_re[l], ssm_b_im[l])
        gh = n_grp // 2
        bbd = jnp.stack([
            jnp.concatenate([_block_diag(bb_re[hf * gh:(hf + 1) * gh]), _block_diag(bb_im[hf * gh:(hf + 1) * gh])],
                            axis=1) for hf in range(2)]).astype(BF16)
        c_re_t = jnp.transpose(ssm_c_re[l], (0, 2, 1))
        c_im_t = jnp.transpose(ssm_c_im[l], (0, 2, 1))
        cmat = jnp.stack([
            jnp.concatenate([_block_diag(c_re_t[hf * gh:(hf + 1) * gh]), -_block_diag(c_im_t[hf * gh:(hf + 1) * gh])],
                            axis=0) for hf in range(2)]).astype(BF16)
        atab = jnp.stack([
            jnp.repeat(arr.reshape(2, 1, gh * SSM_STATE), bsz, axis=1).reshape(2 * bsz, gh * SSM_STATE)
            for arr in (a_re, a_im)])

        wi = w_in[l]
        wu = wi[:, :s1].astype(BF16)
        wcq = wi[:, s1:s2].astype(BF16)
        wckv = wi[:, s2:s3].astype(BF16)
        wkr = jnp.pad(wi[:, s3:], ((0, 0), (MLA_NOPE, LANES - MLA_QK))).astype(BF16)
        wq = _pad_heads(mla_w_q_up[l], MLA_QK).astype(BF16)
        wkv = mla_w_kv_up[l].reshape(kv_rank, MLA_HEADS, MLA_NOPE + MLA_V)
        wk = _pad_heads(wkv[:, :, :MLA_NOPE].reshape(kv_rank, -1), MLA_NOPE).astype(BF16)
        wv = _pad_heads(wkv[:, :, MLA_NOPE:].reshape(kv_rank, -1), MLA_V).astype(BF16)
        u, q, k, v = _proj(x, pos, row(norm_mix_g[l]), wu, wcq, wckv, wkr, row(mla_q_norm_g[l]), wq,
                           row(mla_kv_norm_g[l]), wk, wv, invf, tm=min(512, seq), sub=min(512, seq))

        y_ssm = _ssm(u, bbd, atab, cmat, row(ssm_d[l]), ssm_w_glu[l].astype(BF16), row(ssm_b_glu[l]),
                     row(out_norm_ssm_g[l]), tt=min(256, seq))
        y_mla = _attn(q, k, v, tq=min(512, seq), tk=min(256, seq), hp=8, strip=32)

        km, vm = _memkv(mem, row(norm_mem_g[l]), xattn_w_k[l].astype(BF16), xattn_w_v[l].astype(BF16))
        wos = w_out[l][:d_ssm].astype(BF16)
        wom = w_out[l][d_ssm:].astype(BF16)
        wr = jnp.concatenate([moe_w_expert[l].reshape(d, MOE_EXPERTS), moe_w_group[l]], axis=1)
        wr = jnp.pad(wr, ((0, 0), (0, LANES - MOE_EXPERTS - MOE_GROUPS))).astype(F32)
        wr_hi = wr.astype(BF16)
        wr_lo = (wr - wr_hi.astype(F32)).astype(BF16)
        br = jnp.concatenate([moe_b_expert[l].reshape(-1), moe_b_group[l]])
        br = jnp.pad(br, (0, LANES - MOE_EXPERTS - MOE_GROUPS)).reshape(1, LANES).astype(F32)
        x2, h3, route = _post(x, y_ssm, y_mla, row(out_norm_mla_g[l]), wos, wom, row(norm_xattn_g[l]),
                              xattn_w_q[l].astype(BF16), km, vm, xattn_w_o[l].astype(BF16), row(norm_moe_g[l]),
                              wr_hi, wr_lo, br, tm=min(512, seq), sub=min(256, seq))

        meta = _route_metadata(route, MOE_TILE)
        xs, gs = _dispatch(h3, route, meta["pos"], meta["n_pad"], tt=min(256, seq))
        ys = _moe(xs, gs, meta["tile_g"], meta["n_valid"], moe_w_gate[l].astype(BF16), moe_w_up[l].astype(BF16),
                  moe_w_down[l].astype(BF16), tm=MOE_TILE)
        x = _combine(ys, meta["pos"], x2, row(norm_final_g), tt=min(256, seq))
    return x
```

```python
import functools
import math
from typing import NamedTuple

import jax
import jax.numpy as jnp
from jax import lax
from jax.experimental import pallas as pl
from jax.experimental.pallas import tpu as pltpu

F32 = jnp.float32
BF16 = jnp.bfloat16

EPS = 1e-6
NEG_INF = -1e30
CHUNK = 64

LANES = 128
SSM_GROUP = 16
SSM_STATE = 64
MLA_HEADS = 8
MLA_NOPE = 64
MLA_ROPE = 32
MLA_QK = MLA_NOPE + MLA_ROPE
MLA_V = 64
ROPE_THETA = 10000.0
XATTN_HEADS = 4
MOE_GROUPS = 4
MOE_PER_GROUP = 8
MOE_EXPERTS = MOE_GROUPS * MOE_PER_GROUP
VMEM_LIMIT = 48 * 1024 * 1024
VMEM_LIMIT_LARGE = 56 * 1024 * 1024
LOG2E = math.log2(math.e)
VT_ROWS = 80
GROUP_LANE = 64


class Tiles(NamedTuple):
    proj: int
    ssm: int
    attn_q: int
    attn_k: int
    attn_heads: int
    attn_strip: int
    post: int
    post_sub: int
    moe: int
    copy: int


def _tiles(seq):
    return Tiles(proj=min(512, seq), ssm=min(256, seq), attn_q=min(512, seq), attn_k=min(256, seq),
                 attn_heads=MLA_HEADS, attn_strip=32, post=min(512, seq), post_sub=min(256, seq),
                 moe=256, copy=min(256, seq))


def _dot(a, b):
    return jnp.dot(a, b, preferred_element_type=F32)


def _dot_nt(a, b):
    return lax.dot_general(a, b, (((1,), (1,)), ((), ())), preferred_element_type=F32)


def _rms(x, g):
    ms = jnp.mean(x * x, axis=-1, keepdims=True)
    return x * lax.rsqrt(ms + EPS) * g


def _params(*sem, vmem=VMEM_LIMIT):
    return pltpu.CompilerParams(dimension_semantics=sem, vmem_limit_bytes=vmem)


def _full(shape):
    zeros = (0,) * len(shape)
    return pl.BlockSpec(shape, lambda *_: zeros)


def _ssm_prep_kernel(lre_ref, lim_ref, ldt_ref, bre_ref, bim_ref, are_ref, aim_ref, bbre_ref, bbim_ref):
    lre = jnp.minimum(lre_ref[...], -1e-4)
    lim = lim_ref[...]
    dt = jnp.exp(ldt_ref[...])
    mag = jnp.exp(lre * dt)
    are = mag * jnp.cos(lim * dt)
    aim = mag * jnp.sin(lim * dt)
    are_ref[...] = are
    aim_ref[...] = aim
    nre = are - 1.0
    den = lre * lre + lim * lim
    fre = (nre * lre + aim * lim) / den
    fim = (aim * lre - nre * lim) / den
    bre = bre_ref[...]
    bim = bim_ref[...]
    bbre_ref[...] = fre * bre - fim * bim
    bbim_ref[...] = fre * bim + fim * bre


def _ssm_prep(lam_re, lam_im, log_dt, b_re, b_im):
    g, p = lam_re.shape
    hh = b_re.shape[-1]
    bt_re = jnp.transpose(b_re, (0, 2, 1))
    bt_im = jnp.transpose(b_im, (0, 2, 1))
    outs = pl.pallas_call(
        _ssm_prep_kernel,
        out_shape=[jax.ShapeDtypeStruct((g, 1, p), F32)] * 2 + [jax.ShapeDtypeStruct((g, hh, p), F32)] * 2,
        name="ssm_prep",
    )(lam_re.reshape(g, 1, p), lam_im.reshape(g, 1, p), log_dt.reshape(g, 1, 1), bt_re, bt_im)
    a_re, a_im, bb_re, bb_im = outs
    return a_re.reshape(g, p), a_im.reshape(g, p), bb_re, bb_im


def _proj_kernel(x_ref, pos_ref, gmix_ref, wu_ref, wcq_ref, wckv_ref, wkr_ref, qg_ref, wq_ref, kvg_ref,
                 wk_ref, wv_ref, invf_ref, u_ref, q_ref, k_ref, v_ref):
    lane = lax.broadcasted_iota(jnp.int32, (1, LANES), 1)
    half = MLA_ROPE // 2
    ones_col = jnp.where(lane == MLA_V, 1.0, 0.0)
    h = _rms(x_ref[0], gmix_ref[...]).astype(BF16)
    u_ref[0] = _dot(h, wu_ref[...]).astype(BF16)
    cq = _dot(h, wcq_ref[...])
    q = _dot(_rms(cq, qg_ref[...]).astype(BF16), wq_ref[...]) * (MLA_QK ** -0.5 * LOG2E)
    ckv = _dot(h, wckv_ref[...])
    ckvn = _rms(ckv, kvg_ref[...]).astype(BF16)
    kk = _dot(ckvn, wk_ref[...])
    vv = _dot(ckvn, wv_ref[...])
    kr = _dot(h, wkr_ref[...])

    ang = pos_ref[0].astype(F32) * invf_ref[...]
    cosv = jnp.cos(ang)
    sinv = jnp.sin(ang)
    c_tab = jnp.where(lane < MLA_NOPE, 1.0, cosv)
    s_lo = jnp.where((lane >= MLA_NOPE) & (lane < MLA_NOPE + half), -sinv, 0.0)
    s_hi = jnp.where((lane >= MLA_NOPE + half) & (lane < MLA_QK), sinv, 0.0)

    def rot(val):
        return (val * c_tab + pltpu.roll(val, LANES - half, 1) * s_lo + pltpu.roll(val, half, 1) * s_hi)

    kpe = rot(kr)
    for hh in range(MLA_HEADS):
        sl = slice(hh * LANES, (hh + 1) * LANES)
        q_ref[0, hh] = rot(q[:, sl]).T.astype(BF16)
        k_ref[0, hh] = (kk[:, sl] + kpe).astype(BF16)
        v_ref[0, hh] = (vv[:, sl] + ones_col).T[:VT_ROWS].astype(BF16)


def _proj(x, pos, gmix, wu, wcq, wckv, wkr, qg, wq, kvg, wk, wv, invf, tm):
    b, s, d = x.shape
    grid = (b, s // tm)
    tok = lambda n: pl.BlockSpec((1, tm, n), lambda i, j: (i, j, 0))
    head = pl.BlockSpec((1, MLA_HEADS, tm, LANES), lambda i, j: (i, 0, j, 0))
    head_t = pl.BlockSpec((1, MLA_HEADS, LANES, tm), lambda i, j: (i, 0, 0, j))
    head_vt = pl.BlockSpec((1, MLA_HEADS, VT_ROWS, tm), lambda i, j: (i, 0, 0, j))
    tshape = lambda n: jax.ShapeDtypeStruct((b, MLA_HEADS, n, s), BF16)
    return pl.pallas_call(
        _proj_kernel,
        grid=grid,
        in_specs=[tok(d), tok(1), _full(gmix.shape), _full(wu.shape), _full(wcq.shape), _full(wckv.shape),
                  _full(wkr.shape), _full(qg.shape), _full(wq.shape), _full(kvg.shape), _full(wk.shape),
                  _full(wv.shape), _full(invf.shape)],
        out_specs=[tok(wu.shape[1]), head_t, head, head_vt],
        out_shape=[jax.ShapeDtypeStruct((b, s, wu.shape[1]), BF16), tshape(LANES),
                   jax.ShapeDtypeStruct((b, MLA_HEADS, s, LANES), BF16), tshape(VT_ROWS)],
        compiler_params=_params("parallel", "parallel"),
        name="proj",
    )(x, pos, gmix, wu, wcq, wckv, wkr, qg, wq, kvg, wk, wv, invf)


def _ssm_kernel(u_ref, bbd_ref, atab_ref, cmat_ref, dskip_ref, wglu_ref, bglu_ref, gout_ref, y_ref,
                xs_ref, st_ref, *, tt, slab, nb, lane_tiles):
    j = pl.program_id(0)
    half_w = u_ref.shape[-1] // 2

    @pl.when(j == 0)
    def _():
        st_ref[...] = jnp.zeros_like(st_ref)

    for hf in range(2):
        ub = u_ref[:, :, hf * half_w:(hf + 1) * half_w].reshape(nb * tt, half_w)
        bu = _dot(ub, bbd_ref[hf])
        for b in range(nb):
            r0 = (hf * nb + b) * slab
            for c in range(2 * lane_tiles):
                xs_ref[c, r0:r0 + tt, :] = bu[b * tt:(b + 1) * tt, c * LANES:(c + 1) * LANES]

    rows = 2 * nb
    group = 4
    for c0 in range(0, lane_tiles, group):
        cs = list(range(c0, c0 + group))
        a_re = [atab_ref[0, :, c * LANES:(c + 1) * LANES] for c in cs]
        a_im = [atab_ref[1, :, c * LANES:(c + 1) * LANES] for c in cs]
        init = tuple(st_ref[0, :, c * LANES:(c + 1) * LANES] for c in cs) + \
            tuple(st_ref[1, :, c * LANES:(c + 1) * LANES] for c in cs)

        def step(t, carry, cs=cs, a_re=a_re, a_im=a_im):
            new_re, new_im = [], []
            for i, c in enumerate(cs):
                x_re, x_im = carry[i], carry[group + i]
                idx = pl.ds(t, rows, stride=slab)
                n_re = a_re[i] * x_re - a_im[i] * x_im + xs_ref[c, idx, :]
                n_im = a_re[i] * x_im + a_im[i] * x_re + xs_ref[lane_tiles + c, idx, :]
                xs_ref[c, idx, :] = n_re
                xs_ref[lane_tiles + c, idx, :] = n_im
                new_re.append(n_re)
                new_im.append(n_im)
            return tuple(new_re) + tuple(new_im)

        fin = lax.fori_loop(0, tt, step, init, unroll=4)
        for i, c in enumerate(cs):
            st_ref[0, :, c * LANES:(c + 1) * LANES] = fin[i]
            st_ref[1, :, c * LANES:(c + 1) * LANES] = fin[group + i]

    ys = []
    for hf in range(2):
        xb = []
        for b in range(nb):
            r0 = (hf * nb + b) * slab
            xb.append(jnp.concatenate([xs_ref[c, r0:r0 + tt, :] for c in range(2 * lane_tiles)], axis=1))
        xh = jnp.concatenate(xb, axis=0).astype(BF16)
        ys.append(_dot(xh, cmat_ref[hf]))
    y = jnp.concatenate(ys, axis=1)
    u = u_ref[...].astype(F32).reshape(nb * tt, 2 * half_w)
    y = jax.nn.gelu(y + dskip_ref[...] * u)
    y = y * jax.nn.sigmoid(_dot(y.astype(BF16), wglu_ref[...]) + bglu_ref[...])
    y = _rms(y, gout_ref[...])
    y_ref[...] = y.reshape(nb, tt, 2 * half_w).astype(BF16)


def _ssm(u, bbd, atab, cmat, dskip, wglu, bglu, gout, tt):
    nb, s, dssm = u.shape
    lane_tiles = bbd.shape[-1] // (2 * LANES)
    slab = tt + 4
    kern = functools.partial(_ssm_kernel, tt=tt, slab=slab, nb=nb, lane_tiles=lane_tiles)
    return pl.pallas_call(
        kern,
        grid=(s // tt,),
        in_specs=[pl.BlockSpec((nb, tt, dssm), lambda j: (0, j, 0)), _full(bbd.shape), _full(atab.shape),
                  _full(cmat.shape), _full(dskip.shape), _full(wglu.shape), _full(bglu.shape), _full(gout.shape)],
        out_specs=pl.BlockSpec((nb, tt, dssm), lambda j: (0, j, 0)),
        out_shape=jax.ShapeDtypeStruct((nb, s, dssm), BF16),
        scratch_shapes=[pltpu.VMEM((2 * lane_tiles, 2 * nb * slab, LANES), F32),
                        pltpu.VMEM((2, 2 * nb, lane_tiles * LANES), F32)],
        compiler_params=_params("arbitrary"),
        name="ssm",
    )(u, bbd, atab, cmat, dskip, wglu, bglu, gout)


def _attn_kernel(qt_ref, k_ref, vt_ref, o_ref, s_ref, p_ref, m_ref, a_ref, acc_ref, *, tq, tk, hp, strip):
    qi = pl.program_id(2)
    m_ref[...] = jnp.full(m_ref.shape, NEG_INF, F32)
    acc_ref[...] = jnp.zeros(acc_ref.shape, F32)
    q_chunk = lax.broadcasted_iota(jnp.int32, (1, tq), 1) // CHUNK

    def tile(k0, diag_off):
        qs = slice(0 if diag_off is None else diag_off, tq)
        for hh in range(hp):
            s_ref[hh, :, qs] = _dot(k_ref[0, hh, pl.ds(k0, tk), :], qt_ref[0, hh, :, qs])

        def strip_of(hh, r):
            s = s_ref[hh, r:r + strip, qs]
            if diag_off is None:
                return s
            return jnp.where(q_chunk[:, qs] >= (diag_off + r) // CHUNK, s, NEG_INF)

        for hh in range(hp):
            mt = strip_of(hh, 0)
            for r in range(strip, tk, strip):
                mt = jnp.maximum(mt, strip_of(hh, r))
            m_old = m_ref[hh, :, qs]
            m_new = jnp.maximum(m_old, jnp.max(mt, axis=0, keepdims=True))
            a_ref[hh, :, qs] = jnp.exp2(m_old - m_new)
            m_ref[hh, :, qs] = m_new
            for r in range(0, tk, strip):
                p_ref[hh, r:r + strip, qs] = jnp.exp2(strip_of(hh, r) - m_new).astype(BF16)
        for hh in range(hp):
            pv = _dot(vt_ref[0, hh, :, pl.ds(k0, tk)], p_ref[hh, :, qs])
            acc_ref[hh, :, qs] = acc_ref[hh, :, qs] * a_ref[hh, :, qs] + pv

    def body(j, c):
        tile(pl.multiple_of(j * tk, tk), None)
        return c

    lax.fori_loop(0, qi * (tq // tk), body, 0)
    for off in range(0, tq, tk):
        tile(pl.multiple_of(qi * tq + off, tk), off)
    feat = lax.broadcasted_iota(jnp.int32, (VT_ROWS, 1), 0)
    pad = jnp.zeros((LANES - VT_ROWS, tq), F32)
    for hh in range(hp):
        acc = acc_ref[hh]
        out = acc * (1.0 / acc[MLA_V:MLA_V + 1, :])
        out = jnp.where(feat < MLA_V, out, 0.0)
        o_ref[0, hh] = jnp.concatenate([out, pad], axis=0).T.astype(BF16)


def _attn(qt, k, vt, tq, tk, hp, strip):
    b, h, s, _ = k.shape
    assert strip <= CHUNK and CHUNK % strip == 0 and tk % CHUNK == 0 and tq % tk == 0
    return pl.pallas_call(
        functools.partial(_attn_kernel, tq=tq, tk=tk, hp=hp, strip=strip),
        grid=(b, h // hp, s // tq),
        in_specs=[pl.BlockSpec((1, hp, LANES, tq), lambda i, j, t: (i, j, 0, t)),
                  pl.BlockSpec((1, hp, s, LANES), lambda i, j, t: (i, j, 0, 0)),
                  pl.BlockSpec((1, hp, VT_ROWS, s), lambda i, j, t: (i, j, 0, 0))],
        out_specs=pl.BlockSpec((1, hp, tq, LANES), lambda i, j, t: (i, j, t, 0)),
        out_shape=jax.ShapeDtypeStruct((b, h, s, LANES), BF16),
        scratch_shapes=[pltpu.VMEM((hp, tk, tq), F32), pltpu.VMEM((hp, tk, tq), BF16),
                        pltpu.VMEM((hp, 1, tq), F32), pltpu.VMEM((hp, 1, tq), F32),
                        pltpu.VMEM((hp, VT_ROWS, tq), F32)],
        compiler_params=_params("parallel", "parallel", "arbitrary", vmem=VMEM_LIMIT_LARGE),
        name="attn",
    )(qt, k, vt)


def _memkv_kernel(mem_ref, g_ref, wk_ref, wv_ref, k_ref, v_ref, *, hd):
    mn = _rms(mem_ref[0], g_ref[...]).astype(BF16)
    kk = _dot(mn, wk_ref[...])
    vv = _dot(mn, wv_ref[...])
    for hh in range(XATTN_HEADS):
        k_ref[0, hh] = kk[:, hh * hd:(hh + 1) * hd].astype(BF16)
        v_ref[0, hh] = vv[:, hh * hd:(hh + 1) * hd].astype(BF16)


def _memkv(mem, g, wk, wv):
    b, nm, d = mem.shape
    hd = d // XATTN_HEADS
    ospec = pl.BlockSpec((1, XATTN_HEADS, nm, hd), lambda i: (i, 0, 0, 0))
    return pl.pallas_call(
        functools.partial(_memkv_kernel, hd=hd),
        grid=(b,),
        in_specs=[pl.BlockSpec((1, nm, d), lambda i: (i, 0, 0)), _full(g.shape), _full(wk.shape), _full(wv.shape)],
        out_specs=[ospec, ospec],
        out_shape=[jax.ShapeDtypeStruct((b, XATTN_HEADS, nm, hd), BF16)] * 2,
        compiler_params=_params("parallel"),
        name="mem_kv",
    )(mem, g, wk, wv)


def _post_kernel(x_ref, ys_ref, ym_ref, gm_ref, wos_ref, wom_ref, gx_ref, wq_ref, km_ref, vm_ref, wo_ref,
                 gmoe_ref, wrh_ref, wrl_ref, br_ref, x2_ref, h3_ref, route_ref, *, hd, sub):
    for r0 in range(0, x_ref.shape[1], sub):
        rows = slice(r0, r0 + sub)
        x = x_ref[0, rows, :]
        heads = [ym_ref[0, hh, rows, :].astype(F32) for hh in range(MLA_HEADS)]
        ym = jnp.concatenate([heads[k] + pltpu.roll(heads[k + 1], MLA_V, 1) for k in range(0, MLA_HEADS, 2)],
                             axis=1)
        ymn = _rms(ym, gm_ref[...]).astype(BF16)
        x1 = x + _dot(ys_ref[0, rows, :], wos_ref[...]) + _dot(ymn, wom_ref[...])

        h2 = _rms(x1, gx_ref[...]).astype(BF16)
        qx = (_dot(h2, wq_ref[...]) * (hd ** -0.5)).astype(BF16)
        outs = []
        for hh in range(XATTN_HEADS):
            s = _dot_nt(qx[:, hh * hd:(hh + 1) * hd], km_ref[0, hh])
            p = jnp.exp(s - jnp.max(s, axis=-1, keepdims=True))
            p = p * (1.0 / jnp.sum(p, axis=-1, keepdims=True))
            outs.append(_dot(p.astype(BF16), vm_ref[0, hh]))
        o = jnp.concatenate(outs, axis=1).astype(BF16)
        x2 = x1 + _dot(o, wo_ref[...])
        x2_ref[0, rows, :] = x2

        h3 = _rms(x2, gmoe_ref[...])
        h3_ref[0, rows, :] = h3

        h_hi = h3.astype(BF16)
        h_lo = (h3 - h_hi.astype(F32)).astype(BF16)
        logits = (_dot(h_hi, wrh_ref[...]) + _dot(h_hi, wrl_ref[...]) + _dot(h_lo, wrh_ref[...])) + br_ref[...]
        lane = lax.broadcasted_iota(jnp.int32, logits.shape, 1)
        is_g = (lane >= MOE_EXPERTS) & (lane < MOE_EXPERTS + MOE_GROUPS)
        gl = jnp.where(is_g, logits, NEG_INF)
        gmax = jnp.max(gl, axis=-1, keepdims=True)
        g_w = 1.0 / jnp.sum(jnp.exp(gl - gmax), axis=-1, keepdims=True)
        g_idx = jnp.min(jnp.where(gl == gmax, lane, 4 * LANES), axis=-1, keepdims=True) - MOE_EXPERTS
        in_grp = (lane >= g_idx * MOE_PER_GROUP) & (lane < (g_idx + 1) * MOE_PER_GROUP)
        el = jnp.where(in_grp, logits, NEG_INF)
        v1 = jnp.max(el, axis=-1, keepdims=True)
        i1 = jnp.min(jnp.where(el == v1, lane, 4 * LANES), axis=-1, keepdims=True)
        el2 = jnp.where(lane == i1, NEG_INF, el)
        v2 = jnp.max(el2, axis=-1, keepdims=True)
        i2 = jnp.min(jnp.where(el2 == v2, lane, 4 * LANES), axis=-1, keepdims=True)
        e2 = jnp.exp(v2 - v1)
        w1 = g_w / (1.0 + e2)
        w2 = g_w * e2 / (1.0 + e2)
        route_ref[0, rows, :] = (jnp.where(lane == i1, w1, 0.0) + jnp.where(lane == i2, w2, 0.0)
                                 + jnp.where(lane == GROUP_LANE, g_idx.astype(F32), 0.0))


def _post(x, ys, ym, gm, wos, wom, gx, wq, km, vm, wo, gmoe, wrh, wrl, br, tm, sub):
    b, s, d = x.shape
    hd = d // XATTN_HEADS
    tok = lambda n: pl.BlockSpec((1, tm, n), lambda i, j: (i, j, 0))
    mem = pl.BlockSpec((1,) + km.shape[1:], lambda i, j: (i, 0, 0, 0))
    return pl.pallas_call(
        functools.partial(_post_kernel, hd=hd, sub=sub),
        grid=(b, s // tm),
        in_specs=[tok(d), tok(ys.shape[-1]), pl.BlockSpec((1, MLA_HEADS, tm, LANES), lambda i, j: (i, 0, j, 0)),
                  _full(gm.shape), _full(wos.shape), _full(wom.shape), _full(gx.shape), _full(wq.shape), mem, mem,
                  _full(wo.shape), _full(gmoe.shape), _full(wrh.shape), _full(wrl.shape), _full(br.shape)],
        out_specs=[tok(d), tok(d), tok(LANES)],
        out_shape=[jax.ShapeDtypeStruct((b, s, d), F32), jax.ShapeDtypeStruct((b, s, d), F32),
                   jax.ShapeDtypeStruct((b, s, LANES), F32)],
        compiler_params=_params("parallel", "parallel"),
        name="post",
    )(x, ys, ym, gm, wos, wom, gx, wq, km, vm, wo, gmoe, wrh, wrl, br)


def _route_metadata(route, tm):
    b, s, _ = route.shape
    n_tiles = (s + MOE_GROUPS * tm) // tm
    gid = route[..., GROUP_LANE].astype(jnp.int32)
    groups = jnp.arange(MOE_GROUPS, dtype=jnp.int32)[None, :, None]
    onehot = (gid[:, None, :] == groups).astype(jnp.int32)
    csum = jnp.cumsum(onehot, axis=2)
    cnt = csum[:, :, -1]
    padded = (cnt + tm - 1) // tm * tm
    seg_end = jnp.cumsum(padded, axis=-1)
    dest = jnp.sum(onehot * ((seg_end - padded)[:, :, None] + csum - 1), axis=1)
    tile_start = jnp.arange(n_tiles, dtype=jnp.int32) * tm
    tile_g = jnp.sum((seg_end[:, None, :] <= tile_start[None, :, None]).astype(jnp.int32), axis=-1)
    tile_g = jnp.minimum(tile_g, MOE_GROUPS - 1)
    n_valid = (seg_end[:, -1] // tm).astype(jnp.int32)
    return dict(pos=dest, tile_g=tile_g.reshape(b * n_tiles), n_valid=n_valid, n_pad=n_tiles * tm)


def _dispatch_kernel(h_ref, g_ref, dst_ref, xs_ref, gs_ref, *, tt):
    @pl.when(pl.program_id(1) == 0)
    def _():
        xs_ref[...] = jnp.zeros(xs_ref.shape, xs_ref.dtype)
        gs_ref[...] = jnp.zeros(gs_ref.shape, gs_ref.dtype)

    for t in range(tt):
        r = dst_ref[0, 0, t]
        xs_ref[0, pl.ds(r, 1), :] = h_ref[0, t:t + 1, :]
        gs_ref[0, pl.ds(r, 1), :] = g_ref[0, t:t + 1, :]


def _dispatch(h3, route, pos, n_pad, tt):
    b, s, d = h3.shape
    nj = s // tt
    row_block = lambda n: pl.BlockSpec((1, n_pad, n), lambda i, j: (i, 0, 0))
    return pl.pallas_call(
        functools.partial(_dispatch_kernel, tt=tt),
        grid=(b, nj),
        in_specs=[pl.BlockSpec((1, tt, d), lambda i, j: (i, j, 0)),
                  pl.BlockSpec((1, tt, LANES), lambda i, j: (i, j, 0)),
                  pl.BlockSpec((1, 1, tt), lambda i, j: (i * nj + j, 0, 0), memory_space=pltpu.SMEM)],
        out_specs=[row_block(d), row_block(LANES)],
        out_shape=[jax.ShapeDtypeStruct((b, n_pad, d), F32), jax.ShapeDtypeStruct((b, n_pad, LANES), F32)],
        compiler_params=_params("parallel", "arbitrary", vmem=VMEM_LIMIT_LARGE),
        name="dispatch",
    )(h3, route, pos.reshape(b * nj, 1, tt))


def _moe_kernel(tg_ref, nv_ref, h_ref, gate_ref, wg_ref, wu_ref, wd_ref, o_ref, *, nt):
    bi = pl.program_id(0)
    ti = pl.program_id(1)
    grp = tg_ref[bi * nt + ti]

    @pl.when(ti < nv_ref[bi])
    def _():
        h = h_ref[0].astype(BF16)
        gate = gate_ref[0]
        lane = lax.broadcasted_iota(jnp.int32, gate.shape, 1)
        acc = None
        for e in range(MOE_PER_GROUP):
            a = jax.nn.silu(_dot(h, wg_ref[e])) * _dot(h, wu_ref[e])
            gcol = jnp.sum(jnp.where(lane == grp * MOE_PER_GROUP + e, gate, 0.0), axis=-1, keepdims=True)
            y = _dot((a * gcol).astype(BF16), wd_ref[e])
            acc = y if acc is None else acc + y
        o_ref[0] = acc

    @pl.when(ti >= nv_ref[bi])
    def _():
        o_ref[0] = jnp.zeros(o_ref.shape[1:], o_ref.dtype)


def _moe(xs, gs, tile_g, n_valid, wg, wu, wd, tm):
    b, n_pad, d = xs.shape
    nt = n_pad // tm
    ff = wg.shape[-1]
    group = lambda i, j, tg, nv: (tg[i * nt + j], 0, 0)
    tok = lambda n: pl.BlockSpec((1, tm, n), lambda i, j, tg, nv: (i, j, 0))
    grid_spec = pltpu.PrefetchScalarGridSpec(
        num_scalar_prefetch=2,
        grid=(b, nt),
        in_specs=[tok(d), tok(LANES), pl.BlockSpec((MOE_PER_GROUP, d, ff), group),
                  pl.BlockSpec((MOE_PER_GROUP, d, ff), group), pl.BlockSpec((MOE_PER_GROUP, ff, d), group)],
        out_specs=tok(d),
    )
    return pl.pallas_call(
        functools.partial(_moe_kernel, nt=nt),
        grid_spec=grid_spec,
        out_shape=jax.ShapeDtypeStruct((b, n_pad, d), F32),
        compiler_params=_params("parallel", "arbitrary"),
        name="moe",
    )(tile_g, n_valid, xs, gs, wg, wu, wd)


def _combine_kernel(ys_ref, pos_ref, x2_ref, gf_ref, o_ref, buf_ref, *, tt):
    for t in range(tt):
        buf_ref[t:t + 1, :] = ys_ref[0, pl.ds(pos_ref[0, 0, t], 1), :]
    o_ref[0] = _rms(x2_ref[0] + buf_ref[...], gf_ref[...])


def _combine(ys, pos, x2, gf, tt):
    b, s, d = x2.shape
    n_pad = ys.shape[1]
    nj = s // tt
    return pl.pallas_call(
        functools.partial(_combine_kernel, tt=tt),
        grid=(b, nj),
        in_specs=[pl.BlockSpec((1, n_pad, d), lambda i, j: (i, 0, 0)),
                  pl.BlockSpec((1, 1, tt), lambda i, j: (i * nj + j, 0, 0), memory_space=pltpu.SMEM),
                  pl.BlockSpec((1, tt, d), lambda i, j: (i, j, 0)), _full(gf.shape)],
        out_specs=pl.BlockSpec((1, tt, d), lambda i, j: (i, j, 0)),
        out_shape=jax.ShapeDtypeStruct((b, s, d), F32),
        scratch_shapes=[pltpu.VMEM((tt, d), F32)],
        compiler_params=_params("parallel", "arbitrary", vmem=VMEM_LIMIT_LARGE),
        name="combine",
    )(ys, pos.reshape(b * nj, 1, tt), x2, gf)


def _pad_heads(w, per_head, offset=0):
    k = w.shape[0]
    w = w.reshape(k, MLA_HEADS, per_head)
    w = jnp.pad(w, ((0, 0), (0, 0), (offset, LANES - per_head - offset)))
    return w.reshape(k, MLA_HEADS * LANES)


def _block_diag(blocks):
    n, r, c = blocks.shape
    eye = jnp.eye(n, dtype=blocks.dtype)
    return (eye[:, None, :, None] * blocks[:, :, None, :]).reshape(n * r, n * c)


def kernel(x, mem, positions, norm_mix_g, w_in, ssm_lam_re, ssm_lam_im, ssm_log_dt, ssm_b_re, ssm_b_im, ssm_c_re, ssm_c_im, ssm_d, ssm_w_glu, ssm_b_glu, mla_q_norm_g, mla_w_q_up, mla_kv_norm_g, mla_w_kv_up, out_norm_ssm_g, out_norm_mla_g, w_out, norm_xattn_g, norm_mem_g, xattn_w_q, xattn_w_k, xattn_w_v, xattn_w_o, norm_moe_g, moe_w_group, moe_b_group, moe_w_expert, moe_b_expert, moe_w_gate, moe_w_up, moe_w_down, norm_final_g):
    bsz, seq, d = x.shape
    depth = w_in.shape[0]
    assert depth == 1, "the final RMSNorm is fused into the last stage of a single layer"
    d_ssm = ssm_d.shape[-1]
    q_rank = mla_q_norm_g.shape[-1]
    kv_rank = mla_kv_norm_g.shape[-1]
    n_grp = d_ssm // SSM_GROUP
    s1, s2, s3 = d_ssm, d_ssm + q_rank, d_ssm + q_rank + kv_rank
    row = lambda v: v.reshape(1, -1).astype(F32)
    tiles = _tiles(seq)

    half = MLA_ROPE // 2
    inv_freq = ROPE_THETA ** (-jnp.arange(half, dtype=F32) / half)
    invf = jnp.zeros((LANES,), F32).at[MLA_NOPE:MLA_NOPE + half].set(inv_freq)
    invf = invf.at[MLA_NOPE + half:MLA_QK].set(inv_freq).reshape(1, LANES)
    pos = positions.reshape(bsz, seq, 1)

    for l in range(depth):
        a_re, a_im, bb_re, bb_im = _ssm_prep(ssm_lam_re[l], ssm_lam_im[l], ssm_log_dt[l], ssm_b_re[l], ssm_b_im[l])
        gh = n_grp // 2
        bbd = jnp.stack([
            jnp.concatenate([_block_diag(bb_re[hf * gh:(hf + 1) * gh]), _block_diag(bb_im[hf * gh:(hf + 1) * gh])],
                            axis=1) for hf in range(2)]).astype(BF16)
        c_re_t = jnp.transpose(ssm_c_re[l], (0, 2, 1))
        c_im_t = jnp.transpose(ssm_c_im[l], (0, 2, 1))
        cmat = jnp.stack([
            jnp.concatenate([_block_diag(c_re_t[hf * gh:(hf + 1) * gh]), -_block_diag(c_im_t[hf * gh:(hf + 1) * gh])],
                            axis=0) for hf in range(2)]).astype(BF16)
        atab = jnp.stack([
            jnp.repeat(arr.reshape(2, 1, gh * SSM_STATE), bsz, axis=1).reshape(2 * bsz, gh * SSM_STATE)
            for arr in (a_re, a_im)])

        wi = w_in[l]
        wu = wi[:, :s1].astype(BF16)
        wcq = wi[:, s1:s2].astype(BF16)
        wckv = wi[:, s2:s3].astype(BF16)
        wkr = jnp.pad(wi[:, s3:], ((0, 0), (MLA_NOPE, LANES - MLA_QK))).astype(BF16)
        wq = _pad_heads(mla_w_q_up[l], MLA_QK).astype(BF16)
        wkv = mla_w_kv_up[l].reshape(kv_rank, MLA_HEADS, MLA_NOPE + MLA_V)
        wk = _pad_heads(wkv[:, :, :MLA_NOPE].reshape(kv_rank, -1), MLA_NOPE).astype(BF16)
        wv = _pad_heads(wkv[:, :, MLA_NOPE:].reshape(kv_rank, -1), MLA_V).astype(BF16)
        u, q, k, v = _proj(x, pos, row(norm_mix_g[l]), wu, wcq, wckv, wkr, row(mla_q_norm_g[l]), wq,
                           row(mla_kv_norm_g[l]), wk, wv, invf, tm=tiles.proj)

        y_ssm = _ssm(u, bbd, atab, cmat, row(ssm_d[l]), ssm_w_glu[l].astype(BF16), row(ssm_b_glu[l]),
                     row(out_norm_ssm_g[l]), tt=tiles.ssm)
        y_mla = _attn(q, k, v, tq=tiles.attn_q, tk=tiles.attn_k, hp=tiles.attn_heads, strip=tiles.attn_strip)

        km, vm = _memkv(mem, row(norm_mem_g[l]), xattn_w_k[l].astype(BF16), xattn_w_v[l].astype(BF16))
        wos = w_out[l][:d_ssm].astype(BF16)
        wom = w_out[l][d_ssm:].astype(BF16)
        wr = jnp.concatenate([moe_w_expert[l].reshape(d, MOE_EXPERTS), moe_w_group[l]], axis=1)
        wr = jnp.pad(wr, ((0, 0), (0, LANES - MOE_EXPERTS - MOE_GROUPS))).astype(F32)
        wr_hi = wr.astype(BF16)
        wr_lo = (wr - wr_hi.astype(F32)).astype(BF16)
        br = jnp.concatenate([moe_b_expert[l].reshape(-1), moe_b_group[l]])
        br = jnp.pad(br, (0, LANES - MOE_EXPERTS - MOE_GROUPS)).reshape(1, LANES).astype(F32)
        x2, h3, route = _post(x, y_ssm, y_mla, row(out_norm_mla_g[l]), wos, wom, row(norm_xattn_g[l]),
                              xattn_w_q[l].astype(BF16), km, vm, xattn_w_o[l].astype(BF16), row(norm_moe_g[l]),
                              wr_hi, wr_lo, br, tm=tiles.post, sub=tiles.post_sub)

        meta = _route_metadata(route, tiles.moe)
        xs, gs = _dispatch(h3, route, meta["pos"], meta["n_pad"], tt=tiles.copy)
        ys = _moe(xs, gs, meta["tile_g"], meta["n_valid"], moe_w_gate[l].astype(BF16), moe_w_up[l].astype(BF16),
                  moe_w_down[l].astype(BF16), tm=tiles.moe)
        x = _combine(ys, meta["pos"], x2, row(norm_final_g), tt=tiles.copy)
    return x
```

```python
import functools
import math
from typing import NamedTuple

import jax
import jax.numpy as jnp
from jax import lax
from jax.experimental import pallas as pl
from jax.experimental.pallas import tpu as pltpu

F32 = jnp.float32
BF16 = jnp.bfloat16

EPS = 1e-6
NEG_INF = -1e30
CHUNK = 64

LANES = 128
SSM_GROUP = 16
SSM_STATE = 64
MLA_HEADS = 8
MLA_NOPE = 64
MLA_ROPE = 32
MLA_QK = MLA_NOPE + MLA_ROPE
MLA_V = 64
ROPE_THETA = 10000.0
XATTN_HEADS = 4
MOE_GROUPS = 4
MOE_PER_GROUP = 8
MOE_EXPERTS = MOE_GROUPS * MOE_PER_GROUP
VMEM_LIMIT = 48 * 1024 * 1024
VMEM_LIMIT_LARGE = 56 * 1024 * 1024
LOG2E = math.log2(math.e)
VT_ROWS = 80
GROUP_LANE = 64


class Tiles(NamedTuple):
    proj: int
    ssm: int
    attn_q: int
    attn_k: int
    attn_heads: int
    attn_strip: int
    post: int
    post_sub: int
    moe: int
    copy: int


def _tiles(seq):
    return Tiles(proj=min(512, seq), ssm=min(256, seq), attn_q=min(512, seq), attn_k=min(256, seq),
                 attn_heads=MLA_HEADS, attn_strip=32, post=min(512, seq), post_sub=min(256, seq),
                 moe=256, copy=min(256, seq))


def _dot(a, b):
    return jnp.dot(a, b, preferred_element_type=F32)


def _dot_nt(a, b):
    return lax.dot_general(a, b, (((1,), (1,)), ((), ())), preferred_element_type=F32)


def _rms(x, g):
    ms = jnp.mean(x * x, axis=-1, keepdims=True)
    return x * lax.rsqrt(ms + EPS) * g


def _params(*sem, vmem=VMEM_LIMIT):
    return pltpu.CompilerParams(dimension_semantics=sem, vmem_limit_bytes=vmem)


def _full(shape):
    zeros = (0,) * len(shape)
    return pl.BlockSpec(shape, lambda *_: zeros)


def _ssm_prep_kernel(lre_ref, lim_ref, ldt_ref, bre_ref, bim_ref, are_ref, aim_ref, bbre_ref, bbim_ref):
    lre = jnp.minimum(lre_ref[...], -1e-4)
    lim = lim_ref[...]
    dt = jnp.exp(ldt_ref[...])
    mag = jnp.exp(lre * dt)
    are = mag * jnp.cos(lim * dt)
    aim = mag * jnp.sin(lim * dt)
    are_ref[...] = are
    aim_ref[...] = aim
    nre = are - 1.0
    den = lre * lre + lim * lim
    fre = (nre * lre + aim * lim) / den
    fim = (aim * lre - nre * lim) / den
    bre = bre_ref[...]
    bim = bim_ref[...]
    bbre_ref[...] = fre * bre - fim * bim
    bbim_ref[...] = fre * bim + fim * bre


def _ssm_prep(lam_re, lam_im, log_dt, b_re, b_im):
    g, p = lam_re.shape
    hh = b_re.shape[-1]
    bt_re = jnp.transpose(b_re, (0, 2, 1))
    bt_im = jnp.transpose(b_im, (0, 2, 1))
    outs = pl.pallas_call(
        _ssm_prep_kernel,
        out_shape=[jax.ShapeDtypeStruct((g, 1, p), F32)] * 2 + [jax.ShapeDtypeStruct((g, hh, p), F32)] * 2,
        name="ssm_prep",
    )(lam_re.reshape(g, 1, p), lam_im.reshape(g, 1, p), log_dt.reshape(g, 1, 1), bt_re, bt_im)
    a_re, a_im, bb_re, bb_im = outs
    return a_re.reshape(g, p), a_im.reshape(g, p), bb_re, bb_im


def _proj_kernel(x_ref, pos_ref, gmix_ref, wu_ref, wcq_ref, wckv_ref, wkrt_ref, qg_ref, wqt_ref, kvg_ref,
                 wk_ref, wvt_ref, invf_ref, u_ref, q_ref, k_ref, v_ref):
    half = MLA_ROPE // 2
    h = _rms(x_ref[0], gmix_ref[...]).astype(BF16)
    u_ref[0] = _dot(h, wu_ref[...]).astype(BF16)
    cqn = _rms(_dot(h, wcq_ref[...]), qg_ref[...]).astype(BF16)
    ckvn = _rms(_dot(h, wckv_ref[...]), kvg_ref[...]).astype(BF16)

    ang = invf_ref[...] * pos_ref[0].astype(F32)
    cosv = jnp.cos(ang)
    sinv = jnp.sin(ang)

    def rot_t(blk):
        x1 = blk[MLA_NOPE:MLA_NOPE + half]
        x2 = blk[MLA_NOPE + half:MLA_QK]
        return jnp.concatenate([blk[:MLA_NOPE], x1 * cosv - x2 * sinv, x1 * sinv + x2 * cosv, blk[MLA_QK:]], axis=0)

    qt = _dot_nt(wqt_ref[...], cqn) * (MLA_QK ** -0.5 * LOG2E)
    vt = _dot_nt(wvt_ref[...], ckvn)
    kk = _dot(ckvn, wk_ref[...])
    kpe = rot_t(_dot_nt(wkrt_ref[...], h)).T
    ones_row = jnp.where(lax.broadcasted_iota(jnp.int32, (VT_ROWS, 1), 0) == MLA_V, 1.0, 0.0)
    for hh in range(MLA_HEADS):
        q_ref[0, hh] = rot_t(qt[hh * LANES:(hh + 1) * LANES]).astype(BF16)
        k_ref[0, hh] = (kk[:, hh * LANES:(hh + 1) * LANES] + kpe).astype(BF16)
        v_ref[0, hh] = (vt[hh * VT_ROWS:(hh + 1) * VT_ROWS] + ones_row).astype(BF16)


def _proj(x, pos, gmix, wu, wcq, wckv, wkrt, qg, wqt, kvg, wk, wvt, invf, tm):
    b, s, d = x.shape
    grid = (b, s // tm)
    tok = lambda n: pl.BlockSpec((1, tm, n), lambda i, j: (i, j, 0))
    head = pl.BlockSpec((1, MLA_HEADS, tm, LANES), lambda i, j: (i, 0, j, 0))
    head_t = pl.BlockSpec((1, MLA_HEADS, LANES, tm), lambda i, j: (i, 0, 0, j))
    head_vt = pl.BlockSpec((1, MLA_HEADS, VT_ROWS, tm), lambda i, j: (i, 0, 0, j))
    tshape = lambda n: jax.ShapeDtypeStruct((b, MLA_HEADS, n, s), BF16)
    return pl.pallas_call(
        _proj_kernel,
        grid=grid,
        in_specs=[tok(d), pl.BlockSpec((1, 1, tm), lambda i, j: (i, 0, j)), _full(gmix.shape), _full(wu.shape),
                  _full(wcq.shape), _full(wckv.shape), _full(wkrt.shape), _full(qg.shape), _full(wqt.shape),
                  _full(kvg.shape), _full(wk.shape), _full(wvt.shape), _full(invf.shape)],
        out_specs=[tok(wu.shape[1]), head_t, head, head_vt],
        out_shape=[jax.ShapeDtypeStruct((b, s, wu.shape[1]), BF16), tshape(LANES),
                   jax.ShapeDtypeStruct((b, MLA_HEADS, s, LANES), BF16), tshape(VT_ROWS)],
        compiler_params=_params("parallel", "parallel"),
        name="proj",
    )(x, pos, gmix, wu, wcq, wckv, wkrt, qg, wqt, kvg, wk, wvt, invf)


def _ssm_kernel(u_ref, bbd_ref, atab_ref, cmat_ref, dskip_ref, wglu_ref, bglu_ref, gout_ref, y_ref,
                xs_ref, st_ref, *, tt, slab, nb, lane_tiles):
    j = pl.program_id(0)
    half_w = u_ref.shape[-1] // 2

    @pl.when(j == 0)
    def _():
        st_ref[...] = jnp.zeros_like(st_ref)

    for hf in range(2):
        ub = u_ref[:, :, hf * half_w:(hf + 1) * half_w].reshape(nb * tt, half_w)
        bu = _dot(ub, bbd_ref[hf])
        for b in range(nb):
            r0 = (hf * nb + b) * slab
            for c in range(2 * lane_tiles):
                xs_ref[c, r0:r0 + tt, :] = bu[b * tt:(b + 1) * tt, c * LANES:(c + 1) * LANES]

    rows = 2 * nb
    group = 4
    for c0 in range(0, lane_tiles, group):
        cs = list(range(c0, c0 + group))
        a_re = [atab_ref[0, :, c * LANES:(c + 1) * LANES] for c in cs]
        a_im = [atab_ref[1, :, c * LANES:(c + 1) * LANES] for c in cs]
        init = tuple(st_ref[0, :, c * LANES:(c + 1) * LANES] for c in cs) + \
            tuple(st_ref[1, :, c * LANES:(c + 1) * LANES] for c in cs)

        def step(t, carry, cs=cs, a_re=a_re, a_im=a_im):
            new_re, new_im = [], []
            for i, c in enumerate(cs):
                x_re, x_im = carry[i], carry[group + i]
                idx = pl.ds(t, rows, stride=slab)
                n_re = a_re[i] * x_re - a_im[i] * x_im + xs_ref[c, idx, :]
                n_im = a_re[i] * x_im + a_im[i] * x_re + xs_ref[lane_tiles + c, idx, :]
                xs_ref[c, idx, :] = n_re
                xs_ref[lane_tiles + c, idx, :] = n_im
                new_re.append(n_re)
                new_im.append(n_im)
            return tuple(new_re) + tuple(new_im)

        fin = lax.fori_loop(0, tt, step, init, unroll=4)
        for i, c in enumerate(cs):
            st_ref[0, :, c * LANES:(c + 1) * LANES] = fin[i]
            st_ref[1, :, c * LANES:(c + 1) * LANES] = fin[group + i]

    ys = []
    for hf in range(2):
        xb = []
        for b in range(nb):
            r0 = (hf * nb + b) * slab
            xb.append(jnp.concatenate([xs_ref[c, r0:r0 + tt, :] for c in range(2 * lane_tiles)], axis=1))
        xh = jnp.concatenate(xb, axis=0).astype(BF16)
        ys.append(_dot(xh, cmat_ref[hf]))
    y = jnp.concatenate(ys, axis=1)
    u = u_ref[...].astype(F32).reshape(nb * tt, 2 * half_w)
    y = jax.nn.gelu(y + dskip_ref[...] * u)
    y = y * jax.nn.sigmoid(_dot(y.astype(BF16), wglu_ref[...]) + bglu_ref[...])
    y = _rms(y, gout_ref[...])
    y_ref[...] = y.reshape(nb, tt, 2 * half_w).astype(BF16)


def _ssm(u, bbd, atab, cmat, dskip, wglu, bglu, gout, tt):
    nb, s, dssm = u.shape
    lane_tiles = bbd.shape[-1] // (2 * LANES)
    slab = tt + 4
    kern = functools.partial(_ssm_kernel, tt=tt, slab=slab, nb=nb, lane_tiles=lane_tiles)
    return pl.pallas_call(
        kern,
        grid=(s // tt,),
        in_specs=[pl.BlockSpec((nb, tt, dssm), lambda j: (0, j, 0)), _full(bbd.shape), _full(atab.shape),
                  _full(cmat.shape), _full(dskip.shape), _full(wglu.shape), _full(bglu.shape), _full(gout.shape)],
        out_specs=pl.BlockSpec((nb, tt, dssm), lambda j: (0, j, 0)),
        out_shape=jax.ShapeDtypeStruct((nb, s, dssm), BF16),
        scratch_shapes=[pltpu.VMEM((2 * lane_tiles, 2 * nb * slab, LANES), F32),
                        pltpu.VMEM((2, 2 * nb, lane_tiles * LANES), F32)],
        compiler_params=_params("arbitrary"),
        name="ssm",
    )(u, bbd, atab, cmat, dskip, wglu, bglu, gout)


def _attn_kernel(qt_ref, k_ref, vt_ref, o_ref, s_ref, p_ref, m_ref, a_ref, acc_ref, *, tq, tk, hp, strip):
    qi = pl.program_id(2)
    m_ref[...] = jnp.full(m_ref.shape, NEG_INF, F32)
    acc_ref[...] = jnp.zeros(acc_ref.shape, F32)
    q_chunk = lax.broadcasted_iota(jnp.int32, (1, tq), 1) // CHUNK

    def tile(k0, diag_off):
        qs = slice(0 if diag_off is None else diag_off, tq)
        for hh in range(hp):
            s_ref[hh, :, qs] = _dot(k_ref[0, hh, pl.ds(k0, tk), :], qt_ref[0, hh, :, qs])

        def strip_of(hh, r):
            s = s_ref[hh, r:r + strip, qs]
            if diag_off is None:
                return s
            return jnp.where(q_chunk[:, qs] >= (diag_off + r) // CHUNK, s, NEG_INF)

        for hh in range(hp):
            mt = strip_of(hh, 0)
            for r in range(strip, tk, strip):
                mt = jnp.maximum(mt, strip_of(hh, r))
            m_old = m_ref[hh, :, qs]
            m_new = jnp.maximum(m_old, jnp.max(mt, axis=0, keepdims=True))
            a_ref[hh, :, qs] = jnp.exp2(m_old - m_new)
            m_ref[hh, :, qs] = m_new
            for r in range(0, tk, strip):
                p_ref[hh, r:r + strip, qs] = jnp.exp2(strip_of(hh, r) - m_new).astype(BF16)
        for hh in range(hp):
            pv = _dot(vt_ref[0, hh, :, pl.ds(k0, tk)], p_ref[hh, :, qs])
            acc_ref[hh, :, qs] = acc_ref[hh, :, qs] * a_ref[hh, :, qs] + pv

    def body(j, c):
        tile(pl.multiple_of(j * tk, tk), None)
        return c

    lax.fori_loop(0, qi * (tq // tk), body, 0)
    for off in range(0, tq, tk):
        tile(pl.multiple_of(qi * tq + off, tk), off)
    feat = lax.broadcasted_iota(jnp.int32, (VT_ROWS, 1), 0)
    pad = jnp.zeros((LANES - VT_ROWS, tq), F32)
    for hh in range(hp):
        acc = acc_ref[hh]
        out = acc * (1.0 / acc[MLA_V:MLA_V + 1, :])
        out = jnp.where(feat < MLA_V, out, 0.0)
        o_ref[0, hh] = jnp.concatenate([out, pad], axis=0).T.astype(BF16)


def _attn(qt, k, vt, tq, tk, hp, strip):
    b, h, s, _ = k.shape
    assert strip <= CHUNK and CHUNK % strip == 0 and tk % CHUNK == 0 and tq % tk == 0
    return pl.pallas_call(
        functools.partial(_attn_kernel, tq=tq, tk=tk, hp=hp, strip=strip),
        grid=(b, h // hp, s // tq),
        in_specs=[pl.BlockSpec((1, hp, LANES, tq), lambda i, j, t: (i, j, 0, t)),
                  pl.BlockSpec((1, hp, s, LANES), lambda i, j, t: (i, j, 0, 0)),
                  pl.BlockSpec((1, hp, VT_ROWS, s), lambda i, j, t: (i, j, 0, 0))],
        out_specs=pl.BlockSpec((1, hp, tq, LANES), lambda i, j, t: (i, j, t, 0)),
        out_shape=jax.ShapeDtypeStruct((b, h, s, LANES), BF16),
        scratch_shapes=[pltpu.VMEM((hp, tk, tq), F32), pltpu.VMEM((hp, tk, tq), BF16),
                        pltpu.VMEM((hp, 1, tq), F32), pltpu.VMEM((hp, 1, tq), F32),
                        pltpu.VMEM((hp, VT_ROWS, tq), F32)],
        compiler_params=_params("parallel", "parallel", "arbitrary", vmem=VMEM_LIMIT_LARGE),
        name="attn",
    )(qt, k, vt)


def _memkv_kernel(mem_ref, g_ref, wk_ref, wv_ref, k_ref, v_ref, *, hd):
    mn = _rms(mem_ref[0], g_ref[...]).astype(BF16)
    kk = _dot(mn, wk_ref[...])
    vv = _dot(mn, wv_ref[...])
    for hh in range(XATTN_HEADS):
        k_ref[0, hh] = kk[:, hh * hd:(hh + 1) * hd].astype(BF16)
        v_ref[0, hh] = vv[:, hh * hd:(hh + 1) * hd].astype(BF16)


def _memkv(mem, g, wk, wv):
    b, nm, d = mem.shape
    hd = d // XATTN_HEADS
    ospec = pl.BlockSpec((1, XATTN_HEADS, nm, hd), lambda i: (i, 0, 0, 0))
    return pl.pallas_call(
        functools.partial(_memkv_kernel, hd=hd),
        grid=(b,),
        in_specs=[pl.BlockSpec((1, nm, d), lambda i: (i, 0, 0)), _full(g.shape), _full(wk.shape), _full(wv.shape)],
        out_specs=[ospec, ospec],
        out_shape=[jax.ShapeDtypeStruct((b, XATTN_HEADS, nm, hd), BF16)] * 2,
        compiler_params=_params("parallel"),
        name="mem_kv",
    )(mem, g, wk, wv)


def _post_kernel(x_ref, ys_ref, ym_ref, gm_ref, wos_ref, wom_ref, gx_ref, wq_ref, km_ref, vm_ref, wo_ref,
                 gmoe_ref, wrh_ref, wrl_ref, br_ref, x2_ref, route_ref, *, hd, sub):
    for r0 in range(0, x_ref.shape[1], sub):
        rows = slice(r0, r0 + sub)
        x = x_ref[0, rows, :]
        heads = [ym_ref[0, hh, rows, :].astype(F32) for hh in range(MLA_HEADS)]
        ym = jnp.concatenate([heads[k] + pltpu.roll(heads[k + 1], MLA_V, 1) for k in range(0, MLA_HEADS, 2)],
                             axis=1)
        ymn = _rms(ym, gm_ref[...]).astype(BF16)
        x1 = x + _dot(ys_ref[0, rows, :], wos_ref[...]) + _dot(ymn, wom_ref[...])

        h2 = _rms(x1, gx_ref[...]).astype(BF16)
        qx = (_dot(h2, wq_ref[...]) * (hd ** -0.5)).astype(BF16)
        outs = []
        for hh in range(XATTN_HEADS):
            s = _dot_nt(qx[:, hh * hd:(hh + 1) * hd], km_ref[0, hh])
            p = jnp.exp(s - jnp.max(s, axis=-1, keepdims=True))
            p = p * (1.0 / jnp.sum(p, axis=-1, keepdims=True))
            outs.append(_dot(p.astype(BF16), vm_ref[0, hh]))
        o = jnp.concatenate(outs, axis=1).astype(BF16)
        x2 = x1 + _dot(o, wo_ref[...])
        x2_ref[0, rows, :] = x2

        h3 = _rms(x2, gmoe_ref[...])

        h_hi = h3.astype(BF16)
        h_lo = (h3 - h_hi.astype(F32)).astype(BF16)
        logits = (_dot(h_hi, wrh_ref[...]) + _dot(h_hi, wrl_ref[...]) + _dot(h_lo, wrh_ref[...])) + br_ref[...]
        lane = lax.broadcasted_iota(jnp.int32, logits.shape, 1)
        is_g = (lane >= MOE_EXPERTS) & (lane < MOE_EXPERTS + MOE_GROUPS)
        gl = jnp.where(is_g, logits, NEG_INF)
        gmax = jnp.max(gl, axis=-1, keepdims=True)
        g_w = 1.0 / jnp.sum(jnp.exp(gl - gmax), axis=-1, keepdims=True)
        g_idx = jnp.min(jnp.where(gl == gmax, lane, 4 * LANES), axis=-1, keepdims=True) - MOE_EXPERTS
        in_grp = (lane >= g_idx * MOE_PER_GROUP) & (lane < (g_idx + 1) * MOE_PER_GROUP)
        el = jnp.where(in_grp, logits, NEG_INF)
        v1 = jnp.max(el, axis=-1, keepdims=True)
        i1 = jnp.min(jnp.where(el == v1, lane, 4 * LANES), axis=-1, keepdims=True)
        el2 = jnp.where(lane == i1, NEG_INF, el)
        v2 = jnp.max(el2, axis=-1, keepdims=True)
        i2 = jnp.min(jnp.where(el2 == v2, lane, 4 * LANES), axis=-1, keepdims=True)
        e2 = jnp.exp(v2 - v1)
        w1 = g_w / (1.0 + e2)
        w2 = g_w * e2 / (1.0 + e2)
        route_ref[0, rows, :] = (jnp.where(lane == i1, w1, 0.0) + jnp.where(lane == i2, w2, 0.0)
                                 + jnp.where(lane == GROUP_LANE, g_idx.astype(F32), 0.0))


def _post(x, ys, ym, gm, wos, wom, gx, wq, km, vm, wo, gmoe, wrh, wrl, br, tm, sub):
    b, s, d = x.shape
    hd = d // XATTN_HEADS
    tok = lambda n: pl.BlockSpec((1, tm, n), lambda i, j: (i, j, 0))
    mem = pl.BlockSpec((1,) + km.shape[1:], lambda i, j: (i, 0, 0, 0))
    return pl.pallas_call(
        functools.partial(_post_kernel, hd=hd, sub=sub),
        grid=(b, s // tm),
        in_specs=[tok(d), tok(ys.shape[-1]), pl.BlockSpec((1, MLA_HEADS, tm, LANES), lambda i, j: (i, 0, j, 0)),
                  _full(gm.shape), _full(wos.shape), _full(wom.shape), _full(gx.shape), _full(wq.shape), mem, mem,
                  _full(wo.shape), _full(gmoe.shape), _full(wrh.shape), _full(wrl.shape), _full(br.shape)],
        out_specs=[tok(d), tok(LANES)],
        out_shape=[jax.ShapeDtypeStruct((b, s, d), F32), jax.ShapeDtypeStruct((b, s, LANES), F32)],
        compiler_params=_params("parallel", "parallel"),
        name="post",
    )(x, ys, ym, gm, wos, wom, gx, wq, km, vm, wo, gmoe, wrh, wrl, br)


def _route_metadata(route, tm):
    b, s, _ = route.shape
    n_tiles = (s + MOE_GROUPS * tm) // tm
    gid = route[..., GROUP_LANE].astype(jnp.int32)
    groups = jnp.arange(MOE_GROUPS, dtype=jnp.int32)[None, :, None]
    onehot = (gid[:, None, :] == groups).astype(jnp.int32)
    csum = jnp.cumsum(onehot, axis=2)
    cnt = csum[:, :, -1]
    padded = (cnt + tm - 1) // tm * tm
    seg_end = jnp.cumsum(padded, axis=-1)
    dest = jnp.sum(onehot * ((seg_end - padded)[:, :, None] + csum - 1), axis=1)
    tile_start = jnp.arange(n_tiles, dtype=jnp.int32) * tm
    tile_g = jnp.sum((seg_end[:, None, :] <= tile_start[None, :, None]).astype(jnp.int32), axis=-1)
    tile_g = jnp.minimum(tile_g, MOE_GROUPS - 1)
    n_valid = (seg_end[:, -1] // tm).astype(jnp.int32)
    return dict(pos=dest, tile_g=tile_g.reshape(b * n_tiles), n_valid=n_valid, n_pad=n_tiles * tm)


def _dispatch_kernel(h_ref, g_ref, dst_ref, xs_ref, gs_ref, *, tt):
    @pl.when(pl.program_id(1) == 0)
    def _():
        xs_ref[...] = jnp.zeros(xs_ref.shape, xs_ref.dtype)
        gs_ref[...] = jnp.zeros(gs_ref.shape, gs_ref.dtype)

    for t in range(tt):
        r = dst_ref[0, 0, t]
        xs_ref[0, pl.ds(r, 1), :] = h_ref[0, t:t + 1, :]
        gs_ref[0, pl.ds(r, 1), :] = g_ref[0, t:t + 1, :]


def _dispatch(x2, route, pos, n_pad, tt):
    b, s, d = x2.shape
    nj = s // tt
    row_block = lambda n: pl.BlockSpec((1, n_pad, n), lambda i, j: (i, 0, 0))
    return pl.pallas_call(
        functools.partial(_dispatch_kernel, tt=tt),
        grid=(b, nj),
        in_specs=[pl.BlockSpec((1, tt, d), lambda i, j: (i, j, 0)),
                  pl.BlockSpec((1, tt, LANES), lambda i, j: (i, j, 0)),
                  pl.BlockSpec((1, 1, tt), lambda i, j: (i * nj + j, 0, 0), memory_space=pltpu.SMEM)],
        out_specs=[row_block(d), row_block(LANES)],
        out_shape=[jax.ShapeDtypeStruct((b, n_pad, d), F32), jax.ShapeDtypeStruct((b, n_pad, LANES), F32)],
        compiler_params=_params("parallel", "arbitrary", vmem=VMEM_LIMIT_LARGE),
        name="dispatch",
    )(x2, route, pos.reshape(b * nj, 1, tt))


def _moe_kernel(tg_ref, nv_ref, x_ref, gate_ref, gmoe_ref, wg_ref, wu_ref, wd_ref, o_ref, *, nt):
    bi = pl.program_id(0)
    ti = pl.program_id(1)
    grp = tg_ref[bi * nt + ti]

    @pl.when(ti < nv_ref[bi])
    def _():
        h = _rms(x_ref[0], gmoe_ref[...]).astype(BF16)
        gate = gate_ref[0]
        lane = lax.broadcasted_iota(jnp.int32, gate.shape, 1)
        acc = None
        for e in range(MOE_PER_GROUP):
            a = jax.nn.silu(_dot(h, wg_ref[e])) * _dot(h, wu_ref[e])
            gcol = jnp.sum(jnp.where(lane == grp * MOE_PER_GROUP + e, gate, 0.0), axis=-1, keepdims=True)
            y = _dot((a * gcol).astype(BF16), wd_ref[e])
            acc = y if acc is None else acc + y
        o_ref[0] = acc

    @pl.when(ti >= nv_ref[bi])
    def _():
        o_ref[0] = jnp.zeros(o_ref.shape[1:], o_ref.dtype)


def _moe(xs, gs, gmoe, tile_g, n_valid, wg, wu, wd, tm):
    b, n_pad, d = xs.shape
    nt = n_pad // tm
    ff = wg.shape[-1]
    group = lambda i, j, tg, nv: (tg[i * nt + j], 0, 0)
    tok = lambda n: pl.BlockSpec((1, tm, n), lambda i, j, tg, nv: (i, j, 0))
    grid_spec = pltpu.PrefetchScalarGridSpec(
        num_scalar_prefetch=2,
        grid=(b, nt),
        in_specs=[tok(d), tok(LANES), pl.BlockSpec(gmoe.shape, lambda i, j, tg, nv: (0, 0)),
                  pl.BlockSpec((MOE_PER_GROUP, d, ff), group),
                  pl.BlockSpec((MOE_PER_GROUP, d, ff), group), pl.BlockSpec((MOE_PER_GROUP, ff, d), group)],
        out_specs=tok(d),
    )
    return pl.pallas_call(
        functools.partial(_moe_kernel, nt=nt),
        grid_spec=grid_spec,
        out_shape=jax.ShapeDtypeStruct((b, n_pad, d), F32),
        compiler_params=_params("parallel", "arbitrary"),
        name="moe",
    )(tile_g, n_valid, xs, gs, gmoe, wg, wu, wd)


def _combine_kernel(ys_ref, pos_ref, x2_ref, gf_ref, o_ref, buf_ref, *, tt):
    for t in range(tt):
        buf_ref[t:t + 1, :] = ys_ref[0, pl.ds(pos_ref[0, 0, t], 1), :]
    o_ref[0] = _rms(x2_ref[0] + buf_ref[...], gf_ref[...])


def _combine(ys, pos, x2, gf, tt):
    b, s, d = x2.shape
    n_pad = ys.shape[1]
    nj = s // tt
    return pl.pallas_call(
        functools.partial(_combine_kernel, tt=tt),
        grid=(b, nj),
        in_specs=[pl.BlockSpec((1, n_pad, d), lambda i, j: (i, 0, 0)),
                  pl.BlockSpec((1, 1, tt), lambda i, j: (i * nj + j, 0, 0), memory_space=pltpu.SMEM),
                  pl.BlockSpec((1, tt, d), lambda i, j: (i, j, 0)), _full(gf.shape)],
        out_specs=pl.BlockSpec((1, tt, d), lambda i, j: (i, j, 0)),
        out_shape=jax.ShapeDtypeStruct((b, s, d), F32),
        scratch_shapes=[pltpu.VMEM((tt, d), F32)],
        compiler_params=_params("parallel", "arbitrary", vmem=VMEM_LIMIT_LARGE),
        name="combine",
    )(ys, pos.reshape(b * nj, 1, tt), x2, gf)


def _pad_heads(w, per_head, offset=0):
    k = w.shape[0]
    w = w.reshape(k, MLA_HEADS, per_head)
    w = jnp.pad(w, ((0, 0), (0, 0), (offset, LANES - per_head - offset)))
    return w.reshape(k, MLA_HEADS * LANES)


def _block_diag(blocks):
    n, r, c = blocks.shape
    eye = jnp.eye(n, dtype=blocks.dtype)
    return (eye[:, None, :, None] * blocks[:, :, None, :]).reshape(n * r, n * c)


def kernel(x, mem, positions, norm_mix_g, w_in, ssm_lam_re, ssm_lam_im, ssm_log_dt, ssm_b_re, ssm_b_im, ssm_c_re, ssm_c_im, ssm_d, ssm_w_glu, ssm_b_glu, mla_q_norm_g, mla_w_q_up, mla_kv_norm_g, mla_w_kv_up, out_norm_ssm_g, out_norm_mla_g, w_out, norm_xattn_g, norm_mem_g, xattn_w_q, xattn_w_k, xattn_w_v, xattn_w_o, norm_moe_g, moe_w_group, moe_b_group, moe_w_expert, moe_b_expert, moe_w_gate, moe_w_up, moe_w_down, norm_final_g):
    bsz, seq, d = x.shape
    depth = w_in.shape[0]
    assert depth == 1, "the final RMSNorm is fused into the last stage of a single layer"
    d_ssm = ssm_d.shape[-1]
    q_rank = mla_q_norm_g.shape[-1]
    kv_rank = mla_kv_norm_g.shape[-1]
    n_grp = d_ssm // SSM_GROUP
    s1, s2, s3 = d_ssm, d_ssm + q_rank, d_ssm + q_rank + kv_rank
    row = lambda v: v.reshape(1, -1).astype(F32)
    tiles = _tiles(seq)

    half = MLA_ROPE // 2
    invf = (ROPE_THETA ** (-jnp.arange(half, dtype=F32) / half)).reshape(half, 1)
    pos = positions.reshape(bsz, 1, seq)

    for l in range(depth):
        a_re, a_im, bb_re, bb_im = _ssm_prep(ssm_lam_re[l], ssm_lam_im[l], ssm_log_dt[l], ssm_b_re[l], ssm_b_im[l])
        gh = n_grp // 2
        bbd = jnp.stack([
            jnp.concatenate([_block_diag(bb_re[hf * gh:(hf + 1) * gh]), _block_diag(bb_im[hf * gh:(hf + 1) * gh])],
                            axis=1) for hf in range(2)]).astype(BF16)
        c_re_t = jnp.transpose(ssm_c_re[l], (0, 2, 1))
        c_im_t = jnp.transpose(ssm_c_im[l], (0, 2, 1))
        cmat = jnp.stack([
            jnp.concatenate([_block_diag(c_re_t[hf * gh:(hf + 1) * gh]), -_block_diag(c_im_t[hf * gh:(hf + 1) * gh])],
                            axis=0) for hf in range(2)]).astype(BF16)
        atab = jnp.stack([
            jnp.repeat(arr.reshape(2, 1, gh * SSM_STATE), bsz, axis=1).reshape(2 * bsz, gh * SSM_STATE)
            for arr in (a_re, a_im)])

        wi = w_in[l]
        wu = wi[:, :s1].astype(BF16)
        wcq = wi[:, s1:s2].astype(BF16)
        wckv = wi[:, s2:s3].astype(BF16)
        wkrt = jnp.pad(wi[:, s3:], ((0, 0), (MLA_NOPE, LANES - MLA_QK))).T.astype(BF16)
        wqt = _pad_heads(mla_w_q_up[l], MLA_QK).T.astype(BF16)
        wkv = mla_w_kv_up[l].reshape(kv_rank, MLA_HEADS, MLA_NOPE + MLA_V)
        wk = _pad_heads(wkv[:, :, :MLA_NOPE].reshape(kv_rank, -1), MLA_NOPE).astype(BF16)
        wvt = jnp.pad(wkv[:, :, MLA_NOPE:], ((0, 0), (0, 0), (0, VT_ROWS - MLA_V)))
        wvt = wvt.reshape(kv_rank, MLA_HEADS * VT_ROWS).T.astype(BF16)
        u, q, k, v = _proj(x, pos, row(norm_mix_g[l]), wu, wcq, wckv, wkrt, row(mla_q_norm_g[l]), wqt,
                           row(mla_kv_norm_g[l]), wk, wvt, invf, tm=tiles.proj)

        y_ssm = _ssm(u, bbd, atab, cmat, row(ssm_d[l]), ssm_w_glu[l].astype(BF16), row(ssm_b_glu[l]),
                     row(out_norm_ssm_g[l]), tt=tiles.ssm)
        y_mla = _attn(q, k, v, tq=tiles.attn_q, tk=tiles.attn_k, hp=tiles.attn_heads, strip=tiles.attn_strip)

        km, vm = _memkv(mem, row(norm_mem_g[l]), xattn_w_k[l].astype(BF16), xattn_w_v[l].astype(BF16))
        wos = w_out[l][:d_ssm].astype(BF16)
        wom = w_out[l][d_ssm:].astype(BF16)
        wr = jnp.concatenate([moe_w_expert[l].reshape(d, MOE_EXPERTS), moe_w_group[l]], axis=1)
        wr = jnp.pad(wr, ((0, 0), (0, LANES - MOE_EXPERTS - MOE_GROUPS))).astype(F32)
        wr_hi = wr.astype(BF16)
        wr_lo = (wr - wr_hi.astype(F32)).astype(BF16)
        br = jnp.concatenate([moe_b_expert[l].reshape(-1), moe_b_group[l]])
        br = jnp.pad(br, (0, LANES - MOE_EXPERTS - MOE_GROUPS)).reshape(1, LANES).astype(F32)
        x2, route = _post(x, y_ssm, y_mla, row(out_norm_mla_g[l]), wos, wom, row(norm_xattn_g[l]),
                              xattn_w_q[l].astype(BF16), km, vm, xattn_w_o[l].astype(BF16), row(norm_moe_g[l]),
                              wr_hi, wr_lo, br, tm=tiles.post, sub=tiles.post_sub)

        meta = _route_metadata(route, tiles.moe)
        xs, gs = _dispatch(x2, route, meta["pos"], meta["n_pad"], tt=tiles.copy)
        ys = _moe(xs, gs, row(norm_moe_g[l]), meta["tile_g"], meta["n_valid"], moe_w_gate[l].astype(BF16),
                  moe_w_up[l].astype(BF16), moe_w_down[l].astype(BF16), tm=tiles.moe)
        x = _combine(ys, meta["pos"], x2, row(norm_final_g), tt=tiles.copy)
    return x
```

```python
import functools
import math
from typing import NamedTuple

import jax
import jax.numpy as jnp
from jax import lax
from jax.experimental import pallas as pl
from jax.experimental.pallas import tpu as pltpu

F32 = jnp.float32
BF16 = jnp.bfloat16

EPS = 1e-6
NEG_INF = -1e30
CHUNK = 64

LANES = 128
SSM_GROUP = 16
SSM_STATE = 64
MLA_HEADS = 8
MLA_NOPE = 64
MLA_ROPE = 32
MLA_QK = MLA_NOPE + MLA_ROPE
MLA_V = 64
ROPE_THETA = 10000.0
XATTN_HEADS = 4
MOE_GROUPS = 4
MOE_PER_GROUP = 8
MOE_EXPERTS = MOE_GROUPS * MOE_PER_GROUP
VMEM_LIMIT = 48 * 1024 * 1024
VMEM_LIMIT_LARGE = 56 * 1024 * 1024
LOG2E = math.log2(math.e)
VT_ROWS = 80
GROUP_LANE = 64


class Tiles(NamedTuple):
    proj: int
    ssm: int
    attn_q: int
    attn_k: int
    attn_heads: int
    attn_strip: int
    post: int
    moe: int
    copy: int


def _tiles(seq):
    return Tiles(proj=min(512, seq), ssm=min(256, seq), attn_q=min(512, seq), attn_k=min(256, seq),
                 attn_heads=MLA_HEADS, attn_strip=32, post=min(512, seq),
                 moe=256, copy=min(256, seq))


def _dot(a, b):
    return jnp.dot(a, b, preferred_element_type=F32)


def _dot_nt(a, b):
    return lax.dot_general(a, b, (((1,), (1,)), ((), ())), preferred_element_type=F32)


def _rms(x, g):
    ms = jnp.mean(x * x, axis=-1, keepdims=True)
    return x * lax.rsqrt(ms + EPS) * g


def _params(*sem, vmem=VMEM_LIMIT):
    return pltpu.CompilerParams(dimension_semantics=sem, vmem_limit_bytes=vmem)


def _full(shape):
    zeros = (0,) * len(shape)
    return pl.BlockSpec(shape, lambda *_: zeros)


def _ssm_prep_kernel(lre_ref, lim_ref, ldt_ref, bre_ref, bim_ref, are_ref, aim_ref, bbre_ref, bbim_ref):
    lre = jnp.minimum(lre_ref[...], -1e-4)
    lim = lim_ref[...]
    dt = jnp.exp(ldt_ref[...])
    mag = jnp.exp(lre * dt)
    are = mag * jnp.cos(lim * dt)
    aim = mag * jnp.sin(lim * dt)
    are_ref[...] = are
    aim_ref[...] = aim
    nre = are - 1.0
    den = lre * lre + lim * lim
    fre = (nre * lre + aim * lim) / den
    fim = (aim * lre - nre * lim) / den
    bre = bre_ref[...]
    bim = bim_ref[...]
    bbre_ref[...] = fre * bre - fim * bim
    bbim_ref[...] = fre * bim + fim * bre


def _ssm_prep(lam_re, lam_im, log_dt, b_re, b_im):
    g, p = lam_re.shape
    hh = b_re.shape[-1]
    bt_re = jnp.transpose(b_re, (0, 2, 1))
    bt_im = jnp.transpose(b_im, (0, 2, 1))
    outs = pl.pallas_call(
        _ssm_prep_kernel,
        out_shape=[jax.ShapeDtypeStruct((g, 1, p), F32)] * 2 + [jax.ShapeDtypeStruct((g, hh, p), F32)] * 2,
        name="ssm_prep",
    )(lam_re.reshape(g, 1, p), lam_im.reshape(g, 1, p), log_dt.reshape(g, 1, 1), bt_re, bt_im)
    a_re, a_im, bb_re, bb_im = outs
    return a_re.reshape(g, p), a_im.reshape(g, p), bb_re, bb_im


def _proj_kernel(x_ref, pos_ref, gmix_ref, wu_ref, wcq_ref, wckv_ref, wkrt_ref, qg_ref, wqt_ref, kvg_ref,
                 wk_ref, wvt_ref, invf_ref, u_ref, q_ref, k_ref, v_ref):
    half = MLA_ROPE // 2
    h = _rms(x_ref[0], gmix_ref[...]).astype(BF16)
    u_ref[0] = _dot(h, wu_ref[...]).astype(BF16)
    cqn = _rms(_dot(h, wcq_ref[...]), qg_ref[...]).astype(BF16)
    ckvn = _rms(_dot(h, wckv_ref[...]), kvg_ref[...]).astype(BF16)

    ang = invf_ref[...] * pos_ref[0].astype(F32)
    cosv = jnp.cos(ang)
    sinv = jnp.sin(ang)

    def rot_t(blk):
        x1 = blk[MLA_NOPE:MLA_NOPE + half]
        x2 = blk[MLA_NOPE + half:MLA_QK]
        return jnp.concatenate([blk[:MLA_NOPE], x1 * cosv - x2 * sinv, x1 * sinv + x2 * cosv, blk[MLA_QK:]], axis=0)

    qt = _dot_nt(wqt_ref[...], cqn) * (MLA_QK ** -0.5 * LOG2E)
    vt = _dot_nt(wvt_ref[...], ckvn)
    kk = _dot(ckvn, wk_ref[...])
    kpe = rot_t(_dot_nt(wkrt_ref[...], h)).T
    ones_row = jnp.where(lax.broadcasted_iota(jnp.int32, (VT_ROWS, 1), 0) == MLA_V, 1.0, 0.0)
    for hh in range(MLA_HEADS):
        q_ref[0, hh] = rot_t(qt[hh * LANES:(hh + 1) * LANES]).astype(BF16)
        k_ref[0, hh] = (kk[:, hh * LANES:(hh + 1) * LANES] + kpe).astype(BF16)
        v_ref[0, hh] = (vt[hh * VT_ROWS:(hh + 1) * VT_ROWS] + ones_row).astype(BF16)


def _proj(x, pos, gmix, wu, wcq, wckv, wkrt, qg, wqt, kvg, wk, wvt, invf, tm):
    b, s, d = x.shape
    grid = (b, s // tm)
    tok = lambda n: pl.BlockSpec((1, tm, n), lambda i, j: (i, j, 0))
    head = pl.BlockSpec((1, MLA_HEADS, tm, LANES), lambda i, j: (i, 0, j, 0))
    head_t = pl.BlockSpec((1, MLA_HEADS, LANES, tm), lambda i, j: (i, 0, 0, j))
    head_vt = pl.BlockSpec((1, MLA_HEADS, VT_ROWS, tm), lambda i, j: (i, 0, 0, j))
    tshape = lambda n: jax.ShapeDtypeStruct((b, MLA_HEADS, n, s), BF16)
    return pl.pallas_call(
        _proj_kernel,
        grid=grid,
        in_specs=[tok(d), pl.BlockSpec((1, 1, tm), lambda i, j: (i, 0, j)), _full(gmix.shape), _full(wu.shape),
                  _full(wcq.shape), _full(wckv.shape), _full(wkrt.shape), _full(qg.shape), _full(wqt.shape),
                  _full(kvg.shape), _full(wk.shape), _full(wvt.shape), _full(invf.shape)],
        out_specs=[tok(wu.shape[1]), head_t, head, head_vt],
        out_shape=[jax.ShapeDtypeStruct((b, s, wu.shape[1]), BF16), tshape(LANES),
                   jax.ShapeDtypeStruct((b, MLA_HEADS, s, LANES), BF16), tshape(VT_ROWS)],
        compiler_params=_params("parallel", "parallel"),
        name="proj",
    )(x, pos, gmix, wu, wcq, wckv, wkrt, qg, wqt, kvg, wk, wvt, invf)


def _ssm_kernel(u_ref, bbd_ref, atab_ref, cmat_ref, dskip_ref, wglu_ref, bglu_ref, gout_ref, y_ref,
                xs_ref, st_ref, *, tt, slab, nb, lane_tiles):
    j = pl.program_id(0)
    half_w = u_ref.shape[-1] // 2

    @pl.when(j == 0)
    def _():
        st_ref[...] = jnp.zeros_like(st_ref)

    for hf in range(2):
        ub = u_ref[:, :, hf * half_w:(hf + 1) * half_w].reshape(nb * tt, half_w)
        bu = _dot(ub, bbd_ref[hf])
        for b in range(nb):
            r0 = (hf * nb + b) * slab
            for c in range(2 * lane_tiles):
                xs_ref[c, r0:r0 + tt, :] = bu[b * tt:(b + 1) * tt, c * LANES:(c + 1) * LANES]

    rows = 2 * nb
    group = 4
    for c0 in range(0, lane_tiles, group):
        cs = list(range(c0, c0 + group))
        a_re = [atab_ref[0, :, c * LANES:(c + 1) * LANES] for c in cs]
        a_im = [atab_ref[1, :, c * LANES:(c + 1) * LANES] for c in cs]
        init = tuple(st_ref[0, :, c * LANES:(c + 1) * LANES] for c in cs) + \
            tuple(st_ref[1, :, c * LANES:(c + 1) * LANES] for c in cs)

        def step(t, carry, cs=cs, a_re=a_re, a_im=a_im):
            new_re, new_im = [], []
            for i, c in enumerate(cs):
                x_re, x_im = carry[i], carry[group + i]
                idx = pl.ds(t, rows, stride=slab)
                n_re = a_re[i] * x_re - a_im[i] * x_im + xs_ref[c, idx, :]
                n_im = a_re[i] * x_im + a_im[i] * x_re + xs_ref[lane_tiles + c, idx, :]
                xs_ref[c, idx, :] = n_re
                xs_ref[lane_tiles + c, idx, :] = n_im
                new_re.append(n_re)
                new_im.append(n_im)
            return tuple(new_re) + tuple(new_im)

        fin = lax.fori_loop(0, tt, step, init, unroll=4)
        for i, c in enumerate(cs):
            st_ref[0, :, c * LANES:(c + 1) * LANES] = fin[i]
            st_ref[1, :, c * LANES:(c + 1) * LANES] = fin[group + i]

    ys = []
    for hf in range(2):
        xb = []
        for b in range(nb):
            r0 = (hf * nb + b) * slab
            xb.append(jnp.concatenate([xs_ref[c, r0:r0 + tt, :] for c in range(2 * lane_tiles)], axis=1))
        xh = jnp.concatenate(xb, axis=0).astype(BF16)
        ys.append(_dot(xh, cmat_ref[hf]))
    y = jnp.concatenate(ys, axis=1)
    u = u_ref[...].astype(F32).reshape(nb * tt, 2 * half_w)
    y = jax.nn.gelu(y + dskip_ref[...] * u)
    y = y * jax.nn.sigmoid(_dot(y.astype(BF16), wglu_ref[...]) + bglu_ref[...])
    y = _rms(y, gout_ref[...])
    y_ref[...] = y.reshape(nb, tt, 2 * half_w).astype(BF16)


def _ssm(u, bbd, atab, cmat, dskip, wglu, bglu, gout, tt):
    nb, s, dssm = u.shape
    lane_tiles = bbd.shape[-1] // (2 * LANES)
    slab = tt + 4
    kern = functools.partial(_ssm_kernel, tt=tt, slab=slab, nb=nb, lane_tiles=lane_tiles)
    return pl.pallas_call(
        kern,
        grid=(s // tt,),
        in_specs=[pl.BlockSpec((nb, tt, dssm), lambda j: (0, j, 0)), _full(bbd.shape), _full(atab.shape),
                  _full(cmat.shape), _full(dskip.shape), _full(wglu.shape), _full(bglu.shape), _full(gout.shape)],
        out_specs=pl.BlockSpec((nb, tt, dssm), lambda j: (0, j, 0)),
        out_shape=jax.ShapeDtypeStruct((nb, s, dssm), BF16),
        scratch_shapes=[pltpu.VMEM((2 * lane_tiles, 2 * nb * slab, LANES), F32),
                        pltpu.VMEM((2, 2 * nb, lane_tiles * LANES), F32)],
        compiler_params=_params("arbitrary"),
        name="ssm",
    )(u, bbd, atab, cmat, dskip, wglu, bglu, gout)


def _attn_kernel(qt_ref, k_ref, vt_ref, o_ref, s_ref, p_ref, m_ref, a_ref, acc_ref, *, tq, tk, hp, strip):
    qi = pl.program_id(2)
    m_ref[...] = jnp.full(m_ref.shape, NEG_INF, F32)
    acc_ref[...] = jnp.zeros(acc_ref.shape, F32)
    q_chunk = lax.broadcasted_iota(jnp.int32, (1, tq), 1) // CHUNK

    def tile(k0, diag_off):
        qs = slice(0 if diag_off is None else diag_off, tq)
        for hh in range(hp):
            s_ref[hh, :, qs] = _dot(k_ref[0, hh, pl.ds(k0, tk), :], qt_ref[0, hh, :, qs])

        def strip_of(hh, r):
            s = s_ref[hh, r:r + strip, qs]
            if diag_off is None:
                return s
            return jnp.where(q_chunk[:, qs] >= (diag_off + r) // CHUNK, s, NEG_INF)

        for hh in range(hp):
            mt = strip_of(hh, 0)
            for r in range(strip, tk, strip):
                mt = jnp.maximum(mt, strip_of(hh, r))
            m_old = m_ref[hh, :, qs]
            m_new = jnp.maximum(m_old, jnp.max(mt, axis=0, keepdims=True))
            a_ref[hh, :, qs] = jnp.exp2(m_old - m_new)
            m_ref[hh, :, qs] = m_new
            for r in range(0, tk, strip):
                p_ref[hh, r:r + strip, qs] = jnp.exp2(strip_of(hh, r) - m_new).astype(BF16)
        for hh in range(hp):
            pv = _dot(vt_ref[0, hh, :, pl.ds(k0, tk)], p_ref[hh, :, qs])
            acc_ref[hh, :, qs] = acc_ref[hh, :, qs] * a_ref[hh, :, qs] + pv

    def body(j, c):
        tile(pl.multiple_of(j * tk, tk), None)
        return c

    lax.fori_loop(0, qi * (tq // tk), body, 0)
    for off in range(0, tq, tk):
        tile(pl.multiple_of(qi * tq + off, tk), off)
    feat = lax.broadcasted_iota(jnp.int32, (VT_ROWS, 1), 0)
    pad = jnp.zeros((LANES - VT_ROWS, tq), F32)
    for hh in range(hp):
        acc = acc_ref[hh]
        out = acc * (1.0 / acc[MLA_V:MLA_V + 1, :])
        out = jnp.where(feat < MLA_V, out, 0.0)
        o_ref[0, hh] = jnp.concatenate([out, pad], axis=0).T.astype(BF16)


def _attn(qt, k, vt, tq, tk, hp, strip):
    b, h, s, _ = k.shape
    assert strip <= CHUNK and CHUNK % strip == 0 and tk % CHUNK == 0 and tq % tk == 0
    return pl.pallas_call(
        functools.partial(_attn_kernel, tq=tq, tk=tk, hp=hp, strip=strip),
        grid=(b, h // hp, s // tq),
        in_specs=[pl.BlockSpec((1, hp, LANES, tq), lambda i, j, t: (i, j, 0, t)),
                  pl.BlockSpec((1, hp, s, LANES), lambda i, j, t: (i, j, 0, 0)),
                  pl.BlockSpec((1, hp, VT_ROWS, s), lambda i, j, t: (i, j, 0, 0))],
        out_specs=pl.BlockSpec((1, hp, tq, LANES), lambda i, j, t: (i, j, t, 0)),
        out_shape=jax.ShapeDtypeStruct((b, h, s, LANES), BF16),
        scratch_shapes=[pltpu.VMEM((hp, tk, tq), F32), pltpu.VMEM((hp, tk, tq), BF16),
                        pltpu.VMEM((hp, 1, tq), F32), pltpu.VMEM((hp, 1, tq), F32),
                        pltpu.VMEM((hp, VT_ROWS, tq), F32)],
        compiler_params=_params("parallel", "parallel", "arbitrary", vmem=VMEM_LIMIT_LARGE),
        name="attn",
    )(qt, k, vt)


def _memkv_kernel(mem_ref, g_ref, wk_ref, wv_ref, k_ref, v_ref, *, hd):
    mn = _rms(mem_ref[0], g_ref[...]).astype(BF16)
    kk = _dot(mn, wk_ref[...])
    vv = _dot(mn, wv_ref[...])
    for hh in range(XATTN_HEADS):
        k_ref[0, hh] = kk[:, hh * hd:(hh + 1) * hd].astype(BF16)
        v_ref[0, hh] = vv[:, hh * hd:(hh + 1) * hd].T.astype(BF16)


def _memkv(mem, g, wk, wv):
    b, nm, d = mem.shape
    hd = d // XATTN_HEADS
    ospec = pl.BlockSpec((1, XATTN_HEADS, nm, hd), lambda i: (i, 0, 0, 0))
    return pl.pallas_call(
        functools.partial(_memkv_kernel, hd=hd),
        grid=(b,),
        in_specs=[pl.BlockSpec((1, nm, d), lambda i: (i, 0, 0)), _full(g.shape), _full(wk.shape), _full(wv.shape)],
        out_specs=[ospec, ospec],
        out_shape=[jax.ShapeDtypeStruct((b, XATTN_HEADS, nm, hd), BF16)] * 2,
        compiler_params=_params("parallel"),
        name="mem_kv",
    )(mem, g, wk, wv)


def _rms_t(xt, g_col):
    ms = jnp.mean(xt * xt, axis=0, keepdims=True)
    return xt * lax.rsqrt(ms + EPS) * g_col


def _post_kernel(x_ref, ys_ref, ym_ref, gm_ref, wost_ref, womt_ref, gx_ref, wqt_ref, km_ref, vmt_ref, wot_ref,
                 gmoe_ref, wrht_ref, wrlt_ref, br_ref, x2_ref, route_ref, *, hd):
    heads = [ym_ref[0, hh].astype(F32) for hh in range(MLA_HEADS)]
    ym = jnp.concatenate([heads[k] + pltpu.roll(heads[k + 1], MLA_V, 1) for k in range(0, MLA_HEADS, 2)], axis=1)
    ymn = _rms(ym, gm_ref[...]).astype(BF16)
    x1 = x_ref[0].T + _dot_nt(wost_ref[...], ys_ref[0]) + _dot_nt(womt_ref[...], ymn)

    h2 = _rms_t(x1, gx_ref[...]).astype(BF16)
    qx = (_dot(wqt_ref[...], h2) * (hd ** -0.5)).astype(BF16)
    outs = []
    for hh in range(XATTN_HEADS):
        s = _dot(km_ref[0, hh], qx[hh * hd:(hh + 1) * hd])
        p = jnp.exp(s - jnp.max(s, axis=0, keepdims=True))
        inv = 1.0 / jnp.sum(p, axis=0, keepdims=True)
        outs.append(_dot(vmt_ref[0, hh], p.astype(BF16)) * inv)
    o = jnp.concatenate(outs, axis=0).astype(BF16)
    x2 = x1 + _dot(wot_ref[...], o)
    x2_ref[0] = x2.T

    h3 = _rms_t(x2, gmoe_ref[...])
    h_hi = h3.astype(BF16)
    h_lo = (h3 - h_hi.astype(F32)).astype(BF16)
    logits = (_dot(wrht_ref[...], h_hi) + _dot(wrlt_ref[...], h_hi) + _dot(wrht_ref[...], h_lo)) + br_ref[...]
    idx = lax.broadcasted_iota(jnp.int32, logits.shape, 0)
    is_g = (idx >= MOE_EXPERTS) & (idx < MOE_EXPERTS + MOE_GROUPS)
    gl = jnp.where(is_g, logits, NEG_INF)
    gmax = jnp.max(gl, axis=0, keepdims=True)
    g_w = 1.0 / jnp.sum(jnp.exp(gl - gmax), axis=0, keepdims=True)
    g_idx = jnp.min(jnp.where(gl == gmax, idx, 4 * LANES), axis=0, keepdims=True) - MOE_EXPERTS
    in_grp = (idx >= g_idx * MOE_PER_GROUP) & (idx < (g_idx + 1) * MOE_PER_GROUP)
    el = jnp.where(in_grp, logits, NEG_INF)
    v1 = jnp.max(el, axis=0, keepdims=True)
    i1 = jnp.min(jnp.where(el == v1, idx, 4 * LANES), axis=0, keepdims=True)
    el2 = jnp.where(idx == i1, NEG_INF, el)
    v2 = jnp.max(el2, axis=0, keepdims=True)
    i2 = jnp.min(jnp.where(el2 == v2, idx, 4 * LANES), axis=0, keepdims=True)
    e2 = jnp.exp(v2 - v1)
    w1 = g_w / (1.0 + e2)
    w2 = g_w * e2 / (1.0 + e2)
    route = (jnp.where(idx == i1, w1, 0.0) + jnp.where(idx == i2, w2, 0.0)
             + jnp.where(idx == GROUP_LANE, g_idx.astype(F32), 0.0))
    route_ref[0] = route.T


def _post(x, ys, ym, gm, wost, womt, gx, wqt, km, vmt, wot, gmoe, wrht, wrlt, br, tm):
    b, s, d = x.shape
    hd = d // XATTN_HEADS
    tok = lambda n: pl.BlockSpec((1, tm, n), lambda i, j: (i, j, 0))
    mem = pl.BlockSpec((1,) + km.shape[1:], lambda i, j: (i, 0, 0, 0))
    return pl.pallas_call(
        functools.partial(_post_kernel, hd=hd),
        grid=(b, s // tm),
        in_specs=[tok(d), tok(ys.shape[-1]), pl.BlockSpec((1, MLA_HEADS, tm, LANES), lambda i, j: (i, 0, j, 0)),
                  _full(gm.shape), _full(wost.shape), _full(womt.shape), _full(gx.shape), _full(wqt.shape), mem, mem,
                  _full(wot.shape), _full(gmoe.shape), _full(wrht.shape), _full(wrlt.shape), _full(br.shape)],
        out_specs=[tok(d), tok(LANES)],
        out_shape=[jax.ShapeDtypeStruct((b, s, d), F32), jax.ShapeDtypeStruct((b, s, LANES), F32)],
        compiler_params=_params("parallel", "parallel"),
        name="post",
    )(x, ys, ym, gm, wost, womt, gx, wqt, km, vmt, wot, gmoe, wrht, wrlt, br)


def _route_metadata(route, tm):
    b, s, _ = route.shape
    n_tiles = (s + MOE_GROUPS * tm) // tm
    gid = route[..., GROUP_LANE].astype(jnp.int32)
    groups = jnp.arange(MOE_GROUPS, dtype=jnp.int32)[None, :, None]
    onehot = (gid[:, None, :] == groups).astype(jnp.int32)
    csum = jnp.cumsum(onehot, axis=2)
    cnt = csum[:, :, -1]
    padded = (cnt + tm - 1) // tm * tm
    seg_end = jnp.cumsum(padded, axis=-1)
    dest = jnp.sum(onehot * ((seg_end - padded)[:, :, None] + csum - 1), axis=1)
    tile_start = jnp.arange(n_tiles, dtype=jnp.int32) * tm
    tile_g = jnp.sum((seg_end[:, None, :] <= tile_start[None, :, None]).astype(jnp.int32), axis=-1)
    tile_g = jnp.minimum(tile_g, MOE_GROUPS - 1)
    n_valid = (seg_end[:, -1] // tm).astype(jnp.int32)
    return dict(pos=dest, tile_g=tile_g.reshape(b * n_tiles), n_valid=n_valid, n_pad=n_tiles * tm)


def _dispatch_kernel(h_ref, g_ref, dst_ref, xs_ref, gs_ref, *, tt):
    @pl.when(pl.program_id(1) == 0)
    def _():
        xs_ref[...] = jnp.zeros(xs_ref.shape, xs_ref.dtype)
        gs_ref[...] = jnp.zeros(gs_ref.shape, gs_ref.dtype)

    for t in range(tt):
        r = dst_ref[0, 0, t]
        xs_ref[0, pl.ds(r, 1), :] = h_ref[0, t:t + 1, :]
        gs_ref[0, pl.ds(r, 1), :] = g_ref[0, t:t + 1, :]


def _dispatch(x2, route, pos, n_pad, tt):
    b, s, d = x2.shape
    nj = s // tt
    row_block = lambda n: pl.BlockSpec((1, n_pad, n), lambda i, j: (i, 0, 0))
    return pl.pallas_call(
        functools.partial(_dispatch_kernel, tt=tt),
        grid=(b, nj),
        in_specs=[pl.BlockSpec((1, tt, d), lambda i, j: (i, j, 0)),
                  pl.BlockSpec((1, tt, LANES), lambda i, j: (i, j, 0)),
                  pl.BlockSpec((1, 1, tt), lambda i, j: (i * nj + j, 0, 0), memory_space=pltpu.SMEM)],
        out_specs=[row_block(d), row_block(LANES)],
        out_shape=[jax.ShapeDtypeStruct((b, n_pad, d), F32), jax.ShapeDtypeStruct((b, n_pad, LANES), F32)],
        compiler_params=_params("parallel", "arbitrary", vmem=VMEM_LIMIT_LARGE),
        name="dispatch",
    )(x2, route, pos.reshape(b * nj, 1, tt))


def _moe_kernel(tg_ref, nv_ref, x_ref, gate_ref, gmoe_ref, wg_ref, wu_ref, wd_ref, o_ref, *, nt):
    bi = pl.program_id(0)
    ti = pl.program_id(1)
    grp = tg_ref[bi * nt + ti]

    @pl.when(ti < nv_ref[bi])
    def _():
        h = _rms(x_ref[0], gmoe_ref[...]).astype(BF16)
        gate = gate_ref[0]
        lane = lax.broadcasted_iota(jnp.int32, gate.shape, 1)
        acc = None
        for e in range(MOE_PER_GROUP):
            a = jax.nn.silu(_dot(h, wg_ref[e])) * _dot(h, wu_ref[e])
            gcol = jnp.sum(jnp.where(lane == grp * MOE_PER_GROUP + e, gate, 0.0), axis=-1, keepdims=True)
            y = _dot((a * gcol).astype(BF16), wd_ref[e])
            acc = y if acc is None else acc + y
        o_ref[0] = acc

    @pl.when(ti >= nv_ref[bi])
    def _():
        o_ref[0] = jnp.zeros(o_ref.shape[1:], o_ref.dtype)


def _moe(xs, gs, gmoe, tile_g, n_valid, wg, wu, wd, tm):
    b, n_pad, d = xs.shape
    nt = n_pad // tm
    ff = wg.shape[-1]
    group = lambda i, j, tg, nv: (tg[i * nt + j], 0, 0)
    tok = lambda n: pl.BlockSpec((1, tm, n), lambda i, j, tg, nv: (i, j, 0))
    grid_spec = pltpu.PrefetchScalarGridSpec(
        num_scalar_prefetch=2,
        grid=(b, nt),
        in_specs=[tok(d), tok(LANES), pl.BlockSpec(gmoe.shape, lambda i, j, tg, nv: (0, 0)),
                  pl.BlockSpec((MOE_PER_GROUP, d, ff), group),
                  pl.BlockSpec((MOE_PER_GROUP, d, ff), group), pl.BlockSpec((MOE_PER_GROUP, ff, d), group)],
        out_specs=tok(d),
    )
    return pl.pallas_call(
        functools.partial(_moe_kernel, nt=nt),
        grid_spec=grid_spec,
        out_shape=jax.ShapeDtypeStruct((b, n_pad, d), F32),
        compiler_params=_params("parallel", "arbitrary"),
        name="moe",
    )(tile_g, n_valid, xs, gs, gmoe, wg, wu, wd)


def _combine_kernel(ys_ref, pos_ref, x2_ref, gf_ref, o_ref, buf_ref, *, tt):
    for t in range(tt):
        buf_ref[t:t + 1, :] = ys_ref[0, pl.ds(pos_ref[0, 0, t], 1), :]
    o_ref[0] = _rms(x2_ref[0] + buf_ref[...], gf_ref[...])


def _combine(ys, pos, x2, gf, tt):
    b, s, d = x2.shape
    n_pad = ys.shape[1]
    nj = s // tt
    return pl.pallas_call(
        functools.partial(_combine_kernel, tt=tt),
        grid=(b, nj),
        in_specs=[pl.BlockSpec((1, n_pad, d), lambda i, j: (i, 0, 0)),
                  pl.BlockSpec((1, 1, tt), lambda i, j: (i * nj + j, 0, 0), memory_space=pltpu.SMEM),
                  pl.BlockSpec((1, tt, d), lambda i, j: (i, j, 0)), _full(gf.shape)],
        out_specs=pl.BlockSpec((1, tt, d), lambda i, j: (i, j, 0)),
        out_shape=jax.ShapeDtypeStruct((b, s, d), F32),
        scratch_shapes=[pltpu.VMEM((tt, d), F32)],
        compiler_params=_params("parallel", "arbitrary", vmem=VMEM_LIMIT_LARGE),
        name="combine",
    )(ys, pos.reshape(b * nj, 1, tt), x2, gf)


def _pad_heads(w, per_head, offset=0):
    k = w.shape[0]
    w = w.reshape(k, MLA_HEADS, per_head)
    w = jnp.pad(w, ((0, 0), (0, 0), (offset, LANES - per_head - offset)))
    return w.reshape(k, MLA_HEADS * LANES)


def _block_diag(blocks):
    n, r, c = blocks.shape
    eye = jnp.eye(n, dtype=blocks.dtype)
    return (eye[:, None, :, None] * blocks[:, :, None, :]).reshape(n * r, n * c)


def kernel(x, mem, positions, norm_mix_g, w_in, ssm_lam_re, ssm_lam_im, ssm_log_dt, ssm_b_re, ssm_b_im, ssm_c_re, ssm_c_im, ssm_d, ssm_w_glu, ssm_b_glu, mla_q_norm_g, mla_w_q_up, mla_kv_norm_g, mla_w_kv_up, out_norm_ssm_g, out_norm_mla_g, w_out, norm_xattn_g, norm_mem_g, xattn_w_q, xattn_w_k, xattn_w_v, xattn_w_o, norm_moe_g, moe_w_group, moe_b_group, moe_w_expert, moe_b_expert, moe_w_gate, moe_w_up, moe_w_down, norm_final_g):
    bsz, seq, d = x.shape
    depth = w_in.shape[0]
    assert depth == 1, "the final RMSNorm is fused into the last stage of a single layer"
    d_ssm = ssm_d.shape[-1]
    q_rank = mla_q_norm_g.shape[-1]
    kv_rank = mla_kv_norm_g.shape[-1]
    n_grp = d_ssm // SSM_GROUP
    s1, s2, s3 = d_ssm, d_ssm + q_rank, d_ssm + q_rank + kv_rank
    row = lambda v: v.reshape(1, -1).astype(F32)
    tiles = _tiles(seq)

    half = MLA_ROPE // 2
    invf = (ROPE_THETA ** (-jnp.arange(half, dtype=F32) / half)).reshape(half, 1)
    pos = positions.reshape(bsz, 1, seq)

    for l in range(depth):
        a_re, a_im, bb_re, bb_im = _ssm_prep(ssm_lam_re[l], ssm_lam_im[l], ssm_log_dt[l], ssm_b_re[l], ssm_b_im[l])
        gh = n_grp // 2
        bbd = jnp.stack([
            jnp.concatenate([_block_diag(bb_re[hf * gh:(hf + 1) * gh]), _block_diag(bb_im[hf * gh:(hf + 1) * gh])],
                            axis=1) for hf in range(2)]).astype(BF16)
        c_re_t = jnp.transpose(ssm_c_re[l], (0, 2, 1))
        c_im_t = jnp.transpose(ssm_c_im[l], (0, 2, 1))
        cmat = jnp.stack([
            jnp.concatenate([_block_diag(c_re_t[hf * gh:(hf + 1) * gh]), -_block_diag(c_im_t[hf * gh:(hf + 1) * gh])],
                            axis=0) for hf in range(2)]).astype(BF16)
        atab = jnp.stack([
            jnp.repeat(arr.reshape(2, 1, gh * SSM_STATE), bsz, axis=1).reshape(2 * bsz, gh * SSM_STATE)
            for arr in (a_re, a_im)])

        wi = w_in[l]
        wu = wi[:, :s1].astype(BF16)
        wcq = wi[:, s1:s2].astype(BF16)
        wckv = wi[:, s2:s3].astype(BF16)
        wkrt = jnp.pad(wi[:, s3:], ((0, 0), (MLA_NOPE, LANES - MLA_QK))).T.astype(BF16)
        wqt = _pad_heads(mla_w_q_up[l], MLA_QK).T.astype(BF16)
        wkv = mla_w_kv_up[l].reshape(kv_rank, MLA_HEADS, MLA_NOPE + MLA_V)
        wk = _pad_heads(wkv[:, :, :MLA_NOPE].reshape(kv_rank, -1), MLA_NOPE).astype(BF16)
        wvt = jnp.pad(wkv[:, :, MLA_NOPE:], ((0, 0), (0, 0), (0, VT_ROWS - MLA_V)))
        wvt = wvt.reshape(kv_rank, MLA_HEADS * VT_ROWS).T.astype(BF16)
        u, q, k, v = _proj(x, pos, row(norm_mix_g[l]), wu, wcq, wckv, wkrt, row(mla_q_norm_g[l]), wqt,
                           row(mla_kv_norm_g[l]), wk, wvt, invf, tm=tiles.proj)

        y_ssm = _ssm(u, bbd, atab, cmat, row(ssm_d[l]), ssm_w_glu[l].astype(BF16), row(ssm_b_glu[l]),
                     row(out_norm_ssm_g[l]), tt=tiles.ssm)
        y_mla = _attn(q, k, v, tq=tiles.attn_q, tk=tiles.attn_k, hp=tiles.attn_heads, strip=tiles.attn_strip)

        km, vm = _memkv(mem, row(norm_mem_g[l]), xattn_w_k[l].astype(BF16), xattn_w_v[l].astype(BF16))
        col = lambda v: v.reshape(-1, 1).astype(F32)
        wost = w_out[l][:d_ssm].T.astype(BF16)
        womt = w_out[l][d_ssm:].T.astype(BF16)
        wr = jnp.concatenate([moe_w_expert[l].reshape(d, MOE_EXPERTS), moe_w_group[l]], axis=1)
        wrt = jnp.pad(wr, ((0, 0), (0, LANES - MOE_EXPERTS - MOE_GROUPS))).T.astype(F32)
        wrt_hi = wrt.astype(BF16)
        wrt_lo = (wrt - wrt_hi.astype(F32)).astype(BF16)
        br = jnp.concatenate([moe_b_expert[l].reshape(-1), moe_b_group[l]])
        br = jnp.pad(br, (0, LANES - MOE_EXPERTS - MOE_GROUPS)).reshape(LANES, 1).astype(F32)
        x2, route = _post(x, y_ssm, y_mla, row(out_norm_mla_g[l]), wost, womt, col(norm_xattn_g[l]),
                          xattn_w_q[l].T.astype(BF16), km, vm, xattn_w_o[l].T.astype(BF16), col(norm_moe_g[l]),
                          wrt_hi, wrt_lo, br, tm=tiles.post)

        meta = _route_metadata(route, tiles.moe)
        xs, gs = _dispatch(x2, route, meta["pos"], meta["n_pad"], tt=tiles.copy)
        ys = _moe(xs, gs, row(norm_moe_g[l]), meta["tile_g"], meta["n_valid"], moe_w_gate[l].astype(BF16),
                  moe_w_up[l].astype(BF16), moe_w_down[l].astype(BF16), tm=tiles.moe)
        x = _combine(ys, meta["pos"], x2, row(norm_final_g), tt=tiles.copy)
    return x
```

```python
import functools
import math
from typing import NamedTuple

import jax
import jax.numpy as jnp
from jax import lax
from jax.experimental import pallas as pl
from jax.experimental.pallas import tpu as pltpu

F32 = jnp.float32
BF16 = jnp.bfloat16

EPS = 1e-6
NEG_INF = -1e30
CHUNK = 64

LANES = 128
SSM_GROUP = 16
SSM_STATE = 64
MLA_HEADS = 8
MLA_NOPE = 64
MLA_ROPE = 32
MLA_QK = MLA_NOPE + MLA_ROPE
MLA_V = 64
ROPE_THETA = 10000.0
XATTN_HEADS = 4
MOE_GROUPS = 4
MOE_PER_GROUP = 8
MOE_EXPERTS = MOE_GROUPS * MOE_PER_GROUP
VMEM_LIMIT = 48 * 1024 * 1024
VMEM_LIMIT_LARGE = 56 * 1024 * 1024
LOG2E = math.log2(math.e)
VT_ROWS = 80
GROUP_LANE = 64


class Tiles(NamedTuple):
    proj: int
    ssm: int
    attn_q: int
    attn_k: int
    attn_heads: int
    attn_strip: int
    post: int
    moe: int
    copy: int


def _tiles(seq):
    return Tiles(proj=min(512, seq), ssm=min(256, seq), attn_q=min(512, seq), attn_k=min(128, seq),
                 attn_heads=MLA_HEADS, attn_strip=32, post=min(512, seq),
                 moe=256, copy=min(256, seq))


def _dot(a, b):
    return jnp.dot(a, b, preferred_element_type=F32)


def _dot_nt(a, b):
    return lax.dot_general(a, b, (((1,), (1,)), ((), ())), preferred_element_type=F32)


def _rms(x, g):
    ms = jnp.mean(x * x, axis=-1, keepdims=True)
    return x * lax.rsqrt(ms + EPS) * g


def _params(*sem, vmem=VMEM_LIMIT):
    return pltpu.CompilerParams(dimension_semantics=sem, vmem_limit_bytes=vmem)


def _full(shape):
    zeros = (0,) * len(shape)
    return pl.BlockSpec(shape, lambda *_: zeros)


def _ssm_prep_kernel(lre_ref, lim_ref, ldt_ref, bre_ref, bim_ref, are_ref, aim_ref, bbre_ref, bbim_ref):
    lre = jnp.minimum(lre_ref[...], -1e-4)
    lim = lim_ref[...]
    dt = jnp.exp(ldt_ref[...])
    mag = jnp.exp(lre * dt)
    are = mag * jnp.cos(lim * dt)
    aim = mag * jnp.sin(lim * dt)
    are_ref[...] = are
    aim_ref[...] = aim
    nre = are - 1.0
    den = lre * lre + lim * lim
    fre = (nre * lre + aim * lim) / den
    fim = (aim * lre - nre * lim) / den
    bre = bre_ref[...]
    bim = bim_ref[...]
    bbre_ref[...] = fre * bre - fim * bim
    bbim_ref[...] = fre * bim + fim * bre


def _ssm_prep(lam_re, lam_im, log_dt, b_re, b_im):
    g, p = lam_re.shape
    hh = b_re.shape[-1]
    bt_re = jnp.transpose(b_re, (0, 2, 1))
    bt_im = jnp.transpose(b_im, (0, 2, 1))
    outs = pl.pallas_call(
        _ssm_prep_kernel,
        out_shape=[jax.ShapeDtypeStruct((g, 1, p), F32)] * 2 + [jax.ShapeDtypeStruct((g, hh, p), F32)] * 2,
        name="ssm_prep",
    )(lam_re.reshape(g, 1, p), lam_im.reshape(g, 1, p), log_dt.reshape(g, 1, 1), bt_re, bt_im)
    a_re, a_im, bb_re, bb_im = outs
    return a_re.reshape(g, p), a_im.reshape(g, p), bb_re, bb_im


def _proj_kernel(x_ref, pos_ref, gmix_ref, wu_ref, wcq_ref, wckv_ref, wkrt_ref, qg_ref, wqt_ref, kvg_ref,
                 wk_ref, wvt_ref, invf_ref, u_ref, q_ref, k_ref, v_ref):
    half = MLA_ROPE // 2
    h = _rms(x_ref[0], gmix_ref[...]).astype(BF16)
    u_ref[0] = _dot(h, wu_ref[...]).astype(BF16)
    cqn = _rms(_dot(h, wcq_ref[...]), qg_ref[...]).astype(BF16)
    ckvn = _rms(_dot(h, wckv_ref[...]), kvg_ref[...]).astype(BF16)

    ang = invf_ref[...] * pos_ref[0].astype(F32)
    cosv = jnp.cos(ang)
    sinv = jnp.sin(ang)

    def rot_t(blk):
        x1 = blk[MLA_NOPE:MLA_NOPE + half]
        x2 = blk[MLA_NOPE + half:MLA_QK]
        return jnp.concatenate([blk[:MLA_NOPE], x1 * cosv - x2 * sinv, x1 * sinv + x2 * cosv, blk[MLA_QK:]], axis=0)

    qt = _dot_nt(wqt_ref[...], cqn) * (MLA_QK ** -0.5 * LOG2E)
    vt = _dot_nt(wvt_ref[...], ckvn)
    kk = _dot(ckvn, wk_ref[...])
    kpe = rot_t(_dot_nt(wkrt_ref[...], h)).T
    ones_row = jnp.where(lax.broadcasted_iota(jnp.int32, (VT_ROWS, 1), 0) == MLA_V, 1.0, 0.0)
    for hh in range(MLA_HEADS):
        q_ref[0, hh] = rot_t(qt[hh * LANES:(hh + 1) * LANES]).astype(BF16)
        k_ref[0, hh] = (kk[:, hh * LANES:(hh + 1) * LANES] + kpe).astype(BF16)
        v_ref[0, hh] = (vt[hh * VT_ROWS:(hh + 1) * VT_ROWS] + ones_row).astype(BF16)


def _proj(x, pos, gmix, wu, wcq, wckv, wkrt, qg, wqt, kvg, wk, wvt, invf, tm):
    b, s, d = x.shape
    grid = (b, s // tm)
    tok = lambda n: pl.BlockSpec((1, tm, n), lambda i, j: (i, j, 0))
    head = pl.BlockSpec((1, MLA_HEADS, tm, LANES), lambda i, j: (i, 0, j, 0))
    head_t = pl.BlockSpec((1, MLA_HEADS, LANES, tm), lambda i, j: (i, 0, 0, j))
    head_vt = pl.BlockSpec((1, MLA_HEADS, VT_ROWS, tm), lambda i, j: (i, 0, 0, j))
    tshape = lambda n: jax.ShapeDtypeStruct((b, MLA_HEADS, n, s), BF16)
    return pl.pallas_call(
        _proj_kernel,
        grid=grid,
        in_specs=[tok(d), pl.BlockSpec((1, 1, tm), lambda i, j: (i, 0, j)), _full(gmix.shape), _full(wu.shape),
                  _full(wcq.shape), _full(wckv.shape), _full(wkrt.shape), _full(qg.shape), _full(wqt.shape),
                  _full(kvg.shape), _full(wk.shape), _full(wvt.shape), _full(invf.shape)],
        out_specs=[tok(wu.shape[1]), head_t, head, head_vt],
        out_shape=[jax.ShapeDtypeStruct((b, s, wu.shape[1]), BF16), tshape(LANES),
                   jax.ShapeDtypeStruct((b, MLA_HEADS, s, LANES), BF16), tshape(VT_ROWS)],
        compiler_params=_params("parallel", "parallel"),
        name="proj",
    )(x, pos, gmix, wu, wcq, wckv, wkrt, qg, wqt, kvg, wk, wvt, invf)


def _ssm_kernel(u_ref, bbd_ref, atab_ref, cmat_ref, dskip_ref, wglu_ref, bglu_ref, gout_ref, y_ref,
                xs_ref, st_ref, *, tt, slab, nb, lane_tiles):
    j = pl.program_id(0)
    half_w = u_ref.shape[-1] // 2

    @pl.when(j == 0)
    def _():
        st_ref[...] = jnp.zeros_like(st_ref)

    for hf in range(2):
        ub = u_ref[:, :, hf * half_w:(hf + 1) * half_w].reshape(nb * tt, half_w)
        bu = _dot(ub, bbd_ref[hf])
        for b in range(nb):
            r0 = (hf * nb + b) * slab
            for c in range(2 * lane_tiles):
                xs_ref[c, r0:r0 + tt, :] = bu[b * tt:(b + 1) * tt, c * LANES:(c + 1) * LANES]

    rows = 2 * nb
    group = 4
    for c0 in range(0, lane_tiles, group):
        cs = list(range(c0, c0 + group))
        a_re = [atab_ref[0, :, c * LANES:(c + 1) * LANES] for c in cs]
        a_im = [atab_ref[1, :, c * LANES:(c + 1) * LANES] for c in cs]
        init = tuple(st_ref[0, :, c * LANES:(c + 1) * LANES] for c in cs) + \
            tuple(st_ref[1, :, c * LANES:(c + 1) * LANES] for c in cs)

        def step(t, carry, cs=cs, a_re=a_re, a_im=a_im):
            new_re, new_im = [], []
            for i, c in enumerate(cs):
                x_re, x_im = carry[i], carry[group + i]
                idx = pl.ds(t, rows, stride=slab)
                n_re = a_re[i] * x_re - a_im[i] * x_im + xs_ref[c, idx, :]
                n_im = a_re[i] * x_im + a_im[i] * x_re + xs_ref[lane_tiles + c, idx, :]
                xs_ref[c, idx, :] = n_re
                xs_ref[lane_tiles + c, idx, :] = n_im
                new_re.append(n_re)
                new_im.append(n_im)
            return tuple(new_re) + tuple(new_im)

        fin = lax.fori_loop(0, tt, step, init, unroll=4)
        for i, c in enumerate(cs):
            st_ref[0, :, c * LANES:(c + 1) * LANES] = fin[i]
            st_ref[1, :, c * LANES:(c + 1) * LANES] = fin[group + i]

    ys = []
    for hf in range(2):
        xb = []
        for b in range(nb):
            r0 = (hf * nb + b) * slab
            xb.append(jnp.concatenate([xs_ref[c, r0:r0 + tt, :] for c in range(2 * lane_tiles)], axis=1))
        xh = jnp.concatenate(xb, axis=0).astype(BF16)
        ys.append(_dot(xh, cmat_ref[hf]))
    y = jnp.concatenate(ys, axis=1)
    u = u_ref[...].astype(F32).reshape(nb * tt, 2 * half_w)
    y = jax.nn.gelu(y + dskip_ref[...] * u)
    y = y * jax.nn.sigmoid(_dot(y.astype(BF16), wglu_ref[...]) + bglu_ref[...])
    y = _rms(y, gout_ref[...])
    y_ref[...] = y.reshape(nb, tt, 2 * half_w).astype(BF16)


def _ssm(u, bbd, atab, cmat, dskip, wglu, bglu, gout, tt):
    nb, s, dssm = u.shape
    lane_tiles = bbd.shape[-1] // (2 * LANES)
    slab = tt + 4
    kern = functools.partial(_ssm_kernel, tt=tt, slab=slab, nb=nb, lane_tiles=lane_tiles)
    return pl.pallas_call(
        kern,
        grid=(s // tt,),
        in_specs=[pl.BlockSpec((nb, tt, dssm), lambda j: (0, j, 0)), _full(bbd.shape), _full(atab.shape),
                  _full(cmat.shape), _full(dskip.shape), _full(wglu.shape), _full(bglu.shape), _full(gout.shape)],
        out_specs=pl.BlockSpec((nb, tt, dssm), lambda j: (0, j, 0)),
        out_shape=jax.ShapeDtypeStruct((nb, s, dssm), BF16),
        scratch_shapes=[pltpu.VMEM((2 * lane_tiles, 2 * nb * slab, LANES), F32),
                        pltpu.VMEM((2, 2 * nb, lane_tiles * LANES), F32)],
        compiler_params=_params("arbitrary"),
        name="ssm",
    )(u, bbd, atab, cmat, dskip, wglu, bglu, gout)


def _attn_kernel(qt_ref, k_ref, vt_ref, o_ref, s_ref, p_ref, m_ref, a_ref, acc_ref, *, tq, tk, hp, strip):
    qi = pl.program_id(2)
    m_ref[...] = jnp.full(m_ref.shape, NEG_INF, F32)
    acc_ref[...] = jnp.zeros(acc_ref.shape, F32)
    q_chunk = lax.broadcasted_iota(jnp.int32, (1, tq), 1) // CHUNK

    def tile(k0, diag_off):
        qs = slice(0 if diag_off is None else diag_off, tq)
        for hh in range(hp):
            s_ref[hh, :, qs] = _dot(k_ref[0, hh, pl.ds(k0, tk), :], qt_ref[0, hh, :, qs])

        def strip_of(hh, r):
            s = s_ref[hh, r:r + strip, qs]
            if diag_off is None:
                return s
            return jnp.where(q_chunk[:, qs] >= (diag_off + r) // CHUNK, s, NEG_INF)

        for hh in range(hp):
            mt = strip_of(hh, 0)
            for r in range(strip, tk, strip):
                mt = jnp.maximum(mt, strip_of(hh, r))
            m_old = m_ref[hh, :, qs]
            m_new = jnp.maximum(m_old, jnp.max(mt, axis=0, keepdims=True))
            a_ref[hh, :, qs] = jnp.exp2(m_old - m_new)
            m_ref[hh, :, qs] = m_new
            for r in range(0, tk, strip):
                p_ref[hh, r:r + strip, qs] = jnp.exp2(strip_of(hh, r) - m_new).astype(BF16)
        for hh in range(hp):
            pv = _dot(vt_ref[0, hh, :, pl.ds(k0, tk)], p_ref[hh, :, qs])
            acc_ref[hh, :, qs] = acc_ref[hh, :, qs] * a_ref[hh, :, qs] + pv

    def body(j, c):
        tile(pl.multiple_of(j * tk, tk), None)
        return c

    lax.fori_loop(0, qi * (tq // tk), body, 0)
    for off in range(0, tq, tk):
        tile(pl.multiple_of(qi * tq + off, tk), off)
    feat = lax.broadcasted_iota(jnp.int32, (VT_ROWS, 1), 0)
    pad = jnp.zeros((LANES - VT_ROWS, tq), F32)
    for hh in range(hp):
        acc = acc_ref[hh]
        out = acc * (1.0 / acc[MLA_V:MLA_V + 1, :])
        out = jnp.where(feat < MLA_V, out, 0.0)
        o_ref[0, hh] = jnp.concatenate([out, pad], axis=0).T.astype(BF16)


def _attn(qt, k, vt, tq, tk, hp, strip):
    b, h, s, _ = k.shape
    assert strip <= CHUNK and CHUNK % strip == 0 and tk % CHUNK == 0 and tq % tk == 0
    return pl.pallas_call(
        functools.partial(_attn_kernel, tq=tq, tk=tk, hp=hp, strip=strip),
        grid=(b, h // hp, s // tq),
        in_specs=[pl.BlockSpec((1, hp, LANES, tq), lambda i, j, t: (i, j, 0, t)),
                  pl.BlockSpec((1, hp, s, LANES), lambda i, j, t: (i, j, 0, 0)),
                  pl.BlockSpec((1, hp, VT_ROWS, s), lambda i, j, t: (i, j, 0, 0))],
        out_specs=pl.BlockSpec((1, hp, tq, LANES), lambda i, j, t: (i, j, t, 0)),
        out_shape=jax.ShapeDtypeStruct((b, h, s, LANES), BF16),
        scratch_shapes=[pltpu.VMEM((hp, tk, tq), F32), pltpu.VMEM((hp, tk, tq), BF16),
                        pltpu.VMEM((hp, 1, tq), F32), pltpu.VMEM((hp, 1, tq), F32),
                        pltpu.VMEM((hp, VT_ROWS, tq), F32)],
        compiler_params=_params("parallel", "parallel", "arbitrary", vmem=VMEM_LIMIT_LARGE),
        name="attn",
    )(qt, k, vt)


def _memkv_kernel(mem_ref, g_ref, wk_ref, wv_ref, k_ref, v_ref, *, hd):
    mn = _rms(mem_ref[0], g_ref[...]).astype(BF16)
    kk = _dot(mn, wk_ref[...])
    vv = _dot(mn, wv_ref[...])
    for hh in range(XATTN_HEADS):
        k_ref[0, hh] = kk[:, hh * hd:(hh + 1) * hd].astype(BF16)
        v_ref[0, hh] = vv[:, hh * hd:(hh + 1) * hd].T.astype(BF16)


def _memkv(mem, g, wk, wv):
    b, nm, d = mem.shape
    hd = d // XATTN_HEADS
    ospec = pl.BlockSpec((1, XATTN_HEADS, nm, hd), lambda i: (i, 0, 0, 0))
    return pl.pallas_call(
        functools.partial(_memkv_kernel, hd=hd),
        grid=(b,),
        in_specs=[pl.BlockSpec((1, nm, d), lambda i: (i, 0, 0)), _full(g.shape), _full(wk.shape), _full(wv.shape)],
        out_specs=[ospec, ospec],
        out_shape=[jax.ShapeDtypeStruct((b, XATTN_HEADS, nm, hd), BF16)] * 2,
        compiler_params=_params("parallel"),
        name="mem_kv",
    )(mem, g, wk, wv)


def _rms_t(xt, g_col):
    ms = jnp.mean(xt * xt, axis=0, keepdims=True)
    return xt * lax.rsqrt(ms + EPS) * g_col


def _post_kernel(x_ref, ys_ref, ym_ref, gm_ref, wost_ref, womt_ref, gx_ref, wqt_ref, km_ref, vmt_ref, wot_ref,
                 gmoe_ref, wrht_ref, wrlt_ref, br_ref, x2_ref, route_ref, *, hd):
    heads = [ym_ref[0, hh].astype(F32) for hh in range(MLA_HEADS)]
    ym = jnp.concatenate([heads[k] + pltpu.roll(heads[k + 1], MLA_V, 1) for k in range(0, MLA_HEADS, 2)], axis=1)
    ymn = _rms(ym, gm_ref[...]).astype(BF16)
    x1 = x_ref[0].T + _dot_nt(wost_ref[...], ys_ref[0]) + _dot_nt(womt_ref[...], ymn)

    h2 = _rms_t(x1, gx_ref[...]).astype(BF16)
    qx = (_dot(wqt_ref[...], h2) * (hd ** -0.5)).astype(BF16)
    outs = []
    for hh in range(XATTN_HEADS):
        s = _dot(km_ref[0, hh], qx[hh * hd:(hh + 1) * hd])
        p = jnp.exp(s - jnp.max(s, axis=0, keepdims=True))
        inv = 1.0 / jnp.sum(p, axis=0, keepdims=True)
        outs.append(_dot(vmt_ref[0, hh], p.astype(BF16)) * inv)
    o = jnp.concatenate(outs, axis=0).astype(BF16)
    x2 = x1 + _dot(wot_ref[...], o)
    x2_ref[0] = x2.T

    h3 = _rms_t(x2, gmoe_ref[...])
    h_hi = h3.astype(BF16)
    h_lo = (h3 - h_hi.astype(F32)).astype(BF16)
    logits = (_dot(wrht_ref[...], h_hi) + _dot(wrlt_ref[...], h_hi) + _dot(wrht_ref[...], h_lo)) + br_ref[...]
    idx = lax.broadcasted_iota(jnp.int32, logits.shape, 0)
    is_g = (idx >= MOE_EXPERTS) & (idx < MOE_EXPERTS + MOE_GROUPS)
    gl = jnp.where(is_g, logits, NEG_INF)
    gmax = jnp.max(gl, axis=0, keepdims=True)
    g_w = 1.0 / jnp.sum(jnp.exp(gl - gmax), axis=0, keepdims=True)
    g_idx = jnp.min(jnp.where(gl == gmax, idx, 4 * LANES), axis=0, keepdims=True) - MOE_EXPERTS
    in_grp = (idx >= g_idx * MOE_PER_GROUP) & (idx < (g_idx + 1) * MOE_PER_GROUP)
    el = jnp.where(in_grp, logits, NEG_INF)
    v1 = jnp.max(el, axis=0, keepdims=True)
    i1 = jnp.min(jnp.where(el == v1, idx, 4 * LANES), axis=0, keepdims=True)
    el2 = jnp.where(idx == i1, NEG_INF, el)
    v2 = jnp.max(el2, axis=0, keepdims=True)
    i2 = jnp.min(jnp.where(el2 == v2, idx, 4 * LANES), axis=0, keepdims=True)
    e2 = jnp.exp(v2 - v1)
    w1 = g_w / (1.0 + e2)
    w2 = g_w * e2 / (1.0 + e2)
    route = (jnp.where(idx == i1, w1, 0.0) + jnp.where(idx == i2, w2, 0.0)
             + jnp.where(idx == GROUP_LANE, g_idx.astype(F32), 0.0))
    route_ref[0] = route.T


def _post(x, ys, ym, gm, wost, womt, gx, wqt, km, vmt, wot, gmoe, wrht, wrlt, br, tm):
    b, s, d = x.shape
    hd = d // XATTN_HEADS
    tok = lambda n: pl.BlockSpec((1, tm, n), lambda i, j: (i, j, 0))
    mem = pl.BlockSpec((1,) + km.shape[1:], lambda i, j: (i, 0, 0, 0))
    return pl.pallas_call(
        functools.partial(_post_kernel, hd=hd),
        grid=(b, s // tm),
        in_specs=[tok(d), tok(ys.shape[-1]), pl.BlockSpec((1, MLA_HEADS, tm, LANES), lambda i, j: (i, 0, j, 0)),
                  _full(gm.shape), _full(wost.shape), _full(womt.shape), _full(gx.shape), _full(wqt.shape), mem, mem,
                  _full(wot.shape), _full(gmoe.shape), _full(wrht.shape), _full(wrlt.shape), _full(br.shape)],
        out_specs=[tok(d), tok(LANES)],
        out_shape=[jax.ShapeDtypeStruct((b, s, d), F32), jax.ShapeDtypeStruct((b, s, LANES), F32)],
        compiler_params=_params("parallel", "parallel"),
        name="post",
    )(x, ys, ym, gm, wost, womt, gx, wqt, km, vmt, wot, gmoe, wrht, wrlt, br)


def _route_metadata(route, tm):
    b, s, _ = route.shape
    n_tiles = (s + MOE_GROUPS * tm) // tm
    gid = route[..., GROUP_LANE].astype(jnp.int32)
    groups = jnp.arange(MOE_GROUPS, dtype=jnp.int32)[None, :, None]
    onehot = (gid[:, None, :] == groups).astype(jnp.int32)
    csum = jnp.cumsum(onehot, axis=2)
    cnt = csum[:, :, -1]
    padded = (cnt + tm - 1) // tm * tm
    seg_end = jnp.cumsum(padded, axis=-1)
    dest = jnp.sum(onehot * ((seg_end - padded)[:, :, None] + csum - 1), axis=1)
    tile_start = jnp.arange(n_tiles, dtype=jnp.int32) * tm
    tile_g = jnp.sum((seg_end[:, None, :] <= tile_start[None, :, None]).astype(jnp.int32), axis=-1)
    tile_g = jnp.minimum(tile_g, MOE_GROUPS - 1)
    n_valid = (seg_end[:, -1] // tm).astype(jnp.int32)
    return dict(pos=dest, tile_g=tile_g.reshape(b * n_tiles), n_valid=n_valid, n_pad=n_tiles * tm)


def _dispatch_kernel(h_ref, g_ref, dst_ref, xs_ref, gs_ref, *, tt):
    @pl.when(pl.program_id(1) == 0)
    def _():
        xs_ref[...] = jnp.zeros(xs_ref.shape, xs_ref.dtype)
        gs_ref[...] = jnp.zeros(gs_ref.shape, gs_ref.dtype)

    for t in range(tt):
        r = dst_ref[0, 0, t]
        xs_ref[0, pl.ds(r, 1), :] = h_ref[0, t:t + 1, :]
        gs_ref[0, pl.ds(r, 1), :] = g_ref[0, t:t + 1, :]


def _dispatch(x2, route, pos, n_pad, tt):
    b, s, d = x2.shape
    nj = s // tt
    row_block = lambda n: pl.BlockSpec((1, n_pad, n), lambda i, j: (i, 0, 0))
    return pl.pallas_call(
        functools.partial(_dispatch_kernel, tt=tt),
        grid=(b, nj),
        in_specs=[pl.BlockSpec((1, tt, d), lambda i, j: (i, j, 0)),
                  pl.BlockSpec((1, tt, LANES), lambda i, j: (i, j, 0)),
                  pl.BlockSpec((1, 1, tt), lambda i, j: (i * nj + j, 0, 0), memory_space=pltpu.SMEM)],
        out_specs=[row_block(d), row_block(LANES)],
        out_shape=[jax.ShapeDtypeStruct((b, n_pad, d), F32), jax.ShapeDtypeStruct((b, n_pad, LANES), F32)],
        compiler_params=_params("parallel", "arbitrary", vmem=VMEM_LIMIT_LARGE),
        name="dispatch",
    )(x2, route, pos.reshape(b * nj, 1, tt))


def _moe_kernel(tg_ref, nv_ref, x_ref, gate_ref, gmoe_ref, wg_ref, wu_ref, wd_ref, o_ref, *, nt):
    bi = pl.program_id(0)
    ti = pl.program_id(1)
    grp = tg_ref[bi * nt + ti]

    @pl.when(ti < nv_ref[bi])
    def _():
        h = _rms(x_ref[0], gmoe_ref[...]).astype(BF16)
        gate = gate_ref[0]
        lane = lax.broadcasted_iota(jnp.int32, gate.shape, 1)
        acc = None
        for e in range(MOE_PER_GROUP):
            a = jax.nn.silu(_dot(h, wg_ref[e])) * _dot(h, wu_ref[e])
            gcol = jnp.sum(jnp.where(lane == grp * MOE_PER_GROUP + e, gate, 0.0), axis=-1, keepdims=True)
            y = _dot((a * gcol).astype(BF16), wd_ref[e])
            acc = y if acc is None else acc + y
        o_ref[0] = acc

    @pl.when(ti >= nv_ref[bi])
    def _():
        o_ref[0] = jnp.zeros(o_ref.shape[1:], o_ref.dtype)


def _moe(xs, gs, gmoe, tile_g, n_valid, wg, wu, wd, tm):
    b, n_pad, d = xs.shape
    nt = n_pad // tm
    ff = wg.shape[-1]
    group = lambda i, j, tg, nv: (tg[i * nt + j], 0, 0)
    tok = lambda n: pl.BlockSpec((1, tm, n), lambda i, j, tg, nv: (i, j, 0))
    grid_spec = pltpu.PrefetchScalarGridSpec(
        num_scalar_prefetch=2,
        grid=(b, nt),
        in_specs=[tok(d), tok(LANES), pl.BlockSpec(gmoe.shape, lambda i, j, tg, nv: (0, 0)),
                  pl.BlockSpec((MOE_PER_GROUP, d, ff), group),
                  pl.BlockSpec((MOE_PER_GROUP, d, ff), group), pl.BlockSpec((MOE_PER_GROUP, ff, d), group)],
        out_specs=tok(d),
    )
    return pl.pallas_call(
        functools.partial(_moe_kernel, nt=nt),
        grid_spec=grid_spec,
        out_shape=jax.ShapeDtypeStruct((b, n_pad, d), F32),
        compiler_params=_params("parallel", "arbitrary"),
        name="moe",
    )(tile_g, n_valid, xs, gs, gmoe, wg, wu, wd)


def _combine_kernel(ys_ref, pos_ref, x2_ref, gf_ref, o_ref, buf_ref, *, tt):
    for t in range(tt):
        buf_ref[t:t + 1, :] = ys_ref[0, pl.ds(pos_ref[0, 0, t], 1), :]
    o_ref[0] = _rms(x2_ref[0] + buf_ref[...], gf_ref[...])


def _combine(ys, pos, x2, gf, tt):
    b, s, d = x2.shape
    n_pad = ys.shape[1]
    nj = s // tt
    return pl.pallas_call(
        functools.partial(_combine_kernel, tt=tt),
        grid=(b, nj),
        in_specs=[pl.BlockSpec((1, n_pad, d), lambda i, j: (i, 0, 0)),
                  pl.BlockSpec((1, 1, tt), lambda i, j: (i * nj + j, 0, 0), memory_space=pltpu.SMEM),
                  pl.BlockSpec((1, tt, d), lambda i, j: (i, j, 0)), _full(gf.shape)],
        out_specs=pl.BlockSpec((1, tt, d), lambda i, j: (i, j, 0)),
        out_shape=jax.ShapeDtypeStruct((b, s, d), F32),
        scratch_shapes=[pltpu.VMEM((tt, d), F32)],
        compiler_params=_params("parallel", "arbitrary", vmem=VMEM_LIMIT_LARGE),
        name="combine",
    )(ys, pos.reshape(b * nj, 1, tt), x2, gf)


def _pad_heads(w, per_head, offset=0):
    k = w.shape[0]
    w = w.reshape(k, MLA_HEADS, per_head)
    w = jnp.pad(w, ((0, 0), (0, 0), (offset, LANES - per_head - offset)))
    return w.reshape(k, MLA_HEADS * LANES)


def _block_diag(blocks):
    n, r, c = blocks.shape
    eye = jnp.eye(n, dtype=blocks.dtype)
    return (eye[:, None, :, None] * blocks[:, :, None, :]).reshape(n * r, n * c)


def kernel(x, mem, positions, norm_mix_g, w_in, ssm_lam_re, ssm_lam_im, ssm_log_dt, ssm_b_re, ssm_b_im, ssm_c_re, ssm_c_im, ssm_d, ssm_w_glu, ssm_b_glu, mla_q_norm_g, mla_w_q_up, mla_kv_norm_g, mla_w_kv_up, out_norm_ssm_g, out_norm_mla_g, w_out, norm_xattn_g, norm_mem_g, xattn_w_q, xattn_w_k, xattn_w_v, xattn_w_o, norm_moe_g, moe_w_group, moe_b_group, moe_w_expert, moe_b_expert, moe_w_gate, moe_w_up, moe_w_down, norm_final_g):
    bsz, seq, d = x.shape
    depth = w_in.shape[0]
    assert depth == 1, "the final RMSNorm is fused into the last stage of a single layer"
    d_ssm = ssm_d.shape[-1]
    q_rank = mla_q_norm_g.shape[-1]
    kv_rank = mla_kv_norm_g.shape[-1]
    n_grp = d_ssm // SSM_GROUP
    s1, s2, s3 = d_ssm, d_ssm + q_rank, d_ssm + q_rank + kv_rank
    row = lambda v: v.reshape(1, -1).astype(F32)
    tiles = _tiles(seq)

    half = MLA_ROPE // 2
    invf = (ROPE_THETA ** (-jnp.arange(half, dtype=F32) / half)).reshape(half, 1)
    pos = positions.reshape(bsz, 1, seq)

    for l in range(depth):
        a_re, a_im, bb_re, bb_im = _ssm_prep(ssm_lam_re[l], ssm_lam_im[l], ssm_log_dt[l], ssm_b_re[l], ssm_b_im[l])
        gh = n_grp // 2
        bbd = jnp.stack([
            jnp.concatenate([_block_diag(bb_re[hf * gh:(hf + 1) * gh]), _block_diag(bb_im[hf * gh:(hf + 1) * gh])],
                            axis=1) for hf in range(2)]).astype(BF16)
        c_re_t = jnp.transpose(ssm_c_re[l], (0, 2, 1))
        c_im_t = jnp.transpose(ssm_c_im[l], (0, 2, 1))
        cmat = jnp.stack([
            jnp.concatenate([_block_diag(c_re_t[hf * gh:(hf + 1) * gh]), -_block_diag(c_im_t[hf * gh:(hf + 1) * gh])],
                            axis=0) for hf in range(2)]).astype(BF16)
        atab = jnp.stack([
            jnp.repeat(arr.reshape(2, 1, gh * SSM_STATE), bsz, axis=1).reshape(2 * bsz, gh * SSM_STATE)
            for arr in (a_re, a_im)])

        wi = w_in[l]
        wu = wi[:, :s1].astype(BF16)
        wcq = wi[:, s1:s2].astype(BF16)
        wckv = wi[:, s2:s3].astype(BF16)
        wkrt = jnp.pad(wi[:, s3:], ((0, 0), (MLA_NOPE, LANES - MLA_QK))).T.astype(BF16)
        wqt = _pad_heads(mla_w_q_up[l], MLA_QK).T.astype(BF16)
        wkv = mla_w_kv_up[l].reshape(kv_rank, MLA_HEADS, MLA_NOPE + MLA_V)
        wk = _pad_heads(wkv[:, :, :MLA_NOPE].reshape(kv_rank, -1), MLA_NOPE).astype(BF16)
        wvt = jnp.pad(wkv[:, :, MLA_NOPE:], ((0, 0), (0, 0), (0, VT_ROWS - MLA_V)))
        wvt = wvt.reshape(kv_rank, MLA_HEADS * VT_ROWS).T.astype(BF16)
        u, q, k, v = _proj(x, pos, row(norm_mix_g[l]), wu, wcq, wckv, wkrt, row(mla_q_norm_g[l]), wqt,
                           row(mla_kv_norm_g[l]), wk, wvt, invf, tm=tiles.proj)

        y_ssm = _ssm(u, bbd, atab, cmat, row(ssm_d[l]), ssm_w_glu[l].astype(BF16), row(ssm_b_glu[l]),
                     row(out_norm_ssm_g[l]), tt=tiles.ssm)
        y_mla = _attn(q, k, v, tq=tiles.attn_q, tk=tiles.attn_k, hp=tiles.attn_heads, strip=tiles.attn_strip)

        km, vm = _memkv(mem, row(norm_mem_g[l]), xattn_w_k[l].astype(BF16), xattn_w_v[l].astype(BF16))
        col = lambda v: v.reshape(-1, 1).astype(F32)
        wost = w_out[l][:d_ssm].T.astype(BF16)
        womt = w_out[l][d_ssm:].T.astype(BF16)
        wr = jnp.concatenate([moe_w_expert[l].reshape(d, MOE_EXPERTS), moe_w_group[l]], axis=1)
        wrt = jnp.pad(wr, ((0, 0), (0, LANES - MOE_EXPERTS - MOE_GROUPS))).T.astype(F32)
        wrt_hi = wrt.astype(BF16)
        wrt_lo = (wrt - wrt_hi.astype(F32)).astype(BF16)
        br = jnp.concatenate([moe_b_expert[l].reshape(-1), moe_b_group[l]])
        br = jnp.pad(br, (0, LANES - MOE_EXPERTS - MOE_GROUPS)).reshape(LANES, 1).astype(F32)
        x2, route = _post(x, y_ssm, y_mla, row(out_norm_mla_g[l]), wost, womt, col(norm_xattn_g[l]),
                          xattn_w_q[l].T.astype(BF16), km, vm, xattn_w_o[l].T.astype(BF16), col(norm_moe_g[l]),
                          wrt_hi, wrt_lo, br, tm=tiles.post)

        meta = _route_metadata(route, tiles.moe)
        xs, gs = _dispatch(x2, route, meta["pos"], meta["n_pad"], tt=tiles.copy)
        ys = _moe(xs, gs, row(norm_moe_g[l]), meta["tile_g"], meta["n_valid"], moe_w_gate[l].astype(BF16),
                  moe_w_up[l].astype(BF16), moe_w_down[l].astype(BF16), tm=tiles.moe)
        x = _combine(ys, meta["pos"], x2, row(norm_final_g), tt=tiles.copy)
    return x
```

```python
import functools
import math
from typing import NamedTuple

import jax
import jax.numpy as jnp
from jax import lax
from jax.experimental import pallas as pl
from jax.experimental.pallas import tpu as pltpu

F32 = jnp.float32
BF16 = jnp.bfloat16

EPS = 1e-6
NEG_INF = -1e30
CHUNK = 64

LANES = 128
SSM_GROUP = 16
SSM_STATE = 64
MLA_HEADS = 8
MLA_NOPE = 64
MLA_ROPE = 32
MLA_QK = MLA_NOPE + MLA_ROPE
MLA_V = 64
ROPE_THETA = 10000.0
XATTN_HEADS = 4
MOE_GROUPS = 4
MOE_PER_GROUP = 8
MOE_EXPERTS = MOE_GROUPS * MOE_PER_GROUP
VMEM_LIMIT = 48 * 1024 * 1024
VMEM_LIMIT_LARGE = 56 * 1024 * 1024
LOG2E = math.log2(math.e)
VT_ROWS = 80
GROUP_LANE = 64


class Tiles(NamedTuple):
    proj: int
    ssm: int
    attn_q: int
    attn_k: int
    attn_heads: int
    attn_strip: int
    post: int
    moe: int
    copy: int


def _tiles(seq):
    return Tiles(proj=min(512, seq), ssm=min(256, seq), attn_q=min(512, seq), attn_k=min(256, seq),
                 attn_heads=MLA_HEADS, attn_strip=32, post=min(512, seq),
                 moe=256, copy=min(256, seq))


def _dot(a, b):
    return jnp.dot(a, b, preferred_element_type=F32)


def _dot_nt(a, b):
    return lax.dot_general(a, b, (((1,), (1,)), ((), ())), preferred_element_type=F32)


def _rms(x, g):
    ms = jnp.mean(x * x, axis=-1, keepdims=True)
    return x * lax.rsqrt(ms + EPS) * g


def _params(*sem, vmem=VMEM_LIMIT):
    return pltpu.CompilerParams(dimension_semantics=sem, vmem_limit_bytes=vmem)


def _full(shape):
    zeros = (0,) * len(shape)
    return pl.BlockSpec(shape, lambda *_: zeros)


def _ssm_prep_kernel(lre_ref, lim_ref, ldt_ref, bre_ref, bim_ref, are_ref, aim_ref, bbre_ref, bbim_ref):
    lre = jnp.minimum(lre_ref[...], -1e-4)
    lim = lim_ref[...]
    dt = jnp.exp(ldt_ref[...])
    mag = jnp.exp(lre * dt)
    are = mag * jnp.cos(lim * dt)
    aim = mag * jnp.sin(lim * dt)
    are_ref[...] = are
    aim_ref[...] = aim
    nre = are - 1.0
    den = lre * lre + lim * lim
    fre = (nre * lre + aim * lim) / den
    fim = (aim * lre - nre * lim) / den
    bre = bre_ref[...]
    bim = bim_ref[...]
    bbre_ref[...] = fre * bre - fim * bim
    bbim_ref[...] = fre * bim + fim * bre


def _ssm_prep(lam_re, lam_im, log_dt, b_re, b_im):
    g, p = lam_re.shape
    hh = b_re.shape[-1]
    bt_re = jnp.transpose(b_re, (0, 2, 1))
    bt_im = jnp.transpose(b_im, (0, 2, 1))
    outs = pl.pallas_call(
        _ssm_prep_kernel,
        out_shape=[jax.ShapeDtypeStruct((g, 1, p), F32)] * 2 + [jax.ShapeDtypeStruct((g, hh, p), F32)] * 2,
        name="ssm_prep",
    )(lam_re.reshape(g, 1, p), lam_im.reshape(g, 1, p), log_dt.reshape(g, 1, 1), bt_re, bt_im)
    a_re, a_im, bb_re, bb_im = outs
    return a_re.reshape(g, p), a_im.reshape(g, p), bb_re, bb_im


def _proj_kernel(x_ref, pos_ref, gmix_ref, wu_ref, wcq_ref, wckv_ref, wkrt_ref, qg_ref, wqt_ref, kvg_ref,
                 wk_ref, wvt_ref, invf_ref, u_ref, q_ref, k_ref, v_ref):
    half = MLA_ROPE // 2
    h = _rms(x_ref[0], gmix_ref[...]).astype(BF16)
    u_ref[0] = _dot(h, wu_ref[...]).astype(BF16)
    cqn = _rms(_dot(h, wcq_ref[...]), qg_ref[...]).astype(BF16)
    ckvn = _rms(_dot(h, wckv_ref[...]), kvg_ref[...]).astype(BF16)

    ang = invf_ref[...] * pos_ref[0].astype(F32)
    cosv = jnp.cos(ang)
    sinv = jnp.sin(ang)

    def rot_t(blk):
        x1 = blk[MLA_NOPE:MLA_NOPE + half]
        x2 = blk[MLA_NOPE + half:MLA_QK]
        return jnp.concatenate([blk[:MLA_NOPE], x1 * cosv - x2 * sinv, x1 * sinv + x2 * cosv, blk[MLA_QK:]], axis=0)

    qt = _dot_nt(wqt_ref[...], cqn) * (MLA_QK ** -0.5 * LOG2E)
    vt = _dot_nt(wvt_ref[...], ckvn)
    kk = _dot(ckvn, wk_ref[...])
    kpe = rot_t(_dot_nt(wkrt_ref[...], h)).T
    ones_row = jnp.where(lax.broadcasted_iota(jnp.int32, (VT_ROWS, 1), 0) == MLA_V, 1.0, 0.0)
    for hh in range(MLA_HEADS):
        q_ref[0, hh] = rot_t(qt[hh * LANES:(hh + 1) * LANES]).astype(BF16)
        k_ref[0, hh] = (kk[:, hh * LANES:(hh + 1) * LANES] + kpe).astype(BF16)
        v_ref[0, hh] = (vt[hh * VT_ROWS:(hh + 1) * VT_ROWS] + ones_row).astype(BF16)


def _proj(x, pos, gmix, wu, wcq, wckv, wkrt, qg, wqt, kvg, wk, wvt, invf, tm):
    b, s, d = x.shape
    grid = (b, s // tm)
    tok = lambda n: pl.BlockSpec((1, tm, n), lambda i, j: (i, j, 0))
    head = pl.BlockSpec((1, MLA_HEADS, tm, LANES), lambda i, j: (i, 0, j, 0))
    head_t = pl.BlockSpec((1, MLA_HEADS, LANES, tm), lambda i, j: (i, 0, 0, j))
    head_vt = pl.BlockSpec((1, MLA_HEADS, VT_ROWS, tm), lambda i, j: (i, 0, 0, j))
    tshape = lambda n: jax.ShapeDtypeStruct((b, MLA_HEADS, n, s), BF16)
    return pl.pallas_call(
        _proj_kernel,
        grid=grid,
        in_specs=[tok(d), pl.BlockSpec((1, 1, tm), lambda i, j: (i, 0, j)), _full(gmix.shape), _full(wu.shape),
                  _full(wcq.shape), _full(wckv.shape), _full(wkrt.shape), _full(qg.shape), _full(wqt.shape),
                  _full(kvg.shape), _full(wk.shape), _full(wvt.shape), _full(invf.shape)],
        out_specs=[tok(wu.shape[1]), head_t, head, head_vt],
        out_shape=[jax.ShapeDtypeStruct((b, s, wu.shape[1]), BF16), tshape(LANES),
                   jax.ShapeDtypeStruct((b, MLA_HEADS, s, LANES), BF16), tshape(VT_ROWS)],
        compiler_params=_params("parallel", "parallel"),
        name="proj",
    )(x, pos, gmix, wu, wcq, wckv, wkrt, qg, wqt, kvg, wk, wvt, invf)


def _ssm_kernel(u_ref, bbd_ref, atab_ref, cmat_ref, dskip_ref, wglu_ref, bglu_ref, gout_ref, y_ref,
                xs_ref, st_ref, *, tt, slab, nb, lane_tiles):
    j = pl.program_id(0)
    half_w = u_ref.shape[-1] // 2

    @pl.when(j == 0)
    def _():
        st_ref[...] = jnp.zeros_like(st_ref)

    for hf in range(2):
        ub = u_ref[:, :, hf * half_w:(hf + 1) * half_w].reshape(nb * tt, half_w)
        bu = _dot(ub, bbd_ref[hf])
        for b in range(nb):
            r0 = (hf * nb + b) * slab
            for c in range(2 * lane_tiles):
                xs_ref[c, r0:r0 + tt, :] = bu[b * tt:(b + 1) * tt, c * LANES:(c + 1) * LANES]

    rows = 2 * nb
    group = 4
    for c0 in range(0, lane_tiles, group):
        cs = list(range(c0, c0 + group))
        a_re = [atab_ref[0, :, c * LANES:(c + 1) * LANES] for c in cs]
        a_im = [atab_ref[1, :, c * LANES:(c + 1) * LANES] for c in cs]
        init = tuple(st_ref[0, :, c * LANES:(c + 1) * LANES] for c in cs) + \
            tuple(st_ref[1, :, c * LANES:(c + 1) * LANES] for c in cs)

        def step(t, carry, cs=cs, a_re=a_re, a_im=a_im):
            new_re, new_im = [], []
            for i, c in enumerate(cs):
                x_re, x_im = carry[i], carry[group + i]
                idx = pl.ds(t, rows, stride=slab)
                n_re = a_re[i] * x_re - a_im[i] * x_im + xs_ref[c, idx, :]
                n_im = a_re[i] * x_im + a_im[i] * x_re + xs_ref[lane_tiles + c, idx, :]
                xs_ref[c, idx, :] = n_re
                xs_ref[lane_tiles + c, idx, :] = n_im
                new_re.append(n_re)
                new_im.append(n_im)
            return tuple(new_re) + tuple(new_im)

        fin = lax.fori_loop(0, tt, step, init, unroll=4)
        for i, c in enumerate(cs):
            st_ref[0, :, c * LANES:(c + 1) * LANES] = fin[i]
            st_ref[1, :, c * LANES:(c + 1) * LANES] = fin[group + i]

    ys = []
    for hf in range(2):
        xb = []
        for b in range(nb):
            r0 = (hf * nb + b) * slab
            xb.append(jnp.concatenate([xs_ref[c, r0:r0 + tt, :] for c in range(2 * lane_tiles)], axis=1))
        xh = jnp.concatenate(xb, axis=0).astype(BF16)
        ys.append(_dot(xh, cmat_ref[hf]))
    y = jnp.concatenate(ys, axis=1)
    u = u_ref[...].astype(F32).reshape(nb * tt, 2 * half_w)
    y = jax.nn.gelu(y + dskip_ref[...] * u)
    y = y * jax.nn.sigmoid(_dot(y.astype(BF16), wglu_ref[...]) + bglu_ref[...])
    y = _rms(y, gout_ref[...])
    y_ref[...] = y.reshape(nb, tt, 2 * half_w).astype(BF16)


def _ssm(u, bbd, atab, cmat, dskip, wglu, bglu, gout, tt):
    nb, s, dssm = u.shape
    lane_tiles = bbd.shape[-1] // (2 * LANES)
    slab = tt + 4
    kern = functools.partial(_ssm_kernel, tt=tt, slab=slab, nb=nb, lane_tiles=lane_tiles)
    return pl.pallas_call(
        kern,
        grid=(s // tt,),
        in_specs=[pl.BlockSpec((nb, tt, dssm), lambda j: (0, j, 0)), _full(bbd.shape), _full(atab.shape),
                  _full(cmat.shape), _full(dskip.shape), _full(wglu.shape), _full(bglu.shape), _full(gout.shape)],
        out_specs=pl.BlockSpec((nb, tt, dssm), lambda j: (0, j, 0)),
        out_shape=jax.ShapeDtypeStruct((nb, s, dssm), BF16),
        scratch_shapes=[pltpu.VMEM((2 * lane_tiles, 2 * nb * slab, LANES), F32),
                        pltpu.VMEM((2, 2 * nb, lane_tiles * LANES), F32)],
        compiler_params=_params("arbitrary"),
        name="ssm",
    )(u, bbd, atab, cmat, dskip, wglu, bglu, gout)


def _attn_kernel(qt_ref, k_ref, vt_ref, o_ref, s_ref, p_ref, m_ref, a_ref, acc_ref, *, tq, tk, hp, strip):
    qi = pl.program_id(2)
    m_ref[...] = jnp.full(m_ref.shape, NEG_INF, F32)
    acc_ref[...] = jnp.zeros(acc_ref.shape, F32)
    q_chunk = lax.broadcasted_iota(jnp.int32, (1, tq), 1) // CHUNK

    def tile(k0, diag_off):
        qs = slice(0 if diag_off is None else diag_off, tq)
        for hh in range(hp):
            s_ref[hh, :, qs] = _dot(k_ref[0, hh, pl.ds(k0, tk), :], qt_ref[0, hh, :, qs])

        def strip_of(hh, r):
            s = s_ref[hh, r:r + strip, qs]
            if diag_off is None:
                return s
            return jnp.where(q_chunk[:, qs] >= (diag_off + r) // CHUNK, s, NEG_INF)

        for hh in range(hp):
            mt = strip_of(hh, 0)
            for r in range(strip, tk, strip):
                mt = jnp.maximum(mt, strip_of(hh, r))
            m_old = m_ref[hh, :, qs]
            m_new = jnp.maximum(m_old, jnp.max(mt, axis=0, keepdims=True))
            a_ref[hh, :, qs] = jnp.exp2(m_old - m_new)
            m_ref[hh, :, qs] = m_new
            for r in range(0, tk, strip):
                p_ref[hh, r:r + strip, qs] = jnp.exp2(strip_of(hh, r) - m_new).astype(BF16)
        for hh in range(hp):
            pv = _dot(vt_ref[0, hh, :, pl.ds(k0, tk)], p_ref[hh, :, qs])
            acc_ref[hh, :, qs] = acc_ref[hh, :, qs] * a_ref[hh, :, qs] + pv

    def body(j, c):
        tile(pl.multiple_of(j * tk, tk), None)
        return c

    lax.fori_loop(0, qi * (tq // tk), body, 0)
    for off in range(0, tq, tk):
        tile(pl.multiple_of(qi * tq + off, tk), off)
    feat = lax.broadcasted_iota(jnp.int32, (VT_ROWS, 1), 0)
    pad = jnp.zeros((LANES - VT_ROWS, tq), F32)
    for hh in range(hp):
        acc = acc_ref[hh]
        out = acc * (1.0 / acc[MLA_V:MLA_V + 1, :])
        out = jnp.where(feat < MLA_V, out, 0.0)
        o_ref[0, hh] = jnp.concatenate([out, pad], axis=0).T.astype(BF16)


def _attn(qt, k, vt, tq, tk, hp, strip):
    b, h, s, _ = k.shape
    assert strip <= CHUNK and CHUNK % strip == 0 and tk % CHUNK == 0 and tq % tk == 0
    return pl.pallas_call(
        functools.partial(_attn_kernel, tq=tq, tk=tk, hp=hp, strip=strip),
        grid=(b, h // hp, s // tq),
        in_specs=[pl.BlockSpec((1, hp, LANES, tq), lambda i, j, t: (i, j, 0, t)),
                  pl.BlockSpec((1, hp, s, LANES), lambda i, j, t: (i, j, 0, 0)),
                  pl.BlockSpec((1, hp, VT_ROWS, s), lambda i, j, t: (i, j, 0, 0))],
        out_specs=pl.BlockSpec((1, hp, tq, LANES), lambda i, j, t: (i, j, t, 0)),
        out_shape=jax.ShapeDtypeStruct((b, h, s, LANES), BF16),
        scratch_shapes=[pltpu.VMEM((hp, tk, tq), F32), pltpu.VMEM((hp, tk, tq), BF16),
                        pltpu.VMEM((hp, 1, tq), F32), pltpu.VMEM((hp, 1, tq), F32),
                        pltpu.VMEM((hp, VT_ROWS, tq), F32)],
        compiler_params=_params("parallel", "parallel", "arbitrary", vmem=VMEM_LIMIT_LARGE),
        name="attn",
    )(qt, k, vt)


def _memkv_kernel(mem_ref, g_ref, wk_ref, wv_ref, k_ref, v_ref, *, hd):
    mn = _rms(mem_ref[0], g_ref[...]).astype(BF16)
    kk = _dot(mn, wk_ref[...])
    vv = _dot(mn, wv_ref[...])
    for hh in range(XATTN_HEADS):
        k_ref[0, hh] = kk[:, hh * hd:(hh + 1) * hd].astype(BF16)
        v_ref[0, hh] = vv[:, hh * hd:(hh + 1) * hd].T.astype(BF16)


def _memkv(mem, g, wk, wv):
    b, nm, d = mem.shape
    hd = d // XATTN_HEADS
    ospec = pl.BlockSpec((1, XATTN_HEADS, nm, hd), lambda i: (i, 0, 0, 0))
    return pl.pallas_call(
        functools.partial(_memkv_kernel, hd=hd),
        grid=(b,),
        in_specs=[pl.BlockSpec((1, nm, d), lambda i: (i, 0, 0)), _full(g.shape), _full(wk.shape), _full(wv.shape)],
        out_specs=[ospec, ospec],
        out_shape=[jax.ShapeDtypeStruct((b, XATTN_HEADS, nm, hd), BF16)] * 2,
        compiler_params=_params("parallel"),
        name="mem_kv",
    )(mem, g, wk, wv)


def _rms_t(xt, g_col):
    ms = jnp.mean(xt * xt, axis=0, keepdims=True)
    return xt * lax.rsqrt(ms + EPS) * g_col


def _post_kernel(x_ref, ys_ref, ym_ref, gm_ref, woutt_ref, gx_ref, wqt_ref, km_ref, vmt_ref, wot_ref,
                 gmoe_ref, wrht_ref, wrlt_ref, br_ref, x2_ref, route_ref, *, hd):
    heads = [ym_ref[0, hh].astype(F32) for hh in range(MLA_HEADS)]
    ym = jnp.concatenate([heads[k] + pltpu.roll(heads[k + 1], MLA_V, 1) for k in range(0, MLA_HEADS, 2)], axis=1)
    ymn = _rms(ym, gm_ref[...]).astype(BF16)
    y_mix = jnp.concatenate([ys_ref[0], ymn], axis=1)
    x1 = x_ref[0].T + _dot_nt(woutt_ref[...], y_mix)

    h2 = _rms_t(x1, gx_ref[...]).astype(BF16)
    qx = (_dot(wqt_ref[...], h2) * (hd ** -0.5)).astype(BF16)
    outs = []
    for hh in range(XATTN_HEADS):
        s = _dot(km_ref[0, hh], qx[hh * hd:(hh + 1) * hd])
        p = jnp.exp(s - jnp.max(s, axis=0, keepdims=True))
        inv = 1.0 / jnp.sum(p, axis=0, keepdims=True)
        outs.append(_dot(vmt_ref[0, hh], p.astype(BF16)) * inv)
    o = jnp.concatenate(outs, axis=0).astype(BF16)
    x2 = x1 + _dot(wot_ref[...], o)
    x2_ref[0] = x2.T

    h3 = _rms_t(x2, gmoe_ref[...])
    h_hi = h3.astype(BF16)
    h_lo = (h3 - h_hi.astype(F32)).astype(BF16)
    logits = (_dot(wrht_ref[...], h_hi) + _dot(wrlt_ref[...], h_hi) + _dot(wrht_ref[...], h_lo)) + br_ref[...]
    idx = lax.broadcasted_iota(jnp.int32, logits.shape, 0)
    is_g = (idx >= MOE_EXPERTS) & (idx < MOE_EXPERTS + MOE_GROUPS)
    gl = jnp.where(is_g, logits, NEG_INF)
    gmax = jnp.max(gl, axis=0, keepdims=True)
    g_w = 1.0 / jnp.sum(jnp.exp(gl - gmax), axis=0, keepdims=True)
    g_idx = jnp.min(jnp.where(gl == gmax, idx, 4 * LANES), axis=0, keepdims=True) - MOE_EXPERTS
    in_grp = (idx >= g_idx * MOE_PER_GROUP) & (idx < (g_idx + 1) * MOE_PER_GROUP)
    el = jnp.where(in_grp, logits, NEG_INF)
    v1 = jnp.max(el, axis=0, keepdims=True)
    i1 = jnp.min(jnp.where(el == v1, idx, 4 * LANES), axis=0, keepdims=True)
    el2 = jnp.where(idx == i1, NEG_INF, el)
    v2 = jnp.max(el2, axis=0, keepdims=True)
    i2 = jnp.min(jnp.where(el2 == v2, idx, 4 * LANES), axis=0, keepdims=True)
    e2 = jnp.exp(v2 - v1)
    w1 = g_w / (1.0 + e2)
    w2 = g_w * e2 / (1.0 + e2)
    route = (jnp.where(idx == i1, w1, 0.0) + jnp.where(idx == i2, w2, 0.0)
             + jnp.where(idx == GROUP_LANE, g_idx.astype(F32), 0.0))
    route_ref[0] = route.T


def _post(x, ys, ym, gm, woutt, gx, wqt, km, vmt, wot, gmoe, wrht, wrlt, br, tm):
    b, s, d = x.shape
    hd = d // XATTN_HEADS
    tok = lambda n: pl.BlockSpec((1, tm, n), lambda i, j: (i, j, 0))
    mem = pl.BlockSpec((1,) + km.shape[1:], lambda i, j: (i, 0, 0, 0))
    return pl.pallas_call(
        functools.partial(_post_kernel, hd=hd),
        grid=(b, s // tm),
        in_specs=[tok(d), tok(ys.shape[-1]), pl.BlockSpec((1, MLA_HEADS, tm, LANES), lambda i, j: (i, 0, j, 0)),
                  _full(gm.shape), _full(woutt.shape), _full(gx.shape), _full(wqt.shape), mem, mem,
                  _full(wot.shape), _full(gmoe.shape), _full(wrht.shape), _full(wrlt.shape), _full(br.shape)],
        out_specs=[tok(d), tok(LANES)],
        out_shape=[jax.ShapeDtypeStruct((b, s, d), F32), jax.ShapeDtypeStruct((b, s, LANES), F32)],
        compiler_params=_params("parallel", "parallel"),
        name="post",
    )(x, ys, ym, gm, woutt, gx, wqt, km, vmt, wot, gmoe, wrht, wrlt, br)


def _route_metadata(route, tm):
    b, s, _ = route.shape
    n_tiles = (s + MOE_GROUPS * tm) // tm
    gid = route[..., GROUP_LANE].astype(jnp.int32)
    groups = jnp.arange(MOE_GROUPS, dtype=jnp.int32)[None, :, None]
    onehot = (gid[:, None, :] == groups).astype(jnp.int32)
    csum = jnp.cumsum(onehot, axis=2)
    cnt = csum[:, :, -1]
    padded = (cnt + tm - 1) // tm * tm
    seg_end = jnp.cumsum(padded, axis=-1)
    dest = jnp.sum(onehot * ((seg_end - padded)[:, :, None] + csum - 1), axis=1)
    tile_start = jnp.arange(n_tiles, dtype=jnp.int32) * tm
    tile_g = jnp.sum((seg_end[:, None, :] <= tile_start[None, :, None]).astype(jnp.int32), axis=-1)
    tile_g = jnp.minimum(tile_g, MOE_GROUPS - 1)
    n_valid = (seg_end[:, -1] // tm).astype(jnp.int32)
    return dict(pos=dest, tile_g=tile_g.reshape(b * n_tiles), n_valid=n_valid, n_pad=n_tiles * tm)


def _dispatch_kernel(h_ref, g_ref, dst_ref, xs_ref, gs_ref, *, tt):
    @pl.when(pl.program_id(1) == 0)
    def _():
        xs_ref[...] = jnp.zeros(xs_ref.shape, xs_ref.dtype)
        gs_ref[...] = jnp.zeros(gs_ref.shape, gs_ref.dtype)

    for t in range(tt):
        r = dst_ref[0, 0, t]
        xs_ref[0, pl.ds(r, 1), :] = h_ref[0, t:t + 1, :]
        gs_ref[0, pl.ds(r, 1), :] = g_ref[0, t:t + 1, :]


def _dispatch(x2, route, pos, n_pad, tt):
    b, s, d = x2.shape
    nj = s // tt
    row_block = lambda n: pl.BlockSpec((1, n_pad, n), lambda i, j: (i, 0, 0))
    return pl.pallas_call(
        functools.partial(_dispatch_kernel, tt=tt),
        grid=(b, nj),
        in_specs=[pl.BlockSpec((1, tt, d), lambda i, j: (i, j, 0)),
                  pl.BlockSpec((1, tt, LANES), lambda i, j: (i, j, 0)),
                  pl.BlockSpec((1, 1, tt), lambda i, j: (i * nj + j, 0, 0), memory_space=pltpu.SMEM)],
        out_specs=[row_block(d), row_block(LANES)],
        out_shape=[jax.ShapeDtypeStruct((b, n_pad, d), F32), jax.ShapeDtypeStruct((b, n_pad, LANES), F32)],
        compiler_params=_params("parallel", "arbitrary", vmem=VMEM_LIMIT_LARGE),
        name="dispatch",
    )(x2, route, pos.reshape(b * nj, 1, tt))


def _moe_kernel(tg_ref, nv_ref, x_ref, gate_ref, gmoe_ref, wg_ref, wu_ref, wd_ref, o_ref, *, nt):
    bi = pl.program_id(0)
    ti = pl.program_id(1)
    grp = tg_ref[bi * nt + ti]

    @pl.when(ti < nv_ref[bi])
    def _():
        h = _rms(x_ref[0], gmoe_ref[...]).astype(BF16)
        gate = gate_ref[0]
        lane = lax.broadcasted_iota(jnp.int32, gate.shape, 1)
        acts = []
        for e in range(MOE_PER_GROUP):
            a = jax.nn.silu(_dot(h, wg_ref[e])) * _dot(h, wu_ref[e])
            gcol = jnp.sum(jnp.where(lane == grp * MOE_PER_GROUP + e, gate, 0.0), axis=-1, keepdims=True)
            acts.append((a * gcol).astype(BF16))
        wd = wd_ref[...]
        o_ref[0] = _dot(jnp.concatenate(acts, axis=1), wd.reshape(wd.shape[0] * wd.shape[1], wd.shape[2]))

    @pl.when(ti >= nv_ref[bi])
    def _():
        o_ref[0] = jnp.zeros(o_ref.shape[1:], o_ref.dtype)


def _moe(xs, gs, gmoe, tile_g, n_valid, wg, wu, wd, tm):
    b, n_pad, d = xs.shape
    nt = n_pad // tm
    ff = wg.shape[-1]
    group = lambda i, j, tg, nv: (tg[i * nt + j], 0, 0)
    tok = lambda n: pl.BlockSpec((1, tm, n), lambda i, j, tg, nv: (i, j, 0))
    grid_spec = pltpu.PrefetchScalarGridSpec(
        num_scalar_prefetch=2,
        grid=(b, nt),
        in_specs=[tok(d), tok(LANES), pl.BlockSpec(gmoe.shape, lambda i, j, tg, nv: (0, 0)),
                  pl.BlockSpec((MOE_PER_GROUP, d, ff), group),
                  pl.BlockSpec((MOE_PER_GROUP, d, ff), group), pl.BlockSpec((MOE_PER_GROUP, ff, d), group)],
        out_specs=tok(d),
    )
    return pl.pallas_call(
        functools.partial(_moe_kernel, nt=nt),
        grid_spec=grid_spec,
        out_shape=jax.ShapeDtypeStruct((b, n_pad, d), F32),
        compiler_params=_params("parallel", "arbitrary"),
        name="moe",
    )(tile_g, n_valid, xs, gs, gmoe, wg, wu, wd)


def _combine_kernel(ys_ref, pos_ref, x2_ref, gf_ref, o_ref, buf_ref, *, tt):
    for t in range(tt):
        buf_ref[t:t + 1, :] = ys_ref[0, pl.ds(pos_ref[0, 0, t], 1), :]
    o_ref[0] = _rms(x2_ref[0] + buf_ref[...], gf_ref[...])


def _combine(ys, pos, x2, gf, tt):
    b, s, d = x2.shape
    n_pad = ys.shape[1]
    nj = s // tt
    return pl.pallas_call(
        functools.partial(_combine_kernel, tt=tt),
        grid=(b, nj),
        in_specs=[pl.BlockSpec((1, n_pad, d), lambda i, j: (i, 0, 0)),
                  pl.BlockSpec((1, 1, tt), lambda i, j: (i * nj + j, 0, 0), memory_space=pltpu.SMEM),
                  pl.BlockSpec((1, tt, d), lambda i, j: (i, j, 0)), _full(gf.shape)],
        out_specs=pl.BlockSpec((1, tt, d), lambda i, j: (i, j, 0)),
        out_shape=jax.ShapeDtypeStruct((b, s, d), F32),
        scratch_shapes=[pltpu.VMEM((tt, d), F32)],
        compiler_params=_params("parallel", "arbitrary", vmem=VMEM_LIMIT_LARGE),
        name="combine",
    )(ys, pos.reshape(b * nj, 1, tt), x2, gf)


def _pad_heads(w, per_head, offset=0):
    k = w.shape[0]
    w = w.reshape(k, MLA_HEADS, per_head)
    w = jnp.pad(w, ((0, 0), (0, 0), (offset, LANES - per_head - offset)))
    return w.reshape(k, MLA_HEADS * LANES)


def _block_diag(blocks):
    n, r, c = blocks.shape
    eye = jnp.eye(n, dtype=blocks.dtype)
    return (eye[:, None, :, None] * blocks[:, :, None, :]).reshape(n * r, n * c)


def kernel(x, mem, positions, norm_mix_g, w_in, ssm_lam_re, ssm_lam_im, ssm_log_dt, ssm_b_re, ssm_b_im, ssm_c_re, ssm_c_im, ssm_d, ssm_w_glu, ssm_b_glu, mla_q_norm_g, mla_w_q_up, mla_kv_norm_g, mla_w_kv_up, out_norm_ssm_g, out_norm_mla_g, w_out, norm_xattn_g, norm_mem_g, xattn_w_q, xattn_w_k, xattn_w_v, xattn_w_o, norm_moe_g, moe_w_group, moe_b_group, moe_w_expert, moe_b_expert, moe_w_gate, moe_w_up, moe_w_down, norm_final_g):
    bsz, seq, d = x.shape
    depth = w_in.shape[0]
    assert depth == 1, "the final RMSNorm is fused into the last stage of a single layer"
    d_ssm = ssm_d.shape[-1]
    q_rank = mla_q_norm_g.shape[-1]
    kv_rank = mla_kv_norm_g.shape[-1]
    n_grp = d_ssm // SSM_GROUP
    s1, s2, s3 = d_ssm, d_ssm + q_rank, d_ssm + q_rank + kv_rank
    row = lambda v: v.reshape(1, -1).astype(F32)
    tiles = _tiles(seq)

    half = MLA_ROPE // 2
    invf = (ROPE_THETA ** (-jnp.arange(half, dtype=F32) / half)).reshape(half, 1)
    pos = positions.reshape(bsz, 1, seq)

    for l in range(depth):
        a_re, a_im, bb_re, bb_im = _ssm_prep(ssm_lam_re[l], ssm_lam_im[l], ssm_log_dt[l], ssm_b_re[l], ssm_b_im[l])
        gh = n_grp // 2
        bbd = jnp.stack([
            jnp.concatenate([_block_diag(bb_re[hf * gh:(hf + 1) * gh]), _block_diag(bb_im[hf * gh:(hf + 1) * gh])],
                            axis=1) for hf in range(2)]).astype(BF16)
        c_re_t = jnp.transpose(ssm_c_re[l], (0, 2, 1))
        c_im_t = jnp.transpose(ssm_c_im[l], (0, 2, 1))
        cmat = jnp.stack([
            jnp.concatenate([_block_diag(c_re_t[hf * gh:(hf + 1) * gh]), -_block_diag(c_im_t[hf * gh:(hf + 1) * gh])],
                            axis=0) for hf in range(2)]).astype(BF16)
        atab = jnp.stack([
            jnp.repeat(arr.reshape(2, 1, gh * SSM_STATE), bsz, axis=1).reshape(2 * bsz, gh * SSM_STATE)
            for arr in (a_re, a_im)])

        wi = w_in[l]
        wu = wi[:, :s1].astype(BF16)
        wcq = wi[:, s1:s2].astype(BF16)
        wckv = wi[:, s2:s3].astype(BF16)
        wkrt = jnp.pad(wi[:, s3:], ((0, 0), (MLA_NOPE, LANES - MLA_QK))).T.astype(BF16)
        wqt = _pad_heads(mla_w_q_up[l], MLA_QK).T.astype(BF16)
        wkv = mla_w_kv_up[l].reshape(kv_rank, MLA_HEADS, MLA_NOPE + MLA_V)
        wk = _pad_heads(wkv[:, :, :MLA_NOPE].reshape(kv_rank, -1), MLA_NOPE).astype(BF16)
        wvt = jnp.pad(wkv[:, :, MLA_NOPE:], ((0, 0), (0, 0), (0, VT_ROWS - MLA_V)))
        wvt = wvt.reshape(kv_rank, MLA_HEADS * VT_ROWS).T.astype(BF16)
        u, q, k, v = _proj(x, pos, row(norm_mix_g[l]), wu, wcq, wckv, wkrt, row(mla_q_norm_g[l]), wqt,
                           row(mla_kv_norm_g[l]), wk, wvt, invf, tm=tiles.proj)

        y_ssm = _ssm(u, bbd, atab, cmat, row(ssm_d[l]), ssm_w_glu[l].astype(BF16), row(ssm_b_glu[l]),
                     row(out_norm_ssm_g[l]), tt=tiles.ssm)
        y_mla = _attn(q, k, v, tq=tiles.attn_q, tk=tiles.attn_k, hp=tiles.attn_heads, strip=tiles.attn_strip)

        km, vm = _memkv(mem, row(norm_mem_g[l]), xattn_w_k[l].astype(BF16), xattn_w_v[l].astype(BF16))
        col = lambda v: v.reshape(-1, 1).astype(F32)
        woutt = w_out[l].T.astype(BF16)
        wr = jnp.concatenate([moe_w_expert[l].reshape(d, MOE_EXPERTS), moe_w_group[l]], axis=1)
        wrt = jnp.pad(wr, ((0, 0), (0, LANES - MOE_EXPERTS - MOE_GROUPS))).T.astype(F32)
        wrt_hi = wrt.astype(BF16)
        wrt_lo = (wrt - wrt_hi.astype(F32)).astype(BF16)
        br = jnp.concatenate([moe_b_expert[l].reshape(-1), moe_b_group[l]])
        br = jnp.pad(br, (0, LANES - MOE_EXPERTS - MOE_GROUPS)).reshape(LANES, 1).astype(F32)
        x2, route = _post(x, y_ssm, y_mla, row(out_norm_mla_g[l]), woutt, col(norm_xattn_g[l]),
                          xattn_w_q[l].T.astype(BF16), km, vm, xattn_w_o[l].T.astype(BF16), col(norm_moe_g[l]),
                          wrt_hi, wrt_lo, br, tm=tiles.post)

        meta = _route_metadata(route, tiles.moe)
        xs, gs = _dispatch(x2, route, meta["pos"], meta["n_pad"], tt=tiles.copy)
        ys = _moe(xs, gs, row(norm_moe_g[l]), meta["tile_g"], meta["n_valid"], moe_w_gate[l].astype(BF16),
                  moe_w_up[l].astype(BF16), moe_w_down[l].astype(BF16), tm=tiles.moe)
        x = _combine(ys, meta["pos"], x2, row(norm_final_g), tt=tiles.copy)
    return x
```

```python
import functools
import math
from typing import NamedTuple

import jax
import jax.numpy as jnp
from jax import lax
from jax.experimental import pallas as pl
from jax.experimental.pallas import tpu as pltpu

F32 = jnp.float32
BF16 = jnp.bfloat16

EPS = 1e-6
NEG_INF = -1e30
CHUNK = 64

LANES = 128
SSM_GROUP = 16
SSM_STATE = 64
MLA_HEADS = 8
MLA_NOPE = 64
MLA_ROPE = 32
MLA_QK = MLA_NOPE + MLA_ROPE
MLA_V = 64
ROPE_THETA = 10000.0
XATTN_HEADS = 4
MOE_GROUPS = 4
MOE_PER_GROUP = 8
MOE_EXPERTS = MOE_GROUPS * MOE_PER_GROUP
VMEM_LIMIT = 48 * 1024 * 1024
VMEM_LIMIT_LARGE = 56 * 1024 * 1024
LOG2E = math.log2(math.e)
VT_ROWS = 80
GROUP_LANE = 64


class Tiles(NamedTuple):
    proj: int
    ssm: int
    attn_q: int
    attn_k: int
    attn_heads: int
    attn_strip: int
    post: int
    moe: int
    copy: int


def _tiles(seq):
    return Tiles(proj=min(512, seq), ssm=min(256, seq), attn_q=min(512, seq), attn_k=min(256, seq),
                 attn_heads=MLA_HEADS, attn_strip=32, post=min(512, seq),
                 moe=256, copy=min(512, seq))


def _dot(a, b):
    return jnp.dot(a, b, preferred_element_type=F32)


def _dot_nt(a, b):
    return lax.dot_general(a, b, (((1,), (1,)), ((), ())), preferred_element_type=F32)


def _rms(x, g):
    ms = jnp.mean(x * x, axis=-1, keepdims=True)
    return x * lax.rsqrt(ms + EPS) * g


def _params(*sem, vmem=VMEM_LIMIT):
    return pltpu.CompilerParams(dimension_semantics=sem, vmem_limit_bytes=vmem)


def _full(shape):
    zeros = (0,) * len(shape)
    return pl.BlockSpec(shape, lambda *_: zeros)


def _ssm_prep_kernel(lre_ref, lim_ref, ldt_ref, bre_ref, bim_ref, are_ref, aim_ref, bbre_ref, bbim_ref):
    lre = jnp.minimum(lre_ref[...], -1e-4)
    lim = lim_ref[...]
    dt = jnp.exp(ldt_ref[...])
    mag = jnp.exp(lre * dt)
    are = mag * jnp.cos(lim * dt)
    aim = mag * jnp.sin(lim * dt)
    are_ref[...] = are
    aim_ref[...] = aim
    nre = are - 1.0
    den = lre * lre + lim * lim
    fre = (nre * lre + aim * lim) / den
    fim = (aim * lre - nre * lim) / den
    bre = bre_ref[...]
    bim = bim_ref[...]
    bbre_ref[...] = fre * bre - fim * bim
    bbim_ref[...] = fre * bim + fim * bre


def _ssm_prep(lam_re, lam_im, log_dt, b_re, b_im):
    g, p = lam_re.shape
    hh = b_re.shape[-1]
    bt_re = jnp.transpose(b_re, (0, 2, 1))
    bt_im = jnp.transpose(b_im, (0, 2, 1))
    outs = pl.pallas_call(
        _ssm_prep_kernel,
        out_shape=[jax.ShapeDtypeStruct((g, 1, p), F32)] * 2 + [jax.ShapeDtypeStruct((g, hh, p), F32)] * 2,
        name="ssm_prep",
    )(lam_re.reshape(g, 1, p), lam_im.reshape(g, 1, p), log_dt.reshape(g, 1, 1), bt_re, bt_im)
    a_re, a_im, bb_re, bb_im = outs
    return a_re.reshape(g, p), a_im.reshape(g, p), bb_re, bb_im


def _proj_kernel(x_ref, pos_ref, gmix_ref, wu_ref, wcq_ref, wckv_ref, wkrt_ref, qg_ref, wqt_ref, kvg_ref,
                 wk_ref, wvt_ref, invf_ref, u_ref, q_ref, k_ref, v_ref):
    half = MLA_ROPE // 2
    h = _rms(x_ref[0], gmix_ref[...]).astype(BF16)
    u_ref[0] = _dot(h, wu_ref[...]).astype(BF16)
    cqn = _rms(_dot(h, wcq_ref[...]), qg_ref[...]).astype(BF16)
    ckvn = _rms(_dot(h, wckv_ref[...]), kvg_ref[...]).astype(BF16)

    ang = invf_ref[...] * pos_ref[0].astype(F32)
    cosv = jnp.cos(ang)
    sinv = jnp.sin(ang)

    def rot_t(blk):
        x1 = blk[MLA_NOPE:MLA_NOPE + half]
        x2 = blk[MLA_NOPE + half:MLA_QK]
        return jnp.concatenate([blk[:MLA_NOPE], x1 * cosv - x2 * sinv, x1 * sinv + x2 * cosv, blk[MLA_QK:]], axis=0)

    qt = _dot_nt(wqt_ref[...], cqn) * (MLA_QK ** -0.5 * LOG2E)
    vt = _dot_nt(wvt_ref[...], ckvn)
    kk = _dot(ckvn, wk_ref[...])
    kpe = rot_t(_dot_nt(wkrt_ref[...], h)).T
    ones_row = jnp.where(lax.broadcasted_iota(jnp.int32, (VT_ROWS, 1), 0) == MLA_V, 1.0, 0.0)
    for hh in range(MLA_HEADS):
        q_ref[0, hh] = rot_t(qt[hh * LANES:(hh + 1) * LANES]).astype(BF16)
        k_ref[0, hh] = (kk[:, hh * LANES:(hh + 1) * LANES] + kpe).astype(BF16)
        v_ref[0, hh] = (vt[hh * VT_ROWS:(hh + 1) * VT_ROWS] + ones_row).astype(BF16)


def _proj(x, pos, gmix, wu, wcq, wckv, wkrt, qg, wqt, kvg, wk, wvt, invf, tm):
    b, s, d = x.shape
    grid = (b, s // tm)
    tok = lambda n: pl.BlockSpec((1, tm, n), lambda i, j: (i, j, 0))
    head = pl.BlockSpec((1, MLA_HEADS, tm, LANES), lambda i, j: (i, 0, j, 0))
    head_t = pl.BlockSpec((1, MLA_HEADS, LANES, tm), lambda i, j: (i, 0, 0, j))
    head_vt = pl.BlockSpec((1, MLA_HEADS, VT_ROWS, tm), lambda i, j: (i, 0, 0, j))
    tshape = lambda n: jax.ShapeDtypeStruct((b, MLA_HEADS, n, s), BF16)
    return pl.pallas_call(
        _proj_kernel,
        grid=grid,
        in_specs=[tok(d), pl.BlockSpec((1, 1, tm), lambda i, j: (i, 0, j)), _full(gmix.shape), _full(wu.shape),
                  _full(wcq.shape), _full(wckv.shape), _full(wkrt.shape), _full(qg.shape), _full(wqt.shape),
                  _full(kvg.shape), _full(wk.shape), _full(wvt.shape), _full(invf.shape)],
        out_specs=[tok(wu.shape[1]), head_t, head, head_vt],
        out_shape=[jax.ShapeDtypeStruct((b, s, wu.shape[1]), BF16), tshape(LANES),
                   jax.ShapeDtypeStruct((b, MLA_HEADS, s, LANES), BF16), tshape(VT_ROWS)],
        compiler_params=_params("parallel", "parallel"),
        name="proj",
    )(x, pos, gmix, wu, wcq, wckv, wkrt, qg, wqt, kvg, wk, wvt, invf)


def _ssm_kernel(u_ref, bbd_ref, atab_ref, cmat_ref, dskip_ref, wglu_ref, bglu_ref, gout_ref, y_ref,
                xs_ref, st_ref, *, tt, slab, nb, lane_tiles):
    j = pl.program_id(0)
    half_w = u_ref.shape[-1] // 2

    @pl.when(j == 0)
    def _():
        st_ref[...] = jnp.zeros_like(st_ref)

    for hf in range(2):
        ub = u_ref[:, :, hf * half_w:(hf + 1) * half_w].reshape(nb * tt, half_w)
        bu = _dot(ub, bbd_ref[hf])
        for b in range(nb):
            r0 = (hf * nb + b) * slab
            for c in range(2 * lane_tiles):
                xs_ref[c, r0:r0 + tt, :] = bu[b * tt:(b + 1) * tt, c * LANES:(c + 1) * LANES]

    rows = 2 * nb
    group = 4
    for c0 in range(0, lane_tiles, group):
        cs = list(range(c0, c0 + group))
        a_re = [atab_ref[0, :, c * LANES:(c + 1) * LANES] for c in cs]
        a_im = [atab_ref[1, :, c * LANES:(c + 1) * LANES] for c in cs]
        init = tuple(st_ref[0, :, c * LANES:(c + 1) * LANES] for c in cs) + \
            tuple(st_ref[1, :, c * LANES:(c + 1) * LANES] for c in cs)

        def step(t, carry, cs=cs, a_re=a_re, a_im=a_im):
            new_re, new_im = [], []
            for i, c in enumerate(cs):
                x_re, x_im = carry[i], carry[group + i]
                idx = pl.ds(t, rows, stride=slab)
                n_re = a_re[i] * x_re - a_im[i] * x_im + xs_ref[c, idx, :]
                n_im = a_re[i] * x_im + a_im[i] * x_re + xs_ref[lane_tiles + c, idx, :]
                xs_ref[c, idx, :] = n_re
                xs_ref[lane_tiles + c, idx, :] = n_im
                new_re.append(n_re)
                new_im.append(n_im)
            return tuple(new_re) + tuple(new_im)

        fin = lax.fori_loop(0, tt, step, init, unroll=4)
        for i, c in enumerate(cs):
            st_ref[0, :, c * LANES:(c + 1) * LANES] = fin[i]
            st_ref[1, :, c * LANES:(c + 1) * LANES] = fin[group + i]

    ys = []
    for hf in range(2):
        xb = []
        for b in range(nb):
            r0 = (hf * nb + b) * slab
            xb.append(jnp.concatenate([xs_ref[c, r0:r0 + tt, :] for c in range(2 * lane_tiles)], axis=1))
        xh = jnp.concatenate(xb, axis=0).astype(BF16)
        ys.append(_dot(xh, cmat_ref[hf]))
    y = jnp.concatenate(ys, axis=1)
    u = u_ref[...].astype(F32).reshape(nb * tt, 2 * half_w)
    y = jax.nn.gelu(y + dskip_ref[...] * u)
    y = y * jax.nn.sigmoid(_dot(y.astype(BF16), wglu_ref[...]) + bglu_ref[...])
    y = _rms(y, gout_ref[...])
    y_ref[...] = y.reshape(nb, tt, 2 * half_w).astype(BF16)


def _ssm(u, bbd, atab, cmat, dskip, wglu, bglu, gout, tt):
    nb, s, dssm = u.shape
    lane_tiles = bbd.shape[-1] // (2 * LANES)
    slab = tt + 4
    kern = functools.partial(_ssm_kernel, tt=tt, slab=slab, nb=nb, lane_tiles=lane_tiles)
    return pl.pallas_call(
        kern,
        grid=(s // tt,),
        in_specs=[pl.BlockSpec((nb, tt, dssm), lambda j: (0, j, 0)), _full(bbd.shape), _full(atab.shape),
                  _full(cmat.shape), _full(dskip.shape), _full(wglu.shape), _full(bglu.shape), _full(gout.shape)],
        out_specs=pl.BlockSpec((nb, tt, dssm), lambda j: (0, j, 0)),
        out_shape=jax.ShapeDtypeStruct((nb, s, dssm), BF16),
        scratch_shapes=[pltpu.VMEM((2 * lane_tiles, 2 * nb * slab, LANES), F32),
                        pltpu.VMEM((2, 2 * nb, lane_tiles * LANES), F32)],
        compiler_params=_params("arbitrary"),
        name="ssm",
    )(u, bbd, atab, cmat, dskip, wglu, bglu, gout)


def _attn_kernel(qt_ref, k_ref, vt_ref, o_ref, s_ref, p_ref, m_ref, a_ref, acc_ref, *, tq, tk, hp, strip):
    qi = pl.program_id(2)
    m_ref[...] = jnp.full(m_ref.shape, NEG_INF, F32)
    acc_ref[...] = jnp.zeros(acc_ref.shape, F32)
    q_chunk = lax.broadcasted_iota(jnp.int32, (1, tq), 1) // CHUNK

    def tile(k0, diag_off):
        qs = slice(0 if diag_off is None else diag_off, tq)
        for hh in range(hp):
            s_ref[hh, :, qs] = _dot(k_ref[0, hh, pl.ds(k0, tk), :], qt_ref[0, hh, :, qs])

        def strip_of(hh, r):
            s = s_ref[hh, r:r + strip, qs]
            if diag_off is None:
                return s
            return jnp.where(q_chunk[:, qs] >= (diag_off + r) // CHUNK, s, NEG_INF)

        for hh in range(hp):
            mt = strip_of(hh, 0)
            for r in range(strip, tk, strip):
                mt = jnp.maximum(mt, strip_of(hh, r))
            m_old = m_ref[hh, :, qs]
            m_new = jnp.maximum(m_old, jnp.max(mt, axis=0, keepdims=True))
            a_ref[hh, :, qs] = jnp.exp2(m_old - m_new)
            m_ref[hh, :, qs] = m_new
            for r in range(0, tk, strip):
                p_ref[hh, r:r + strip, qs] = jnp.exp2(strip_of(hh, r) - m_new).astype(BF16)
        for hh in range(hp):
            pv = _dot(vt_ref[0, hh, :, pl.ds(k0, tk)], p_ref[hh, :, qs])
            acc_ref[hh, :, qs] = acc_ref[hh, :, qs] * a_ref[hh, :, qs] + pv

    def body(j, c):
        tile(pl.multiple_of(j * tk, tk), None)
        return c

    lax.fori_loop(0, qi * (tq // tk), body, 0)
    for off in range(0, tq, tk):
        tile(pl.multiple_of(qi * tq + off, tk), off)
    feat = lax.broadcasted_iota(jnp.int32, (VT_ROWS, 1), 0)
    pad = jnp.zeros((LANES - VT_ROWS, tq), F32)
    for hh in range(hp):
        acc = acc_ref[hh]
        out = acc * (1.0 / acc[MLA_V:MLA_V + 1, :])
        out = jnp.where(feat < MLA_V, out, 0.0)
        o_ref[0, hh] = jnp.concatenate([out, pad], axis=0).T.astype(BF16)


def _attn(qt, k, vt, tq, tk, hp, strip):
    b, h, s, _ = k.shape
    assert strip <= CHUNK and CHUNK % strip == 0 and tk % CHUNK == 0 and tq % tk == 0
    return pl.pallas_call(
        functools.partial(_attn_kernel, tq=tq, tk=tk, hp=hp, strip=strip),
        grid=(b, h // hp, s // tq),
        in_specs=[pl.BlockSpec((1, hp, LANES, tq), lambda i, j, t: (i, j, 0, t)),
                  pl.BlockSpec((1, hp, s, LANES), lambda i, j, t: (i, j, 0, 0)),
                  pl.BlockSpec((1, hp, VT_ROWS, s), lambda i, j, t: (i, j, 0, 0))],
        out_specs=pl.BlockSpec((1, hp, tq, LANES), lambda i, j, t: (i, j, t, 0)),
        out_shape=jax.ShapeDtypeStruct((b, h, s, LANES), BF16),
        scratch_shapes=[pltpu.VMEM((hp, tk, tq), F32), pltpu.VMEM((hp, tk, tq), BF16),
                        pltpu.VMEM((hp, 1, tq), F32), pltpu.VMEM((hp, 1, tq), F32),
                        pltpu.VMEM((hp, VT_ROWS, tq), F32)],
        compiler_params=_params("parallel", "parallel", "arbitrary", vmem=VMEM_LIMIT_LARGE),
        name="attn",
    )(qt, k, vt)


def _memkv_kernel(mem_ref, g_ref, wk_ref, wv_ref, k_ref, v_ref, *, hd):
    mn = _rms(mem_ref[0], g_ref[...]).astype(BF16)
    kk = _dot(mn, wk_ref[...])
    vv = _dot(mn, wv_ref[...])
    for hh in range(XATTN_HEADS):
        k_ref[0, hh] = kk[:, hh * hd:(hh + 1) * hd].astype(BF16)
        v_ref[0, hh] = vv[:, hh * hd:(hh + 1) * hd].T.astype(BF16)


def _memkv(mem, g, wk, wv):
    b, nm, d = mem.shape
    hd = d // XATTN_HEADS
    ospec = pl.BlockSpec((1, XATTN_HEADS, nm, hd), lambda i: (i, 0, 0, 0))
    return pl.pallas_call(
        functools.partial(_memkv_kernel, hd=hd),
        grid=(b,),
        in_specs=[pl.BlockSpec((1, nm, d), lambda i: (i, 0, 0)), _full(g.shape), _full(wk.shape), _full(wv.shape)],
        out_specs=[ospec, ospec],
        out_shape=[jax.ShapeDtypeStruct((b, XATTN_HEADS, nm, hd), BF16)] * 2,
        compiler_params=_params("parallel"),
        name="mem_kv",
    )(mem, g, wk, wv)


def _rms_t(xt, g_col):
    ms = jnp.mean(xt * xt, axis=0, keepdims=True)
    return xt * lax.rsqrt(ms + EPS) * g_col


def _post_kernel(x_ref, ys_ref, ym_ref, gm_ref, woutt_ref, gx_ref, wqt_ref, km_ref, vmt_ref, wot_ref,
                 gmoe_ref, wrht_ref, wrlt_ref, br_ref, x2_ref, route_ref, *, hd):
    heads = [ym_ref[0, hh].astype(F32) for hh in range(MLA_HEADS)]
    ym = jnp.concatenate([heads[k] + pltpu.roll(heads[k + 1], MLA_V, 1) for k in range(0, MLA_HEADS, 2)], axis=1)
    ymn = _rms(ym, gm_ref[...]).astype(BF16)
    y_mix = jnp.concatenate([ys_ref[0], ymn], axis=1)
    x1 = x_ref[0].T + _dot_nt(woutt_ref[...], y_mix)

    h2 = _rms_t(x1, gx_ref[...]).astype(BF16)
    qx = (_dot(wqt_ref[...], h2) * (hd ** -0.5)).astype(BF16)
    outs = []
    for hh in range(XATTN_HEADS):
        s = _dot(km_ref[0, hh], qx[hh * hd:(hh + 1) * hd])
        p = jnp.exp(s - jnp.max(s, axis=0, keepdims=True))
        inv = 1.0 / jnp.sum(p, axis=0, keepdims=True)
        outs.append(_dot(vmt_ref[0, hh], p.astype(BF16)) * inv)
    o = jnp.concatenate(outs, axis=0).astype(BF16)
    x2 = x1 + _dot(wot_ref[...], o)
    x2_ref[0] = x2.T

    h3 = _rms_t(x2, gmoe_ref[...])
    h_hi = h3.astype(BF16)
    h_lo = (h3 - h_hi.astype(F32)).astype(BF16)
    logits = (_dot(wrht_ref[...], h_hi) + _dot(wrlt_ref[...], h_hi) + _dot(wrht_ref[...], h_lo)) + br_ref[...]
    idx = lax.broadcasted_iota(jnp.int32, logits.shape, 0)
    is_g = (idx >= MOE_EXPERTS) & (idx < MOE_EXPERTS + MOE_GROUPS)
    gl = jnp.where(is_g, logits, NEG_INF)
    gmax = jnp.max(gl, axis=0, keepdims=True)
    g_w = 1.0 / jnp.sum(jnp.exp(gl - gmax), axis=0, keepdims=True)
    g_idx = jnp.min(jnp.where(gl == gmax, idx, 4 * LANES), axis=0, keepdims=True) - MOE_EXPERTS
    in_grp = (idx >= g_idx * MOE_PER_GROUP) & (idx < (g_idx + 1) * MOE_PER_GROUP)
    el = jnp.where(in_grp, logits, NEG_INF)
    v1 = jnp.max(el, axis=0, keepdims=True)
    i1 = jnp.min(jnp.where(el == v1, idx, 4 * LANES), axis=0, keepdims=True)
    el2 = jnp.where(idx == i1, NEG_INF, el)
    v2 = jnp.max(el2, axis=0, keepdims=True)
    i2 = jnp.min(jnp.where(el2 == v2, idx, 4 * LANES), axis=0, keepdims=True)
    e2 = jnp.exp(v2 - v1)
    w1 = g_w / (1.0 + e2)
    w2 = g_w * e2 / (1.0 + e2)
    route = (jnp.where(idx == i1, w1, 0.0) + jnp.where(idx == i2, w2, 0.0)
             + jnp.where(idx == GROUP_LANE, g_idx.astype(F32), 0.0))
    route_ref[0] = route.T


def _post(x, ys, ym, gm, woutt, gx, wqt, km, vmt, wot, gmoe, wrht, wrlt, br, tm):
    b, s, d = x.shape
    hd = d // XATTN_HEADS
    tok = lambda n: pl.BlockSpec((1, tm, n), lambda i, j: (i, j, 0))
    mem = pl.BlockSpec((1,) + km.shape[1:], lambda i, j: (i, 0, 0, 0))
    return pl.pallas_call(
        functools.partial(_post_kernel, hd=hd),
        grid=(b, s // tm),
        in_specs=[tok(d), tok(ys.shape[-1]), pl.BlockSpec((1, MLA_HEADS, tm, LANES), lambda i, j: (i, 0, j, 0)),
                  _full(gm.shape), _full(woutt.shape), _full(gx.shape), _full(wqt.shape), mem, mem,
                  _full(wot.shape), _full(gmoe.shape), _full(wrht.shape), _full(wrlt.shape), _full(br.shape)],
        out_specs=[tok(d), tok(LANES)],
        out_shape=[jax.ShapeDtypeStruct((b, s, d), F32), jax.ShapeDtypeStruct((b, s, LANES), F32)],
        compiler_params=_params("parallel", "parallel"),
        name="post",
    )(x, ys, ym, gm, woutt, gx, wqt, km, vmt, wot, gmoe, wrht, wrlt, br)


def _route_metadata(route, tm):
    b, s, _ = route.shape
    n_tiles = (s + MOE_GROUPS * tm) // tm
    gid = route[..., GROUP_LANE].astype(jnp.int32)
    groups = jnp.arange(MOE_GROUPS, dtype=jnp.int32)[None, :, None]
    onehot = (gid[:, None, :] == groups).astype(jnp.int32)
    csum = jnp.cumsum(onehot, axis=2)
    cnt = csum[:, :, -1]
    padded = (cnt + tm - 1) // tm * tm
    seg_end = jnp.cumsum(padded, axis=-1)
    dest = jnp.sum(onehot * ((seg_end - padded)[:, :, None] + csum - 1), axis=1)
    tile_start = jnp.arange(n_tiles, dtype=jnp.int32) * tm
    tile_g = jnp.sum((seg_end[:, None, :] <= tile_start[None, :, None]).astype(jnp.int32), axis=-1)
    tile_g = jnp.minimum(tile_g, MOE_GROUPS - 1)
    n_valid = (seg_end[:, -1] // tm).astype(jnp.int32)
    return dict(pos=dest, tile_g=tile_g.reshape(b * n_tiles), n_valid=n_valid, n_pad=n_tiles * tm)


def _dispatch_kernel(h_ref, g_ref, dst_ref, xs_ref, gs_ref, *, tt):
    @pl.when(pl.program_id(1) == 0)
    def _():
        xs_ref[...] = jnp.zeros(xs_ref.shape, xs_ref.dtype)
        gs_ref[...] = jnp.zeros(gs_ref.shape, gs_ref.dtype)

    for t in range(tt):
        r = dst_ref[0, 0, t]
        xs_ref[0, pl.ds(r, 1), :] = h_ref[0, t:t + 1, :]
        gs_ref[0, pl.ds(r, 1), :] = g_ref[0, t:t + 1, :]


def _dispatch(x2, route, pos, n_pad, tt):
    b, s, d = x2.shape
    nj = s // tt
    row_block = lambda n: pl.BlockSpec((1, n_pad, n), lambda i, j: (i, 0, 0))
    return pl.pallas_call(
        functools.partial(_dispatch_kernel, tt=tt),
        grid=(b, nj),
        in_specs=[pl.BlockSpec((1, tt, d), lambda i, j: (i, j, 0)),
                  pl.BlockSpec((1, tt, LANES), lambda i, j: (i, j, 0)),
                  pl.BlockSpec((1, 1, tt), lambda i, j: (i * nj + j, 0, 0), memory_space=pltpu.SMEM)],
        out_specs=[row_block(d), row_block(LANES)],
        out_shape=[jax.ShapeDtypeStruct((b, n_pad, d), F32), jax.ShapeDtypeStruct((b, n_pad, LANES), F32)],
        compiler_params=_params("parallel", "arbitrary", vmem=VMEM_LIMIT_LARGE),
        name="dispatch",
    )(x2, route, pos.reshape(b * nj, 1, tt))


def _moe_kernel(tg_ref, nv_ref, x_ref, gate_ref, gmoe_ref, wg_ref, wu_ref, wd_ref, o_ref, *, nt):
    bi = pl.program_id(0)
    ti = pl.program_id(1)
    grp = tg_ref[bi * nt + ti]

    @pl.when(ti < nv_ref[bi])
    def _():
        h = _rms(x_ref[0], gmoe_ref[...]).astype(BF16)
        gate = gate_ref[0]
        lane = lax.broadcasted_iota(jnp.int32, gate.shape, 1)
        acts = []
        for e in range(MOE_PER_GROUP):
            a = jax.nn.silu(_dot(h, wg_ref[e])) * _dot(h, wu_ref[e])
            gcol = jnp.sum(jnp.where(lane == grp * MOE_PER_GROUP + e, gate, 0.0), axis=-1, keepdims=True)
            acts.append((a * gcol).astype(BF16))
        wd = wd_ref[...]
        o_ref[0] = _dot(jnp.concatenate(acts, axis=1), wd.reshape(wd.shape[0] * wd.shape[1], wd.shape[2]))

    @pl.when(ti >= nv_ref[bi])
    def _():
        o_ref[0] = jnp.zeros(o_ref.shape[1:], o_ref.dtype)


def _moe(xs, gs, gmoe, tile_g, n_valid, wg, wu, wd, tm):
    b, n_pad, d = xs.shape
    nt = n_pad // tm
    ff = wg.shape[-1]
    group = lambda i, j, tg, nv: (tg[i * nt + j], 0, 0)
    tok = lambda n: pl.BlockSpec((1, tm, n), lambda i, j, tg, nv: (i, j, 0))
    grid_spec = pltpu.PrefetchScalarGridSpec(
        num_scalar_prefetch=2,
        grid=(b, nt),
        in_specs=[tok(d), tok(LANES), pl.BlockSpec(gmoe.shape, lambda i, j, tg, nv: (0, 0)),
                  pl.BlockSpec((MOE_PER_GROUP, d, ff), group),
                  pl.BlockSpec((MOE_PER_GROUP, d, ff), group), pl.BlockSpec((MOE_PER_GROUP, ff, d), group)],
        out_specs=tok(d),
    )
    return pl.pallas_call(
        functools.partial(_moe_kernel, nt=nt),
        grid_spec=grid_spec,
        out_shape=jax.ShapeDtypeStruct((b, n_pad, d), F32),
        compiler_params=_params("parallel", "arbitrary"),
        name="moe",
    )(tile_g, n_valid, xs, gs, gmoe, wg, wu, wd)


def _combine_kernel(ys_ref, pos_ref, x2_ref, gf_ref, o_ref, buf_ref, *, tt):
    for t in range(tt):
        buf_ref[t:t + 1, :] = ys_ref[0, pl.ds(pos_ref[0, 0, t], 1), :]
    o_ref[0] = _rms(x2_ref[0] + buf_ref[...], gf_ref[...])


def _combine(ys, pos, x2, gf, tt):
    b, s, d = x2.shape
    n_pad = ys.shape[1]
    nj = s // tt
    return pl.pallas_call(
        functools.partial(_combine_kernel, tt=tt),
        grid=(b, nj),
        in_specs=[pl.BlockSpec((1, n_pad, d), lambda i, j: (i, 0, 0)),
                  pl.BlockSpec((1, 1, tt), lambda i, j: (i * nj + j, 0, 0), memory_space=pltpu.SMEM),
                  pl.BlockSpec((1, tt, d), lambda i, j: (i, j, 0)), _full(gf.shape)],
        out_specs=pl.BlockSpec((1, tt, d), lambda i, j: (i, j, 0)),
        out_shape=jax.ShapeDtypeStruct((b, s, d), F32),
        scratch_shapes=[pltpu.VMEM((tt, d), F32)],
        compiler_params=_params("parallel", "arbitrary", vmem=VMEM_LIMIT_LARGE),
        name="combine",
    )(ys, pos.reshape(b * nj, 1, tt), x2, gf)


def _pad_heads(w, per_head, offset=0):
    k = w.shape[0]
    w = w.reshape(k, MLA_HEADS, per_head)
    w = jnp.pad(w, ((0, 0), (0, 0), (offset, LANES - per_head - offset)))
    return w.reshape(k, MLA_HEADS * LANES)


def _block_diag(blocks):
    n, r, c = blocks.shape
    eye = jnp.eye(n, dtype=blocks.dtype)
    return (eye[:, None, :, None] * blocks[:, :, None, :]).reshape(n * r, n * c)


def kernel(x, mem, positions, norm_mix_g, w_in, ssm_lam_re, ssm_lam_im, ssm_log_dt, ssm_b_re, ssm_b_im, ssm_c_re, ssm_c_im, ssm_d, ssm_w_glu, ssm_b_glu, mla_q_norm_g, mla_w_q_up, mla_kv_norm_g, mla_w_kv_up, out_norm_ssm_g, out_norm_mla_g, w_out, norm_xattn_g, norm_mem_g, xattn_w_q, xattn_w_k, xattn_w_v, xattn_w_o, norm_moe_g, moe_w_group, moe_b_group, moe_w_expert, moe_b_expert, moe_w_gate, moe_w_up, moe_w_down, norm_final_g):
    bsz, seq, d = x.shape
    depth = w_in.shape[0]
    assert depth == 1, "the final RMSNorm is fused into the last stage of a single layer"
    d_ssm = ssm_d.shape[-1]
    q_rank = mla_q_norm_g.shape[-1]
    kv_rank = mla_kv_norm_g.shape[-1]
    n_grp = d_ssm // SSM_GROUP
    s1, s2, s3 = d_ssm, d_ssm + q_rank, d_ssm + q_rank + kv_rank
    row = lambda v: v.reshape(1, -1).astype(F32)
    tiles = _tiles(seq)

    half = MLA_ROPE // 2
    invf = (ROPE_THETA ** (-jnp.arange(half, dtype=F32) / half)).reshape(half, 1)
    pos = positions.reshape(bsz, 1, seq)

    for l in range(depth):
        a_re, a_im, bb_re, bb_im = _ssm_prep(ssm_lam_re[l], ssm_lam_im[l], ssm_log_dt[l], ssm_b_re[l], ssm_b_im[l])
        gh = n_grp // 2
        bbd = jnp.stack([
            jnp.concatenate([_block_diag(bb_re[hf * gh:(hf + 1) * gh]), _block_diag(bb_im[hf * gh:(hf + 1) * gh])],
                            axis=1) for hf in range(2)]).astype(BF16)
        c_re_t = jnp.transpose(ssm_c_re[l], (0, 2, 1))
        c_im_t = jnp.transpose(ssm_c_im[l], (0, 2, 1))
        cmat = jnp.stack([
            jnp.concatenate([_block_diag(c_re_t[hf * gh:(hf + 1) * gh]), -_block_diag(c_im_t[hf * gh:(hf + 1) * gh])],
                            axis=0) for hf in range(2)]).astype(BF16)
        atab = jnp.stack([
            jnp.repeat(arr.reshape(2, 1, gh * SSM_STATE), bsz, axis=1).reshape(2 * bsz, gh * SSM_STATE)
            for arr in (a_re, a_im)])

        wi = w_in[l]
        wu = wi[:, :s1].astype(BF16)
        wcq = wi[:, s1:s2].astype(BF16)
        wckv = wi[:, s2:s3].astype(BF16)
        wkrt = jnp.pad(wi[:, s3:], ((0, 0), (MLA_NOPE, LANES - MLA_QK))).T.astype(BF16)
        wqt = _pad_heads(mla_w_q_up[l], MLA_QK).T.astype(BF16)
        wkv = mla_w_kv_up[l].reshape(kv_rank, MLA_HEADS, MLA_NOPE + MLA_V)
        wk = _pad_heads(wkv[:, :, :MLA_NOPE].reshape(kv_rank, -1), MLA_NOPE).astype(BF16)
        wvt = jnp.pad(wkv[:, :, MLA_NOPE:], ((0, 0), (0, 0), (0, VT_ROWS - MLA_V)))
        wvt = wvt.reshape(kv_rank, MLA_HEADS * VT_ROWS).T.astype(BF16)
        u, q, k, v = _proj(x, pos, row(norm_mix_g[l]), wu, wcq, wckv, wkrt, row(mla_q_norm_g[l]), wqt,
                           row(mla_kv_norm_g[l]), wk, wvt, invf, tm=tiles.proj)

        y_ssm = _ssm(u, bbd, atab, cmat, row(ssm_d[l]), ssm_w_glu[l].astype(BF16), row(ssm_b_glu[l]),
                     row(out_norm_ssm_g[l]), tt=tiles.ssm)
        y_mla = _attn(q, k, v, tq=tiles.attn_q, tk=tiles.attn_k, hp=tiles.attn_heads, strip=tiles.attn_strip)

        km, vm = _memkv(mem, row(norm_mem_g[l]), xattn_w_k[l].astype(BF16), xattn_w_v[l].astype(BF16))
        col = lambda v: v.reshape(-1, 1).astype(F32)
        woutt = w_out[l].T.astype(BF16)
        wr = jnp.concatenate([moe_w_expert[l].reshape(d, MOE_EXPERTS), moe_w_group[l]], axis=1)
        wrt = jnp.pad(wr, ((0, 0), (0, LANES - MOE_EXPERTS - MOE_GROUPS))).T.astype(F32)
        wrt_hi = wrt.astype(BF16)
        wrt_lo = (wrt - wrt_hi.astype(F32)).astype(BF16)
        br = jnp.concatenate([moe_b_expert[l].reshape(-1), moe_b_group[l]])
        br = jnp.pad(br, (0, LANES - MOE_EXPERTS - MOE_GROUPS)).reshape(LANES, 1).astype(F32)
        x2, route = _post(x, y_ssm, y_mla, row(out_norm_mla_g[l]), woutt, col(norm_xattn_g[l]),
                          xattn_w_q[l].T.astype(BF16), km, vm, xattn_w_o[l].T.astype(BF16), col(norm_moe_g[l]),
                          wrt_hi, wrt_lo, br, tm=tiles.post)

        meta = _route_metadata(route, tiles.moe)
        xs, gs = _dispatch(x2, route, meta["pos"], meta["n_pad"], tt=tiles.copy)
        ys = _moe(xs, gs, row(norm_moe_g[l]), meta["tile_g"], meta["n_valid"], moe_w_gate[l].astype(BF16),
                  moe_w_up[l].astype(BF16), moe_w_down[l].astype(BF16), tm=tiles.moe)
        x = _combine(ys, meta["pos"], x2, row(norm_final_g), tt=tiles.copy)
    return x
```

```python
import functools
import math
from typing import NamedTuple

import jax
import jax.numpy as jnp
from jax import lax
from jax.experimental import pallas as pl
from jax.experimental.pallas import tpu as pltpu

F32 = jnp.float32
BF16 = jnp.bfloat16

EPS = 1e-6
NEG_INF = -1e30
CHUNK = 64

LANES = 128
SSM_GROUP = 16
SSM_STATE = 64
MLA_HEADS = 8
MLA_NOPE = 64
MLA_ROPE = 32
MLA_QK = MLA_NOPE + MLA_ROPE
MLA_V = 64
ROPE_THETA = 10000.0
XATTN_HEADS = 4
MOE_GROUPS = 4
MOE_PER_GROUP = 8
MOE_EXPERTS = MOE_GROUPS * MOE_PER_GROUP
VMEM_LIMIT = 48 * 1024 * 1024
VMEM_LIMIT_LARGE = 56 * 1024 * 1024
LOG2E = math.log2(math.e)
VT_ROWS = 80
GROUP_LANE = 64


class Tiles(NamedTuple):
    proj: int
    ssm: int
    attn_q: int
    attn_k: int
    attn_heads: int
    attn_strip: int
    post: int
    moe: int
    copy: int


def _tiles(seq):
    return Tiles(proj=min(1024, seq), ssm=min(256, seq), attn_q=min(512, seq), attn_k=min(256, seq),
                 attn_heads=MLA_HEADS, attn_strip=32, post=min(1024, seq),
                 moe=256, copy=min(1024, seq))


def _dot(a, b):
    return jnp.dot(a, b, preferred_element_type=F32)


def _dot_nt(a, b):
    return lax.dot_general(a, b, (((1,), (1,)), ((), ())), preferred_element_type=F32)


def _rms(x, g):
    ms = jnp.mean(x * x, axis=-1, keepdims=True)
    return x * lax.rsqrt(ms + EPS) * g


def _params(*sem, vmem=VMEM_LIMIT):
    return pltpu.CompilerParams(dimension_semantics=sem, vmem_limit_bytes=vmem)


def _full(shape):
    zeros = (0,) * len(shape)
    return pl.BlockSpec(shape, lambda *_: zeros)


def _ssm_prep_kernel(lre_ref, lim_ref, ldt_ref, bre_ref, bim_ref, are_ref, aim_ref, bbre_ref, bbim_ref):
    lre = jnp.minimum(lre_ref[...], -1e-4)
    lim = lim_ref[...]
    dt = jnp.exp(ldt_ref[...])
    mag = jnp.exp(lre * dt)
    are = mag * jnp.cos(lim * dt)
    aim = mag * jnp.sin(lim * dt)
    are_ref[...] = are
    aim_ref[...] = aim
    nre = are - 1.0
    den = lre * lre + lim * lim
    fre = (nre * lre + aim * lim) / den
    fim = (aim * lre - nre * lim) / den
    bre = bre_ref[...]
    bim = bim_ref[...]
    bbre_ref[...] = fre * bre - fim * bim
    bbim_ref[...] = fre * bim + fim * bre


def _ssm_prep(lam_re, lam_im, log_dt, b_re, b_im):
    g, p = lam_re.shape
    hh = b_re.shape[-1]
    bt_re = jnp.transpose(b_re, (0, 2, 1))
    bt_im = jnp.transpose(b_im, (0, 2, 1))
    outs = pl.pallas_call(
        _ssm_prep_kernel,
        out_shape=[jax.ShapeDtypeStruct((g, 1, p), F32)] * 2 + [jax.ShapeDtypeStruct((g, hh, p), F32)] * 2,
        name="ssm_prep",
    )(lam_re.reshape(g, 1, p), lam_im.reshape(g, 1, p), log_dt.reshape(g, 1, 1), bt_re, bt_im)
    a_re, a_im, bb_re, bb_im = outs
    return a_re.reshape(g, p), a_im.reshape(g, p), bb_re, bb_im


def _proj_kernel(x_ref, pos_ref, gmix_ref, wu_ref, wcq_ref, wckv_ref, wkrt_ref, qg_ref, wqt_ref, kvg_ref,
                 wk_ref, wvt_ref, invf_ref, u_ref, q_ref, k_ref, v_ref):
    half = MLA_ROPE // 2
    h = _rms(x_ref[0], gmix_ref[...]).astype(BF16)
    u_ref[0] = _dot(h, wu_ref[...]).astype(BF16)
    cqn = _rms(_dot(h, wcq_ref[...]), qg_ref[...]).astype(BF16)
    ckvn = _rms(_dot(h, wckv_ref[...]), kvg_ref[...]).astype(BF16)

    ang = invf_ref[...] * pos_ref[0].astype(F32)
    cosv = jnp.cos(ang)
    sinv = jnp.sin(ang)

    def rot_t(blk):
        x1 = blk[MLA_NOPE:MLA_NOPE + half]
        x2 = blk[MLA_NOPE + half:MLA_QK]
        return jnp.concatenate([blk[:MLA_NOPE], x1 * cosv - x2 * sinv, x1 * sinv + x2 * cosv, blk[MLA_QK:]], axis=0)

    qt = _dot_nt(wqt_ref[...], cqn) * (MLA_QK ** -0.5 * LOG2E)
    vt = _dot_nt(wvt_ref[...], ckvn)
    kk = _dot(ckvn, wk_ref[...])
    kpe = rot_t(_dot_nt(wkrt_ref[...], h)).T
    ones_row = jnp.where(lax.broadcasted_iota(jnp.int32, (VT_ROWS, 1), 0) == MLA_V, 1.0, 0.0)
    for hh in range(MLA_HEADS):
        q_ref[0, hh] = rot_t(qt[hh * LANES:(hh + 1) * LANES]).astype(BF16)
        k_ref[0, hh] = (kk[:, hh * LANES:(hh + 1) * LANES] + kpe).astype(BF16)
        v_ref[0, hh] = (vt[hh * VT_ROWS:(hh + 1) * VT_ROWS] + ones_row).astype(BF16)


def _proj(x, pos, gmix, wu, wcq, wckv, wkrt, qg, wqt, kvg, wk, wvt, invf, tm):
    b, s, d = x.shape
    grid = (b, s // tm)
    tok = lambda n: pl.BlockSpec((1, tm, n), lambda i, j: (i, j, 0))
    head = pl.BlockSpec((1, MLA_HEADS, tm, LANES), lambda i, j: (i, 0, j, 0))
    head_t = pl.BlockSpec((1, MLA_HEADS, LANES, tm), lambda i, j: (i, 0, 0, j))
    head_vt = pl.BlockSpec((1, MLA_HEADS, VT_ROWS, tm), lambda i, j: (i, 0, 0, j))
    tshape = lambda n: jax.ShapeDtypeStruct((b, MLA_HEADS, n, s), BF16)
    return pl.pallas_call(
        _proj_kernel,
        grid=grid,
        in_specs=[tok(d), pl.BlockSpec((1, 1, tm), lambda i, j: (i, 0, j)), _full(gmix.shape), _full(wu.shape),
                  _full(wcq.shape), _full(wckv.shape), _full(wkrt.shape), _full(qg.shape), _full(wqt.shape),
                  _full(kvg.shape), _full(wk.shape), _full(wvt.shape), _full(invf.shape)],
        out_specs=[tok(wu.shape[1]), head_t, head, head_vt],
        out_shape=[jax.ShapeDtypeStruct((b, s, wu.shape[1]), BF16), tshape(LANES),
                   jax.ShapeDtypeStruct((b, MLA_HEADS, s, LANES), BF16), tshape(VT_ROWS)],
        compiler_params=_params("parallel", "parallel"),
        name="proj",
    )(x, pos, gmix, wu, wcq, wckv, wkrt, qg, wqt, kvg, wk, wvt, invf)


def _ssm_kernel(u_ref, bbd_ref, atab_ref, cmat_ref, dskip_ref, wglu_ref, bglu_ref, gout_ref, y_ref,
                xs_ref, st_ref, *, tt, slab, nb, lane_tiles):
    j = pl.program_id(0)
    half_w = u_ref.shape[-1] // 2

    @pl.when(j == 0)
    def _():
        st_ref[...] = jnp.zeros_like(st_ref)

    for hf in range(2):
        ub = u_ref[:, :, hf * half_w:(hf + 1) * half_w].reshape(nb * tt, half_w)
        bu = _dot(ub, bbd_ref[hf])
        for b in range(nb):
            r0 = (hf * nb + b) * slab
            for c in range(2 * lane_tiles):
                xs_ref[c, r0:r0 + tt, :] = bu[b * tt:(b + 1) * tt, c * LANES:(c + 1) * LANES]

    rows = 2 * nb
    group = 4
    for c0 in range(0, lane_tiles, group):
        cs = list(range(c0, c0 + group))
        a_re = [atab_ref[0, :, c * LANES:(c + 1) * LANES] for c in cs]
        a_im = [atab_ref[1, :, c * LANES:(c + 1) * LANES] for c in cs]
        init = tuple(st_ref[0, :, c * LANES:(c + 1) * LANES] for c in cs) + \
            tuple(st_ref[1, :, c * LANES:(c + 1) * LANES] for c in cs)

        def step(t, carry, cs=cs, a_re=a_re, a_im=a_im):
            new_re, new_im = [], []
            for i, c in enumerate(cs):
                x_re, x_im = carry[i], carry[group + i]
                idx = pl.ds(t, rows, stride=slab)
                n_re = a_re[i] * x_re - a_im[i] * x_im + xs_ref[c, idx, :]
                n_im = a_re[i] * x_im + a_im[i] * x_re + xs_ref[lane_tiles + c, idx, :]
                xs_ref[c, idx, :] = n_re
                xs_ref[lane_tiles + c, idx, :] = n_im
                new_re.append(n_re)
                new_im.append(n_im)
            return tuple(new_re) + tuple(new_im)

        fin = lax.fori_loop(0, tt, step, init, unroll=4)
        for i, c in enumerate(cs):
            st_ref[0, :, c * LANES:(c + 1) * LANES] = fin[i]
            st_ref[1, :, c * LANES:(c + 1) * LANES] = fin[group + i]

    ys = []
    for hf in range(2):
        xb = []
        for b in range(nb):
            r0 = (hf * nb + b) * slab
            xb.append(jnp.concatenate([xs_ref[c, r0:r0 + tt, :] for c in range(2 * lane_tiles)], axis=1))
        xh = jnp.concatenate(xb, axis=0).astype(BF16)
        ys.append(_dot(xh, cmat_ref[hf]))
    y = jnp.concatenate(ys, axis=1)
    u = u_ref[...].astype(F32).reshape(nb * tt, 2 * half_w)
    y = jax.nn.gelu(y + dskip_ref[...] * u)
    y = y * jax.nn.sigmoid(_dot(y.astype(BF16), wglu_ref[...]) + bglu_ref[...])
    y = _rms(y, gout_ref[...])
    y_ref[...] = y.reshape(nb, tt, 2 * half_w).astype(BF16)


def _ssm(u, bbd, atab, cmat, dskip, wglu, bglu, gout, tt):
    nb, s, dssm = u.shape
    lane_tiles = bbd.shape[-1] // (2 * LANES)
    slab = tt + 4
    kern = functools.partial(_ssm_kernel, tt=tt, slab=slab, nb=nb, lane_tiles=lane_tiles)
    return pl.pallas_call(
        kern,
        grid=(s // tt,),
        in_specs=[pl.BlockSpec((nb, tt, dssm), lambda j: (0, j, 0)), _full(bbd.shape), _full(atab.shape),
                  _full(cmat.shape), _full(dskip.shape), _full(wglu.shape), _full(bglu.shape), _full(gout.shape)],
        out_specs=pl.BlockSpec((nb, tt, dssm), lambda j: (0, j, 0)),
        out_shape=jax.ShapeDtypeStruct((nb, s, dssm), BF16),
        scratch_shapes=[pltpu.VMEM((2 * lane_tiles, 2 * nb * slab, LANES), F32),
                        pltpu.VMEM((2, 2 * nb, lane_tiles * LANES), F32)],
        compiler_params=_params("arbitrary"),
        name="ssm",
    )(u, bbd, atab, cmat, dskip, wglu, bglu, gout)


def _attn_kernel(qt_ref, k_ref, vt_ref, o_ref, s_ref, p_ref, m_ref, a_ref, acc_ref, *, tq, tk, hp, strip):
    qi = pl.program_id(2)
    m_ref[...] = jnp.full(m_ref.shape, NEG_INF, F32)
    acc_ref[...] = jnp.zeros(acc_ref.shape, F32)
    q_chunk = lax.broadcasted_iota(jnp.int32, (1, tq), 1) // CHUNK

    def tile(k0, diag_off):
        qs = slice(0 if diag_off is None else diag_off, tq)
        for hh in range(hp):
            s_ref[hh, :, qs] = _dot(k_ref[0, hh, pl.ds(k0, tk), :], qt_ref[0, hh, :, qs])

        def strip_of(hh, r):
            s = s_ref[hh, r:r + strip, qs]
            if diag_off is None:
                return s
            return jnp.where(q_chunk[:, qs] >= (diag_off + r) // CHUNK, s, NEG_INF)

        for hh in range(hp):
            mt = strip_of(hh, 0)
            for r in range(strip, tk, strip):
                mt = jnp.maximum(mt, strip_of(hh, r))
            m_old = m_ref[hh, :, qs]
            m_new = jnp.maximum(m_old, jnp.max(mt, axis=0, keepdims=True))
            a_ref[hh, :, qs] = jnp.exp2(m_old - m_new)
            m_ref[hh, :, qs] = m_new
            for r in range(0, tk, strip):
                p_ref[hh, r:r + strip, qs] = jnp.exp2(strip_of(hh, r) - m_new).astype(BF16)
        for hh in range(hp):
            pv = _dot(vt_ref[0, hh, :, pl.ds(k0, tk)], p_ref[hh, :, qs])
            acc_ref[hh, :, qs] = acc_ref[hh, :, qs] * a_ref[hh, :, qs] + pv

    def body(j, c):
        tile(pl.multiple_of(j * tk, tk), None)
        return c

    lax.fori_loop(0, qi * (tq // tk), body, 0)
    for off in range(0, tq, tk):
        tile(pl.multiple_of(qi * tq + off, tk), off)
    feat = lax.broadcasted_iota(jnp.int32, (VT_ROWS, 1), 0)
    pad = jnp.zeros((LANES - VT_ROWS, tq), F32)
    for hh in range(hp):
        acc = acc_ref[hh]
        out = acc * (1.0 / acc[MLA_V:MLA_V + 1, :])
        out = jnp.where(feat < MLA_V, out, 0.0)
        o_ref[0, hh] = jnp.concatenate([out, pad], axis=0).T.astype(BF16)


def _attn(qt, k, vt, tq, tk, hp, strip):
    b, h, s, _ = k.shape
    assert strip <= CHUNK and CHUNK % strip == 0 and tk % CHUNK == 0 and tq % tk == 0
    return pl.pallas_call(
        functools.partial(_attn_kernel, tq=tq, tk=tk, hp=hp, strip=strip),
        grid=(b, h // hp, s // tq),
        in_specs=[pl.BlockSpec((1, hp, LANES, tq), lambda i, j, t: (i, j, 0, t)),
                  pl.BlockSpec((1, hp, s, LANES), lambda i, j, t: (i, j, 0, 0)),
                  pl.BlockSpec((1, hp, VT_ROWS, s), lambda i, j, t: (i, j, 0, 0))],
        out_specs=pl.BlockSpec((1, hp, tq, LANES), lambda i, j, t: (i, j, t, 0)),
        out_shape=jax.ShapeDtypeStruct((b, h, s, LANES), BF16),
        scratch_shapes=[pltpu.VMEM((hp, tk, tq), F32), pltpu.VMEM((hp, tk, tq), BF16),
                        pltpu.VMEM((hp, 1, tq), F32), pltpu.VMEM((hp, 1, tq), F32),
                        pltpu.VMEM((hp, VT_ROWS, tq), F32)],
        compiler_params=_params("parallel", "parallel", "arbitrary", vmem=VMEM_LIMIT_LARGE),
        name="attn",
    )(qt, k, vt)


def _memkv_kernel(mem_ref, g_ref, wk_ref, wv_ref, k_ref, v_ref, *, hd):
    mn = _rms(mem_ref[0], g_ref[...]).astype(BF16)
    kk = _dot(mn, wk_ref[...])
    vv = _dot(mn, wv_ref[...])
    for hh in range(XATTN_HEADS):
        k_ref[0, hh] = kk[:, hh * hd:(hh + 1) * hd].astype(BF16)
        v_ref[0, hh] = vv[:, hh * hd:(hh + 1) * hd].T.astype(BF16)


def _memkv(mem, g, wk, wv):
    b, nm, d = mem.shape
    hd = d // XATTN_HEADS
    ospec = pl.BlockSpec((1, XATTN_HEADS, nm, hd), lambda i: (i, 0, 0, 0))
    return pl.pallas_call(
        functools.partial(_memkv_kernel, hd=hd),
        grid=(b,),
        in_specs=[pl.BlockSpec((1, nm, d), lambda i: (i, 0, 0)), _full(g.shape), _full(wk.shape), _full(wv.shape)],
        out_specs=[ospec, ospec],
        out_shape=[jax.ShapeDtypeStruct((b, XATTN_HEADS, nm, hd), BF16)] * 2,
        compiler_params=_params("parallel"),
        name="mem_kv",
    )(mem, g, wk, wv)


def _rms_t(xt, g_col):
    ms = jnp.mean(xt * xt, axis=0, keepdims=True)
    return xt * lax.rsqrt(ms + EPS) * g_col


def _post_kernel(x_ref, ys_ref, ym_ref, gm_ref, woutt_ref, gx_ref, wqt_ref, km_ref, vmt_ref, wot_ref,
                 gmoe_ref, wrht_ref, wrlt_ref, br_ref, x2_ref, route_ref, *, hd):
    heads = [ym_ref[0, hh].astype(F32) for hh in range(MLA_HEADS)]
    ym = jnp.concatenate([heads[k] + pltpu.roll(heads[k + 1], MLA_V, 1) for k in range(0, MLA_HEADS, 2)], axis=1)
    ymn = _rms(ym, gm_ref[...]).astype(BF16)
    y_mix = jnp.concatenate([ys_ref[0], ymn], axis=1)
    x1 = x_ref[0].T + _dot_nt(woutt_ref[...], y_mix)

    h2 = _rms_t(x1, gx_ref[...]).astype(BF16)
    qx = (_dot(wqt_ref[...], h2) * (hd ** -0.5)).astype(BF16)
    outs = []
    for hh in range(XATTN_HEADS):
        s = _dot(km_ref[0, hh], qx[hh * hd:(hh + 1) * hd])
        p = jnp.exp(s - jnp.max(s, axis=0, keepdims=True))
        inv = 1.0 / jnp.sum(p, axis=0, keepdims=True)
        outs.append(_dot(vmt_ref[0, hh], p.astype(BF16)) * inv)
    o = jnp.concatenate(outs, axis=0).astype(BF16)
    x2 = x1 + _dot(wot_ref[...], o)
    x2_ref[0] = x2.T

    h3 = _rms_t(x2, gmoe_ref[...])
    h_hi = h3.astype(BF16)
    h_lo = (h3 - h_hi.astype(F32)).astype(BF16)
    logits = (_dot(wrht_ref[...], h_hi) + _dot(wrlt_ref[...], h_hi) + _dot(wrht_ref[...], h_lo)) + br_ref[...]
    idx = lax.broadcasted_iota(jnp.int32, logits.shape, 0)
    is_g = (idx >= MOE_EXPERTS) & (idx < MOE_EXPERTS + MOE_GROUPS)
    gl = jnp.where(is_g, logits, NEG_INF)
    gmax = jnp.max(gl, axis=0, keepdims=True)
    g_w = 1.0 / jnp.sum(jnp.exp(gl - gmax), axis=0, keepdims=True)
    g_idx = jnp.min(jnp.where(gl == gmax, idx, 4 * LANES), axis=0, keepdims=True) - MOE_EXPERTS
    in_grp = (idx >= g_idx * MOE_PER_GROUP) & (idx < (g_idx + 1) * MOE_PER_GROUP)
    el = jnp.where(in_grp, logits, NEG_INF)
    v1 = jnp.max(el, axis=0, keepdims=True)
    i1 = jnp.min(jnp.where(el == v1, idx, 4 * LANES), axis=0, keepdims=True)
    el2 = jnp.where(idx == i1, NEG_INF, el)
    v2 = jnp.max(el2, axis=0, keepdims=True)
    i2 = jnp.min(jnp.where(el2 == v2, idx, 4 * LANES), axis=0, keepdims=True)
    e2 = jnp.exp(v2 - v1)
    w1 = g_w / (1.0 + e2)
    w2 = g_w * e2 / (1.0 + e2)
    route = (jnp.where(idx == i1, w1, 0.0) + jnp.where(idx == i2, w2, 0.0)
             + jnp.where(idx == GROUP_LANE, g_idx.astype(F32), 0.0))
    route_ref[0] = route.T


def _post(x, ys, ym, gm, woutt, gx, wqt, km, vmt, wot, gmoe, wrht, wrlt, br, tm):
    b, s, d = x.shape
    hd = d // XATTN_HEADS
    tok = lambda n: pl.BlockSpec((1, tm, n), lambda i, j: (i, j, 0))
    mem = pl.BlockSpec((1,) + km.shape[1:], lambda i, j: (i, 0, 0, 0))
    return pl.pallas_call(
        functools.partial(_post_kernel, hd=hd),
        grid=(b, s // tm),
        in_specs=[tok(d), tok(ys.shape[-1]), pl.BlockSpec((1, MLA_HEADS, tm, LANES), lambda i, j: (i, 0, j, 0)),
                  _full(gm.shape), _full(woutt.shape), _full(gx.shape), _full(wqt.shape), mem, mem,
                  _full(wot.shape), _full(gmoe.shape), _full(wrht.shape), _full(wrlt.shape), _full(br.shape)],
        out_specs=[tok(d), tok(LANES)],
        out_shape=[jax.ShapeDtypeStruct((b, s, d), F32), jax.ShapeDtypeStruct((b, s, LANES), F32)],
        compiler_params=_params("parallel", "parallel", vmem=VMEM_LIMIT_LARGE),
        name="post",
    )(x, ys, ym, gm, woutt, gx, wqt, km, vmt, wot, gmoe, wrht, wrlt, br)


def _route_metadata(route, tm):
    b, s, _ = route.shape
    n_tiles = (s + MOE_GROUPS * tm) // tm
    gid = route[..., GROUP_LANE].astype(jnp.int32)
    groups = jnp.arange(MOE_GROUPS, dtype=jnp.int32)[None, :, None]
    onehot = (gid[:, None, :] == groups).astype(jnp.int32)
    csum = jnp.cumsum(onehot, axis=2)
    cnt = csum[:, :, -1]
    padded = (cnt + tm - 1) // tm * tm
    seg_end = jnp.cumsum(padded, axis=-1)
    dest = jnp.sum(onehot * ((seg_end - padded)[:, :, None] + csum - 1), axis=1)
    tile_start = jnp.arange(n_tiles, dtype=jnp.int32) * tm
    tile_g = jnp.sum((seg_end[:, None, :] <= tile_start[None, :, None]).astype(jnp.int32), axis=-1)
    tile_g = jnp.minimum(tile_g, MOE_GROUPS - 1)
    n_valid = (seg_end[:, -1] // tm).astype(jnp.int32)
    return dict(pos=dest, tile_g=tile_g.reshape(b * n_tiles), n_valid=n_valid, n_pad=n_tiles * tm)


def _dispatch_kernel(h_ref, g_ref, dst_ref, xs_ref, gs_ref, *, tt):
    @pl.when(pl.program_id(1) == 0)
    def _():
        xs_ref[...] = jnp.zeros(xs_ref.shape, xs_ref.dtype)
        gs_ref[...] = jnp.zeros(gs_ref.shape, gs_ref.dtype)

    for t in range(tt):
        r = dst_ref[0, 0, t]
        xs_ref[0, pl.ds(r, 1), :] = h_ref[0, t:t + 1, :]
        gs_ref[0, pl.ds(r, 1), :] = g_ref[0, t:t + 1, :]


def _dispatch(x2, route, pos, n_pad, tt):
    b, s, d = x2.shape
    nj = s // tt
    row_block = lambda n: pl.BlockSpec((1, n_pad, n), lambda i, j: (i, 0, 0), pipeline_mode=pl.Buffered(1))
    return pl.pallas_call(
        functools.partial(_dispatch_kernel, tt=tt),
        grid=(b, nj),
        in_specs=[pl.BlockSpec((1, tt, d), lambda i, j: (i, j, 0)),
                  pl.BlockSpec((1, tt, LANES), lambda i, j: (i, j, 0)),
                  pl.BlockSpec((1, 1, tt), lambda i, j: (i * nj + j, 0, 0), memory_space=pltpu.SMEM)],
        out_specs=[row_block(d), row_block(LANES)],
        out_shape=[jax.ShapeDtypeStruct((b, n_pad, d), F32), jax.ShapeDtypeStruct((b, n_pad, LANES), F32)],
        compiler_params=_params("parallel", "arbitrary", vmem=VMEM_LIMIT_LARGE),
        name="dispatch",
    )(x2, route, pos.reshape(b * nj, 1, tt))


def _moe_kernel(tg_ref, nv_ref, x_ref, gate_ref, gmoe_ref, wg_ref, wu_ref, wd_ref, o_ref, *, nt):
    bi = pl.program_id(0)
    ti = pl.program_id(1)
    grp = tg_ref[bi * nt + ti]

    @pl.when(ti < nv_ref[bi])
    def _():
        h = _rms(x_ref[0], gmoe_ref[...]).astype(BF16)
        gate = gate_ref[0]
        lane = lax.broadcasted_iota(jnp.int32, gate.shape, 1)
        acts = []
        for e in range(MOE_PER_GROUP):
            a = jax.nn.silu(_dot(h, wg_ref[e])) * _dot(h, wu_ref[e])
            gcol = jnp.sum(jnp.where(lane == grp * MOE_PER_GROUP + e, gate, 0.0), axis=-1, keepdims=True)
            acts.append((a * gcol).astype(BF16))
        wd = wd_ref[...]
        o_ref[0] = _dot(jnp.concatenate(acts, axis=1), wd.reshape(wd.shape[0] * wd.shape[1], wd.shape[2]))

    @pl.when(ti >= nv_ref[bi])
    def _():
        o_ref[0] = jnp.zeros(o_ref.shape[1:], o_ref.dtype)


def _moe(xs, gs, gmoe, tile_g, n_valid, wg, wu, wd, tm):
    b, n_pad, d = xs.shape
    nt = n_pad // tm
    ff = wg.shape[-1]
    group = lambda i, j, tg, nv: (tg[i * nt + j], 0, 0)
    tok = lambda n: pl.BlockSpec((1, tm, n), lambda i, j, tg, nv: (i, j, 0))
    grid_spec = pltpu.PrefetchScalarGridSpec(
        num_scalar_prefetch=2,
        grid=(b, nt),
        in_specs=[tok(d), tok(LANES), pl.BlockSpec(gmoe.shape, lambda i, j, tg, nv: (0, 0)),
                  pl.BlockSpec((MOE_PER_GROUP, d, ff), group),
                  pl.BlockSpec((MOE_PER_GROUP, d, ff), group), pl.BlockSpec((MOE_PER_GROUP, ff, d), group)],
        out_specs=tok(d),
    )
    return pl.pallas_call(
        functools.partial(_moe_kernel, nt=nt),
        grid_spec=grid_spec,
        out_shape=jax.ShapeDtypeStruct((b, n_pad, d), F32),
        compiler_params=_params("parallel", "arbitrary"),
        name="moe",
    )(tile_g, n_valid, xs, gs, gmoe, wg, wu, wd)


def _combine_kernel(ys_ref, pos_ref, x2_ref, gf_ref, o_ref, buf_ref, *, tt):
    for t in range(tt):
        buf_ref[t:t + 1, :] = ys_ref[0, pl.ds(pos_ref[0, 0, t], 1), :]
    o_ref[0] = _rms(x2_ref[0] + buf_ref[...], gf_ref[...])


def _combine(ys, pos, x2, gf, tt):
    b, s, d = x2.shape
    n_pad = ys.shape[1]
    nj = s // tt
    return pl.pallas_call(
        functools.partial(_combine_kernel, tt=tt),
        grid=(b, nj),
        in_specs=[pl.BlockSpec((1, n_pad, d), lambda i, j: (i, 0, 0), pipeline_mode=pl.Buffered(1)),
                  pl.BlockSpec((1, 1, tt), lambda i, j: (i * nj + j, 0, 0), memory_space=pltpu.SMEM),
                  pl.BlockSpec((1, tt, d), lambda i, j: (i, j, 0)), _full(gf.shape)],
        out_specs=pl.BlockSpec((1, tt, d), lambda i, j: (i, j, 0)),
        out_shape=jax.ShapeDtypeStruct((b, s, d), F32),
        scratch_shapes=[pltpu.VMEM((tt, d), F32)],
        compiler_params=_params("parallel", "arbitrary", vmem=VMEM_LIMIT_LARGE),
        name="combine",
    )(ys, pos.reshape(b * nj, 1, tt), x2, gf)


def _pad_heads(w, per_head, offset=0):
    k = w.shape[0]
    w = w.reshape(k, MLA_HEADS, per_head)
    w = jnp.pad(w, ((0, 0), (0, 0), (offset, LANES - per_head - offset)))
    return w.reshape(k, MLA_HEADS * LANES)


def _block_diag(blocks):
    n, r, c = blocks.shape
    eye = jnp.eye(n, dtype=blocks.dtype)
    return (eye[:, None, :, None] * blocks[:, :, None, :]).reshape(n * r, n * c)


def kernel(x, mem, positions, norm_mix_g, w_in, ssm_lam_re, ssm_lam_im, ssm_log_dt, ssm_b_re, ssm_b_im, ssm_c_re, ssm_c_im, ssm_d, ssm_w_glu, ssm_b_glu, mla_q_norm_g, mla_w_q_up, mla_kv_norm_g, mla_w_kv_up, out_norm_ssm_g, out_norm_mla_g, w_out, norm_xattn_g, norm_mem_g, xattn_w_q, xattn_w_k, xattn_w_v, xattn_w_o, norm_moe_g, moe_w_group, moe_b_group, moe_w_expert, moe_b_expert, moe_w_gate, moe_w_up, moe_w_down, norm_final_g):
    bsz, seq, d = x.shape
    depth = w_in.shape[0]
    assert depth == 1, "the final RMSNorm is fused into the last stage of a single layer"
    d_ssm = ssm_d.shape[-1]
    q_rank = mla_q_norm_g.shape[-1]
    kv_rank = mla_kv_norm_g.shape[-1]
    n_grp = d_ssm // SSM_GROUP
    s1, s2, s3 = d_ssm, d_ssm + q_rank, d_ssm + q_rank + kv_rank
    row = lambda v: v.reshape(1, -1).astype(F32)
    tiles = _tiles(seq)

    half = MLA_ROPE // 2
    invf = (ROPE_THETA ** (-jnp.arange(half, dtype=F32) / half)).reshape(half, 1)
    pos = positions.reshape(bsz, 1, seq)

    for l in range(depth):
        a_re, a_im, bb_re, bb_im = _ssm_prep(ssm_lam_re[l], ssm_lam_im[l], ssm_log_dt[l], ssm_b_re[l], ssm_b_im[l])
        gh = n_grp // 2
        bbd = jnp.stack([
            jnp.concatenate([_block_diag(bb_re[hf * gh:(hf + 1) * gh]), _block_diag(bb_im[hf * gh:(hf + 1) * gh])],
                            axis=1) for hf in range(2)]).astype(BF16)
        c_re_t = jnp.transpose(ssm_c_re[l], (0, 2, 1))
        c_im_t = jnp.transpose(ssm_c_im[l], (0, 2, 1))
        cmat = jnp.stack([
            jnp.concatenate([_block_diag(c_re_t[hf * gh:(hf + 1) * gh]), -_block_diag(c_im_t[hf * gh:(hf + 1) * gh])],
                            axis=0) for hf in range(2)]).astype(BF16)
        atab = jnp.stack([
            jnp.repeat(arr.reshape(2, 1, gh * SSM_STATE), bsz, axis=1).reshape(2 * bsz, gh * SSM_STATE)
            for arr in (a_re, a_im)])

        wi = w_in[l]
        wu = wi[:, :s1].astype(BF16)
        wcq = wi[:, s1:s2].astype(BF16)
        wckv = wi[:, s2:s3].astype(BF16)
        wkrt = jnp.pad(wi[:, s3:], ((0, 0), (MLA_NOPE, LANES - MLA_QK))).T.astype(BF16)
        wqt = _pad_heads(mla_w_q_up[l], MLA_QK).T.astype(BF16)
        wkv = mla_w_kv_up[l].reshape(kv_rank, MLA_HEADS, MLA_NOPE + MLA_V)
        wk = _pad_heads(wkv[:, :, :MLA_NOPE].reshape(kv_rank, -1), MLA_NOPE).astype(BF16)
        wvt = jnp.pad(wkv[:, :, MLA_NOPE:], ((0, 0), (0, 0), (0, VT_ROWS - MLA_V)))
        wvt = wvt.reshape(kv_rank, MLA_HEADS * VT_ROWS).T.astype(BF16)
        u, q, k, v = _proj(x, pos, row(norm_mix_g[l]), wu, wcq, wckv, wkrt, row(mla_q_norm_g[l]), wqt,
                           row(mla_kv_norm_g[l]), wk, wvt, invf, tm=tiles.proj)

        y_ssm = _ssm(u, bbd, atab, cmat, row(ssm_d[l]), ssm_w_glu[l].astype(BF16), row(ssm_b_glu[l]),
                     row(out_norm_ssm_g[l]), tt=tiles.ssm)
        y_mla = _attn(q, k, v, tq=tiles.attn_q, tk=tiles.attn_k, hp=tiles.attn_heads, strip=tiles.attn_strip)

        km, vm = _memkv(mem, row(norm_mem_g[l]), xattn_w_k[l].astype(BF16), xattn_w_v[l].astype(BF16))
        col = lambda v: v.reshape(-1, 1).astype(F32)
        woutt = w_out[l].T.astype(BF16)
        wr = jnp.concatenate([moe_w_expert[l].reshape(d, MOE_EXPERTS), moe_w_group[l]], axis=1)
        wrt = jnp.pad(wr, ((0, 0), (0, LANES - MOE_EXPERTS - MOE_GROUPS))).T.astype(F32)
        wrt_hi = wrt.astype(BF16)
        wrt_lo = (wrt - wrt_hi.astype(F32)).astype(BF16)
        br = jnp.concatenate([moe_b_expert[l].reshape(-1), moe_b_group[l]])
        br = jnp.pad(br, (0, LANES - MOE_EXPERTS - MOE_GROUPS)).reshape(LANES, 1).astype(F32)
        x2, route = _post(x, y_ssm, y_mla, row(out_norm_mla_g[l]), woutt, col(norm_xattn_g[l]),
                          xattn_w_q[l].T.astype(BF16), km, vm, xattn_w_o[l].T.astype(BF16), col(norm_moe_g[l]),
                          wrt_hi, wrt_lo, br, tm=tiles.post)

        meta = _route_metadata(route, tiles.moe)
        xs, gs = _dispatch(x2, route, meta["pos"], meta["n_pad"], tt=tiles.copy)
        ys = _moe(xs, gs, row(norm_moe_g[l]), meta["tile_g"], meta["n_valid"], moe_w_gate[l].astype(BF16),
                  moe_w_up[l].astype(BF16), moe_w_down[l].astype(BF16), tm=tiles.moe)
        x = _combine(ys, meta["pos"], x2, row(norm_final_g), tt=tiles.copy)
    return x
```

```python
import functools
import math
from typing import NamedTuple

import jax
import jax.numpy as jnp
from jax import lax
from jax.experimental import pallas as pl
from jax.experimental.pallas import tpu as pltpu

F32 = jnp.float32
BF16 = jnp.bfloat16

EPS = 1e-6
NEG_INF = -1e30
CHUNK = 64

LANES = 128
SSM_GROUP = 16
SSM_STATE = 64
MLA_HEADS = 8
MLA_NOPE = 64
MLA_ROPE = 32
MLA_QK = MLA_NOPE + MLA_ROPE
MLA_V = 64
ROPE_THETA = 10000.0
XATTN_HEADS = 4
MOE_GROUPS = 4
MOE_PER_GROUP = 8
MOE_EXPERTS = MOE_GROUPS * MOE_PER_GROUP
VMEM_LIMIT = 48 * 1024 * 1024
VMEM_LIMIT_LARGE = 56 * 1024 * 1024
LOG2E = math.log2(math.e)
VT_ROWS = 80
GROUP_LANE = 64


class Tiles(NamedTuple):
    proj: int
    ssm: int
    attn_q: int
    attn_k: int
    attn_heads: int
    attn_strip: int
    post: int
    moe: int
    moe_block: int
    copy: int


def _tiles(seq):
    return Tiles(proj=min(1024, seq), ssm=min(256, seq), attn_q=min(512, seq), attn_k=min(256, seq),
                 attn_heads=MLA_HEADS, attn_strip=32, post=min(1024, seq),
                 moe=512, moe_block=256, copy=min(1024, seq))


def _dot(a, b):
    return jnp.dot(a, b, preferred_element_type=F32)


def _dot_nt(a, b):
    return lax.dot_general(a, b, (((1,), (1,)), ((), ())), preferred_element_type=F32)


def _rms(x, g):
    ms = jnp.mean(x * x, axis=-1, keepdims=True)
    return x * lax.rsqrt(ms + EPS) * g


def _params(*sem, vmem=VMEM_LIMIT):
    return pltpu.CompilerParams(dimension_semantics=sem, vmem_limit_bytes=vmem)


def _full(shape):
    zeros = (0,) * len(shape)
    return pl.BlockSpec(shape, lambda *_: zeros)


def _ssm_prep_kernel(lre_ref, lim_ref, ldt_ref, bre_ref, bim_ref, are_ref, aim_ref, bbre_ref, bbim_ref):
    lre = jnp.minimum(lre_ref[...], -1e-4)
    lim = lim_ref[...]
    dt = jnp.exp(ldt_ref[...])
    mag = jnp.exp(lre * dt)
    are = mag * jnp.cos(lim * dt)
    aim = mag * jnp.sin(lim * dt)
    are_ref[...] = are
    aim_ref[...] = aim
    nre = are - 1.0
    den = lre * lre + lim * lim
    fre = (nre * lre + aim * lim) / den
    fim = (aim * lre - nre * lim) / den
    bre = bre_ref[...]
    bim = bim_ref[...]
    bbre_ref[...] = fre * bre - fim * bim
    bbim_ref[...] = fre * bim + fim * bre


def _ssm_prep(lam_re, lam_im, log_dt, b_re, b_im):
    g, p = lam_re.shape
    hh = b_re.shape[-1]
    bt_re = jnp.transpose(b_re, (0, 2, 1))
    bt_im = jnp.transpose(b_im, (0, 2, 1))
    outs = pl.pallas_call(
        _ssm_prep_kernel,
        out_shape=[jax.ShapeDtypeStruct((g, 1, p), F32)] * 2 + [jax.ShapeDtypeStruct((g, hh, p), F32)] * 2,
        name="ssm_prep",
    )(lam_re.reshape(g, 1, p), lam_im.reshape(g, 1, p), log_dt.reshape(g, 1, 1), bt_re, bt_im)
    a_re, a_im, bb_re, bb_im = outs
    return a_re.reshape(g, p), a_im.reshape(g, p), bb_re, bb_im


def _proj_kernel(x_ref, pos_ref, gmix_ref, wu_ref, wcq_ref, wckv_ref, wkrt_ref, qg_ref, wqt_ref, kvg_ref,
                 wk_ref, wvt_ref, invf_ref, u_ref, q_ref, k_ref, v_ref):
    half = MLA_ROPE // 2
    h = _rms(x_ref[0], gmix_ref[...]).astype(BF16)
    u_ref[0] = _dot(h, wu_ref[...]).astype(BF16)
    cqn = _rms(_dot(h, wcq_ref[...]), qg_ref[...]).astype(BF16)
    ckvn = _rms(_dot(h, wckv_ref[...]), kvg_ref[...]).astype(BF16)

    ang = invf_ref[...] * pos_ref[0].astype(F32)
    cosv = jnp.cos(ang)
    sinv = jnp.sin(ang)

    def rot_t(blk):
        x1 = blk[MLA_NOPE:MLA_NOPE + half]
        x2 = blk[MLA_NOPE + half:MLA_QK]
        return jnp.concatenate([blk[:MLA_NOPE], x1 * cosv - x2 * sinv, x1 * sinv + x2 * cosv, blk[MLA_QK:]], axis=0)

    qt = _dot_nt(wqt_ref[...], cqn) * (MLA_QK ** -0.5 * LOG2E)
    vt = _dot_nt(wvt_ref[...], ckvn)
    kk = _dot(ckvn, wk_ref[...])
    kpe = rot_t(_dot_nt(wkrt_ref[...], h)).T
    ones_row = jnp.where(lax.broadcasted_iota(jnp.int32, (VT_ROWS, 1), 0) == MLA_V, 1.0, 0.0)
    for hh in range(MLA_HEADS):
        q_ref[0, hh] = rot_t(qt[hh * LANES:(hh + 1) * LANES]).astype(BF16)
        k_ref[0, hh] = (kk[:, hh * LANES:(hh + 1) * LANES] + kpe).astype(BF16)
        v_ref[0, hh] = (vt[hh * VT_ROWS:(hh + 1) * VT_ROWS] + ones_row).astype(BF16)


def _proj(x, pos, gmix, wu, wcq, wckv, wkrt, qg, wqt, kvg, wk, wvt, invf, tm):
    b, s, d = x.shape
    grid = (b, s // tm)
    tok = lambda n: pl.BlockSpec((1, tm, n), lambda i, j: (i, j, 0))
    head = pl.BlockSpec((1, MLA_HEADS, tm, LANES), lambda i, j: (i, 0, j, 0))
    head_t = pl.BlockSpec((1, MLA_HEADS, LANES, tm), lambda i, j: (i, 0, 0, j))
    head_vt = pl.BlockSpec((1, MLA_HEADS, VT_ROWS, tm), lambda i, j: (i, 0, 0, j))
    tshape = lambda n: jax.ShapeDtypeStruct((b, MLA_HEADS, n, s), BF16)
    return pl.pallas_call(
        _proj_kernel,
        grid=grid,
        in_specs=[tok(d), pl.BlockSpec((1, 1, tm), lambda i, j: (i, 0, j)), _full(gmix.shape), _full(wu.shape),
                  _full(wcq.shape), _full(wckv.shape), _full(wkrt.shape), _full(qg.shape), _full(wqt.shape),
                  _full(kvg.shape), _full(wk.shape), _full(wvt.shape), _full(invf.shape)],
        out_specs=[tok(wu.shape[1]), head_t, head, head_vt],
        out_shape=[jax.ShapeDtypeStruct((b, s, wu.shape[1]), BF16), tshape(LANES),
                   jax.ShapeDtypeStruct((b, MLA_HEADS, s, LANES), BF16), tshape(VT_ROWS)],
        compiler_params=_params("parallel", "parallel"),
        name="proj",
    )(x, pos, gmix, wu, wcq, wckv, wkrt, qg, wqt, kvg, wk, wvt, invf)


def _ssm_kernel(u_ref, bbd_ref, atab_ref, cmat_ref, dskip_ref, wglu_ref, bglu_ref, gout_ref, y_ref,
                xs_ref, st_ref, *, tt, slab, nb, lane_tiles):
    j = pl.program_id(0)
    half_w = u_ref.shape[-1] // 2

    @pl.when(j == 0)
    def _():
        st_ref[...] = jnp.zeros_like(st_ref)

    for hf in range(2):
        ub = u_ref[:, :, hf * half_w:(hf + 1) * half_w].reshape(nb * tt, half_w)
        bu = _dot(ub, bbd_ref[hf])
        for b in range(nb):
            r0 = (hf * nb + b) * slab
            for c in range(2 * lane_tiles):
                xs_ref[c, r0:r0 + tt, :] = bu[b * tt:(b + 1) * tt, c * LANES:(c + 1) * LANES]

    rows = 2 * nb
    group = 4
    for c0 in range(0, lane_tiles, group):
        cs = list(range(c0, c0 + group))
        a_re = [atab_ref[0, :, c * LANES:(c + 1) * LANES] for c in cs]
        a_im = [atab_ref[1, :, c * LANES:(c + 1) * LANES] for c in cs]
        init = tuple(st_ref[0, :, c * LANES:(c + 1) * LANES] for c in cs) + \
            tuple(st_ref[1, :, c * LANES:(c + 1) * LANES] for c in cs)

        def step(t, carry, cs=cs, a_re=a_re, a_im=a_im):
            new_re, new_im = [], []
            for i, c in enumerate(cs):
                x_re, x_im = carry[i], carry[group + i]
                idx = pl.ds(t, rows, stride=slab)
                n_re = a_re[i] * x_re - a_im[i] * x_im + xs_ref[c, idx, :]
                n_im = a_re[i] * x_im + a_im[i] * x_re + xs_ref[lane_tiles + c, idx, :]
                xs_ref[c, idx, :] = n_re
                xs_ref[lane_tiles + c, idx, :] = n_im
                new_re.append(n_re)
                new_im.append(n_im)
            return tuple(new_re) + tuple(new_im)

        fin = lax.fori_loop(0, tt, step, init, unroll=4)
        for i, c in enumerate(cs):
            st_ref[0, :, c * LANES:(c + 1) * LANES] = fin[i]
            st_ref[1, :, c * LANES:(c + 1) * LANES] = fin[group + i]

    ys = []
    for hf in range(2):
        xb = []
        for b in range(nb):
            r0 = (hf * nb + b) * slab
            xb.append(jnp.concatenate([xs_ref[c, r0:r0 + tt, :] for c in range(2 * lane_tiles)], axis=1))
        xh = jnp.concatenate(xb, axis=0).astype(BF16)
        ys.append(_dot(xh, cmat_ref[hf]))
    y = jnp.concatenate(ys, axis=1)
    u = u_ref[...].astype(F32).reshape(nb * tt, 2 * half_w)
    y = jax.nn.gelu(y + dskip_ref[...] * u)
    y = y * jax.nn.sigmoid(_dot(y.astype(BF16), wglu_ref[...]) + bglu_ref[...])
    y = _rms(y, gout_ref[...])
    y_ref[...] = y.reshape(nb, tt, 2 * half_w).astype(BF16)


def _ssm(u, bbd, atab, cmat, dskip, wglu, bglu, gout, tt):
    nb, s, dssm = u.shape
    lane_tiles = bbd.shape[-1] // (2 * LANES)
    slab = tt + 4
    kern = functools.partial(_ssm_kernel, tt=tt, slab=slab, nb=nb, lane_tiles=lane_tiles)
    return pl.pallas_call(
        kern,
        grid=(s // tt,),
        in_specs=[pl.BlockSpec((nb, tt, dssm), lambda j: (0, j, 0)), _full(bbd.shape), _full(atab.shape),
                  _full(cmat.shape), _full(dskip.shape), _full(wglu.shape), _full(bglu.shape), _full(gout.shape)],
        out_specs=pl.BlockSpec((nb, tt, dssm), lambda j: (0, j, 0)),
        out_shape=jax.ShapeDtypeStruct((nb, s, dssm), BF16),
        scratch_shapes=[pltpu.VMEM((2 * lane_tiles, 2 * nb * slab, LANES), F32),
                        pltpu.VMEM((2, 2 * nb, lane_tiles * LANES), F32)],
        compiler_params=_params("arbitrary"),
        name="ssm",
    )(u, bbd, atab, cmat, dskip, wglu, bglu, gout)


def _attn_kernel(qt_ref, k_ref, vt_ref, o_ref, s_ref, p_ref, m_ref, a_ref, acc_ref, *, tq, tk, hp, strip):
    qi = pl.program_id(2)
    m_ref[...] = jnp.full(m_ref.shape, NEG_INF, F32)
    acc_ref[...] = jnp.zeros(acc_ref.shape, F32)
    q_chunk = lax.broadcasted_iota(jnp.int32, (1, tq), 1) // CHUNK

    def tile(k0, diag_off):
        qs = slice(0 if diag_off is None else diag_off, tq)
        for hh in range(hp):
            s_ref[hh, :, qs] = _dot(k_ref[0, hh, pl.ds(k0, tk), :], qt_ref[0, hh, :, qs])

        def strip_of(hh, r):
            s = s_ref[hh, r:r + strip, qs]
            if diag_off is None:
                return s
            return jnp.where(q_chunk[:, qs] >= (diag_off + r) // CHUNK, s, NEG_INF)

        for hh in range(hp):
            mt = strip_of(hh, 0)
            for r in range(strip, tk, strip):
                mt = jnp.maximum(mt, strip_of(hh, r))
            m_old = m_ref[hh, :, qs]
            m_new = jnp.maximum(m_old, jnp.max(mt, axis=0, keepdims=True))
            a_ref[hh, :, qs] = jnp.exp2(m_old - m_new)
            m_ref[hh, :, qs] = m_new
            for r in range(0, tk, strip):
                p_ref[hh, r:r + strip, qs] = jnp.exp2(strip_of(hh, r) - m_new).astype(BF16)
        for hh in range(hp):
            pv = _dot(vt_ref[0, hh, :, pl.ds(k0, tk)], p_ref[hh, :, qs])
            acc_ref[hh, :, qs] = acc_ref[hh, :, qs] * a_ref[hh, :, qs] + pv

    def body(j, c):
        tile(pl.multiple_of(j * tk, tk), None)
        return c

    lax.fori_loop(0, qi * (tq // tk), body, 0)
    for off in range(0, tq, tk):
        tile(pl.multiple_of(qi * tq + off, tk), off)
    feat = lax.broadcasted_iota(jnp.int32, (VT_ROWS, 1), 0)
    pad = jnp.zeros((LANES - VT_ROWS, tq), F32)
    for hh in range(hp):
        acc = acc_ref[hh]
        out = acc * (1.0 / acc[MLA_V:MLA_V + 1, :])
        out = jnp.where(feat < MLA_V, out, 0.0)
        o_ref[0, hh] = jnp.concatenate([out, pad], axis=0).T.astype(BF16)


def _attn(qt, k, vt, tq, tk, hp, strip):
    b, h, s, _ = k.shape
    assert strip <= CHUNK and CHUNK % strip == 0 and tk % CHUNK == 0 and tq % tk == 0
    return pl.pallas_call(
        functools.partial(_attn_kernel, tq=tq, tk=tk, hp=hp, strip=strip),
        grid=(b, h // hp, s // tq),
        in_specs=[pl.BlockSpec((1, hp, LANES, tq), lambda i, j, t: (i, j, 0, t)),
                  pl.BlockSpec((1, hp, s, LANES), lambda i, j, t: (i, j, 0, 0)),
                  pl.BlockSpec((1, hp, VT_ROWS, s), lambda i, j, t: (i, j, 0, 0))],
        out_specs=pl.BlockSpec((1, hp, tq, LANES), lambda i, j, t: (i, j, t, 0)),
        out_shape=jax.ShapeDtypeStruct((b, h, s, LANES), BF16),
        scratch_shapes=[pltpu.VMEM((hp, tk, tq), F32), pltpu.VMEM((hp, tk, tq), BF16),
                        pltpu.VMEM((hp, 1, tq), F32), pltpu.VMEM((hp, 1, tq), F32),
                        pltpu.VMEM((hp, VT_ROWS, tq), F32)],
        compiler_params=_params("parallel", "parallel", "arbitrary", vmem=VMEM_LIMIT_LARGE),
        name="attn",
    )(qt, k, vt)


def _memkv_kernel(mem_ref, g_ref, wk_ref, wv_ref, k_ref, v_ref, *, hd):
    mn = _rms(mem_ref[0], g_ref[...]).astype(BF16)
    kk = _dot(mn, wk_ref[...])
    vv = _dot(mn, wv_ref[...])
    for hh in range(XATTN_HEADS):
        k_ref[0, hh] = kk[:, hh * hd:(hh + 1) * hd].astype(BF16)
        v_ref[0, hh] = vv[:, hh * hd:(hh + 1) * hd].T.astype(BF16)


def _memkv(mem, g, wk, wv):
    b, nm, d = mem.shape
    hd = d // XATTN_HEADS
    ospec = pl.BlockSpec((1, XATTN_HEADS, nm, hd), lambda i: (i, 0, 0, 0))
    return pl.pallas_call(
        functools.partial(_memkv_kernel, hd=hd),
        grid=(b,),
        in_specs=[pl.BlockSpec((1, nm, d), lambda i: (i, 0, 0)), _full(g.shape), _full(wk.shape), _full(wv.shape)],
        out_specs=[ospec, ospec],
        out_shape=[jax.ShapeDtypeStruct((b, XATTN_HEADS, nm, hd), BF16)] * 2,
        compiler_params=_params("parallel"),
        name="mem_kv",
    )(mem, g, wk, wv)


def _rms_t(xt, g_col):
    ms = jnp.mean(xt * xt, axis=0, keepdims=True)
    return xt * lax.rsqrt(ms + EPS) * g_col


def _post_kernel(x_ref, ys_ref, ym_ref, gm_ref, woutt_ref, gx_ref, wqt_ref, km_ref, vmt_ref, wot_ref,
                 gmoe_ref, wrht_ref, wrlt_ref, br_ref, x2_ref, route_ref, *, hd):
    heads = [ym_ref[0, hh].astype(F32) for hh in range(MLA_HEADS)]
    ym = jnp.concatenate([heads[k] + pltpu.roll(heads[k + 1], MLA_V, 1) for k in range(0, MLA_HEADS, 2)], axis=1)
    ymn = _rms(ym, gm_ref[...]).astype(BF16)
    y_mix = jnp.concatenate([ys_ref[0], ymn], axis=1)
    x1 = x_ref[0].T + _dot_nt(woutt_ref[...], y_mix)

    h2 = _rms_t(x1, gx_ref[...]).astype(BF16)
    qx = (_dot(wqt_ref[...], h2) * (hd ** -0.5)).astype(BF16)
    outs = []
    for hh in range(XATTN_HEADS):
        s = _dot(km_ref[0, hh], qx[hh * hd:(hh + 1) * hd])
        p = jnp.exp(s - jnp.max(s, axis=0, keepdims=True))
        inv = 1.0 / jnp.sum(p, axis=0, keepdims=True)
        outs.append(_dot(vmt_ref[0, hh], p.astype(BF16)) * inv)
    o = jnp.concatenate(outs, axis=0).astype(BF16)
    x2 = x1 + _dot(wot_ref[...], o)
    x2_ref[0] = x2.T

    h3 = _rms_t(x2, gmoe_ref[...])
    h_hi = h3.astype(BF16)
    h_lo = (h3 - h_hi.astype(F32)).astype(BF16)
    logits = (_dot(wrht_ref[...], h_hi) + _dot(wrlt_ref[...], h_hi) + _dot(wrht_ref[...], h_lo)) + br_ref[...]
    idx = lax.broadcasted_iota(jnp.int32, logits.shape, 0)
    is_g = (idx >= MOE_EXPERTS) & (idx < MOE_EXPERTS + MOE_GROUPS)
    gl = jnp.where(is_g, logits, NEG_INF)
    gmax = jnp.max(gl, axis=0, keepdims=True)
    g_w = 1.0 / jnp.sum(jnp.exp(gl - gmax), axis=0, keepdims=True)
    g_idx = jnp.min(jnp.where(gl == gmax, idx, 4 * LANES), axis=0, keepdims=True) - MOE_EXPERTS
    in_grp = (idx >= g_idx * MOE_PER_GROUP) & (idx < (g_idx + 1) * MOE_PER_GROUP)
    el = jnp.where(in_grp, logits, NEG_INF)
    v1 = jnp.max(el, axis=0, keepdims=True)
    i1 = jnp.min(jnp.where(el == v1, idx, 4 * LANES), axis=0, keepdims=True)
    el2 = jnp.where(idx == i1, NEG_INF, el)
    v2 = jnp.max(el2, axis=0, keepdims=True)
    i2 = jnp.min(jnp.where(el2 == v2, idx, 4 * LANES), axis=0, keepdims=True)
    e2 = jnp.exp(v2 - v1)
    w1 = g_w / (1.0 + e2)
    w2 = g_w * e2 / (1.0 + e2)
    route = (jnp.where(idx == i1, w1, 0.0) + jnp.where(idx == i2, w2, 0.0)
             + jnp.where(idx == GROUP_LANE, g_idx.astype(F32), 0.0))
    route_ref[0] = route.T


def _post(x, ys, ym, gm, woutt, gx, wqt, km, vmt, wot, gmoe, wrht, wrlt, br, tm):
    b, s, d = x.shape
    hd = d // XATTN_HEADS
    tok = lambda n: pl.BlockSpec((1, tm, n), lambda i, j: (i, j, 0))
    mem = pl.BlockSpec((1,) + km.shape[1:], lambda i, j: (i, 0, 0, 0))
    return pl.pallas_call(
        functools.partial(_post_kernel, hd=hd),
        grid=(b, s // tm),
        in_specs=[tok(d), tok(ys.shape[-1]), pl.BlockSpec((1, MLA_HEADS, tm, LANES), lambda i, j: (i, 0, j, 0)),
                  _full(gm.shape), _full(woutt.shape), _full(gx.shape), _full(wqt.shape), mem, mem,
                  _full(wot.shape), _full(gmoe.shape), _full(wrht.shape), _full(wrlt.shape), _full(br.shape)],
        out_specs=[tok(d), tok(LANES)],
        out_shape=[jax.ShapeDtypeStruct((b, s, d), F32), jax.ShapeDtypeStruct((b, s, LANES), F32)],
        compiler_params=_params("parallel", "parallel", vmem=VMEM_LIMIT_LARGE),
        name="post",
    )(x, ys, ym, gm, woutt, gx, wqt, km, vmt, wot, gmoe, wrht, wrlt, br)


def _route_metadata(route, tm, blk):
    b, s, _ = route.shape
    n_tiles = (s + MOE_GROUPS * tm) // tm
    gid = route[..., GROUP_LANE].astype(jnp.int32)
    groups = jnp.arange(MOE_GROUPS, dtype=jnp.int32)[None, :, None]
    onehot = (gid[:, None, :] == groups).astype(jnp.int32)
    csum = jnp.cumsum(onehot, axis=2)
    cnt = csum[:, :, -1]
    used = (cnt + blk - 1) // blk * blk
    padded = (used + tm - 1) // tm * tm
    seg_end = jnp.cumsum(padded, axis=-1)
    seg_start = seg_end - padded
    dest = jnp.sum(onehot * (seg_start[:, :, None] + csum - 1), axis=1)
    tile_start = jnp.arange(n_tiles, dtype=jnp.int32) * tm
    tile_g = jnp.sum((seg_end[:, None, :] <= tile_start[None, :, None]).astype(jnp.int32), axis=-1)
    tile_g = jnp.minimum(tile_g, MOE_GROUPS - 1)
    in_group = (tile_g[:, :, None] == jnp.arange(MOE_GROUPS, dtype=jnp.int32)).astype(jnp.int32)
    used_end = jnp.sum(in_group * (seg_start + used)[:, None, :], axis=-1)
    tile_blocks = jnp.clip(used_end - tile_start[None, :], 0, tm) // blk
    return dict(pos=dest, tile_g=tile_g.reshape(b * n_tiles), tile_blocks=tile_blocks.reshape(b * n_tiles),
                n_pad=n_tiles * tm)


def _dispatch_kernel(h_ref, g_ref, dst_ref, xs_ref, gs_ref, *, tt):
    @pl.when(pl.program_id(1) == 0)
    def _():
        xs_ref[...] = jnp.zeros(xs_ref.shape, xs_ref.dtype)
        gs_ref[...] = jnp.zeros(gs_ref.shape, gs_ref.dtype)

    for t in range(tt):
        r = dst_ref[0, 0, t]
        xs_ref[0, pl.ds(r, 1), :] = h_ref[0, t:t + 1, :]
        gs_ref[0, pl.ds(r, 1), :] = g_ref[0, t:t + 1, :]


def _dispatch(x2, route, pos, n_pad, tt):
    b, s, d = x2.shape
    nj = s // tt
    row_block = lambda n: pl.BlockSpec((1, n_pad, n), lambda i, j: (i, 0, 0), pipeline_mode=pl.Buffered(1))
    return pl.pallas_call(
        functools.partial(_dispatch_kernel, tt=tt),
        grid=(b, nj),
        in_specs=[pl.BlockSpec((1, tt, d), lambda i, j: (i, j, 0)),
                  pl.BlockSpec((1, tt, LANES), lambda i, j: (i, j, 0)),
                  pl.BlockSpec((1, 1, tt), lambda i, j: (i * nj + j, 0, 0), memory_space=pltpu.SMEM)],
        out_specs=[row_block(d), row_block(LANES)],
        out_shape=[jax.ShapeDtypeStruct((b, n_pad, d), F32), jax.ShapeDtypeStruct((b, n_pad, LANES), F32)],
        compiler_params=_params("parallel", "arbitrary", vmem=VMEM_LIMIT_LARGE),
        name="dispatch",
    )(x2, route, pos.reshape(b * nj, 1, tt))


def _moe_kernel(tg_ref, tb_ref, x_ref, gate_ref, gmoe_ref, wg_ref, wu_ref, wd_ref, o_ref, *, nt, blk):
    tile = pl.program_id(0) * nt + pl.program_id(1)
    grp = tg_ref[tile]
    n_blocks = tb_ref[tile]
    tm = x_ref.shape[1]

    def experts(rows):
        h = _rms(x_ref[0, rows, :], gmoe_ref[...]).astype(BF16)
        gate = gate_ref[0, rows, :]
        lane = lax.broadcasted_iota(jnp.int32, gate.shape, 1)
        acts = []
        for e in range(MOE_PER_GROUP):
            a = jax.nn.silu(_dot(h, wg_ref[e])) * _dot(h, wu_ref[e])
            gcol = jnp.sum(jnp.where(lane == grp * MOE_PER_GROUP + e, gate, 0.0), axis=-1, keepdims=True)
            acts.append((a * gcol).astype(BF16))
        wd = wd_ref[...]
        return _dot(jnp.concatenate(acts, axis=1), wd.reshape(wd.shape[0] * wd.shape[1], wd.shape[2]))

    for filled in range(tm // blk + 1):
        @pl.when(n_blocks == filled)
        def _(filled=filled):
            if filled:
                o_ref[0, :filled * blk, :] = experts(slice(0, filled * blk))
            if filled * blk < tm:
                o_ref[0, filled * blk:, :] = jnp.zeros((tm - filled * blk, o_ref.shape[2]), o_ref.dtype)


def _moe(xs, gs, gmoe, tile_g, tile_blocks, wg, wu, wd, tm, blk):
    b, n_pad, d = xs.shape
    nt = n_pad // tm
    ff = wg.shape[-1]
    group = lambda i, j, tg, tb: (tg[i * nt + j], 0, 0)
    tok = lambda n: pl.BlockSpec((1, tm, n), lambda i, j, tg, tb: (i, j, 0))
    grid_spec = pltpu.PrefetchScalarGridSpec(
        num_scalar_prefetch=2,
        grid=(b, nt),
        in_specs=[tok(d), tok(LANES), pl.BlockSpec(gmoe.shape, lambda i, j, tg, tb: (0, 0)),
                  pl.BlockSpec((MOE_PER_GROUP, d, ff), group),
                  pl.BlockSpec((MOE_PER_GROUP, d, ff), group), pl.BlockSpec((MOE_PER_GROUP, ff, d), group)],
        out_specs=tok(d),
    )
    return pl.pallas_call(
        functools.partial(_moe_kernel, nt=nt, blk=blk),
        grid_spec=grid_spec,
        out_shape=jax.ShapeDtypeStruct((b, n_pad, d), F32),
        compiler_params=_params("parallel", "arbitrary"),
        name="moe",
    )(tile_g, tile_blocks, xs, gs, gmoe, wg, wu, wd)


def _combine_kernel(ys_ref, pos_ref, x2_ref, gf_ref, o_ref, buf_ref, *, tt):
    for t in range(tt):
        buf_ref[t:t + 1, :] = ys_ref[0, pl.ds(pos_ref[0, 0, t], 1), :]
    o_ref[0] = _rms(x2_ref[0] + buf_ref[...], gf_ref[...])


def _combine(ys, pos, x2, gf, tt):
    b, s, d = x2.shape
    n_pad = ys.shape[1]
    nj = s // tt
    return pl.pallas_call(
        functools.partial(_combine_kernel, tt=tt),
        grid=(b, nj),
        in_specs=[pl.BlockSpec((1, n_pad, d), lambda i, j: (i, 0, 0), pipeline_mode=pl.Buffered(1)),
                  pl.BlockSpec((1, 1, tt), lambda i, j: (i * nj + j, 0, 0), memory_space=pltpu.SMEM),
                  pl.BlockSpec((1, tt, d), lambda i, j: (i, j, 0)), _full(gf.shape)],
        out_specs=pl.BlockSpec((1, tt, d), lambda i, j: (i, j, 0)),
        out_shape=jax.ShapeDtypeStruct((b, s, d), F32),
        scratch_shapes=[pltpu.VMEM((tt, d), F32)],
        compiler_params=_params("parallel", "arbitrary", vmem=VMEM_LIMIT_LARGE),
        name="combine",
    )(ys, pos.reshape(b * nj, 1, tt), x2, gf)


def _pad_heads(w, per_head, offset=0):
    k = w.shape[0]
    w = w.reshape(k, MLA_HEADS, per_head)
    w = jnp.pad(w, ((0, 0), (0, 0), (offset, LANES - per_head - offset)))
    return w.reshape(k, MLA_HEADS * LANES)


def _block_diag(blocks):
    n, r, c = blocks.shape
    eye = jnp.eye(n, dtype=blocks.dtype)
    return (eye[:, None, :, None] * blocks[:, :, None, :]).reshape(n * r, n * c)


def kernel(x, mem, positions, norm_mix_g, w_in, ssm_lam_re, ssm_lam_im, ssm_log_dt, ssm_b_re, ssm_b_im, ssm_c_re, ssm_c_im, ssm_d, ssm_w_glu, ssm_b_glu, mla_q_norm_g, mla_w_q_up, mla_kv_norm_g, mla_w_kv_up, out_norm_ssm_g, out_norm_mla_g, w_out, norm_xattn_g, norm_mem_g, xattn_w_q, xattn_w_k, xattn_w_v, xattn_w_o, norm_moe_g, moe_w_group, moe_b_group, moe_w_expert, moe_b_expert, moe_w_gate, moe_w_up, moe_w_down, norm_final_g):
    bsz, seq, d = x.shape
    depth = w_in.shape[0]
    assert depth == 1, "the final RMSNorm is fused into the last stage of a single layer"
    d_ssm = ssm_d.shape[-1]
    q_rank = mla_q_norm_g.shape[-1]
    kv_rank = mla_kv_norm_g.shape[-1]
    n_grp = d_ssm // SSM_GROUP
    s1, s2, s3 = d_ssm, d_ssm + q_rank, d_ssm + q_rank + kv_rank
    row = lambda v: v.reshape(1, -1).astype(F32)
    tiles = _tiles(seq)

    half = MLA_ROPE // 2
    invf = (ROPE_THETA ** (-jnp.arange(half, dtype=F32) / half)).reshape(half, 1)
    pos = positions.reshape(bsz, 1, seq)

    for l in range(depth):
        a_re, a_im, bb_re, bb_im = _ssm_prep(ssm_lam_re[l], ssm_lam_im[l], ssm_log_dt[l], ssm_b_re[l], ssm_b_im[l])
        gh = n_grp // 2
        bbd = jnp.stack([
            jnp.concatenate([_block_diag(bb_re[hf * gh:(hf + 1) * gh]), _block_diag(bb_im[hf * gh:(hf + 1) * gh])],
                            axis=1) for hf in range(2)]).astype(BF16)
        c_re_t = jnp.transpose(ssm_c_re[l], (0, 2, 1))
        c_im_t = jnp.transpose(ssm_c_im[l], (0, 2, 1))
        cmat = jnp.stack([
            jnp.concatenate([_block_diag(c_re_t[hf * gh:(hf + 1) * gh]), -_block_diag(c_im_t[hf * gh:(hf + 1) * gh])],
                            axis=0) for hf in range(2)]).astype(BF16)
        atab = jnp.stack([
            jnp.repeat(arr.reshape(2, 1, gh * SSM_STATE), bsz, axis=1).reshape(2 * bsz, gh * SSM_STATE)
            for arr in (a_re, a_im)])

        wi = w_in[l]
        wu = wi[:, :s1].astype(BF16)
        wcq = wi[:, s1:s2].astype(BF16)
        wckv = wi[:, s2:s3].astype(BF16)
        wkrt = jnp.pad(wi[:, s3:], ((0, 0), (MLA_NOPE, LANES - MLA_QK))).T.astype(BF16)
        wqt = _pad_heads(mla_w_q_up[l], MLA_QK).T.astype(BF16)
        wkv = mla_w_kv_up[l].reshape(kv_rank, MLA_HEADS, MLA_NOPE + MLA_V)
        wk = _pad_heads(wkv[:, :, :MLA_NOPE].reshape(kv_rank, -1), MLA_NOPE).astype(BF16)
        wvt = jnp.pad(wkv[:, :, MLA_NOPE:], ((0, 0), (0, 0), (0, VT_ROWS - MLA_V)))
        wvt = wvt.reshape(kv_rank, MLA_HEADS * VT_ROWS).T.astype(BF16)
        u, q, k, v = _proj(x, pos, row(norm_mix_g[l]), wu, wcq, wckv, wkrt, row(mla_q_norm_g[l]), wqt,
                           row(mla_kv_norm_g[l]), wk, wvt, invf, tm=tiles.proj)

        y_ssm = _ssm(u, bbd, atab, cmat, row(ssm_d[l]), ssm_w_glu[l].astype(BF16), row(ssm_b_glu[l]),
                     row(out_norm_ssm_g[l]), tt=tiles.ssm)
        y_mla = _attn(q, k, v, tq=tiles.attn_q, tk=tiles.attn_k, hp=tiles.attn_heads, strip=tiles.attn_strip)

        km, vm = _memkv(mem, row(norm_mem_g[l]), xattn_w_k[l].astype(BF16), xattn_w_v[l].astype(BF16))
        col = lambda v: v.reshape(-1, 1).astype(F32)
        woutt = w_out[l].T.astype(BF16)
        wr = jnp.concatenate([moe_w_expert[l].reshape(d, MOE_EXPERTS), moe_w_group[l]], axis=1)
        wrt = jnp.pad(wr, ((0, 0), (0, LANES - MOE_EXPERTS - MOE_GROUPS))).T.astype(F32)
        wrt_hi = wrt.astype(BF16)
        wrt_lo = (wrt - wrt_hi.astype(F32)).astype(BF16)
        br = jnp.concatenate([moe_b_expert[l].reshape(-1), moe_b_group[l]])
        br = jnp.pad(br, (0, LANES - MOE_EXPERTS - MOE_GROUPS)).reshape(LANES, 1).astype(F32)
        x2, route = _post(x, y_ssm, y_mla, row(out_norm_mla_g[l]), woutt, col(norm_xattn_g[l]),
                          xattn_w_q[l].T.astype(BF16), km, vm, xattn_w_o[l].T.astype(BF16), col(norm_moe_g[l]),
                          wrt_hi, wrt_lo, br, tm=tiles.post)

        meta = _route_metadata(route, tiles.moe, tiles.moe_block)
        xs, gs = _dispatch(x2, route, meta["pos"], meta["n_pad"], tt=tiles.copy)
        ys = _moe(xs, gs, row(norm_moe_g[l]), meta["tile_g"], meta["tile_blocks"], moe_w_gate[l].astype(BF16),
                  moe_w_up[l].astype(BF16), moe_w_down[l].astype(BF16), tm=tiles.moe, blk=tiles.moe_block)
        x = _combine(ys, meta["pos"], x2, row(norm_final_g), tt=tiles.copy)
    return x
```

```python
import functools
import math
from typing import NamedTuple

import jax
import jax.numpy as jnp
from jax import lax
from jax.experimental import pallas as pl
from jax.experimental.pallas import tpu as pltpu

F32 = jnp.float32
BF16 = jnp.bfloat16

EPS = 1e-6
NEG_INF = -1e30
CHUNK = 64

LANES = 128
SSM_GROUP = 16
SSM_STATE = 64
MLA_HEADS = 8
MLA_NOPE = 64
MLA_ROPE = 32
MLA_QK = MLA_NOPE + MLA_ROPE
MLA_V = 64
ROPE_THETA = 10000.0
XATTN_HEADS = 4
MOE_GROUPS = 4
MOE_PER_GROUP = 8
MOE_EXPERTS = MOE_GROUPS * MOE_PER_GROUP
VMEM_LIMIT = 48 * 1024 * 1024
VMEM_LIMIT_LARGE = 56 * 1024 * 1024
LOG2E = math.log2(math.e)
VT_ROWS = 80


class Tiles(NamedTuple):
    proj: int
    ssm: int
    attn_q: int
    attn_k: int
    attn_heads: int
    attn_strip: int
    post: int
    moe: int
    copy: int


def _tiles(seq):
    return Tiles(proj=min(1024, seq), ssm=min(256, seq), attn_q=min(512, seq), attn_k=min(256, seq),
                 attn_heads=MLA_HEADS, attn_strip=32, post=min(1024, seq),
                 moe=256, copy=min(1024, seq))


def _dot(a, b):
    return jnp.dot(a, b, preferred_element_type=F32)


def _dot_nt(a, b):
    return lax.dot_general(a, b, (((1,), (1,)), ((), ())), preferred_element_type=F32)


def _rms(x, g):
    ms = jnp.mean(x * x, axis=-1, keepdims=True)
    return x * lax.rsqrt(ms + EPS) * g


def _params(*sem, vmem=VMEM_LIMIT):
    return pltpu.CompilerParams(dimension_semantics=sem, vmem_limit_bytes=vmem)


def _full(shape):
    zeros = (0,) * len(shape)
    return pl.BlockSpec(shape, lambda *_: zeros)


def _ssm_prep_kernel(lre_ref, lim_ref, ldt_ref, bre_ref, bim_ref, are_ref, aim_ref, bbre_ref, bbim_ref):
    lre = jnp.minimum(lre_ref[...], -1e-4)
    lim = lim_ref[...]
    dt = jnp.exp(ldt_ref[...])
    mag = jnp.exp(lre * dt)
    are = mag * jnp.cos(lim * dt)
    aim = mag * jnp.sin(lim * dt)
    are_ref[...] = are
    aim_ref[...] = aim
    nre = are - 1.0
    den = lre * lre + lim * lim
    fre = (nre * lre + aim * lim) / den
    fim = (aim * lre - nre * lim) / den
    bre = bre_ref[...]
    bim = bim_ref[...]
    bbre_ref[...] = fre * bre - fim * bim
    bbim_ref[...] = fre * bim + fim * bre


def _ssm_prep(lam_re, lam_im, log_dt, b_re, b_im):
    g, p = lam_re.shape
    hh = b_re.shape[-1]
    bt_re = jnp.transpose(b_re, (0, 2, 1))
    bt_im = jnp.transpose(b_im, (0, 2, 1))
    outs = pl.pallas_call(
        _ssm_prep_kernel,
        out_shape=[jax.ShapeDtypeStruct((g, 1, p), F32)] * 2 + [jax.ShapeDtypeStruct((g, hh, p), F32)] * 2,
        name="ssm_prep",
    )(lam_re.reshape(g, 1, p), lam_im.reshape(g, 1, p), log_dt.reshape(g, 1, 1), bt_re, bt_im)
    a_re, a_im, bb_re, bb_im = outs
    return a_re.reshape(g, p), a_im.reshape(g, p), bb_re, bb_im


def _proj_kernel(x_ref, pos_ref, gmix_ref, wu_ref, wcq_ref, wckv_ref, wkrt_ref, qg_ref, wqt_ref, kvg_ref,
                 wk_ref, wvt_ref, invf_ref, u_ref, q_ref, k_ref, v_ref):
    half = MLA_ROPE // 2
    h = _rms(x_ref[0], gmix_ref[...]).astype(BF16)
    u_ref[0] = _dot(h, wu_ref[...]).astype(BF16)
    cqn = _rms(_dot(h, wcq_ref[...]), qg_ref[...]).astype(BF16)
    ckvn = _rms(_dot(h, wckv_ref[...]), kvg_ref[...]).astype(BF16)

    ang = invf_ref[...] * pos_ref[0].astype(F32)
    cosv = jnp.cos(ang)
    sinv = jnp.sin(ang)

    def rot_t(blk):
        x1 = blk[MLA_NOPE:MLA_NOPE + half]
        x2 = blk[MLA_NOPE + half:MLA_QK]
        return jnp.concatenate([blk[:MLA_NOPE], x1 * cosv - x2 * sinv, x1 * sinv + x2 * cosv, blk[MLA_QK:]], axis=0)

    qt = _dot_nt(wqt_ref[...], cqn) * (MLA_QK ** -0.5 * LOG2E)
    vt = _dot_nt(wvt_ref[...], ckvn)
    kk = _dot(ckvn, wk_ref[...])
    kpe = rot_t(_dot_nt(wkrt_ref[...], h)).T
    ones_row = jnp.where(lax.broadcasted_iota(jnp.int32, (VT_ROWS, 1), 0) == MLA_V, 1.0, 0.0)
    for hh in range(MLA_HEADS):
        q_ref[0, hh] = rot_t(qt[hh * LANES:(hh + 1) * LANES]).astype(BF16)
        k_ref[0, hh] = (kk[:, hh * LANES:(hh + 1) * LANES] + kpe).astype(BF16)
        v_ref[0, hh] = (vt[hh * VT_ROWS:(hh + 1) * VT_ROWS] + ones_row).astype(BF16)


def _proj(x, pos, gmix, wu, wcq, wckv, wkrt, qg, wqt, kvg, wk, wvt, invf, tm):
    b, s, d = x.shape
    grid = (b, s // tm)
    tok = lambda n: pl.BlockSpec((1, tm, n), lambda i, j: (i, j, 0))
    head = pl.BlockSpec((1, MLA_HEADS, tm, LANES), lambda i, j: (i, 0, j, 0))
    head_t = pl.BlockSpec((1, MLA_HEADS, LANES, tm), lambda i, j: (i, 0, 0, j))
    head_vt = pl.BlockSpec((1, MLA_HEADS, VT_ROWS, tm), lambda i, j: (i, 0, 0, j))
    tshape = lambda n: jax.ShapeDtypeStruct((b, MLA_HEADS, n, s), BF16)
    return pl.pallas_call(
        _proj_kernel,
        grid=grid,
        in_specs=[tok(d), pl.BlockSpec((1, 1, tm), lambda i, j: (i, 0, j)), _full(gmix.shape), _full(wu.shape),
                  _full(wcq.shape), _full(wckv.shape), _full(wkrt.shape), _full(qg.shape), _full(wqt.shape),
                  _full(kvg.shape), _full(wk.shape), _full(wvt.shape), _full(invf.shape)],
        out_specs=[tok(wu.shape[1]), head_t, head, head_vt],
        out_shape=[jax.ShapeDtypeStruct((b, s, wu.shape[1]), BF16), tshape(LANES),
                   jax.ShapeDtypeStruct((b, MLA_HEADS, s, LANES), BF16), tshape(VT_ROWS)],
        compiler_params=_params("parallel", "parallel"),
        name="proj",
    )(x, pos, gmix, wu, wcq, wckv, wkrt, qg, wqt, kvg, wk, wvt, invf)


def _ssm_kernel(u_ref, bbd_ref, atab_ref, cmat_ref, dskip_ref, wglu_ref, bglu_ref, gout_ref, y_ref,
                xs_ref, st_ref, *, tt, slab, nb, lane_tiles):
    j = pl.program_id(0)
    half_w = u_ref.shape[-1] // 2

    @pl.when(j == 0)
    def _():
        st_ref[...] = jnp.zeros_like(st_ref)

    for hf in range(2):
        ub = u_ref[:, :, hf * half_w:(hf + 1) * half_w].reshape(nb * tt, half_w)
        bu = _dot(ub, bbd_ref[hf])
        for b in range(nb):
            r0 = (hf * nb + b) * slab
            for c in range(2 * lane_tiles):
                xs_ref[c, r0:r0 + tt, :] = bu[b * tt:(b + 1) * tt, c * LANES:(c + 1) * LANES]

    rows = 2 * nb
    group = 4
    for c0 in range(0, lane_tiles, group):
        cs = list(range(c0, c0 + group))
        a_re = [atab_ref[0, :, c * LANES:(c + 1) * LANES] for c in cs]
        a_im = [atab_ref[1, :, c * LANES:(c + 1) * LANES] for c in cs]
        init = tuple(st_ref[0, :, c * LANES:(c + 1) * LANES] for c in cs) + \
            tuple(st_ref[1, :, c * LANES:(c + 1) * LANES] for c in cs)

        def step(t, carry, cs=cs, a_re=a_re, a_im=a_im):
            new_re, new_im = [], []
            for i, c in enumerate(cs):
                x_re, x_im = carry[i], carry[group + i]
                idx = pl.ds(t, rows, stride=slab)
                n_re = a_re[i] * x_re - a_im[i] * x_im + xs_ref[c, idx, :]
                n_im = a_re[i] * x_im + a_im[i] * x_re + xs_ref[lane_tiles + c, idx, :]
                xs_ref[c, idx, :] = n_re
                xs_ref[lane_tiles + c, idx, :] = n_im
                new_re.append(n_re)
                new_im.append(n_im)
            return tuple(new_re) + tuple(new_im)

        fin = lax.fori_loop(0, tt, step, init, unroll=4)
        for i, c in enumerate(cs):
            st_ref[0, :, c * LANES:(c + 1) * LANES] = fin[i]
            st_ref[1, :, c * LANES:(c + 1) * LANES] = fin[group + i]

    ys = []
    for hf in range(2):
        xb = []
        for b in range(nb):
            r0 = (hf * nb + b) * slab
            xb.append(jnp.concatenate([xs_ref[c, r0:r0 + tt, :] for c in range(2 * lane_tiles)], axis=1))
        xh = jnp.concatenate(xb, axis=0).astype(BF16)
        ys.append(_dot(xh, cmat_ref[hf]))
    y = jnp.concatenate(ys, axis=1)
    u = u_ref[...].astype(F32).reshape(nb * tt, 2 * half_w)
    y = jax.nn.gelu(y + dskip_ref[...] * u)
    y = y * jax.nn.sigmoid(_dot(y.astype(BF16), wglu_ref[...]) + bglu_ref[...])
    y = _rms(y, gout_ref[...])
    y_ref[...] = y.reshape(nb, tt, 2 * half_w).astype(BF16)


def _ssm(u, bbd, atab, cmat, dskip, wglu, bglu, gout, tt):
    nb, s, dssm = u.shape
    lane_tiles = bbd.shape[-1] // (2 * LANES)
    slab = tt + 4
    kern = functools.partial(_ssm_kernel, tt=tt, slab=slab, nb=nb, lane_tiles=lane_tiles)
    return pl.pallas_call(
        kern,
        grid=(s // tt,),
        in_specs=[pl.BlockSpec((nb, tt, dssm), lambda j: (0, j, 0)), _full(bbd.shape), _full(atab.shape),
                  _full(cmat.shape), _full(dskip.shape), _full(wglu.shape), _full(bglu.shape), _full(gout.shape)],
        out_specs=pl.BlockSpec((nb, tt, dssm), lambda j: (0, j, 0)),
        out_shape=jax.ShapeDtypeStruct((nb, s, dssm), BF16),
        scratch_shapes=[pltpu.VMEM((2 * lane_tiles, 2 * nb * slab, LANES), F32),
                        pltpu.VMEM((2, 2 * nb, lane_tiles * LANES), F32)],
        compiler_params=_params("arbitrary"),
        name="ssm",
    )(u, bbd, atab, cmat, dskip, wglu, bglu, gout)


def _attn_kernel(qt_ref, k_ref, vt_ref, o_ref, s_ref, p_ref, m_ref, a_ref, acc_ref, *, tq, tk, hp, strip):
    qi = pl.program_id(2)
    m_ref[...] = jnp.full(m_ref.shape, NEG_INF, F32)
    acc_ref[...] = jnp.zeros(acc_ref.shape, F32)
    q_chunk = lax.broadcasted_iota(jnp.int32, (1, tq), 1) // CHUNK

    def tile(k0, diag_off):
        qs = slice(0 if diag_off is None else diag_off, tq)
        for hh in range(hp):
            s_ref[hh, :, qs] = _dot(k_ref[0, hh, pl.ds(k0, tk), :], qt_ref[0, hh, :, qs])

        def strip_of(hh, r):
            s = s_ref[hh, r:r + strip, qs]
            if diag_off is None:
                return s
            return jnp.where(q_chunk[:, qs] >= (diag_off + r) // CHUNK, s, NEG_INF)

        for hh in range(hp):
            mt = strip_of(hh, 0)
            for r in range(strip, tk, strip):
                mt = jnp.maximum(mt, strip_of(hh, r))
            m_old = m_ref[hh, :, qs]
            m_new = jnp.maximum(m_old, jnp.max(mt, axis=0, keepdims=True))
            a_ref[hh, :, qs] = jnp.exp2(m_old - m_new)
            m_ref[hh, :, qs] = m_new
            for r in range(0, tk, strip):
                p_ref[hh, r:r + strip, qs] = jnp.exp2(strip_of(hh, r) - m_new).astype(BF16)
        for hh in range(hp):
            pv = _dot(vt_ref[0, hh, :, pl.ds(k0, tk)], p_ref[hh, :, qs])
            acc_ref[hh, :, qs] = acc_ref[hh, :, qs] * a_ref[hh, :, qs] + pv

    def body(j, c):
        tile(pl.multiple_of(j * tk, tk), None)
        return c

    lax.fori_loop(0, qi * (tq // tk), body, 0)
    for off in range(0, tq, tk):
        tile(pl.multiple_of(qi * tq + off, tk), off)
    feat = lax.broadcasted_iota(jnp.int32, (VT_ROWS, 1), 0)
    pad = jnp.zeros((LANES - VT_ROWS, tq), F32)
    for hh in range(hp):
        acc = acc_ref[hh]
        out = acc * (1.0 / acc[MLA_V:MLA_V + 1, :])
        out = jnp.where(feat < MLA_V, out, 0.0)
        o_ref[0, hh] = jnp.concatenate([out, pad], axis=0).T.astype(BF16)


def _attn(qt, k, vt, tq, tk, hp, strip):
    b, h, s, _ = k.shape
    assert strip <= CHUNK and CHUNK % strip == 0 and tk % CHUNK == 0 and tq % tk == 0
    return pl.pallas_call(
        functools.partial(_attn_kernel, tq=tq, tk=tk, hp=hp, strip=strip),
        grid=(b, h // hp, s // tq),
        in_specs=[pl.BlockSpec((1, hp, LANES, tq), lambda i, j, t: (i, j, 0, t)),
                  pl.BlockSpec((1, hp, s, LANES), lambda i, j, t: (i, j, 0, 0)),
                  pl.BlockSpec((1, hp, VT_ROWS, s), lambda i, j, t: (i, j, 0, 0))],
        out_specs=pl.BlockSpec((1, hp, tq, LANES), lambda i, j, t: (i, j, t, 0)),
        out_shape=jax.ShapeDtypeStruct((b, h, s, LANES), BF16),
        scratch_shapes=[pltpu.VMEM((hp, tk, tq), F32), pltpu.VMEM((hp, tk, tq), BF16),
                        pltpu.VMEM((hp, 1, tq), F32), pltpu.VMEM((hp, 1, tq), F32),
                        pltpu.VMEM((hp, VT_ROWS, tq), F32)],
        compiler_params=_params("parallel", "parallel", "arbitrary", vmem=VMEM_LIMIT_LARGE),
        name="attn",
    )(qt, k, vt)


def _memkv_kernel(mem_ref, g_ref, wk_ref, wv_ref, k_ref, v_ref, *, hd):
    mn = _rms(mem_ref[0], g_ref[...]).astype(BF16)
    kk = _dot(mn, wk_ref[...])
    vv = _dot(mn, wv_ref[...])
    for hh in range(XATTN_HEADS):
        k_ref[0, hh] = kk[:, hh * hd:(hh + 1) * hd].astype(BF16)
        v_ref[0, hh] = vv[:, hh * hd:(hh + 1) * hd].T.astype(BF16)


def _memkv(mem, g, wk, wv):
    b, nm, d = mem.shape
    hd = d // XATTN_HEADS
    ospec = pl.BlockSpec((1, XATTN_HEADS, nm, hd), lambda i: (i, 0, 0, 0))
    return pl.pallas_call(
        functools.partial(_memkv_kernel, hd=hd),
        grid=(b,),
        in_specs=[pl.BlockSpec((1, nm, d), lambda i: (i, 0, 0)), _full(g.shape), _full(wk.shape), _full(wv.shape)],
        out_specs=[ospec, ospec],
        out_shape=[jax.ShapeDtypeStruct((b, XATTN_HEADS, nm, hd), BF16)] * 2,
        compiler_params=_params("parallel"),
        name="mem_kv",
    )(mem, g, wk, wv)


def _rms_t(xt, g_col):
    ms = jnp.mean(xt * xt, axis=0, keepdims=True)
    return xt * lax.rsqrt(ms + EPS) * g_col


def _post_kernel(x_ref, ys_ref, ym_ref, gm_ref, woutt_ref, gx_ref, wqt_ref, km_ref, vmt_ref, wot_ref,
                 gmoe_ref, wrht_ref, wrlt_ref, br_ref, x2_ref, route_ref, gid_ref, *, hd):
    heads = [ym_ref[0, hh].astype(F32) for hh in range(MLA_HEADS)]
    ym = jnp.concatenate([heads[k] + pltpu.roll(heads[k + 1], MLA_V, 1) for k in range(0, MLA_HEADS, 2)], axis=1)
    ymn = _rms(ym, gm_ref[...]).astype(BF16)
    y_mix = jnp.concatenate([ys_ref[0], ymn], axis=1)
    x1 = x_ref[0].T + _dot_nt(woutt_ref[...], y_mix)

    h2 = _rms_t(x1, gx_ref[...]).astype(BF16)
    qx = (_dot(wqt_ref[...], h2) * (hd ** -0.5)).astype(BF16)
    outs = []
    for hh in range(XATTN_HEADS):
        s = _dot(km_ref[0, hh], qx[hh * hd:(hh + 1) * hd])
        p = jnp.exp(s - jnp.max(s, axis=0, keepdims=True))
        inv = 1.0 / jnp.sum(p, axis=0, keepdims=True)
        outs.append(_dot(vmt_ref[0, hh], p.astype(BF16)) * inv)
    o = jnp.concatenate(outs, axis=0).astype(BF16)
    x2 = x1 + _dot(wot_ref[...], o)
    x2_ref[0] = x2.T

    h3 = _rms_t(x2, gmoe_ref[...])
    h_hi = h3.astype(BF16)
    h_lo = (h3 - h_hi.astype(F32)).astype(BF16)
    logits = (_dot(wrht_ref[...], h_hi) + _dot(wrlt_ref[...], h_hi) + _dot(wrht_ref[...], h_lo)) + br_ref[...]
    idx = lax.broadcasted_iota(jnp.int32, logits.shape, 0)
    is_g = (idx >= MOE_EXPERTS) & (idx < MOE_EXPERTS + MOE_GROUPS)
    gl = jnp.where(is_g, logits, NEG_INF)
    gmax = jnp.max(gl, axis=0, keepdims=True)
    g_w = 1.0 / jnp.sum(jnp.exp(gl - gmax), axis=0, keepdims=True)
    g_idx = jnp.min(jnp.where(gl == gmax, idx, 4 * LANES), axis=0, keepdims=True) - MOE_EXPERTS
    in_grp = (idx >= g_idx * MOE_PER_GROUP) & (idx < (g_idx + 1) * MOE_PER_GROUP)
    el = jnp.where(in_grp, logits, NEG_INF)
    v1 = jnp.max(el, axis=0, keepdims=True)
    i1 = jnp.min(jnp.where(el == v1, idx, 4 * LANES), axis=0, keepdims=True)
    el2 = jnp.where(idx == i1, NEG_INF, el)
    v2 = jnp.max(el2, axis=0, keepdims=True)
    i2 = jnp.min(jnp.where(el2 == v2, idx, 4 * LANES), axis=0, keepdims=True)
    e2 = jnp.exp(v2 - v1)
    w1 = g_w / (1.0 + e2)
    w2 = g_w * e2 / (1.0 + e2)
    route = jnp.where(idx == i1, w1, 0.0) + jnp.where(idx == i2, w2, 0.0)
    route_ref[0] = route.T
    gid_ref[0] = g_idx


def _post(x, ys, ym, gm, woutt, gx, wqt, km, vmt, wot, gmoe, wrht, wrlt, br, tm):
    b, s, d = x.shape
    hd = d // XATTN_HEADS
    tok = lambda n: pl.BlockSpec((1, tm, n), lambda i, j: (i, j, 0))
    mem = pl.BlockSpec((1,) + km.shape[1:], lambda i, j: (i, 0, 0, 0))
    return pl.pallas_call(
        functools.partial(_post_kernel, hd=hd),
        grid=(b, s // tm),
        in_specs=[tok(d), tok(ys.shape[-1]), pl.BlockSpec((1, MLA_HEADS, tm, LANES), lambda i, j: (i, 0, j, 0)),
                  _full(gm.shape), _full(woutt.shape), _full(gx.shape), _full(wqt.shape), mem, mem,
                  _full(wot.shape), _full(gmoe.shape), _full(wrht.shape), _full(wrlt.shape), _full(br.shape)],
        out_specs=[tok(d), tok(LANES), pl.BlockSpec((1, 1, tm), lambda i, j: (i, 0, j))],
        out_shape=[jax.ShapeDtypeStruct((b, s, d), F32), jax.ShapeDtypeStruct((b, s, LANES), F32),
                   jax.ShapeDtypeStruct((b, 1, s), jnp.int32)],
        compiler_params=_params("parallel", "parallel", vmem=VMEM_LIMIT_LARGE),
        name="post",
    )(x, ys, ym, gm, woutt, gx, wqt, km, vmt, wot, gmoe, wrht, wrlt, br)


def _route_metadata(gid, tm):
    b, s = gid.shape
    n_tiles = (s + MOE_GROUPS * tm) // tm
    groups = jnp.arange(MOE_GROUPS, dtype=jnp.int32)[None, :, None]
    onehot = (gid[:, None, :] == groups).astype(jnp.int32)
    csum = jnp.cumsum(onehot, axis=2)
    cnt = csum[:, :, -1]
    padded = (cnt + tm - 1) // tm * tm
    seg_end = jnp.cumsum(padded, axis=-1)
    dest = jnp.sum(onehot * ((seg_end - padded)[:, :, None] + csum - 1), axis=1)
    tile_start = jnp.arange(n_tiles, dtype=jnp.int32) * tm
    tile_g = jnp.sum((seg_end[:, None, :] <= tile_start[None, :, None]).astype(jnp.int32), axis=-1)
    tile_g = jnp.minimum(tile_g, MOE_GROUPS - 1)
    n_valid = (seg_end[:, -1] // tm).astype(jnp.int32)
    return dict(pos=dest, tile_g=tile_g.reshape(b * n_tiles), n_valid=n_valid, n_pad=n_tiles * tm)


def _dispatch_kernel(pos_ref, h_ref, g_ref, xs_ref, gs_ref, *, tt, seq):
    @pl.when(pl.program_id(1) == 0)
    def _():
        xs_ref[...] = jnp.zeros(xs_ref.shape, xs_ref.dtype)
        gs_ref[...] = jnp.zeros(gs_ref.shape, gs_ref.dtype)

    base = pl.program_id(0) * seq + pl.program_id(1) * tt
    for t in range(tt):
        r = pos_ref[base + t]
        xs_ref[0, pl.ds(r, 1), :] = h_ref[0, t:t + 1, :]
        gs_ref[0, pl.ds(r, 1), :] = g_ref[0, t:t + 1, :]


def _dispatch(x2, route, pos, n_pad, tt):
    b, s, d = x2.shape
    row_block = lambda n: pl.BlockSpec((1, n_pad, n), lambda i, j, p: (i, 0, 0), pipeline_mode=pl.Buffered(1))
    grid_spec = pltpu.PrefetchScalarGridSpec(
        num_scalar_prefetch=1,
        grid=(b, s // tt),
        in_specs=[pl.BlockSpec((1, tt, d), lambda i, j, p: (i, j, 0)),
                  pl.BlockSpec((1, tt, LANES), lambda i, j, p: (i, j, 0))],
        out_specs=[row_block(d), row_block(LANES)],
    )
    return pl.pallas_call(
        functools.partial(_dispatch_kernel, tt=tt, seq=s),
        grid_spec=grid_spec,
        out_shape=[jax.ShapeDtypeStruct((b, n_pad, d), F32), jax.ShapeDtypeStruct((b, n_pad, LANES), F32)],
        compiler_params=_params("parallel", "arbitrary", vmem=VMEM_LIMIT_LARGE),
        name="dispatch",
    )(pos.reshape(b * s), x2, route)


def _moe_kernel(tg_ref, nv_ref, x_ref, gate_ref, gmoe_ref, wg_ref, wu_ref, wd_ref, o_ref, *, nt):
    bi = pl.program_id(0)
    ti = pl.program_id(1)
    grp = tg_ref[bi * nt + ti]

    @pl.when(ti < nv_ref[bi])
    def _():
        h = _rms(x_ref[0], gmoe_ref[...]).astype(BF16)
        gate = gate_ref[0]
        lane = lax.broadcasted_iota(jnp.int32, gate.shape, 1)
        acts = []
        for e in range(MOE_PER_GROUP):
            a = jax.nn.silu(_dot(h, wg_ref[e])) * _dot(h, wu_ref[e])
            gcol = jnp.sum(jnp.where(lane == grp * MOE_PER_GROUP + e, gate, 0.0), axis=-1, keepdims=True)
            acts.append((a * gcol).astype(BF16))
        wd = wd_ref[...]
        o_ref[0] = _dot(jnp.concatenate(acts, axis=1), wd.reshape(wd.shape[0] * wd.shape[1], wd.shape[2]))

    @pl.when(ti >= nv_ref[bi])
    def _():
        o_ref[0] = jnp.zeros(o_ref.shape[1:], o_ref.dtype)


def _moe(xs, gs, gmoe, tile_g, n_valid, wg, wu, wd, tm):
    b, n_pad, d = xs.shape
    nt = n_pad // tm
    ff = wg.shape[-1]
    group = lambda i, j, tg, nv: (tg[i * nt + j], 0, 0)
    tok = lambda n: pl.BlockSpec((1, tm, n), lambda i, j, tg, nv: (i, j, 0))
    grid_spec = pltpu.PrefetchScalarGridSpec(
        num_scalar_prefetch=2,
        grid=(b, nt),
        in_specs=[tok(d), tok(LANES), pl.BlockSpec(gmoe.shape, lambda i, j, tg, nv: (0, 0)),
                  pl.BlockSpec((MOE_PER_GROUP, d, ff), group),
                  pl.BlockSpec((MOE_PER_GROUP, d, ff), group), pl.BlockSpec((MOE_PER_GROUP, ff, d), group)],
        out_specs=tok(d),
    )
    return pl.pallas_call(
        functools.partial(_moe_kernel, nt=nt),
        grid_spec=grid_spec,
        out_shape=jax.ShapeDtypeStruct((b, n_pad, d), F32),
        compiler_params=_params("parallel", "arbitrary"),
        name="moe",
    )(tile_g, n_valid, xs, gs, gmoe, wg, wu, wd)


def _combine_kernel(pos_ref, ys_ref, x2_ref, gf_ref, o_ref, buf_ref, *, tt, seq):
    base = pl.program_id(0) * seq + pl.program_id(1) * tt
    for t in range(tt):
        buf_ref[t:t + 1, :] = ys_ref[0, pl.ds(pos_ref[base + t], 1), :]
    o_ref[0] = _rms(x2_ref[0] + buf_ref[...], gf_ref[...])


def _combine(ys, pos, x2, gf, tt):
    b, s, d = x2.shape
    n_pad = ys.shape[1]
    grid_spec = pltpu.PrefetchScalarGridSpec(
        num_scalar_prefetch=1,
        grid=(b, s // tt),
        in_specs=[pl.BlockSpec((1, n_pad, d), lambda i, j, p: (i, 0, 0), pipeline_mode=pl.Buffered(1)),
                  pl.BlockSpec((1, tt, d), lambda i, j, p: (i, j, 0)),
                  pl.BlockSpec(gf.shape, lambda i, j, p: (0, 0))],
        out_specs=pl.BlockSpec((1, tt, d), lambda i, j, p: (i, j, 0)),
        scratch_shapes=[pltpu.VMEM((tt, d), F32)],
    )
    return pl.pallas_call(
        functools.partial(_combine_kernel, tt=tt, seq=s),
        grid_spec=grid_spec,
        out_shape=jax.ShapeDtypeStruct((b, s, d), F32),
        compiler_params=_params("parallel", "arbitrary", vmem=VMEM_LIMIT_LARGE),
        name="combine",
    )(pos.reshape(b * s), ys, x2, gf)


def _pad_heads(w, per_head):
    k = w.shape[0]
    w = w.reshape(k, MLA_HEADS, per_head)
    w = jnp.pad(w, ((0, 0), (0, 0), (0, LANES - per_head)))
    return w.reshape(k, MLA_HEADS * LANES)


def _block_diag(blocks):
    n, r, c = blocks.shape
    eye = jnp.eye(n, dtype=blocks.dtype)
    return (eye[:, None, :, None] * blocks[:, :, None, :]).reshape(n * r, n * c)


def kernel(x, mem, positions, norm_mix_g, w_in, ssm_lam_re, ssm_lam_im, ssm_log_dt, ssm_b_re, ssm_b_im, ssm_c_re, ssm_c_im, ssm_d, ssm_w_glu, ssm_b_glu, mla_q_norm_g, mla_w_q_up, mla_kv_norm_g, mla_w_kv_up, out_norm_ssm_g, out_norm_mla_g, w_out, norm_xattn_g, norm_mem_g, xattn_w_q, xattn_w_k, xattn_w_v, xattn_w_o, norm_moe_g, moe_w_group, moe_b_group, moe_w_expert, moe_b_expert, moe_w_gate, moe_w_up, moe_w_down, norm_final_g):
    bsz, seq, d = x.shape
    depth = w_in.shape[0]
    assert depth == 1, "the final RMSNorm is fused into the last stage of a single layer"
    d_ssm = ssm_d.shape[-1]
    q_rank = mla_q_norm_g.shape[-1]
    kv_rank = mla_kv_norm_g.shape[-1]
    n_grp = d_ssm // SSM_GROUP
    s1, s2, s3 = d_ssm, d_ssm + q_rank, d_ssm + q_rank + kv_rank
    row = lambda v: v.reshape(1, -1).astype(F32)
    tiles = _tiles(seq)

    half = MLA_ROPE // 2
    invf = (ROPE_THETA ** (-jnp.arange(half, dtype=F32) / half)).reshape(half, 1)
    pos = positions.reshape(bsz, 1, seq)

    for l in range(depth):
        a_re, a_im, bb_re, bb_im = _ssm_prep(ssm_lam_re[l], ssm_lam_im[l], ssm_log_dt[l], ssm_b_re[l], ssm_b_im[l])
        gh = n_grp // 2
        bbd = jnp.stack([
            jnp.concatenate([_block_diag(bb_re[hf * gh:(hf + 1) * gh]), _block_diag(bb_im[hf * gh:(hf + 1) * gh])],
                            axis=1) for hf in range(2)]).astype(BF16)
        c_re_t = jnp.transpose(ssm_c_re[l], (0, 2, 1))
        c_im_t = jnp.transpose(ssm_c_im[l], (0, 2, 1))
        cmat = jnp.stack([
            jnp.concatenate([_block_diag(c_re_t[hf * gh:(hf + 1) * gh]), -_block_diag(c_im_t[hf * gh:(hf + 1) * gh])],
                            axis=0) for hf in range(2)]).astype(BF16)
        atab = jnp.stack([
            jnp.repeat(arr.reshape(2, 1, gh * SSM_STATE), bsz, axis=1).reshape(2 * bsz, gh * SSM_STATE)
            for arr in (a_re, a_im)])

        wi = w_in[l]
        wu = wi[:, :s1].astype(BF16)
        wcq = wi[:, s1:s2].astype(BF16)
        wckv = wi[:, s2:s3].astype(BF16)
        wkrt = jnp.pad(wi[:, s3:], ((0, 0), (MLA_NOPE, LANES - MLA_QK))).T.astype(BF16)
        wqt = _pad_heads(mla_w_q_up[l], MLA_QK).T.astype(BF16)
        wkv = mla_w_kv_up[l].reshape(kv_rank, MLA_HEADS, MLA_NOPE + MLA_V)
        wk = _pad_heads(wkv[:, :, :MLA_NOPE].reshape(kv_rank, -1), MLA_NOPE).astype(BF16)
        wvt = jnp.pad(wkv[:, :, MLA_NOPE:], ((0, 0), (0, 0), (0, VT_ROWS - MLA_V)))
        wvt = wvt.reshape(kv_rank, MLA_HEADS * VT_ROWS).T.astype(BF16)
        u, q, k, v = _proj(x, pos, row(norm_mix_g[l]), wu, wcq, wckv, wkrt, row(mla_q_norm_g[l]), wqt,
                           row(mla_kv_norm_g[l]), wk, wvt, invf, tm=tiles.proj)

        y_ssm = _ssm(u, bbd, atab, cmat, row(ssm_d[l]), ssm_w_glu[l].astype(BF16), row(ssm_b_glu[l]),
                     row(out_norm_ssm_g[l]), tt=tiles.ssm)
        y_mla = _attn(q, k, v, tq=tiles.attn_q, tk=tiles.attn_k, hp=tiles.attn_heads, strip=tiles.attn_strip)

        km, vm = _memkv(mem, row(norm_mem_g[l]), xattn_w_k[l].astype(BF16), xattn_w_v[l].astype(BF16))
        col = lambda v: v.reshape(-1, 1).astype(F32)
        woutt = w_out[l].T.astype(BF16)
        wr = jnp.concatenate([moe_w_expert[l].reshape(d, MOE_EXPERTS), moe_w_group[l]], axis=1)
        wrt = jnp.pad(wr, ((0, 0), (0, LANES - MOE_EXPERTS - MOE_GROUPS))).T.astype(F32)
        wrt_hi = wrt.astype(BF16)
        wrt_lo = (wrt - wrt_hi.astype(F32)).astype(BF16)
        br = jnp.concatenate([moe_b_expert[l].reshape(-1), moe_b_group[l]])
        br = jnp.pad(br, (0, LANES - MOE_EXPERTS - MOE_GROUPS)).reshape(LANES, 1).astype(F32)
        x2, route, gid = _post(x, y_ssm, y_mla, row(out_norm_mla_g[l]), woutt, col(norm_xattn_g[l]),
                          xattn_w_q[l].T.astype(BF16), km, vm, xattn_w_o[l].T.astype(BF16), col(norm_moe_g[l]),
                          wrt_hi, wrt_lo, br, tm=tiles.post)

        meta = _route_metadata(gid.reshape(bsz, seq), tiles.moe)
        xs, gs = _dispatch(x2, route, meta["pos"], meta["n_pad"], tt=tiles.copy)
        ys = _moe(xs, gs, row(norm_moe_g[l]), meta["tile_g"], meta["n_valid"], moe_w_gate[l].astype(BF16),
                  moe_w_up[l].astype(BF16), moe_w_down[l].astype(BF16), tm=tiles.moe)
        x = _combine(ys, meta["pos"], x2, row(norm_final_g), tt=tiles.copy)
    return x
```

```python
import functools
import math
from typing import NamedTuple

import jax
import jax.numpy as jnp
from jax import lax
from jax.experimental import pallas as pl
from jax.experimental.pallas import tpu as pltpu

F32 = jnp.float32
BF16 = jnp.bfloat16

EPS = 1e-6
NEG_INF = -1e30
CHUNK = 64

LANES = 128
SSM_GROUP = 16
SSM_STATE = 64
MLA_HEADS = 8
MLA_NOPE = 64
MLA_ROPE = 32
MLA_QK = MLA_NOPE + MLA_ROPE
MLA_V = 64
ROPE_THETA = 10000.0
XATTN_HEADS = 4
MOE_GROUPS = 4
MOE_PER_GROUP = 8
MOE_EXPERTS = MOE_GROUPS * MOE_PER_GROUP
VMEM_LIMIT = 48 * 1024 * 1024
VMEM_LIMIT_LARGE = 56 * 1024 * 1024
LOG2E = math.log2(math.e)
VT_ROWS = 80


class Tiles(NamedTuple):
    proj: int
    ssm: int
    attn_q: int
    attn_k: int
    attn_heads: int
    attn_strip: int
    post: int
    moe: int
    copy: int


def _tiles(seq):
    return Tiles(proj=min(1024, seq), ssm=min(256, seq), attn_q=min(512, seq), attn_k=min(256, seq),
                 attn_heads=MLA_HEADS, attn_strip=32, post=min(1024, seq),
                 moe=256, copy=min(1024, seq))


def _dot(a, b):
    return jnp.dot(a, b, preferred_element_type=F32)


def _dot_nt(a, b):
    return lax.dot_general(a, b, (((1,), (1,)), ((), ())), preferred_element_type=F32)


def _rms(x, g):
    ms = jnp.mean(x * x, axis=-1, keepdims=True)
    return x * lax.rsqrt(ms + EPS) * g


def _params(*sem, vmem=VMEM_LIMIT):
    return pltpu.CompilerParams(dimension_semantics=sem, vmem_limit_bytes=vmem)


def _full(shape):
    zeros = (0,) * len(shape)
    return pl.BlockSpec(shape, lambda *_: zeros)


def _ssm_prep_kernel(lre_ref, lim_ref, ldt_ref, bre_ref, bim_ref, are_ref, aim_ref, bbre_ref, bbim_ref):
    lre = jnp.minimum(lre_ref[...], -1e-4)
    lim = lim_ref[...]
    dt = jnp.exp(ldt_ref[...])
    mag = jnp.exp(lre * dt)
    are = mag * jnp.cos(lim * dt)
    aim = mag * jnp.sin(lim * dt)
    are_ref[...] = are
    aim_ref[...] = aim
    nre = are - 1.0
    den = lre * lre + lim * lim
    fre = (nre * lre + aim * lim) / den
    fim = (aim * lre - nre * lim) / den
    bre = bre_ref[...]
    bim = bim_ref[...]
    bbre_ref[...] = fre * bre - fim * bim
    bbim_ref[...] = fre * bim + fim * bre


def _ssm_prep(lam_re, lam_im, log_dt, b_re, b_im):
    g, p = lam_re.shape
    hh = b_re.shape[-1]
    bt_re = jnp.transpose(b_re, (0, 2, 1))
    bt_im = jnp.transpose(b_im, (0, 2, 1))
    outs = pl.pallas_call(
        _ssm_prep_kernel,
        out_shape=[jax.ShapeDtypeStruct((g, 1, p), F32)] * 2 + [jax.ShapeDtypeStruct((g, hh, p), F32)] * 2,
        name="ssm_prep",
    )(lam_re.reshape(g, 1, p), lam_im.reshape(g, 1, p), log_dt.reshape(g, 1, 1), bt_re, bt_im)
    a_re, a_im, bb_re, bb_im = outs
    return a_re.reshape(g, p), a_im.reshape(g, p), bb_re, bb_im


def _proj_kernel(x_ref, pos_ref, gmix_ref, wu_ref, wcq_ref, wckv_ref, wkrt_ref, qg_ref, wqt_ref, kvg_ref,
                 wk_ref, wvt_ref, invf_ref, u_ref, q_ref, k_ref, v_ref):
    half = MLA_ROPE // 2
    h = _rms(x_ref[0], gmix_ref[...]).astype(BF16)
    u_ref[0] = _dot(h, wu_ref[...]).astype(BF16)
    cqn = _rms(_dot(h, wcq_ref[...]), qg_ref[...]).astype(BF16)
    ckvn = _rms(_dot(h, wckv_ref[...]), kvg_ref[...]).astype(BF16)

    ang = invf_ref[...] * pos_ref[0].astype(F32)
    cosv = jnp.cos(ang)
    sinv = jnp.sin(ang)

    def rot_t(blk):
        x1 = blk[MLA_NOPE:MLA_NOPE + half]
        x2 = blk[MLA_NOPE + half:MLA_QK]
        return jnp.concatenate([blk[:MLA_NOPE], x1 * cosv - x2 * sinv, x1 * sinv + x2 * cosv, blk[MLA_QK:]], axis=0)

    qt = _dot_nt(wqt_ref[...], cqn) * (MLA_QK ** -0.5 * LOG2E)
    vt = _dot_nt(wvt_ref[...], ckvn)
    kk = _dot(ckvn, wk_ref[...])
    kpe = rot_t(_dot_nt(wkrt_ref[...], h)).T
    ones_row = jnp.where(lax.broadcasted_iota(jnp.int32, (VT_ROWS, 1), 0) == MLA_V, 1.0, 0.0)
    for hh in range(MLA_HEADS):
        q_ref[0, hh] = rot_t(qt[hh * LANES:(hh + 1) * LANES]).astype(BF16)
        k_ref[0, hh] = (kk[:, hh * LANES:(hh + 1) * LANES] + kpe).astype(BF16)
        v_ref[0, hh] = (vt[hh * VT_ROWS:(hh + 1) * VT_ROWS] + ones_row).astype(BF16)


def _proj(x, pos, gmix, wu, wcq, wckv, wkrt, qg, wqt, kvg, wk, wvt, invf, tm):
    b, s, d = x.shape
    grid = (b, s // tm)
    tok = lambda n: pl.BlockSpec((1, tm, n), lambda i, j: (i, j, 0))
    head = pl.BlockSpec((1, MLA_HEADS, tm, LANES), lambda i, j: (i, 0, j, 0))
    head_t = pl.BlockSpec((1, MLA_HEADS, LANES, tm), lambda i, j: (i, 0, 0, j))
    head_vt = pl.BlockSpec((1, MLA_HEADS, VT_ROWS, tm), lambda i, j: (i, 0, 0, j))
    tshape = lambda n: jax.ShapeDtypeStruct((b, MLA_HEADS, n, s), BF16)
    return pl.pallas_call(
        _proj_kernel,
        grid=grid,
        in_specs=[tok(d), pl.BlockSpec((1, 1, tm), lambda i, j: (i, 0, j)), _full(gmix.shape), _full(wu.shape),
                  _full(wcq.shape), _full(wckv.shape), _full(wkrt.shape), _full(qg.shape), _full(wqt.shape),
                  _full(kvg.shape), _full(wk.shape), _full(wvt.shape), _full(invf.shape)],
        out_specs=[tok(wu.shape[1]), head_t, head, head_vt],
        out_shape=[jax.ShapeDtypeStruct((b, s, wu.shape[1]), BF16), tshape(LANES),
                   jax.ShapeDtypeStruct((b, MLA_HEADS, s, LANES), BF16), tshape(VT_ROWS)],
        compiler_params=_params("parallel", "parallel"),
        name="proj",
    )(x, pos, gmix, wu, wcq, wckv, wkrt, qg, wqt, kvg, wk, wvt, invf)


def _ssm_kernel(u_ref, bbd_ref, atab_ref, cmat_ref, dskip_ref, wglu_ref, bglu_ref, gout_ref, y_ref,
                xs_ref, st_ref, *, tt, slab, nb, lane_tiles):
    j = pl.program_id(0)
    half_w = u_ref.shape[-1] // 2

    @pl.when(j == 0)
    def _():
        st_ref[...] = jnp.zeros_like(st_ref)

    for hf in range(2):
        ub = u_ref[:, :, hf * half_w:(hf + 1) * half_w].reshape(nb * tt, half_w)
        bu = _dot(ub, bbd_ref[hf])
        for b in range(nb):
            r0 = (hf * nb + b) * slab
            for c in range(2 * lane_tiles):
                xs_ref[c, r0:r0 + tt, :] = bu[b * tt:(b + 1) * tt, c * LANES:(c + 1) * LANES]

    rows = 2 * nb
    group = 4
    for c0 in range(0, lane_tiles, group):
        cs = list(range(c0, c0 + group))
        a_re = [atab_ref[0, :, c * LANES:(c + 1) * LANES] for c in cs]
        a_im = [atab_ref[1, :, c * LANES:(c + 1) * LANES] for c in cs]
        init = tuple(st_ref[0, :, c * LANES:(c + 1) * LANES] for c in cs) + \
            tuple(st_ref[1, :, c * LANES:(c + 1) * LANES] for c in cs)

        def step(t, carry, cs=cs, a_re=a_re, a_im=a_im):
            new_re, new_im = [], []
            for i, c in enumerate(cs):
                x_re, x_im = carry[i], carry[group + i]
                idx = pl.ds(t, rows, stride=slab)
                n_re = a_re[i] * x_re - a_im[i] * x_im + xs_ref[c, idx, :]
                n_im = a_re[i] * x_im + a_im[i] * x_re + xs_ref[lane_tiles + c, idx, :]
                xs_ref[c, idx, :] = n_re
                xs_ref[lane_tiles + c, idx, :] = n_im
                new_re.append(n_re)
                new_im.append(n_im)
            return tuple(new_re) + tuple(new_im)

        fin = lax.fori_loop(0, tt, step, init, unroll=8)
        for i, c in enumerate(cs):
            st_ref[0, :, c * LANES:(c + 1) * LANES] = fin[i]
            st_ref[1, :, c * LANES:(c + 1) * LANES] = fin[group + i]

    ys = []
    for hf in range(2):
        xb = []
        for b in range(nb):
            r0 = (hf * nb + b) * slab
            xb.append(jnp.concatenate([xs_ref[c, r0:r0 + tt, :] for c in range(2 * lane_tiles)], axis=1))
        xh = jnp.concatenate(xb, axis=0).astype(BF16)
        ys.append(_dot(xh, cmat_ref[hf]))
    y = jnp.concatenate(ys, axis=1)
    u = u_ref[...].astype(F32).reshape(nb * tt, 2 * half_w)
    y = jax.nn.gelu(y + dskip_ref[...] * u)
    y = y * jax.nn.sigmoid(_dot(y.astype(BF16), wglu_ref[...]) + bglu_ref[...])
    y = _rms(y, gout_ref[...])
    y_ref[...] = y.reshape(nb, tt, 2 * half_w).astype(BF16)


def _ssm(u, bbd, atab, cmat, dskip, wglu, bglu, gout, tt):
    nb, s, dssm = u.shape
    lane_tiles = bbd.shape[-1] // (2 * LANES)
    slab = tt + 4
    kern = functools.partial(_ssm_kernel, tt=tt, slab=slab, nb=nb, lane_tiles=lane_tiles)
    return pl.pallas_call(
        kern,
        grid=(s // tt,),
        in_specs=[pl.BlockSpec((nb, tt, dssm), lambda j: (0, j, 0)), _full(bbd.shape), _full(atab.shape),
                  _full(cmat.shape), _full(dskip.shape), _full(wglu.shape), _full(bglu.shape), _full(gout.shape)],
        out_specs=pl.BlockSpec((nb, tt, dssm), lambda j: (0, j, 0)),
        out_shape=jax.ShapeDtypeStruct((nb, s, dssm), BF16),
        scratch_shapes=[pltpu.VMEM((2 * lane_tiles, 2 * nb * slab, LANES), F32),
                        pltpu.VMEM((2, 2 * nb, lane_tiles * LANES), F32)],
        compiler_params=_params("arbitrary"),
        name="ssm",
    )(u, bbd, atab, cmat, dskip, wglu, bglu, gout)


def _attn_kernel(qt_ref, k_ref, vt_ref, o_ref, s_ref, p_ref, m_ref, a_ref, acc_ref, *, tq, tk, hp, strip):
    qi = pl.program_id(2)
    m_ref[...] = jnp.full(m_ref.shape, NEG_INF, F32)
    acc_ref[...] = jnp.zeros(acc_ref.shape, F32)
    q_chunk = lax.broadcasted_iota(jnp.int32, (1, tq), 1) // CHUNK

    def tile(k0, diag_off):
        qs = slice(0 if diag_off is None else diag_off, tq)
        for hh in range(hp):
            s_ref[hh, :, qs] = _dot(k_ref[0, hh, pl.ds(k0, tk), :], qt_ref[0, hh, :, qs])

        def strip_of(hh, r):
            s = s_ref[hh, r:r + strip, qs]
            if diag_off is None:
                return s
            return jnp.where(q_chunk[:, qs] >= (diag_off + r) // CHUNK, s, NEG_INF)

        for hh in range(hp):
            mt = strip_of(hh, 0)
            for r in range(strip, tk, strip):
                mt = jnp.maximum(mt, strip_of(hh, r))
            m_old = m_ref[hh, :, qs]
            m_new = jnp.maximum(m_old, jnp.max(mt, axis=0, keepdims=True))
            a_ref[hh, :, qs] = jnp.exp2(m_old - m_new)
            m_ref[hh, :, qs] = m_new
            for r in range(0, tk, strip):
                p_ref[hh, r:r + strip, qs] = jnp.exp2(strip_of(hh, r) - m_new).astype(BF16)
        for hh in range(hp):
            pv = _dot(vt_ref[0, hh, :, pl.ds(k0, tk)], p_ref[hh, :, qs])
            acc_ref[hh, :, qs] = acc_ref[hh, :, qs] * a_ref[hh, :, qs] + pv

    def body(j, c):
        tile(pl.multiple_of(j * tk, tk), None)
        return c

    lax.fori_loop(0, qi * (tq // tk), body, 0)
    for off in range(0, tq, tk):
        tile(pl.multiple_of(qi * tq + off, tk), off)
    feat = lax.broadcasted_iota(jnp.int32, (VT_ROWS, 1), 0)
    pad = jnp.zeros((LANES - VT_ROWS, tq), F32)
    for hh in range(hp):
        acc = acc_ref[hh]
        out = acc * (1.0 / acc[MLA_V:MLA_V + 1, :])
        out = jnp.where(feat < MLA_V, out, 0.0)
        o_ref[0, hh] = jnp.concatenate([out, pad], axis=0).T.astype(BF16)


def _attn(qt, k, vt, tq, tk, hp, strip):
    b, h, s, _ = k.shape
    assert strip <= CHUNK and CHUNK % strip == 0 and tk % CHUNK == 0 and tq % tk == 0
    return pl.pallas_call(
        functools.partial(_attn_kernel, tq=tq, tk=tk, hp=hp, strip=strip),
        grid=(b, h // hp, s // tq),
        in_specs=[pl.BlockSpec((1, hp, LANES, tq), lambda i, j, t: (i, j, 0, t)),
                  pl.BlockSpec((1, hp, s, LANES), lambda i, j, t: (i, j, 0, 0)),
                  pl.BlockSpec((1, hp, VT_ROWS, s), lambda i, j, t: (i, j, 0, 0))],
        out_specs=pl.BlockSpec((1, hp, tq, LANES), lambda i, j, t: (i, j, t, 0)),
        out_shape=jax.ShapeDtypeStruct((b, h, s, LANES), BF16),
        scratch_shapes=[pltpu.VMEM((hp, tk, tq), F32), pltpu.VMEM((hp, tk, tq), BF16),
                        pltpu.VMEM((hp, 1, tq), F32), pltpu.VMEM((hp, 1, tq), F32),
                        pltpu.VMEM((hp, VT_ROWS, tq), F32)],
        compiler_params=_params("parallel", "parallel", "arbitrary", vmem=VMEM_LIMIT_LARGE),
        name="attn",
    )(qt, k, vt)


def _memkv_kernel(mem_ref, g_ref, wk_ref, wv_ref, k_ref, v_ref, *, hd):
    mn = _rms(mem_ref[0], g_ref[...]).astype(BF16)
    kk = _dot(mn, wk_ref[...])
    vv = _dot(mn, wv_ref[...])
    for hh in range(XATTN_HEADS):
        k_ref[0, hh] = kk[:, hh * hd:(hh + 1) * hd].astype(BF16)
        v_ref[0, hh] = vv[:, hh * hd:(hh + 1) * hd].T.astype(BF16)


def _memkv(mem, g, wk, wv):
    b, nm, d = mem.shape
    hd = d // XATTN_HEADS
    ospec = pl.BlockSpec((1, XATTN_HEADS, nm, hd), lambda i: (i, 0, 0, 0))
    return pl.pallas_call(
        functools.partial(_memkv_kernel, hd=hd),
        grid=(b,),
        in_specs=[pl.BlockSpec((1, nm, d), lambda i: (i, 0, 0)), _full(g.shape), _full(wk.shape), _full(wv.shape)],
        out_specs=[ospec, ospec],
        out_shape=[jax.ShapeDtypeStruct((b, XATTN_HEADS, nm, hd), BF16)] * 2,
        compiler_params=_params("parallel"),
        name="mem_kv",
    )(mem, g, wk, wv)


def _rms_t(xt, g_col):
    ms = jnp.mean(xt * xt, axis=0, keepdims=True)
    return xt * lax.rsqrt(ms + EPS) * g_col


def _post_kernel(x_ref, ys_ref, ym_ref, gm_ref, woutt_ref, gx_ref, wqt_ref, km_ref, vmt_ref, wot_ref,
                 gmoe_ref, wrht_ref, wrlt_ref, br_ref, x2_ref, route_ref, gid_ref, *, hd):
    heads = [ym_ref[0, hh].astype(F32) for hh in range(MLA_HEADS)]
    ym = jnp.concatenate([heads[k] + pltpu.roll(heads[k + 1], MLA_V, 1) for k in range(0, MLA_HEADS, 2)], axis=1)
    ymn = _rms(ym, gm_ref[...]).astype(BF16)
    y_mix = jnp.concatenate([ys_ref[0], ymn], axis=1)
    x1 = x_ref[0].T + _dot_nt(woutt_ref[...], y_mix)

    h2 = _rms_t(x1, gx_ref[...]).astype(BF16)
    qx = (_dot(wqt_ref[...], h2) * (hd ** -0.5)).astype(BF16)
    outs = []
    for hh in range(XATTN_HEADS):
        s = _dot(km_ref[0, hh], qx[hh * hd:(hh + 1) * hd])
        p = jnp.exp(s - jnp.max(s, axis=0, keepdims=True))
        inv = 1.0 / jnp.sum(p, axis=0, keepdims=True)
        outs.append(_dot(vmt_ref[0, hh], p.astype(BF16)) * inv)
    o = jnp.concatenate(outs, axis=0).astype(BF16)
    x2 = x1 + _dot(wot_ref[...], o)
    x2_ref[0] = x2.T

    h3 = _rms_t(x2, gmoe_ref[...])
    h_hi = h3.astype(BF16)
    h_lo = (h3 - h_hi.astype(F32)).astype(BF16)
    logits = (_dot(wrht_ref[...], h_hi) + _dot(wrlt_ref[...], h_hi) + _dot(wrht_ref[...], h_lo)) + br_ref[...]
    idx = lax.broadcasted_iota(jnp.int32, logits.shape, 0)
    is_g = (idx >= MOE_EXPERTS) & (idx < MOE_EXPERTS + MOE_GROUPS)
    gl = jnp.where(is_g, logits, NEG_INF)
    gmax = jnp.max(gl, axis=0, keepdims=True)
    g_w = 1.0 / jnp.sum(jnp.exp(gl - gmax), axis=0, keepdims=True)
    g_idx = jnp.min(jnp.where(gl == gmax, idx, 4 * LANES), axis=0, keepdims=True) - MOE_EXPERTS
    in_grp = (idx >= g_idx * MOE_PER_GROUP) & (idx < (g_idx + 1) * MOE_PER_GROUP)
    el = jnp.where(in_grp, logits, NEG_INF)
    v1 = jnp.max(el, axis=0, keepdims=True)
    i1 = jnp.min(jnp.where(el == v1, idx, 4 * LANES), axis=0, keepdims=True)
    el2 = jnp.where(idx == i1, NEG_INF, el)
    v2 = jnp.max(el2, axis=0, keepdims=True)
    i2 = jnp.min(jnp.where(el2 == v2, idx, 4 * LANES), axis=0, keepdims=True)
    e2 = jnp.exp(v2 - v1)
    w1 = g_w / (1.0 + e2)
    w2 = g_w * e2 / (1.0 + e2)
    route = jnp.where(idx == i1, w1, 0.0) + jnp.where(idx == i2, w2, 0.0)
    route_ref[0] = route.T
    gid_ref[0] = g_idx


def _post(x, ys, ym, gm, woutt, gx, wqt, km, vmt, wot, gmoe, wrht, wrlt, br, tm):
    b, s, d = x.shape
    hd = d // XATTN_HEADS
    tok = lambda n: pl.BlockSpec((1, tm, n), lambda i, j: (i, j, 0))
    mem = pl.BlockSpec((1,) + km.shape[1:], lambda i, j: (i, 0, 0, 0))
    return pl.pallas_call(
        functools.partial(_post_kernel, hd=hd),
        grid=(b, s // tm),
        in_specs=[tok(d), tok(ys.shape[-1]), pl.BlockSpec((1, MLA_HEADS, tm, LANES), lambda i, j: (i, 0, j, 0)),
                  _full(gm.shape), _full(woutt.shape), _full(gx.shape), _full(wqt.shape), mem, mem,
                  _full(wot.shape), _full(gmoe.shape), _full(wrht.shape), _full(wrlt.shape), _full(br.shape)],
        out_specs=[tok(d), tok(LANES), pl.BlockSpec((1, 1, tm), lambda i, j: (i, 0, j))],
        out_shape=[jax.ShapeDtypeStruct((b, s, d), F32), jax.ShapeDtypeStruct((b, s, LANES), F32),
                   jax.ShapeDtypeStruct((b, 1, s), jnp.int32)],
        compiler_params=_params("parallel", "parallel", vmem=VMEM_LIMIT_LARGE),
        name="post",
    )(x, ys, ym, gm, woutt, gx, wqt, km, vmt, wot, gmoe, wrht, wrlt, br)


def _route_metadata(gid, tm):
    b, s = gid.shape
    n_tiles = (s + MOE_GROUPS * tm) // tm
    groups = jnp.arange(MOE_GROUPS, dtype=jnp.int32)[None, :, None]
    onehot = (gid[:, None, :] == groups).astype(jnp.int32)
    csum = jnp.cumsum(onehot, axis=2)
    cnt = csum[:, :, -1]
    padded = (cnt + tm - 1) // tm * tm
    seg_end = jnp.cumsum(padded, axis=-1)
    dest = jnp.sum(onehot * ((seg_end - padded)[:, :, None] + csum - 1), axis=1)
    tile_start = jnp.arange(n_tiles, dtype=jnp.int32) * tm
    tile_g = jnp.sum((seg_end[:, None, :] <= tile_start[None, :, None]).astype(jnp.int32), axis=-1)
    tile_g = jnp.minimum(tile_g, MOE_GROUPS - 1)
    n_valid = (seg_end[:, -1] // tm).astype(jnp.int32)
    return dict(pos=dest, tile_g=tile_g.reshape(b * n_tiles), n_valid=n_valid, n_pad=n_tiles * tm)


def _dispatch_kernel(h_ref, g_ref, dst_ref, xs_ref, gs_ref, *, tt):
    @pl.when(pl.program_id(1) == 0)
    def _():
        xs_ref[...] = jnp.zeros(xs_ref.shape, xs_ref.dtype)
        gs_ref[...] = jnp.zeros(gs_ref.shape, gs_ref.dtype)

    for t in range(tt):
        r = dst_ref[0, 0, t]
        xs_ref[0, pl.ds(r, 1), :] = h_ref[0, t:t + 1, :]
        gs_ref[0, pl.ds(r, 1), :] = g_ref[0, t:t + 1, :]


def _dispatch(x2, route, pos, n_pad, tt):
    b, s, d = x2.shape
    nj = s // tt
    row_block = lambda n: pl.BlockSpec((1, n_pad, n), lambda i, j: (i, 0, 0), pipeline_mode=pl.Buffered(1))
    return pl.pallas_call(
        functools.partial(_dispatch_kernel, tt=tt),
        grid=(b, nj),
        in_specs=[pl.BlockSpec((1, tt, d), lambda i, j: (i, j, 0)),
                  pl.BlockSpec((1, tt, LANES), lambda i, j: (i, j, 0)),
                  pl.BlockSpec((1, 1, tt), lambda i, j: (i * nj + j, 0, 0), memory_space=pltpu.SMEM)],
        out_specs=[row_block(d), row_block(LANES)],
        out_shape=[jax.ShapeDtypeStruct((b, n_pad, d), F32), jax.ShapeDtypeStruct((b, n_pad, LANES), F32)],
        compiler_params=_params("parallel", "arbitrary", vmem=VMEM_LIMIT_LARGE),
        name="dispatch",
    )(x2, route, pos.reshape(b * nj, 1, tt))


def _moe_kernel(tg_ref, nv_ref, x_ref, gate_ref, gmoe_ref, wg_ref, wu_ref, wd_ref, o_ref, *, nt):
    bi = pl.program_id(0)
    ti = pl.program_id(1)
    grp = tg_ref[bi * nt + ti]

    @pl.when(ti < nv_ref[bi])
    def _():
        h = _rms(x_ref[0], gmoe_ref[...]).astype(BF16)
        gate = gate_ref[0]
        lane = lax.broadcasted_iota(jnp.int32, gate.shape, 1)
        acts = []
        for e in range(MOE_PER_GROUP):
            a = jax.nn.silu(_dot(h, wg_ref[e])) * _dot(h, wu_ref[e])
            gcol = jnp.sum(jnp.where(lane == grp * MOE_PER_GROUP + e, gate, 0.0), axis=-1, keepdims=True)
            acts.append((a * gcol).astype(BF16))
        wd = wd_ref[...]
        o_ref[0] = _dot(jnp.concatenate(acts, axis=1), wd.reshape(wd.shape[0] * wd.shape[1], wd.shape[2]))

    @pl.when(ti >= nv_ref[bi])
    def _():
        o_ref[0] = jnp.zeros(o_ref.shape[1:], o_ref.dtype)


def _moe(xs, gs, gmoe, tile_g, n_valid, wg, wu, wd, tm):
    b, n_pad, d = xs.shape
    nt = n_pad // tm
    ff = wg.shape[-1]
    group = lambda i, j, tg, nv: (tg[i * nt + j], 0, 0)
    tok = lambda n: pl.BlockSpec((1, tm, n), lambda i, j, tg, nv: (i, j, 0))
    grid_spec = pltpu.PrefetchScalarGridSpec(
        num_scalar_prefetch=2,
        grid=(b, nt),
        in_specs=[tok(d), tok(LANES), pl.BlockSpec(gmoe.shape, lambda i, j, tg, nv: (0, 0)),
                  pl.BlockSpec((MOE_PER_GROUP, d, ff), group),
                  pl.BlockSpec((MOE_PER_GROUP, d, ff), group), pl.BlockSpec((MOE_PER_GROUP, ff, d), group)],
        out_specs=tok(d),
    )
    return pl.pallas_call(
        functools.partial(_moe_kernel, nt=nt),
        grid_spec=grid_spec,
        out_shape=jax.ShapeDtypeStruct((b, n_pad, d), F32),
        compiler_params=_params("parallel", "arbitrary"),
        name="moe",
    )(tile_g, n_valid, xs, gs, gmoe, wg, wu, wd)


def _combine_kernel(ys_ref, pos_ref, x2_ref, gf_ref, o_ref, buf_ref, *, tt):
    for t in range(tt):
        buf_ref[t:t + 1, :] = ys_ref[0, pl.ds(pos_ref[0, 0, t], 1), :]
    o_ref[0] = _rms(x2_ref[0] + buf_ref[...], gf_ref[...])


def _combine(ys, pos, x2, gf, tt):
    b, s, d = x2.shape
    n_pad = ys.shape[1]
    nj = s // tt
    return pl.pallas_call(
        functools.partial(_combine_kernel, tt=tt),
        grid=(b, nj),
        in_specs=[pl.BlockSpec((1, n_pad, d), lambda i, j: (i, 0, 0), pipeline_mode=pl.Buffered(1)),
                  pl.BlockSpec((1, 1, tt), lambda i, j: (i * nj + j, 0, 0), memory_space=pltpu.SMEM),
                  pl.BlockSpec((1, tt, d), lambda i, j: (i, j, 0)), _full(gf.shape)],
        out_specs=pl.BlockSpec((1, tt, d), lambda i, j: (i, j, 0)),
        out_shape=jax.ShapeDtypeStruct((b, s, d), F32),
        scratch_shapes=[pltpu.VMEM((tt, d), F32)],
        compiler_params=_params("parallel", "arbitrary", vmem=VMEM_LIMIT_LARGE),
        name="combine",
    )(ys, pos.reshape(b * nj, 1, tt), x2, gf)


def _pad_heads(w, per_head):
    k = w.shape[0]
    w = w.reshape(k, MLA_HEADS, per_head)
    w = jnp.pad(w, ((0, 0), (0, 0), (0, LANES - per_head)))
    return w.reshape(k, MLA_HEADS * LANES)


def _block_diag(blocks):
    n, r, c = blocks.shape
    eye = jnp.eye(n, dtype=blocks.dtype)
    return (eye[:, None, :, None] * blocks[:, :, None, :]).reshape(n * r, n * c)


def kernel(x, mem, positions, norm_mix_g, w_in, ssm_lam_re, ssm_lam_im, ssm_log_dt, ssm_b_re, ssm_b_im, ssm_c_re, ssm_c_im, ssm_d, ssm_w_glu, ssm_b_glu, mla_q_norm_g, mla_w_q_up, mla_kv_norm_g, mla_w_kv_up, out_norm_ssm_g, out_norm_mla_g, w_out, norm_xattn_g, norm_mem_g, xattn_w_q, xattn_w_k, xattn_w_v, xattn_w_o, norm_moe_g, moe_w_group, moe_b_group, moe_w_expert, moe_b_expert, moe_w_gate, moe_w_up, moe_w_down, norm_final_g):
    bsz, seq, d = x.shape
    depth = w_in.shape[0]
    assert depth == 1, "the final RMSNorm is fused into the last stage of a single layer"
    d_ssm = ssm_d.shape[-1]
    q_rank = mla_q_norm_g.shape[-1]
    kv_rank = mla_kv_norm_g.shape[-1]
    n_grp = d_ssm // SSM_GROUP
    s1, s2, s3 = d_ssm, d_ssm + q_rank, d_ssm + q_rank + kv_rank
    row = lambda v: v.reshape(1, -1).astype(F32)
    tiles = _tiles(seq)

    half = MLA_ROPE // 2
    invf = (ROPE_THETA ** (-jnp.arange(half, dtype=F32) / half)).reshape(half, 1)
    pos = positions.reshape(bsz, 1, seq)

    for l in range(depth):
        a_re, a_im, bb_re, bb_im = _ssm_prep(ssm_lam_re[l], ssm_lam_im[l], ssm_log_dt[l], ssm_b_re[l], ssm_b_im[l])
        gh = n_grp // 2
        bbd = jnp.stack([
            jnp.concatenate([_block_diag(bb_re[hf * gh:(hf + 1) * gh]), _block_diag(bb_im[hf * gh:(hf + 1) * gh])],
                            axis=1) for hf in range(2)]).astype(BF16)
        c_re_t = jnp.transpose(ssm_c_re[l], (0, 2, 1))
        c_im_t = jnp.transpose(ssm_c_im[l], (0, 2, 1))
        cmat = jnp.stack([
            jnp.concatenate([_block_diag(c_re_t[hf * gh:(hf + 1) * gh]), -_block_diag(c_im_t[hf * gh:(hf + 1) * gh])],
                            axis=0) for hf in range(2)]).astype(BF16)
        atab = jnp.stack([
            jnp.repeat(arr.reshape(2, 1, gh * SSM_STATE), bsz, axis=1).reshape(2 * bsz, gh * SSM_STATE)
            for arr in (a_re, a_im)])

        wi = w_in[l]
        wu = wi[:, :s1].astype(BF16)
        wcq = wi[:, s1:s2].astype(BF16)
        wckv = wi[:, s2:s3].astype(BF16)
        wkrt = jnp.pad(wi[:, s3:], ((0, 0), (MLA_NOPE, LANES - MLA_QK))).T.astype(BF16)
        wqt = _pad_heads(mla_w_q_up[l], MLA_QK).T.astype(BF16)
        wkv = mla_w_kv_up[l].reshape(kv_rank, MLA_HEADS, MLA_NOPE + MLA_V)
        wk = _pad_heads(wkv[:, :, :MLA_NOPE].reshape(kv_rank, -1), MLA_NOPE).astype(BF16)
        wvt = jnp.pad(wkv[:, :, MLA_NOPE:], ((0, 0), (0, 0), (0, VT_ROWS - MLA_V)))
        wvt = wvt.reshape(kv_rank, MLA_HEADS * VT_ROWS).T.astype(BF16)
        u, q, k, v = _proj(x, pos, row(norm_mix_g[l]), wu, wcq, wckv, wkrt, row(mla_q_norm_g[l]), wqt,
                           row(mla_kv_norm_g[l]), wk, wvt, invf, tm=tiles.proj)

        y_ssm = _ssm(u, bbd, atab, cmat, row(ssm_d[l]), ssm_w_glu[l].astype(BF16), row(ssm_b_glu[l]),
                     row(out_norm_ssm_g[l]), tt=tiles.ssm)
        y_mla = _attn(q, k, v, tq=tiles.attn_q, tk=tiles.attn_k, hp=tiles.attn_heads, strip=tiles.attn_strip)

        km, vm = _memkv(mem, row(norm_mem_g[l]), xattn_w_k[l].astype(BF16), xattn_w_v[l].astype(BF16))
        col = lambda v: v.reshape(-1, 1).astype(F32)
        woutt = w_out[l].T.astype(BF16)
        wr = jnp.concatenate([moe_w_expert[l].reshape(d, MOE_EXPERTS), moe_w_group[l]], axis=1)
        wrt = jnp.pad(wr, ((0, 0), (0, LANES - MOE_EXPERTS - MOE_GROUPS))).T.astype(F32)
        wrt_hi = wrt.astype(BF16)
        wrt_lo = (wrt - wrt_hi.astype(F32)).astype(BF16)
        br = jnp.concatenate([moe_b_expert[l].reshape(-1), moe_b_group[l]])
        br = jnp.pad(br, (0, LANES - MOE_EXPERTS - MOE_GROUPS)).reshape(LANES, 1).astype(F32)
        x2, route, gid = _post(x, y_ssm, y_mla, row(out_norm_mla_g[l]), woutt, col(norm_xattn_g[l]),
                          xattn_w_q[l].T.astype(BF16), km, vm, xattn_w_o[l].T.astype(BF16), col(norm_moe_g[l]),
                          wrt_hi, wrt_lo, br, tm=tiles.post)

        meta = _route_metadata(gid.reshape(bsz, seq), tiles.moe)
        xs, gs = _dispatch(x2, route, meta["pos"], meta["n_pad"], tt=tiles.copy)
        ys = _moe(xs, gs, row(norm_moe_g[l]), meta["tile_g"], meta["n_valid"], moe_w_gate[l].astype(BF16),
                  moe_w_up[l].astype(BF16), moe_w_down[l].astype(BF16), tm=tiles.moe)
        x = _combine(ys, meta["pos"], x2, row(norm_final_g), tt=tiles.copy)
    return x
```

```python
import functools
import math
from typing import NamedTuple

import jax
import jax.numpy as jnp
from jax import lax
from jax.experimental import pallas as pl
from jax.experimental.pallas import tpu as pltpu

F32 = jnp.float32
BF16 = jnp.bfloat16

EPS = 1e-6
NEG_INF = -1e30
CHUNK = 64

LANES = 128
SSM_GROUP = 16
SSM_STATE = 64
MLA_HEADS = 8
MLA_NOPE = 64
MLA_ROPE = 32
MLA_QK = MLA_NOPE + MLA_ROPE
MLA_V = 64
ROPE_THETA = 10000.0
XATTN_HEADS = 4
MOE_GROUPS = 4
MOE_PER_GROUP = 8
MOE_EXPERTS = MOE_GROUPS * MOE_PER_GROUP
VMEM_LIMIT = 48 * 1024 * 1024
VMEM_LIMIT_LARGE = 56 * 1024 * 1024
LOG2E = math.log2(math.e)
VT_ROWS = 80


class Tiles(NamedTuple):
    proj: int
    ssm: int
    attn_q: int
    attn_k: int
    attn_heads: int
    attn_strip: int
    post: int
    moe: int
    copy: int


def _tiles(seq):
    return Tiles(proj=min(1024, seq), ssm=min(256, seq), attn_q=min(512, seq), attn_k=min(256, seq),
                 attn_heads=MLA_HEADS, attn_strip=32, post=min(1024, seq),
                 moe=256, copy=min(1024, seq))


def _dot(a, b):
    return jnp.dot(a, b, preferred_element_type=F32)


def _dot_nt(a, b):
    return lax.dot_general(a, b, (((1,), (1,)), ((), ())), preferred_element_type=F32)


def _rms(x, g):
    ms = jnp.mean(x * x, axis=-1, keepdims=True)
    return x * lax.rsqrt(ms + EPS) * g


def _params(*sem, vmem=VMEM_LIMIT):
    return pltpu.CompilerParams(dimension_semantics=sem, vmem_limit_bytes=vmem)


def _full(shape):
    zeros = (0,) * len(shape)
    return pl.BlockSpec(shape, lambda *_: zeros)


def _ssm_prep_kernel(lre_ref, lim_ref, ldt_ref, bre_ref, bim_ref, are_ref, aim_ref, bbre_ref, bbim_ref):
    lre = jnp.minimum(lre_ref[...], -1e-4)
    lim = lim_ref[...]
    dt = jnp.exp(ldt_ref[...])
    mag = jnp.exp(lre * dt)
    are = mag * jnp.cos(lim * dt)
    aim = mag * jnp.sin(lim * dt)
    are_ref[...] = are
    aim_ref[...] = aim
    nre = are - 1.0
    den = lre * lre + lim * lim
    fre = (nre * lre + aim * lim) / den
    fim = (aim * lre - nre * lim) / den
    bre = bre_ref[...]
    bim = bim_ref[...]
    bbre_ref[...] = fre * bre - fim * bim
    bbim_ref[...] = fre * bim + fim * bre


def _ssm_prep(lam_re, lam_im, log_dt, b_re, b_im):
    g, p = lam_re.shape
    hh = b_re.shape[-1]
    bt_re = jnp.transpose(b_re, (0, 2, 1))
    bt_im = jnp.transpose(b_im, (0, 2, 1))
    outs = pl.pallas_call(
        _ssm_prep_kernel,
        out_shape=[jax.ShapeDtypeStruct((g, 1, p), F32)] * 2 + [jax.ShapeDtypeStruct((g, hh, p), F32)] * 2,
        name="ssm_prep",
    )(lam_re.reshape(g, 1, p), lam_im.reshape(g, 1, p), log_dt.reshape(g, 1, 1), bt_re, bt_im)
    a_re, a_im, bb_re, bb_im = outs
    return a_re.reshape(g, p), a_im.reshape(g, p), bb_re, bb_im


def _proj_kernel(x_ref, pos_ref, gmix_ref, wu_ref, wcq_ref, wckv_ref, wkrt_ref, qg_ref, wqt_ref, kvg_ref,
                 wk_ref, wvt_ref, invf_ref, u_ref, q_ref, k_ref, v_ref):
    half = MLA_ROPE // 2
    h = _rms(x_ref[0], gmix_ref[...]).astype(BF16)
    u_ref[0] = _dot(h, wu_ref[...]).astype(BF16)
    cqn = _rms(_dot(h, wcq_ref[...]), qg_ref[...]).astype(BF16)
    ckvn = _rms(_dot(h, wckv_ref[...]), kvg_ref[...]).astype(BF16)

    ang = invf_ref[...] * pos_ref[0].astype(F32)
    cosv = jnp.cos(ang)
    sinv = jnp.sin(ang)

    def rot_t(blk):
        x1 = blk[MLA_NOPE:MLA_NOPE + half]
        x2 = blk[MLA_NOPE + half:MLA_QK]
        return jnp.concatenate([blk[:MLA_NOPE], x1 * cosv - x2 * sinv, x1 * sinv + x2 * cosv, blk[MLA_QK:]], axis=0)

    qt = _dot_nt(wqt_ref[...], cqn) * (MLA_QK ** -0.5 * LOG2E)
    vt = _dot_nt(wvt_ref[...], ckvn)
    kk = _dot(ckvn, wk_ref[...])
    kpe = rot_t(_dot_nt(wkrt_ref[...], h)).T
    ones_row = jnp.where(lax.broadcasted_iota(jnp.int32, (VT_ROWS, 1), 0) == MLA_V, 1.0, 0.0)
    for hh in range(MLA_HEADS):
        q_ref[0, hh] = rot_t(qt[hh * LANES:(hh + 1) * LANES]).astype(BF16)
        k_ref[0, hh] = (kk[:, hh * LANES:(hh + 1) * LANES] + kpe).astype(BF16)
        v_ref[0, hh] = (vt[hh * VT_ROWS:(hh + 1) * VT_ROWS] + ones_row).astype(BF16)


def _proj(x, pos, gmix, wu, wcq, wckv, wkrt, qg, wqt, kvg, wk, wvt, invf, tm):
    b, s, d = x.shape
    grid = (b, s // tm)
    tok = lambda n: pl.BlockSpec((1, tm, n), lambda i, j: (i, j, 0))
    head = pl.BlockSpec((1, MLA_HEADS, tm, LANES), lambda i, j: (i, 0, j, 0))
    head_t = pl.BlockSpec((1, MLA_HEADS, LANES, tm), lambda i, j: (i, 0, 0, j))
    head_vt = pl.BlockSpec((1, MLA_HEADS, VT_ROWS, tm), lambda i, j: (i, 0, 0, j))
    tshape = lambda n: jax.ShapeDtypeStruct((b, MLA_HEADS, n, s), BF16)
    return pl.pallas_call(
        _proj_kernel,
        grid=grid,
        in_specs=[tok(d), pl.BlockSpec((1, 1, tm), lambda i, j: (i, 0, j)), _full(gmix.shape), _full(wu.shape),
                  _full(wcq.shape), _full(wckv.shape), _full(wkrt.shape), _full(qg.shape), _full(wqt.shape),
                  _full(kvg.shape), _full(wk.shape), _full(wvt.shape), _full(invf.shape)],
        out_specs=[tok(wu.shape[1]), head_t, head, head_vt],
        out_shape=[jax.ShapeDtypeStruct((b, s, wu.shape[1]), BF16), tshape(LANES),
                   jax.ShapeDtypeStruct((b, MLA_HEADS, s, LANES), BF16), tshape(VT_ROWS)],
        compiler_params=_params("parallel", "parallel"),
        name="proj",
    )(x, pos, gmix, wu, wcq, wckv, wkrt, qg, wqt, kvg, wk, wvt, invf)


def _ssm_kernel(u_ref, bbd_ref, atab_ref, cmat_ref, dskip_ref, wglu_ref, bglu_ref, gout_ref, y_ref,
                xs_ref, st_ref, *, tt, slab, nb, lane_tiles):
    j = pl.program_id(0)
    half_w = u_ref.shape[-1] // 2

    @pl.when(j == 0)
    def _():
        st_ref[...] = jnp.zeros_like(st_ref)

    for hf in range(2):
        ub = u_ref[:, :, hf * half_w:(hf + 1) * half_w].reshape(nb * tt, half_w)
        bu = _dot(ub, bbd_ref[hf])
        for b in range(nb):
            r0 = (hf * nb + b) * slab
            for c in range(2 * lane_tiles):
                xs_ref[c, r0:r0 + tt, :] = bu[b * tt:(b + 1) * tt, c * LANES:(c + 1) * LANES]

    rows = 2 * nb
    group = 4
    for c0 in range(0, lane_tiles, group):
        cs = list(range(c0, c0 + group))
        a_re = [atab_ref[0, :, c * LANES:(c + 1) * LANES] for c in cs]
        a_im = [atab_ref[1, :, c * LANES:(c + 1) * LANES] for c in cs]
        init = tuple(st_ref[0, :, c * LANES:(c + 1) * LANES] for c in cs) + \
            tuple(st_ref[1, :, c * LANES:(c + 1) * LANES] for c in cs)

        def step(t, carry, cs=cs, a_re=a_re, a_im=a_im):
            new_re, new_im = [], []
            for i, c in enumerate(cs):
                x_re, x_im = carry[i], carry[group + i]
                idx = pl.ds(t, rows, stride=slab)
                n_re = a_re[i] * x_re - a_im[i] * x_im + xs_ref[c, idx, :]
                n_im = a_re[i] * x_im + a_im[i] * x_re + xs_ref[lane_tiles + c, idx, :]
                xs_ref[c, idx, :] = n_re
                xs_ref[lane_tiles + c, idx, :] = n_im
                new_re.append(n_re)
                new_im.append(n_im)
            return tuple(new_re) + tuple(new_im)

        fin = lax.fori_loop(0, tt, step, init, unroll=8)
        for i, c in enumerate(cs):
            st_ref[0, :, c * LANES:(c + 1) * LANES] = fin[i]
            st_ref[1, :, c * LANES:(c + 1) * LANES] = fin[group + i]

    ys = []
    for hf in range(2):
        xb = []
        for b in range(nb):
            r0 = (hf * nb + b) * slab
            xb.append(jnp.concatenate([xs_ref[c, r0:r0 + tt, :] for c in range(2 * lane_tiles)], axis=1))
        xh = jnp.concatenate(xb, axis=0).astype(BF16)
        ys.append(_dot(xh, cmat_ref[hf]))
    y = jnp.concatenate(ys, axis=1)
    u = u_ref[...].astype(F32).reshape(nb * tt, 2 * half_w)
    y = jax.nn.gelu(y + dskip_ref[...] * u)
    y = y * jax.nn.sigmoid(_dot(y.astype(BF16), wglu_ref[...]) + bglu_ref[...])
    y = _rms(y, gout_ref[...])
    y_ref[...] = y.reshape(nb, tt, 2 * half_w).astype(BF16)


def _ssm(u, bbd, atab, cmat, dskip, wglu, bglu, gout, tt):
    nb, s, dssm = u.shape
    lane_tiles = bbd.shape[-1] // (2 * LANES)
    slab = tt + 4
    kern = functools.partial(_ssm_kernel, tt=tt, slab=slab, nb=nb, lane_tiles=lane_tiles)
    return pl.pallas_call(
        kern,
        grid=(s // tt,),
        in_specs=[pl.BlockSpec((nb, tt, dssm), lambda j: (0, j, 0)), _full(bbd.shape), _full(atab.shape),
                  _full(cmat.shape), _full(dskip.shape), _full(wglu.shape), _full(bglu.shape), _full(gout.shape)],
        out_specs=pl.BlockSpec((nb, tt, dssm), lambda j: (0, j, 0)),
        out_shape=jax.ShapeDtypeStruct((nb, s, dssm), BF16),
        scratch_shapes=[pltpu.VMEM((2 * lane_tiles, 2 * nb * slab, LANES), F32),
                        pltpu.VMEM((2, 2 * nb, lane_tiles * LANES), F32)],
        compiler_params=_params("arbitrary"),
        name="ssm",
    )(u, bbd, atab, cmat, dskip, wglu, bglu, gout)


def _attn_kernel(qt_ref, k_ref, vt_ref, o_ref, s_ref, p_ref, m_ref, a_ref, acc_ref, *, tq, tk, hp, strip):
    qi = pl.program_id(2)
    m_ref[...] = jnp.full(m_ref.shape, NEG_INF, F32)
    acc_ref[...] = jnp.zeros(acc_ref.shape, F32)
    q_chunk = lax.broadcasted_iota(jnp.int32, (1, tq), 1) // CHUNK

    def tile(k0, diag_off):
        qs = slice(0 if diag_off is None else diag_off, tq)
        for hh in range(hp):
            s_ref[hh, :, qs] = _dot(k_ref[0, hh, pl.ds(k0, tk), :], qt_ref[0, hh, :, qs])

        def strip_of(hh, r):
            s = s_ref[hh, r:r + strip, qs]
            if diag_off is None:
                return s
            return jnp.where(q_chunk[:, qs] >= (diag_off + r) // CHUNK, s, NEG_INF)

        for hh in range(hp):
            mt = strip_of(hh, 0)
            for r in range(strip, tk, strip):
                mt = jnp.maximum(mt, strip_of(hh, r))
            m_old = m_ref[hh, :, qs]
            m_new = jnp.maximum(m_old, jnp.max(mt, axis=0, keepdims=True))
            a_ref[hh, :, qs] = jnp.exp2(m_old - m_new)
            m_ref[hh, :, qs] = m_new
            for r in range(0, tk, strip):
                p_ref[hh, r:r + strip, qs] = jnp.exp2(strip_of(hh, r) - m_new).astype(BF16)
        for hh in range(hp):
            pv = _dot(vt_ref[0, hh, :, pl.ds(k0, tk)], p_ref[hh, :, qs])
            acc_ref[hh, :, qs] = acc_ref[hh, :, qs] * a_ref[hh, :, qs] + pv

    def body(j, c):
        tile(pl.multiple_of(j * tk, tk), None)
        return c

    lax.fori_loop(0, qi * (tq // tk), body, 0)
    for off in range(0, tq, tk):
        tile(pl.multiple_of(qi * tq + off, tk), off)
    feat = lax.broadcasted_iota(jnp.int32, (VT_ROWS, 1), 0)
    pad = jnp.zeros((LANES - VT_ROWS, tq), F32)
    for hh in range(hp):
        acc = acc_ref[hh]
        out = acc * (1.0 / acc[MLA_V:MLA_V + 1, :])
        out = jnp.where(feat < MLA_V, out, 0.0)
        o_ref[0, hh] = jnp.concatenate([out, pad], axis=0).T.astype(BF16)


def _attn(qt, k, vt, tq, tk, hp, strip):
    b, h, s, _ = k.shape
    assert strip <= CHUNK and CHUNK % strip == 0 and tk % CHUNK == 0 and tq % tk == 0
    return pl.pallas_call(
        functools.partial(_attn_kernel, tq=tq, tk=tk, hp=hp, strip=strip),
        grid=(b, h // hp, s // tq),
        in_specs=[pl.BlockSpec((1, hp, LANES, tq), lambda i, j, t: (i, j, 0, t)),
                  pl.BlockSpec((1, hp, s, LANES), lambda i, j, t: (i, j, 0, 0)),
                  pl.BlockSpec((1, hp, VT_ROWS, s), lambda i, j, t: (i, j, 0, 0))],
        out_specs=pl.BlockSpec((1, hp, tq, LANES), lambda i, j, t: (i, j, t, 0)),
        out_shape=jax.ShapeDtypeStruct((b, h, s, LANES), BF16),
        scratch_shapes=[pltpu.VMEM((hp, tk, tq), F32), pltpu.VMEM((hp, tk, tq), BF16),
                        pltpu.VMEM((hp, 1, tq), F32), pltpu.VMEM((hp, 1, tq), F32),
                        pltpu.VMEM((hp, VT_ROWS, tq), F32)],
        compiler_params=_params("parallel", "parallel", "arbitrary", vmem=VMEM_LIMIT_LARGE),
        name="attn",
    )(qt, k, vt)


def _memkv_kernel(mem_ref, g_ref, wk_ref, wv_ref, k_ref, v_ref, *, hd):
    mn = _rms(mem_ref[0], g_ref[...]).astype(BF16)
    kk = _dot(mn, wk_ref[...])
    vv = _dot(mn, wv_ref[...])
    for hh in range(XATTN_HEADS):
        k_ref[0, hh] = kk[:, hh * hd:(hh + 1) * hd].astype(BF16)
        v_ref[0, hh] = vv[:, hh * hd:(hh + 1) * hd].T.astype(BF16)


def _memkv(mem, g, wk, wv):
    b, nm, d = mem.shape
    hd = d // XATTN_HEADS
    ospec = pl.BlockSpec((1, XATTN_HEADS, nm, hd), lambda i: (i, 0, 0, 0))
    return pl.pallas_call(
        functools.partial(_memkv_kernel, hd=hd),
        grid=(b,),
        in_specs=[pl.BlockSpec((1, nm, d), lambda i: (i, 0, 0)), _full(g.shape), _full(wk.shape), _full(wv.shape)],
        out_specs=[ospec, ospec],
        out_shape=[jax.ShapeDtypeStruct((b, XATTN_HEADS, nm, hd), BF16)] * 2,
        compiler_params=_params("parallel"),
        name="mem_kv",
    )(mem, g, wk, wv)


def _rms_t(xt, g_col):
    ms = jnp.mean(xt * xt, axis=0, keepdims=True)
    return xt * lax.rsqrt(ms + EPS) * g_col


def _post_kernel(x_ref, ys_ref, ym_ref, gm_ref, woutt_ref, gx_ref, wqt_ref, km_ref, vmt_ref, wot_ref,
                 gmoe_ref, wrht_ref, wrlt_ref, br_ref, x2_ref, route_ref, gid_ref, *, hd):
    heads = [ym_ref[0, hh].astype(F32) for hh in range(MLA_HEADS)]
    ym = jnp.concatenate([heads[k] + pltpu.roll(heads[k + 1], MLA_V, 1) for k in range(0, MLA_HEADS, 2)], axis=1)
    ymn = _rms(ym, gm_ref[...]).astype(BF16)
    y_mix = jnp.concatenate([ys_ref[0], ymn], axis=1)
    x1 = x_ref[0].T + _dot_nt(woutt_ref[...], y_mix)

    h2 = _rms_t(x1, gx_ref[...]).astype(BF16)
    qx = (_dot(wqt_ref[...], h2) * (hd ** -0.5)).astype(BF16)
    outs = []
    for hh in range(XATTN_HEADS):
        s = _dot(km_ref[0, hh], qx[hh * hd:(hh + 1) * hd])
        p = jnp.exp(s - jnp.max(s, axis=0, keepdims=True))
        inv = 1.0 / jnp.sum(p, axis=0, keepdims=True)
        outs.append(_dot(vmt_ref[0, hh], p.astype(BF16)) * inv)
    o = jnp.concatenate(outs, axis=0).astype(BF16)
    x2 = x1 + _dot(wot_ref[...], o)
    x2_ref[0] = x2.T

    h3 = _rms_t(x2, gmoe_ref[...])
    h_hi = h3.astype(BF16)
    h_lo = (h3 - h_hi.astype(F32)).astype(BF16)
    logits = (_dot(wrht_ref[...], h_hi) + _dot(wrlt_ref[...], h_hi) + _dot(wrht_ref[...], h_lo)) + br_ref[...]
    idx = lax.broadcasted_iota(jnp.int32, logits.shape, 0)
    is_g = (idx >= MOE_EXPERTS) & (idx < MOE_EXPERTS + MOE_GROUPS)
    gl = jnp.where(is_g, logits, NEG_INF)
    gmax = jnp.max(gl, axis=0, keepdims=True)
    g_w = 1.0 / jnp.sum(jnp.exp(gl - gmax), axis=0, keepdims=True)
    g_idx = jnp.min(jnp.where(gl == gmax, idx, 4 * LANES), axis=0, keepdims=True) - MOE_EXPERTS
    in_grp = (idx >= g_idx * MOE_PER_GROUP) & (idx < (g_idx + 1) * MOE_PER_GROUP)
    el = jnp.where(in_grp, logits, NEG_INF)
    v1 = jnp.max(el, axis=0, keepdims=True)
    i1 = jnp.min(jnp.where(el == v1, idx, 4 * LANES), axis=0, keepdims=True)
    el2 = jnp.where(idx == i1, NEG_INF, el)
    v2 = jnp.max(el2, axis=0, keepdims=True)
    i2 = jnp.min(jnp.where(el2 == v2, idx, 4 * LANES), axis=0, keepdims=True)
    e2 = jnp.exp(v2 - v1)
    w1 = g_w / (1.0 + e2)
    w2 = g_w * e2 / (1.0 + e2)
    route = jnp.where(idx == i1, w1, 0.0) + jnp.where(idx == i2, w2, 0.0)
    route_ref[0] = route.T
    gid_ref[0] = g_idx


def _post(x, ys, ym, gm, woutt, gx, wqt, km, vmt, wot, gmoe, wrht, wrlt, br, tm):
    b, s, d = x.shape
    hd = d // XATTN_HEADS
    tok = lambda n: pl.BlockSpec((1, tm, n), lambda i, j: (i, j, 0))
    mem = pl.BlockSpec((1,) + km.shape[1:], lambda i, j: (i, 0, 0, 0))
    return pl.pallas_call(
        functools.partial(_post_kernel, hd=hd),
        grid=(b, s // tm),
        in_specs=[tok(d), tok(ys.shape[-1]), pl.BlockSpec((1, MLA_HEADS, tm, LANES), lambda i, j: (i, 0, j, 0)),
                  _full(gm.shape), _full(woutt.shape), _full(gx.shape), _full(wqt.shape), mem, mem,
                  _full(wot.shape), _full(gmoe.shape), _full(wrht.shape), _full(wrlt.shape), _full(br.shape)],
        out_specs=[tok(d), tok(LANES), pl.BlockSpec((1, 1, tm), lambda i, j: (i, 0, j))],
        out_shape=[jax.ShapeDtypeStruct((b, s, d), F32), jax.ShapeDtypeStruct((b, s, LANES), F32),
                   jax.ShapeDtypeStruct((b, 1, s), jnp.int32)],
        compiler_params=_params("parallel", "parallel", vmem=VMEM_LIMIT_LARGE),
        name="post",
    )(x, ys, ym, gm, woutt, gx, wqt, km, vmt, wot, gmoe, wrht, wrlt, br)


def _route_metadata(gid, tm):
    b, s = gid.shape
    n_tiles = (s + MOE_GROUPS * tm) // tm
    groups = jnp.arange(MOE_GROUPS, dtype=jnp.int32)[None, :, None]
    onehot = (gid[:, None, :] == groups).astype(jnp.int32)
    csum = jnp.cumsum(onehot, axis=2)
    cnt = csum[:, :, -1]
    padded = (cnt + tm - 1) // tm * tm
    seg_end = jnp.cumsum(padded, axis=-1)
    dest = jnp.sum(onehot * ((seg_end - padded)[:, :, None] + csum - 1), axis=1)
    tile_start = jnp.arange(n_tiles, dtype=jnp.int32) * tm
    tile_g = jnp.sum((seg_end[:, None, :] <= tile_start[None, :, None]).astype(jnp.int32), axis=-1)
    tile_g = jnp.minimum(tile_g, MOE_GROUPS - 1)
    n_valid = (seg_end[:, -1] // tm).astype(jnp.int32)
    return dict(pos=dest, tile_g=tile_g.reshape(b * n_tiles), n_valid=n_valid, n_pad=n_tiles * tm)


def _dispatch_kernel(h_ref, g_ref, dst_ref, xs_ref, gs_ref, *, tt):
    @pl.when(pl.program_id(1) == 0)
    def _():
        xs_ref[...] = jnp.zeros(xs_ref.shape, xs_ref.dtype)
        gs_ref[...] = jnp.zeros(gs_ref.shape, gs_ref.dtype)

    for t in range(tt):
        r = dst_ref[0, 0, t]
        xs_ref[0, pl.ds(r, 1), :] = h_ref[0, t:t + 1, :]
        gs_ref[0, pl.ds(r, 1), :] = g_ref[0, t:t + 1, :]


def _dispatch(x2, route, pos, n_pad, tt):
    b, s, d = x2.shape
    nj = s // tt
    row_block = lambda n: pl.BlockSpec((1, n_pad, n), lambda i, j: (i, 0, 0), pipeline_mode=pl.Buffered(1))
    return pl.pallas_call(
        functools.partial(_dispatch_kernel, tt=tt),
        grid=(b, nj),
        in_specs=[pl.BlockSpec((1, tt, d), lambda i, j: (i, j, 0)),
                  pl.BlockSpec((1, tt, LANES), lambda i, j: (i, j, 0)),
                  pl.BlockSpec((1, 1, tt), lambda i, j: (i * nj + j, 0, 0), memory_space=pltpu.SMEM)],
        out_specs=[row_block(d), row_block(LANES)],
        out_shape=[jax.ShapeDtypeStruct((b, n_pad, d), F32), jax.ShapeDtypeStruct((b, n_pad, LANES), F32)],
        compiler_params=_params("parallel", "arbitrary", vmem=VMEM_LIMIT_LARGE),
        name="dispatch",
    )(x2, route, pos.reshape(b * nj, 1, tt))


def _moe_kernel(tg_ref, nv_ref, x_ref, gate_ref, gmoe_ref, wg_ref, wu_ref, wd_ref, o_ref, *, nt):
    bi = pl.program_id(0)
    ti = pl.program_id(1)
    grp = tg_ref[bi * nt + ti]

    @pl.when(ti < nv_ref[bi])
    def _():
        h = _rms(x_ref[0], gmoe_ref[...]).astype(BF16)
        gate = gate_ref[0]
        lane = lax.broadcasted_iota(jnp.int32, gate.shape, 1)
        acts = []
        for e in range(MOE_PER_GROUP):
            a = jax.nn.silu(_dot(h, wg_ref[e])) * _dot(h, wu_ref[e])
            gcol = jnp.sum(jnp.where(lane == grp * MOE_PER_GROUP + e, gate, 0.0), axis=-1, keepdims=True)
            acts.append((a * gcol).astype(BF16))
        wd = wd_ref[...]
        o_ref[0] = _dot(jnp.concatenate(acts, axis=1), wd.reshape(wd.shape[0] * wd.shape[1], wd.shape[2]))

    @pl.when(ti >= nv_ref[bi])
    def _():
        o_ref[0] = jnp.zeros(o_ref.shape[1:], o_ref.dtype)


def _moe(xs, gs, gmoe, tile_g, n_valid, wg, wu, wd, tm):
    b, n_pad, d = xs.shape
    nt = n_pad // tm
    ff = wg.shape[-1]
    group = lambda i, j, tg, nv: (tg[i * nt + j], 0, 0)
    tok = lambda n: pl.BlockSpec((1, tm, n), lambda i, j, tg, nv: (i, j, 0))
    grid_spec = pltpu.PrefetchScalarGridSpec(
        num_scalar_prefetch=2,
        grid=(b, nt),
        in_specs=[tok(d), tok(LANES), pl.BlockSpec(gmoe.shape, lambda i, j, tg, nv: (0, 0)),
                  pl.BlockSpec((MOE_PER_GROUP, d, ff), group),
                  pl.BlockSpec((MOE_PER_GROUP, d, ff), group), pl.BlockSpec((MOE_PER_GROUP, ff, d), group)],
        out_specs=tok(d),
    )
    return pl.pallas_call(
        functools.partial(_moe_kernel, nt=nt),
        grid_spec=grid_spec,
        out_shape=jax.ShapeDtypeStruct((b, n_pad, d), F32),
        compiler_params=_params("parallel", "arbitrary"),
        name="moe",
    )(tile_g, n_valid, xs, gs, gmoe, wg, wu, wd)


def _combine_kernel(pos_ref, ys_hbm, x2_ref, gf_ref, o_ref, buf_ref, sem, *, tt, seq):
    bi = pl.program_id(0)
    base = bi * seq + pl.program_id(1) * tt

    def row_copy(src_row, t):
        return pltpu.make_async_copy(ys_hbm.at[bi, pl.ds(src_row, 1), :], buf_ref.at[pl.ds(t, 1), :], sem)

    for t in range(tt):
        row_copy(pos_ref[base + t], t).start(priority=t % 2)

    pltpu.make_async_copy(ys_hbm.at[bi, pl.ds(0, tt), :], buf_ref, sem).wait()
    o_ref[0] = _rms(x2_ref[0] + buf_ref[...], gf_ref[...])


def _combine(ys, pos, x2, gf, tt):
    b, s, d = x2.shape
    grid_spec = pltpu.PrefetchScalarGridSpec(
        num_scalar_prefetch=1,
        grid=(b, s // tt),
        in_specs=[pl.BlockSpec(memory_space=pl.ANY),
                  pl.BlockSpec((1, tt, d), lambda i, j, p: (i, j, 0)),
                  pl.BlockSpec(gf.shape, lambda i, j, p: (0, 0))],
        out_specs=pl.BlockSpec((1, tt, d), lambda i, j, p: (i, j, 0)),
        scratch_shapes=[pltpu.VMEM((tt, d), F32), pltpu.SemaphoreType.DMA(())],
    )
    return pl.pallas_call(
        functools.partial(_combine_kernel, tt=tt, seq=s),
        grid_spec=grid_spec,
        out_shape=jax.ShapeDtypeStruct((b, s, d), F32),
        compiler_params=_params("parallel", "arbitrary"),
        name="combine",
    )(pos.reshape(b * s), ys, x2, gf)


def _pad_heads(w, per_head):
    k = w.shape[0]
    w = w.reshape(k, MLA_HEADS, per_head)
    w = jnp.pad(w, ((0, 0), (0, 0), (0, LANES - per_head)))
    return w.reshape(k, MLA_HEADS * LANES)


def _block_diag(blocks):
    n, r, c = blocks.shape
    eye = jnp.eye(n, dtype=blocks.dtype)
    return (eye[:, None, :, None] * blocks[:, :, None, :]).reshape(n * r, n * c)


def kernel(x, mem, positions, norm_mix_g, w_in, ssm_lam_re, ssm_lam_im, ssm_log_dt, ssm_b_re, ssm_b_im, ssm_c_re, ssm_c_im, ssm_d, ssm_w_glu, ssm_b_glu, mla_q_norm_g, mla_w_q_up, mla_kv_norm_g, mla_w_kv_up, out_norm_ssm_g, out_norm_mla_g, w_out, norm_xattn_g, norm_mem_g, xattn_w_q, xattn_w_k, xattn_w_v, xattn_w_o, norm_moe_g, moe_w_group, moe_b_group, moe_w_expert, moe_b_expert, moe_w_gate, moe_w_up, moe_w_down, norm_final_g):
    bsz, seq, d = x.shape
    depth = w_in.shape[0]
    assert depth == 1, "the final RMSNorm is fused into the last stage of a single layer"
    d_ssm = ssm_d.shape[-1]
    q_rank = mla_q_norm_g.shape[-1]
    kv_rank = mla_kv_norm_g.shape[-1]
    n_grp = d_ssm // SSM_GROUP
    s1, s2, s3 = d_ssm, d_ssm + q_rank, d_ssm + q_rank + kv_rank
    row = lambda v: v.reshape(1, -1).astype(F32)
    tiles = _tiles(seq)

    half = MLA_ROPE // 2
    invf = (ROPE_THETA ** (-jnp.arange(half, dtype=F32) / half)).reshape(half, 1)
    pos = positions.reshape(bsz, 1, seq)

    for l in range(depth):
        a_re, a_im, bb_re, bb_im = _ssm_prep(ssm_lam_re[l], ssm_lam_im[l], ssm_log_dt[l], ssm_b_re[l], ssm_b_im[l])
        gh = n_grp // 2
        bbd = jnp.stack([
            jnp.concatenate([_block_diag(bb_re[hf * gh:(hf + 1) * gh]), _block_diag(bb_im[hf * gh:(hf + 1) * gh])],
                            axis=1) for hf in range(2)]).astype(BF16)
        c_re_t = jnp.transpose(ssm_c_re[l], (0, 2, 1))
        c_im_t = jnp.transpose(ssm_c_im[l], (0, 2, 1))
        cmat = jnp.stack([
            jnp.concatenate([_block_diag(c_re_t[hf * gh:(hf + 1) * gh]), -_block_diag(c_im_t[hf * gh:(hf + 1) * gh])],
                            axis=0) for hf in range(2)]).astype(BF16)
        atab = jnp.stack([
            jnp.repeat(arr.reshape(2, 1, gh * SSM_STATE), bsz, axis=1).reshape(2 * bsz, gh * SSM_STATE)
            for arr in (a_re, a_im)])

        wi = w_in[l]
        wu = wi[:, :s1].astype(BF16)
        wcq = wi[:, s1:s2].astype(BF16)
        wckv = wi[:, s2:s3].astype(BF16)
        wkrt = jnp.pad(wi[:, s3:], ((0, 0), (MLA_NOPE, LANES - MLA_QK))).T.astype(BF16)
        wqt = _pad_heads(mla_w_q_up[l], MLA_QK).T.astype(BF16)
        wkv = mla_w_kv_up[l].reshape(kv_rank, MLA_HEADS, MLA_NOPE + MLA_V)
        wk = _pad_heads(wkv[:, :, :MLA_NOPE].reshape(kv_rank, -1), MLA_NOPE).astype(BF16)
        wvt = jnp.pad(wkv[:, :, MLA_NOPE:], ((0, 0), (0, 0), (0, VT_ROWS - MLA_V)))
        wvt = wvt.reshape(kv_rank, MLA_HEADS * VT_ROWS).T.astype(BF16)
        u, q, k, v = _proj(x, pos, row(norm_mix_g[l]), wu, wcq, wckv, wkrt, row(mla_q_norm_g[l]), wqt,
                           row(mla_kv_norm_g[l]), wk, wvt, invf, tm=tiles.proj)

        y_ssm = _ssm(u, bbd, atab, cmat, row(ssm_d[l]), ssm_w_glu[l].astype(BF16), row(ssm_b_glu[l]),
                     row(out_norm_ssm_g[l]), tt=tiles.ssm)
        y_mla = _attn(q, k, v, tq=tiles.attn_q, tk=tiles.attn_k, hp=tiles.attn_heads, strip=tiles.attn_strip)

        km, vm = _memkv(mem, row(norm_mem_g[l]), xattn_w_k[l].astype(BF16), xattn_w_v[l].astype(BF16))
        col = lambda v: v.reshape(-1, 1).astype(F32)
        woutt = w_out[l].T.astype(BF16)
        wr = jnp.concatenate([moe_w_expert[l].reshape(d, MOE_EXPERTS), moe_w_group[l]], axis=1)
        wrt = jnp.pad(wr, ((0, 0), (0, LANES - MOE_EXPERTS - MOE_GROUPS))).T.astype(F32)
        wrt_hi = wrt.astype(BF16)
        wrt_lo = (wrt - wrt_hi.astype(F32)).astype(BF16)
        br = jnp.concatenate([moe_b_expert[l].reshape(-1), moe_b_group[l]])
        br = jnp.pad(br, (0, LANES - MOE_EXPERTS - MOE_GROUPS)).reshape(LANES, 1).astype(F32)
        x2, route, gid = _post(x, y_ssm, y_mla, row(out_norm_mla_g[l]), woutt, col(norm_xattn_g[l]),
                          xattn_w_q[l].T.astype(BF16), km, vm, xattn_w_o[l].T.astype(BF16), col(norm_moe_g[l]),
                          wrt_hi, wrt_lo, br, tm=tiles.post)

        meta = _route_metadata(gid.reshape(bsz, seq), tiles.moe)
        xs, gs = _dispatch(x2, route, meta["pos"], meta["n_pad"], tt=tiles.copy)
        ys = _moe(xs, gs, row(norm_moe_g[l]), meta["tile_g"], meta["n_valid"], moe_w_gate[l].astype(BF16),
                  moe_w_up[l].astype(BF16), moe_w_down[l].astype(BF16), tm=tiles.moe)
        x = _combine(ys, meta["pos"], x2, row(norm_final_g), tt=tiles.copy)
    return x
```
